```python
import jax, jax.numpy as jnp
from jax import lax
import numpy as np

D_MODEL = 1024
BATCH = 2
SEQ = 8192
DEPTH = 1

HEAD_DIM = 64
A_HEADS = 8
B_HEADS = 8
IDX_HEADS = 8
IDX_DIM = 64
DILATED_PATTERNS = ((128, 1), (512, 4), (2048, 16))
TOPK_MAX = 256
D_A = A_HEADS * HEAD_DIM
D_B = B_HEADS * HEAD_DIM
D_FF = ((8 * D_MODEL + 767) // 768) * 256
ROPE_THETA = 10000.0
RMS_EPS = 1e-6
BLOCK = 128
ATTN_SCALE = HEAD_DIM ** -0.5
IDX_SCALE = (IDX_DIM ** -0.5) * (IDX_HEADS ** -0.5)
IN_SPLITS = (D_A, D_A, D_A, D_B, HEAD_DIM, HEAD_DIM, IDX_HEADS * IDX_DIM, IDX_DIM, IDX_HEADS, D_MODEL, D_MODEL)
IN_OFFSETS = tuple(int(o) for o in np.cumsum(IN_SPLITS)[:-1])
D_IN = int(sum(IN_SPLITS))

kernel_name = 'hybrid_dilated_dsa_gated_block'


def rmsnorm(x, g):
    x32 = x.astype(jnp.float32)
    y = x32 * lax.rsqrt(jnp.mean(x32 * x32, axis=-1, keepdims=True) + RMS_EPS)
    return (y * g.astype(jnp.float32)).astype(x.dtype)


def rope(z, pos):
    half = z.shape[-1] // 2
    inv_freq = ROPE_THETA ** (-jnp.arange(half, dtype=jnp.float32) / half)
    ang = pos.astype(jnp.float32)[:, None] * inv_freq[None, :]
    cos = jnp.cos(ang)[None, :, None, :]
    sin = jnp.sin(ang)[None, :, None, :]
    z32 = z.astype(jnp.float32)
    z1, z2 = z32[..., :half], z32[..., half:]
    return jnp.concatenate([z1 * cos - z2 * sin, z2 * cos + z1 * sin], axis=-1).astype(z.dtype)


def dilated_window_attention(q, k, v, window, dilation):
    B, T, H, hd = q.shape
    steps = window // dilation
    seg = dilation * BLOCK
    t_pad = -(-T // seg) * seg
    m_len = t_pad // dilation
    nb = m_len // BLOCK

    def to_blocks(z):
        z = jnp.pad(z, ((0, 0), (0, t_pad - T), (0, 0), (0, 0)))
        z = z.reshape(B, m_len, dilation, H, hd).transpose(0, 2, 3, 1, 4)
        return z.reshape(B, dilation, H, nb, BLOCK, hd)

    qb, kb, vb = to_blocks(q), to_blocks(k), to_blocks(v)

    def with_prev(z):
        prev = jnp.pad(z[:, :, :, :-1], ((0, 0), (0, 0), (0, 0), (1, 0), (0, 0), (0, 0)))
        return jnp.concatenate([prev, z], axis=4)

    k2, v2 = with_prev(kb), with_prev(vb)
    s = jnp.einsum('brhnqd,brhnkd->brhnqk', qb, k2).astype(jnp.float32) * ATTN_SCALE
    qi = jnp.arange(BLOCK)[:, None]
    kj = jnp.arange(2 * BLOCK)[None, :]
    dist = qi + BLOCK - kj
    band = (dist >= 0) & (dist <= steps)
    has_prev = (jnp.arange(nb)[:, None, None] > 0) | (kj >= BLOCK)[None]
    mask = band[None] & has_prev
    s = jnp.where(mask, s, -jnp.inf)
    m = jnp.max(s, axis=-1, keepdims=True)
    p = jnp.exp(s - m)
    den = jnp.sum(p, axis=-1, keepdims=True)
    o = jnp.einsum('brhnqk,brhnkd->brhnqd', p.astype(v.dtype), v2).astype(jnp.float32) / den
    lse = (m + jnp.log(den))[..., 0]
    o = o.reshape(B, dilation, H, m_len, hd).transpose(0, 3, 1, 2, 4).reshape(B, t_pad, H, hd)[:, :T]
    lse = lse.reshape(B, dilation, H, m_len).transpose(0, 3, 1, 2).reshape(B, t_pad, H)[:, :T]
    return o, lse


def dilated_mixture(q, k, v):
    outs, lses = [], []
    for window, dilation in DILATED_PATTERNS:
        o, l = dilated_window_attention(q, k, v, window, dilation)
        outs.append(o)
        lses.append(l)
    w = jax.nn.softmax(jnp.stack(lses, axis=0), axis=0)
    return jnp.sum(w[..., None] * jnp.stack(outs, axis=0), axis=0)


def indexed_sparse_attention(q, k, v, qi, ki, wi):
    B, T, H, hd = q.shape
    topk = min(TOPK_MAX, T // 4)
    nb = T // BLOCK
    key_pos = jnp.arange(T)

    def blocks(z):
        return jnp.moveaxis(z.reshape((B, nb, BLOCK) + z.shape[2:]), 1, 0)

    def one_block(args):
        q_blk, qi_blk, wi_blk, start = args
        q_pos = start + jnp.arange(BLOCK)
        causal = key_pos[None, :] <= q_pos[:, None]
        dots = jnp.einsum('bqhd,bsd->bqhs', qi_blk, ki).astype(jnp.float32)
        score = jnp.einsum('bqh,bqhs->bqs', wi_blk.astype(jnp.float32), jax.nn.relu(dots)) * IDX_SCALE
        score = jnp.where(causal[None], score, -jnp.inf)
        _, sel = lax.top_k(score, topk)
        valid = sel <= q_pos[None, :, None]
        k_sel = jax.vmap(lambda kk, ii: kk[ii])(k, sel)
        v_sel = jax.vmap(lambda vv, ii: vv[ii])(v, sel)
        s = jnp.einsum('bqhd,bqkd->bhqk', q_blk, k_sel).astype(jnp.float32) * ATTN_SCALE
        s = jnp.where(valid[:, None], s, -jnp.inf)
        p = jax.nn.softmax(s, axis=-1)
        return jnp.einsum('bhqk,bqkd->bqhd', p.astype(v.dtype), v_sel)

    starts = jnp.arange(nb, dtype=jnp.int32) * BLOCK
    out = lax.map(one_block, (blocks(q), blocks(qi), blocks(wi), starts))
    return jnp.moveaxis(out, 0, 1).reshape(B, T, H, hd)


def setup_inputs(seed: int = 0) -> dict:
    key = jax.random.key(seed)
    ks = jax.random.split(key, 12)
    f32 = jnp.float32

    def dense(k, shape, fan_in):
        return jax.random.normal(k, shape, f32) * fan_in ** -0.5

    def gain(k, shape):
        return 1.0 + 0.02 * jax.random.normal(k, shape, f32)

    return {
        'x': jax.random.normal(ks[0], (BATCH, SEQ, D_MODEL), f32),
        'norm_mix': gain(ks[1], (DEPTH, D_MODEL)),
        'w_in': dense(ks[2], (DEPTH, D_MODEL, D_IN), D_MODEL),
        'w_up_a': dense(ks[3], (DEPTH, D_A, D_MODEL), D_A),
        'w_up_b': dense(ks[4], (DEPTH, D_B, D_MODEL), D_B),
        'w_out': dense(ks[5], (DEPTH, D_MODEL, D_MODEL), D_MODEL),
        'norm_ffn': gain(ks[6], (DEPTH, D_MODEL)),
        'w_gate': dense(ks[7], (DEPTH, D_MODEL, D_FF), D_MODEL),
        'w_up': dense(ks[8], (DEPTH, D_MODEL, D_FF), D_MODEL),
        'w_down': dense(ks[9], (DEPTH, D_FF, D_MODEL), D_FF),
        'norm_final': gain(ks[10], (D_MODEL,)),
    }


def reference(x, norm_mix, w_in, w_up_a, w_up_b, w_out, norm_ffn, w_gate, w_up, w_down, norm_final):
    B, T, _ = x.shape
    pos = jnp.arange(T, dtype=jnp.int32)
    for layer in range(DEPTH):
        h = rmsnorm(x, norm_mix[layer])
        proj = h @ w_in[layer]
        qa, ka, va, qb, kb, vb, qi, ki, wi, ga, gb = jnp.split(proj, IN_OFFSETS, axis=-1)
        qa = rope(qa.reshape(B, T, A_HEADS, HEAD_DIM), pos)
        ka = rope(ka.reshape(B, T, A_HEADS, HEAD_DIM), pos)
        va = va.reshape(B, T, A_HEADS, HEAD_DIM)
        y_a = dilated_mixture(qa, ka, va).astype(x.dtype).reshape(B, T, D_A)
        qb = rope(qb.reshape(B, T, B_HEADS, HEAD_DIM), pos)
        kb = rope(kb[:, :, None], pos)[:, :, 0]
        qi = rope(qi.reshape(B, T, IDX_HEADS, IDX_DIM), pos)
        ki = rope(ki[:, :, None], pos)[:, :, 0]
        y_b = indexed_sparse_attention(qb, kb, vb, qi, ki, wi).reshape(B, T, D_B)
        merged = jax.nn.sigmoid(ga) * (y_a @ w_up_a[layer]) + jax.nn.sigmoid(gb) * (y_b @ w_up_b[layer])
        x = x + merged @ w_out[layer]
        h = rmsnorm(x, norm_ffn[layer])
        x = x + (jax.nn.silu(h @ w_gate[layer]) * (h @ w_up[layer])) @ w_down[layer]
    return rmsnorm(x, norm_final)
```

```python
import functools

import numpy as np
import jax
import jax.numpy as jnp
from jax import lax
from jax.experimental import pallas as pl
from jax.experimental.pallas import tpu as pltpu

D_MODEL = 1024
HEAD_DIM = 64
HALF = HEAD_DIM // 2
N_HEADS = 8
D_HEADS = N_HEADS * HEAD_DIM
IDX_HEADS = 8
DILATED_PATTERNS = ((128, 1), (512, 4), (2048, 16))
TOPK_MAX = 256
D_FF = 2816
ROPE_THETA = 10000.0
RMS_EPS = 1e-6
BLOCK = 128
ATTN_SCALE = HEAD_DIM ** -0.5
IDX_SCALE = (HEAD_DIM ** -0.5) * (IDX_HEADS ** -0.5)

LANES = 128
INT_MIN = -(2 ** 31)
KEY_CHUNK = 512
ROW_TILE = 512
FF_CHUNK = 1408
VMEM_LIMIT = 56 * 1024 * 1024
MXU_DTYPE = jnp.bfloat16

_SPLITS = (D_HEADS, D_HEADS, D_HEADS, D_HEADS, HEAD_DIM, HEAD_DIM, IDX_HEADS * HEAD_DIM, HEAD_DIM, IDX_HEADS,
           D_MODEL, D_MODEL)
_OFF = np.concatenate([[0], np.cumsum(_SPLITS)])
(_QA, _KA, _VA, _QB, _KB, _VB, _QI, _KI, _WI, _GA, _GB) = (int(o) for o in _OFF[:-1])

_P_QA, _P_KA, _P_VA, _P_QB, _P_QI = 0, 512, 1024, 1536, 2048
_P_KBD, _P_KID, _P_MISC = 2560, 2688, 2816
_P_TOTAL = 2944


def _pair_perm():
    idx = np.empty(D_HEADS, np.int64)
    for j in range(D_HEADS):
        g, l = divmod(j, LANES)
        quarter, e = divmod(l, HALF)
        head = 2 * g + (quarter % 2)
        idx[j] = head * HEAD_DIM + (quarter // 2) * HALF + e
    return idx


def _dup_perm():
    idx = np.empty(LANES, np.int64)
    for l in range(LANES):
        quarter, e = divmod(l, HALF)
        idx[l] = (quarter // 2) * HALF + e
    return idx


def _packed_columns():
    pp, dp = _pair_perm(), _dup_perm()
    return np.concatenate([
        _QA + pp, _KA + pp, _VA + np.arange(D_HEADS), _QB + pp, _QI + pp,
        _KB + dp, _KI + dp, _VB + np.arange(HEAD_DIM), _WI + np.arange(IDX_HEADS)])


def _rms(x, g):
    ms = jnp.mean(x * x, axis=-1, keepdims=True)
    return x * lax.rsqrt(ms + RMS_EPS) * g


def _nt_dot(a, b):
    return lax.dot_general(a, b, (((1,), (1,)), ((), ())), preferred_element_type=jnp.float32)


def _proj_kernel(x_ref, g_ref, w_ref, cos_ref, sin_ref,
                 qa_ref, ka_ref, va_ref, qb_ref, qi_ref, kbd_ref, kid_ref, misc_ref, h_ref):
    h_ref[...] = _rms(x_ref[...], g_ref[...]).astype(MXU_DTYPE)
    cos = cos_ref[...]
    sin = sin_ref[...]

    def mm(c0, width):
        return jnp.dot(h_ref[...], w_ref[:, c0:c0 + width], preferred_element_type=jnp.float32)

    def rope(z):
        parts = []
        for g in range(z.shape[1] // LANES):
            zg = z[:, g * LANES:(g + 1) * LANES]
            parts.append(zg * cos + pltpu.roll(zg, 2 * HALF, axis=1) * sin)
        return parts[0] if len(parts) == 1 else jnp.concatenate(parts, axis=1)

    qa_ref[...] = (rope(mm(_P_QA, D_HEADS)) * ATTN_SCALE).astype(MXU_DTYPE)
    ka_ref[...] = rope(mm(_P_KA, D_HEADS)).astype(MXU_DTYPE)
    va_ref[...] = mm(_P_VA, D_HEADS).astype(MXU_DTYPE)
    qb_ref[...] = (rope(mm(_P_QB, D_HEADS)) * ATTN_SCALE).astype(MXU_DTYPE)
    qi_ref[...] = rope(mm(_P_QI, D_HEADS)).astype(MXU_DTYPE)
    kbd_ref[...] = rope(mm(_P_KBD, LANES)).astype(MXU_DTYPE)
    kid_ref[...] = rope(mm(_P_KID, LANES)).astype(MXU_DTYPE)
    misc_ref[...] = mm(_P_MISC, LANES)


def _project(x2, g, w_pack, cos_t, sin_t, seq):
    n = x2.shape[0]
    tm = ROW_TILE
    tiles_per_seq = seq // tm
    row = lambda i: (i, 0)
    const = lambda i: (0, 0)
    pos = lambda i: (i % tiles_per_seq, 0)
    wide = jax.ShapeDtypeStruct((n, D_HEADS), MXU_DTYPE)
    narrow = jax.ShapeDtypeStruct((n, LANES), MXU_DTYPE)
    return pl.pallas_call(
        _proj_kernel,
        grid=(n // tm,),
        in_specs=[
            pl.BlockSpec((tm, D_MODEL), row),
            pl.BlockSpec((1, D_MODEL), const),
            pl.BlockSpec((D_MODEL, _P_TOTAL), const),
            pl.BlockSpec((tm, LANES), pos),
            pl.BlockSpec((tm, LANES), pos),
        ],
        out_specs=[pl.BlockSpec((tm, D_HEADS), row)] * 5 + [pl.BlockSpec((tm, LANES), row)] * 3,
        out_shape=[wide] * 5 + [narrow, narrow, jax.ShapeDtypeStruct((n, LANES), jnp.float32)],
        scratch_shapes=[pltpu.VMEM((tm, D_MODEL), MXU_DTYPE)],
        compiler_params=pltpu.CompilerParams(dimension_semantics=("arbitrary",), vmem_limit_bytes=VMEM_LIMIT),
        name="proj_rope",
    )(x2, g, w_pack, cos_t, sin_t)


def _dil_kernel(q_ref, k_ref, v_ref, kp_ref, vp_ref, o_ref, lse_ref):
    n = pl.program_id(2)
    qi = lax.broadcasted_iota(jnp.int32, (BLOCK, 2 * BLOCK), 0)
    kj = lax.broadcasted_iota(jnp.int32, (BLOCK, 2 * BLOCK), 1)
    band = (kj >= qi) & (kj <= qi + BLOCK) & ((kj >= BLOCK) | (n > 0))
    lane = lax.broadcasted_iota(jnp.int32, (1, LANES), 1)
    first_v = lane < HEAD_DIM
    for p in range(N_HEADS // 2):
        cols = slice(p * LANES, (p + 1) * LANES)
        qp = q_ref[0, :, cols]
        k2 = jnp.concatenate([kp_ref[0, :, cols], k_ref[0, :, cols]], axis=0)
        v2 = jnp.concatenate([vp_ref[0, :, cols], v_ref[0, :, cols]], axis=0)
        outs, lses = [], []
        for hh in range(2):
            head_lanes = ((lane // HALF) % 2) == hh
            qm = jnp.where(head_lanes, qp, jnp.zeros_like(qp))
            s = jnp.where(band, _nt_dot(qm, k2), -jnp.inf)
            m = jnp.max(s, axis=-1, keepdims=True)
            e = jnp.exp(s - m)
            den = jnp.sum(e, axis=-1, keepdims=True)
            o = jnp.dot(e.astype(MXU_DTYPE), v2, preferred_element_type=jnp.float32) / den
            outs.append(o)
            lses.append(m + jnp.log(den))
        o_ref[0, :, cols] = jnp.where(first_v, outs[0], outs[1]).astype(o_ref.dtype)
        lse_ref[0, :, cols] = jnp.where(first_v, lses[0], lses[1])


def _dilated(qa, ka, va, batch, seq, dilation):
    m_len = seq // dilation
    nb = m_len // BLOCK
    width = dilation * D_HEADS
    view = lambda z: z.reshape(batch, m_len, width)
    cur = lambda b, r, n: (b, n, r)
    prev = lambda b, r, n: (b, jnp.maximum(n - 1, 0), r)
    blk = (1, BLOCK, D_HEADS)
    o, lse = pl.pallas_call(
        _dil_kernel,
        grid=(batch, dilation, nb),
        in_specs=[pl.BlockSpec(blk, cur), pl.BlockSpec(blk, cur), pl.BlockSpec(blk, cur),
                  pl.BlockSpec(blk, prev), pl.BlockSpec(blk, prev)],
        out_specs=[pl.BlockSpec(blk, cur), pl.BlockSpec(blk, cur)],
        out_shape=[jax.ShapeDtypeStruct((batch, m_len, width), MXU_DTYPE),
                   jax.ShapeDtypeStruct((batch, m_len, width), jnp.float32)],
        compiler_params=pltpu.CompilerParams(dimension_semantics=("arbitrary",) * 3),
        name=f"dilated_d{dilation}",
    )(view(qa), view(ka), view(va), view(ka), view(va))
    return o.reshape(batch * seq, D_HEADS), lse.reshape(batch * seq, D_HEADS)


def _sparse_kernel(qi_ref, qb_ref, wt_ref, kid_ref, kbd_ref, vbt_ref, out_ref,
                   keys_ref, qis_ref, qbs_ref, m_ref, l_ref, acc_ref, *, topk, index_bits):
    i = pl.program_id(1)
    nch = i // (KEY_CHUNK // BLOCK) + 1
    lane = lax.broadcasted_iota(jnp.int32, (1, LANES), 1)
    t_idx = i * BLOCK + lane
    row_iota = lax.broadcasted_iota(jnp.int32, (KEY_CHUNK, LANES), 0)

    for h in range(N_HEADS):
        cols = slice((h // 2) * LANES, (h // 2 + 1) * LANES)
        head_lanes = ((lane // HALF) % 2) == (h % 2)
        rows = slice(h * BLOCK, (h + 1) * BLOCK)
        qis_ref[rows, :] = jnp.where(head_lanes, qi_ref[0, :, cols], jnp.zeros((), MXU_DTYPE))
        qbs_ref[rows, :] = jnp.where(head_lanes, qb_ref[0, :, cols], jnp.zeros((), MXU_DTYPE))

    def chunk_start(c):
        return pl.multiple_of(c * KEY_CHUNK, KEY_CHUNK)

    def score_chunk(c, carry):
        r0 = chunk_start(c)
        kc = kid_ref[0, pl.ds(r0, KEY_CHUNK), :]
        acc = jnp.zeros((KEY_CHUNK, LANES), jnp.float32)
        for p in range(N_HEADS // 2):
            d = _nt_dot(kc, qis_ref[p * 2 * BLOCK:(p + 1) * 2 * BLOCK, :])
            for hh in range(2):
                h = 2 * p + hh
                acc = acc + jnp.maximum(d[:, hh * BLOCK:(hh + 1) * BLOCK], 0.0) * wt_ref[0, h:h + 1, :]
        bits = lax.bitcast_convert_type(acc * IDX_SCALE, jnp.int32)
        key = bits ^ ((bits >> 31) & jnp.int32(0x7FFFFFFF))
        keys_ref[pl.ds(r0, KEY_CHUNK), :] = jnp.where(r0 + row_iota <= t_idx, key, jnp.int32(INT_MIN))
        return carry

    lax.fori_loop(0, nch, score_chunk, 0)

    def count(pred):
        def body(c, acc):
            r0 = chunk_start(c)
            hit = jnp.where(pred(keys_ref[pl.ds(r0, KEY_CHUNK), :], r0), 1.0, 0.0)
            return acc + jnp.sum(hit.reshape(KEY_CHUNK // 8, 8, LANES), axis=0)
        acc = lax.fori_loop(0, nch, body, jnp.zeros((8, LANES), jnp.float32))
        return jnp.sum(acc, axis=0, keepdims=True)

    def bit_step(b, res):
        cand = res | (jnp.int32(1) << (31 - b))
        thr_c = cand ^ jnp.int32(INT_MIN)
        cnt = count(lambda x, r0: x >= thr_c)
        return jnp.where(cnt >= topk, cand, res)

    res = lax.fori_loop(0, 32, bit_step, jnp.zeros((1, LANES), jnp.int32))
    thr = jnp.maximum(res ^ jnp.int32(INT_MIN), jnp.int32(INT_MIN + 1))

    cnt_ge = count(lambda x, r0: x >= thr)
    tie_lane = cnt_ge > topk

    @pl.when(jnp.max(jnp.where(tie_lane, 1.0, 0.0)) > 0.0)
    def _():
        need = topk - count(lambda x, r0: x > thr)

        def idx_step(b, lim):
            cand = lim | (jnp.int32(1) << (index_bits - 1 - b))
            below = count(lambda x, r0: (x == thr) & (r0 + row_iota < cand))
            return jnp.where(below < need, cand, lim)

        lim = lax.fori_loop(0, index_bits, idx_step, jnp.zeros((1, LANES), jnp.int32))

        def demote(c, carry):
            r0 = chunk_start(c)
            x = keys_ref[pl.ds(r0, KEY_CHUNK), :]
            drop = (x == thr) & (r0 + row_iota > lim) & tie_lane
            keys_ref[pl.ds(r0, KEY_CHUNK), :] = jnp.where(drop, x - 1, x)
            return carry

        lax.fori_loop(0, nch, demote, 0)

    m_ref[...] = jnp.full(m_ref.shape, -1e30, jnp.float32)
    l_ref[...] = jnp.zeros(l_ref.shape, jnp.float32)
    acc_ref[...] = jnp.zeros(acc_ref.shape, jnp.float32)

    def attend_chunk(c, carry):
        r0 = chunk_start(c)
        kc = kbd_ref[0, pl.ds(r0, KEY_CHUNK), :]
        vt = vbt_ref[0, c]
        sel = keys_ref[pl.ds(r0, KEY_CHUNK), :] >= thr
        for p in range(N_HEADS // 2):
            s = _nt_dot(kc, qbs_ref[p * 2 * BLOCK:(p + 1) * 2 * BLOCK, :])
            probs = []
            for hh in range(2):
                cols = slice((2 * p + hh) * BLOCK, (2 * p + hh + 1) * BLOCK)
                sh = jnp.where(sel, s[:, hh * BLOCK:(hh + 1) * BLOCK], -jnp.inf)
                m_old = m_ref[:, cols]
                m_new = jnp.maximum(m_old, jnp.max(sh, axis=0, keepdims=True))
                e = jnp.exp(sh - m_new)
                alpha = jnp.exp(m_old - m_new)
                m_ref[:, cols] = m_new
                l_ref[:, cols] = alpha * l_ref[:, cols] + jnp.sum(e, axis=0, keepdims=True)
                acc_ref[:, cols] = alpha * acc_ref[:, cols]
                probs.append(e.astype(MXU_DTYPE))
            pcols = slice(p * 2 * BLOCK, (p + 1) * 2 * BLOCK)
            acc_ref[:, pcols] += jnp.dot(vt, jnp.concatenate(probs, axis=1), preferred_element_type=jnp.float32)
        return carry

    lax.fori_loop(0, nch, attend_chunk, 0)

    o_t = acc_ref[...] / l_ref[...]
    for p in range(N_HEADS // 2):
        pair = jnp.concatenate([o_t[:, (2 * p) * BLOCK:(2 * p + 1) * BLOCK],
                                o_t[:, (2 * p + 1) * BLOCK:(2 * p + 2) * BLOCK]], axis=0)
        out_ref[0, :, p * LANES:(p + 1) * LANES] = pair.T.astype(out_ref.dtype)


def _sparse(qi, qb, wt, kid, kbd, vbt, batch, seq):
    topk = min(TOPK_MAX, seq // 4)
    nq = seq // BLOCK
    qblk = pl.BlockSpec((1, BLOCK, D_HEADS), lambda b, i: (b, i, 0))
    full = pl.BlockSpec((1, seq, LANES), lambda b, i: (b, 0, 0))
    kern = functools.partial(_sparse_kernel, topk=topk, index_bits=max(1, (seq - 1).bit_length()))
    return pl.pallas_call(
        kern,
        grid=(batch, nq),
        in_specs=[qblk, qblk,
                  pl.BlockSpec((1, IDX_HEADS, BLOCK), lambda b, i: (b, 0, i)),
                  full, full,
                  pl.BlockSpec((1, seq // KEY_CHUNK, HEAD_DIM, KEY_CHUNK), lambda b, i: (b, 0, 0, 0))],
        out_specs=qblk,
        out_shape=jax.ShapeDtypeStruct((batch, seq, D_HEADS), MXU_DTYPE),
        scratch_shapes=[pltpu.VMEM((seq, LANES), jnp.int32),
                        pltpu.VMEM((N_HEADS * BLOCK, LANES), MXU_DTYPE),
                        pltpu.VMEM((N_HEADS * BLOCK, LANES), MXU_DTYPE),
                        pltpu.VMEM((1, N_HEADS * BLOCK), jnp.float32),
                        pltpu.VMEM((1, N_HEADS * BLOCK), jnp.float32),
                        pltpu.VMEM((HEAD_DIM, N_HEADS * BLOCK), jnp.float32)],
        compiler_params=pltpu.CompilerParams(dimension_semantics=("arbitrary", "arbitrary"),
                                             vmem_limit_bytes=VMEM_LIMIT),
        name="indexer_sparse_attn",
    )(qi, qb, wt, kid, kbd, vbt)


def _merge_kernel(x_ref, g_ref, o1_ref, o2_ref, o3_ref, l1_ref, l2_ref, l3_ref, yb_ref,
                  wg_ref, wua_ref, wub_ref, wo_ref, x1_ref):
    x = x_ref[...]
    h = _rms(x, g_ref[...]).astype(MXU_DTYPE)
    l1, l2, l3 = l1_ref[...], l2_ref[...], l3_ref[...]
    m = jnp.maximum(jnp.maximum(l1, l2), l3)
    e1, e2, e3 = jnp.exp(l1 - m), jnp.exp(l2 - m), jnp.exp(l3 - m)
    ya = (e1 * o1_ref[...].astype(jnp.float32) + e2 * o2_ref[...].astype(jnp.float32)
          + e3 * o3_ref[...].astype(jnp.float32)) / (e1 + e2 + e3)
    ua = jnp.dot(ya.astype(MXU_DTYPE), wua_ref[...], preferred_element_type=jnp.float32)
    ub = jnp.dot(yb_ref[...], wub_ref[...], preferred_element_type=jnp.float32)
    ga = jnp.dot(h, wg_ref[:, :D_MODEL], preferred_element_type=jnp.float32)
    gb = jnp.dot(h, wg_ref[:, D_MODEL:], preferred_element_type=jnp.float32)
    merged = jax.nn.sigmoid(ga) * ua + jax.nn.sigmoid(gb) * ub
    x1_ref[...] = x + jnp.dot(merged.astype(MXU_DTYPE), wo_ref[...], preferred_element_type=jnp.float32)


def _merge(x2, g, os_, lses, yb, wg, wua, wub, wo):
    n = x2.shape[0]
    tm = ROW_TILE
    row = lambda i: (i, 0)
    const = lambda i: (0, 0)
    half = pl.BlockSpec((tm, D_HEADS), row)
    return pl.pallas_call(
        _merge_kernel,
        grid=(n // tm,),
        in_specs=[pl.BlockSpec((tm, D_MODEL), row), pl.BlockSpec((1, D_MODEL), const)] + [half] * 7 + [
            pl.BlockSpec((D_MODEL, 2 * D_MODEL), const),
            pl.BlockSpec((D_HEADS, D_MODEL), const),
            pl.BlockSpec((D_HEADS, D_MODEL), const),
            pl.BlockSpec((D_MODEL, D_MODEL), const)],
        out_specs=pl.BlockSpec((tm, D_MODEL), row),
        out_shape=jax.ShapeDtypeStruct((n, D_MODEL), jnp.float32),
        compiler_params=pltpu.CompilerParams(dimension_semantics=("arbitrary",), vmem_limit_bytes=VMEM_LIMIT),
        name="mix_gate_out",
    )(x2, g, *os_, *lses, yb, wg, wua, wub, wo)


def _ffn_kernel(x_ref, g_ref, wgate_ref, wup_ref, wdown_ref, gf_ref, out_ref):
    x = x_ref[...]
    h = _rms(x, g_ref[...]).astype(MXU_DTYPE)
    y = x
    for c0 in range(0, D_FF, FF_CHUNK):
        a = jnp.dot(h, wgate_ref[:, c0:c0 + FF_CHUNK], preferred_element_type=jnp.float32)
        u = jnp.dot(h, wup_ref[:, c0:c0 + FF_CHUNK], preferred_element_type=jnp.float32)
        act = (a * jax.nn.sigmoid(a) * u).astype(MXU_DTYPE)
        y = y + jnp.dot(act, wdown_ref[c0:c0 + FF_CHUNK, :], preferred_element_type=jnp.float32)
    out_ref[...] = _rms(y, gf_ref[...])


def _ffn(x1, g, wgate, wup, wdown, gf):
    n = x1.shape[0]
    tm = ROW_TILE
    row = lambda i: (i, 0)
    const = lambda i: (0, 0)
    return pl.pallas_call(
        _ffn_kernel,
        grid=(n // tm,),
        in_specs=[pl.BlockSpec((tm, D_MODEL), row), pl.BlockSpec((1, D_MODEL), const),
                  pl.BlockSpec((D_MODEL, D_FF), const), pl.BlockSpec((D_MODEL, D_FF), const),
                  pl.BlockSpec((D_FF, D_MODEL), const), pl.BlockSpec((1, D_MODEL), const)],
        out_specs=pl.BlockSpec((tm, D_MODEL), row),
        out_shape=jax.ShapeDtypeStruct((n, D_MODEL), jnp.float32),
        compiler_params=pltpu.CompilerParams(dimension_semantics=("arbitrary",), vmem_limit_bytes=VMEM_LIMIT),
        name="ffn_norm",
    )(x1, g, wgate, wup, wdown, gf)


def _rope_tables(seq):
    inv_freq = ROPE_THETA ** (-jnp.arange(HALF, dtype=jnp.float32) / HALF)
    ang = jnp.arange(seq, dtype=jnp.int32).astype(jnp.float32)[:, None] * inv_freq[None, :]
    cos, sin = jnp.cos(ang), jnp.sin(ang)
    return jnp.tile(cos, (1, 4)), jnp.concatenate([-sin, -sin, sin, sin], axis=1)


def kernel(x, norm_mix, w_in, w_up_a, w_up_b, w_out, norm_ffn, w_gate, w_up, w_down, norm_final):
    batch, seq, _ = x.shape
    assert seq % max(d * BLOCK for _, d in DILATED_PATTERNS) == 0 and seq % KEY_CHUNK == 0
    assert all(w // d == BLOCK for w, d in DILATED_PATTERNS)
    n = batch * seq
    bf = MXU_DTYPE
    xf = x.reshape(n, D_MODEL)
    cos_t, sin_t = _rope_tables(seq)
    for layer in range(w_in.shape[0]):
        w = w_in[layer]
        w_pack = jnp.pad(w[:, _packed_columns()], ((0, 0), (0, _P_TOTAL - _P_MISC - HEAD_DIM - IDX_HEADS))).astype(bf)
        w_gates = w[:, _GA:].astype(bf)
        qa, ka, va, qb, qi, kbd, kid, misc = _project(xf, norm_mix[layer][None], w_pack, cos_t, sin_t, seq)

        dil = [_dilated(qa, ka, va, batch, seq, d) for _, d in DILATED_PATTERNS]

        vb = misc[:, :HEAD_DIM].astype(bf).reshape(batch, seq // KEY_CHUNK, KEY_CHUNK, HEAD_DIM)
        vbt = jnp.swapaxes(vb, 2, 3)
        wt = jnp.swapaxes(misc[:, HEAD_DIM:HEAD_DIM + IDX_HEADS].reshape(batch, seq, IDX_HEADS), 1, 2)
        r3 = lambda z: z.reshape(batch, seq, z.shape[-1])
        yb = _sparse(r3(qi), r3(qb), wt, r3(kid), r3(kbd), vbt, batch, seq).reshape(n, D_HEADS)

        x1 = _merge(xf, norm_mix[layer][None], [o for o, _ in dil], [l for _, l in dil], yb,
                    w_gates, w_up_a[layer].astype(bf), w_up_b[layer].astype(bf), w_out[layer].astype(bf))
        last = layer == w_in.shape[0] - 1
        assert last, "the final norm is fused into the FFN kernel of the last layer"
        xf = _ffn(x1, norm_ffn[layer][None], w_gate[layer].astype(bf), w_up[layer].astype(bf),
                  w_down[layer].astype(bf), norm_final[None])
    return xf.reshape(batch, seq, D_MODEL)
```

```python
import functools

import numpy as np
import jax
import jax.numpy as jnp
from jax import lax
from jax.experimental import pallas as pl
from jax.experimental.pallas import tpu as pltpu

D_MODEL = 1024
HEAD_DIM = 64
HALF = HEAD_DIM // 2
N_HEADS = 8
D_HEADS = N_HEADS * HEAD_DIM
IDX_HEADS = 8
DILATED_PATTERNS = ((128, 1), (512, 4), (2048, 16))
TOPK_MAX = 256
D_FF = 2816
ROPE_THETA = 10000.0
RMS_EPS = 1e-6
BLOCK = 128
ATTN_SCALE = HEAD_DIM ** -0.5
IDX_SCALE = (HEAD_DIM ** -0.5) * (IDX_HEADS ** -0.5)

LANES = 128
INT_MIN = -(2 ** 31)
KEY_CHUNK = 512
ROW_TILE = 512
FF_CHUNK = 1408
VMEM_LIMIT = 56 * 1024 * 1024
MXU_DTYPE = jnp.bfloat16

_SPLITS = (D_HEADS, D_HEADS, D_HEADS, D_HEADS, HEAD_DIM, HEAD_DIM, IDX_HEADS * HEAD_DIM, HEAD_DIM, IDX_HEADS,
           D_MODEL, D_MODEL)
_OFF = np.concatenate([[0], np.cumsum(_SPLITS)])
(_QA, _KA, _VA, _QB, _KB, _VB, _QI, _KI, _WI, _GA, _GB) = (int(o) for o in _OFF[:-1])

_P_QA, _P_KA, _P_VA, _P_QB, _P_QI = 0, 512, 1024, 1536, 2048
_P_KBD, _P_KID, _P_MISC = 2560, 2688, 2816
_P_TOTAL = 2944


def _pair_perm():
    idx = np.empty(D_HEADS, np.int64)
    for j in range(D_HEADS):
        g, l = divmod(j, LANES)
        quarter, e = divmod(l, HALF)
        head = 2 * g + (quarter % 2)
        idx[j] = head * HEAD_DIM + (quarter // 2) * HALF + e
    return idx


def _dup_perm():
    idx = np.empty(LANES, np.int64)
    for l in range(LANES):
        quarter, e = divmod(l, HALF)
        idx[l] = (quarter // 2) * HALF + e
    return idx


def _packed_columns():
    pp, dp = _pair_perm(), _dup_perm()
    return np.concatenate([
        _QA + pp, _KA + pp, _VA + np.arange(D_HEADS), _QB + pp, _QI + pp,
        _KB + dp, _KI + dp, _VB + np.arange(HEAD_DIM), _WI + np.arange(IDX_HEADS)])


def _rms(x, g):
    ms = jnp.mean(x * x, axis=-1, keepdims=True)
    return x * lax.rsqrt(ms + RMS_EPS) * g


def _nt_dot(a, b):
    return lax.dot_general(a, b, (((1,), (1,)), ((), ())), preferred_element_type=jnp.float32)


def _proj_kernel(x_ref, g_ref, w_ref, cos_ref, sin_ref,
                 qa_ref, ka_ref, va_ref, qb_ref, qi_ref, kbd_ref, kid_ref, misc_ref, h_ref):
    h_ref[...] = _rms(x_ref[...], g_ref[...]).astype(MXU_DTYPE)
    cos = cos_ref[...]
    sin = sin_ref[...]

    def mm(c0, width):
        return jnp.dot(h_ref[...], w_ref[:, c0:c0 + width], preferred_element_type=jnp.float32)

    def rope(z):
        parts = []
        for g in range(z.shape[1] // LANES):
            zg = z[:, g * LANES:(g + 1) * LANES]
            parts.append(zg * cos + pltpu.roll(zg, 2 * HALF, axis=1) * sin)
        return parts[0] if len(parts) == 1 else jnp.concatenate(parts, axis=1)

    qa_ref[...] = (rope(mm(_P_QA, D_HEADS)) * ATTN_SCALE).astype(MXU_DTYPE)
    ka_ref[...] = rope(mm(_P_KA, D_HEADS)).astype(MXU_DTYPE)
    va_ref[...] = mm(_P_VA, D_HEADS).astype(MXU_DTYPE)
    qb_ref[...] = (rope(mm(_P_QB, D_HEADS)) * ATTN_SCALE).astype(MXU_DTYPE)
    qi_ref[...] = rope(mm(_P_QI, D_HEADS)).astype(MXU_DTYPE)
    kbd_ref[...] = rope(mm(_P_KBD, LANES)).astype(MXU_DTYPE)
    kid_ref[...] = rope(mm(_P_KID, LANES)).astype(MXU_DTYPE)
    misc_ref[...] = mm(_P_MISC, LANES)


def _project(x2, g, w_pack, cos_t, sin_t, seq):
    n = x2.shape[0]
    tm = ROW_TILE
    tiles_per_seq = seq // tm
    row = lambda i: (i, 0)
    const = lambda i: (0, 0)
    pos = lambda i: (i % tiles_per_seq, 0)
    wide = jax.ShapeDtypeStruct((n, D_HEADS), MXU_DTYPE)
    narrow = jax.ShapeDtypeStruct((n, LANES), MXU_DTYPE)
    return pl.pallas_call(
        _proj_kernel,
        grid=(n // tm,),
        in_specs=[
            pl.BlockSpec((tm, D_MODEL), row),
            pl.BlockSpec((1, D_MODEL), const),
            pl.BlockSpec((D_MODEL, _P_TOTAL), const),
            pl.BlockSpec((tm, LANES), pos),
            pl.BlockSpec((tm, LANES), pos),
        ],
        out_specs=[pl.BlockSpec((tm, D_HEADS), row)] * 5 + [pl.BlockSpec((tm, LANES), row)] * 3,
        out_shape=[wide] * 5 + [narrow, narrow, jax.ShapeDtypeStruct((n, LANES), jnp.float32)],
        scratch_shapes=[pltpu.VMEM((tm, D_MODEL), MXU_DTYPE)],
        compiler_params=pltpu.CompilerParams(dimension_semantics=("arbitrary",), vmem_limit_bytes=VMEM_LIMIT),
        name="proj_rope",
    )(x2, g, w_pack, cos_t, sin_t)


def _dil_kernel(q_ref, k_ref, v_ref, kp_ref, vp_ref, o_ref, lse_ref):
    n = pl.program_id(2)
    qi = lax.broadcasted_iota(jnp.int32, (BLOCK, 2 * BLOCK), 0)
    kj = lax.broadcasted_iota(jnp.int32, (BLOCK, 2 * BLOCK), 1)
    band = (kj >= qi) & (kj <= qi + BLOCK) & ((kj >= BLOCK) | (n > 0))
    lane = lax.broadcasted_iota(jnp.int32, (1, LANES), 1)
    first_v = lane < HEAD_DIM
    for p in range(N_HEADS // 2):
        cols = slice(p * LANES, (p + 1) * LANES)
        qp = q_ref[0, :, cols]
        k2 = jnp.concatenate([kp_ref[0, :, cols], k_ref[0, :, cols]], axis=0)
        v2 = jnp.concatenate([vp_ref[0, :, cols], v_ref[0, :, cols]], axis=0)
        outs, lses = [], []
        for hh in range(2):
            head_lanes = ((lane // HALF) % 2) == hh
            qm = jnp.where(head_lanes, qp, jnp.zeros_like(qp))
            s = jnp.where(band, _nt_dot(qm, k2), -jnp.inf)
            m = jnp.max(s, axis=-1, keepdims=True)
            e = jnp.exp(s - m)
            den = jnp.sum(e, axis=-1, keepdims=True)
            o = jnp.dot(e.astype(MXU_DTYPE), v2, preferred_element_type=jnp.float32) / den
            outs.append(o)
            lses.append(m + jnp.log(den))
        o_ref[0, :, cols] = jnp.where(first_v, outs[0], outs[1]).astype(o_ref.dtype)
        lse_ref[0, :, cols] = jnp.where(first_v, lses[0], lses[1])


def _dilated(qa, ka, va, batch, seq, dilation):
    m_len = seq // dilation
    nb = m_len // BLOCK
    width = dilation * D_HEADS
    view = lambda z: z.reshape(batch, m_len, width)
    cur = lambda b, r, n: (b, n, r)
    prev = lambda b, r, n: (b, jnp.maximum(n - 1, 0), r)
    blk = (1, BLOCK, D_HEADS)
    o, lse = pl.pallas_call(
        _dil_kernel,
        grid=(batch, dilation, nb),
        in_specs=[pl.BlockSpec(blk, cur), pl.BlockSpec(blk, cur), pl.BlockSpec(blk, cur),
                  pl.BlockSpec(blk, prev), pl.BlockSpec(blk, prev)],
        out_specs=[pl.BlockSpec(blk, cur), pl.BlockSpec(blk, cur)],
        out_shape=[jax.ShapeDtypeStruct((batch, m_len, width), MXU_DTYPE),
                   jax.ShapeDtypeStruct((batch, m_len, width), jnp.float32)],
        compiler_params=pltpu.CompilerParams(dimension_semantics=("arbitrary",) * 3),
        name=f"dilated_d{dilation}",
    )(view(qa), view(ka), view(va), view(ka), view(va))
    return o.reshape(batch * seq, D_HEADS), lse.reshape(batch * seq, D_HEADS)


def _key_to_f32(key):
    bits = key ^ ((key >> 31) & jnp.int32(0x7FFFFFFF))
    return lax.bitcast_convert_type(bits, jnp.float32)


def _fold_rows(x, op):
    y = x.reshape(8, KEY_CHUNK // 8, LANES)
    y = op(y, axis=0)
    return op(y.reshape(KEY_CHUNK // 64, 8, LANES), axis=0)


def _sparse_kernel(qi_ref, qb_ref, wt_ref, kid_ref, kbd_ref, vbt_ref, out_ref,
                   sc_ref, qis_ref, qbs_ref, acc_ref, *, topk, index_bits):
    i = pl.program_id(1)
    nch = i // (KEY_CHUNK // BLOCK) + 1
    lane = lax.broadcasted_iota(jnp.int32, (1, LANES), 1)
    t_idx = i * BLOCK + lane
    row_iota = lax.broadcasted_iota(jnp.int32, (KEY_CHUNK, LANES), 0)
    neg_inf = jnp.float32(-jnp.inf)
    f32_lowest = jnp.float32(jnp.finfo(jnp.float32).min)

    for h in range(N_HEADS):
        cols = slice((h // 2) * LANES, (h // 2 + 1) * LANES)
        head_lanes = ((lane // HALF) % 2) == (h % 2)
        rows = slice(h * BLOCK, (h + 1) * BLOCK)
        qis_ref[rows, :] = jnp.where(head_lanes, qi_ref[0, :, cols], jnp.zeros((), MXU_DTYPE))
        qbs_ref[rows, :] = jnp.where(head_lanes, qb_ref[0, :, cols], jnp.zeros((), MXU_DTYPE))

    def chunk_start(c):
        return pl.multiple_of(c * KEY_CHUNK, KEY_CHUNK)

    def pair_q(ref, p):
        return ref[p * 2 * BLOCK:(p + 1) * 2 * BLOCK, :]

    def score_chunk(c, carry):
        r0 = chunk_start(c)
        kc = kid_ref[0, pl.ds(r0, KEY_CHUNK), :]
        acc = jnp.zeros((KEY_CHUNK, LANES), jnp.float32)
        for p in range(N_HEADS // 2):
            d = _nt_dot(kc, pair_q(qis_ref, p))
            for hh in range(2):
                h = 2 * p + hh
                acc = acc + jnp.maximum(d[:, hh * BLOCK:(hh + 1) * BLOCK], 0.0) * wt_ref[0, h:h + 1, :]
        sc_ref[pl.ds(r0, KEY_CHUNK), :] = jnp.where(r0 + row_iota <= t_idx, acc * IDX_SCALE, neg_inf)
        return carry

    lax.fori_loop(0, nch, score_chunk, 0)

    def count(pred):
        def body(c, acc):
            r0 = chunk_start(c)
            hit = jnp.where(pred(sc_ref[pl.ds(r0, KEY_CHUNK), :], r0), 1.0, 0.0)
            return acc + jnp.sum(hit.reshape(8, KEY_CHUNK // 8, LANES), axis=0)
        acc = lax.fori_loop(0, nch, body, jnp.zeros((KEY_CHUNK // 8, LANES), jnp.float32))
        return jnp.sum(acc, axis=0, keepdims=True)

    def bit_step(b, res):
        cand = res | (jnp.int32(1) << (31 - b))
        thr_c = _key_to_f32(cand ^ jnp.int32(INT_MIN))
        cnt = count(lambda x, r0: x >= thr_c)
        return jnp.where(cnt >= topk, cand, res)

    res = lax.fori_loop(0, 32, bit_step, jnp.zeros((1, LANES), jnp.int32))
    thr = _key_to_f32(res ^ jnp.int32(INT_MIN))
    thr = jnp.where(thr > f32_lowest, thr, f32_lowest)

    cnt_ge = count(lambda x, r0: x >= thr)
    tie_lane = cnt_ge > topk

    @pl.when(jnp.max(jnp.where(tie_lane, 1.0, 0.0)) > 0.0)
    def _():
        need = topk - count(lambda x, r0: x > thr)

        def idx_step(b, lim):
            cand = lim | (jnp.int32(1) << (index_bits - 1 - b))
            below = count(lambda x, r0: (x == thr) & (r0 + row_iota < cand))
            return jnp.where(below < need, cand, lim)

        lim = lax.fori_loop(0, index_bits, idx_step, jnp.zeros((1, LANES), jnp.int32))

        def demote(c, carry):
            r0 = chunk_start(c)
            x = sc_ref[pl.ds(r0, KEY_CHUNK), :]
            drop = (x == thr) & (r0 + row_iota > lim) & tie_lane
            sc_ref[pl.ds(r0, KEY_CHUNK), :] = jnp.where(drop, neg_inf, x)
            return carry

        lax.fori_loop(0, nch, demote, 0)

    def max_chunk(c, m):
        r0 = chunk_start(c)
        kc = kbd_ref[0, pl.ds(r0, KEY_CHUNK), :]
        sel = sc_ref[pl.ds(r0, KEY_CHUNK), :] >= thr
        parts = []
        for p in range(N_HEADS // 2):
            s = _nt_dot(kc, pair_q(qbs_ref, p))
            for hh in range(2):
                parts.append(_fold_rows(jnp.where(sel, s[:, hh * BLOCK:(hh + 1) * BLOCK], neg_inf), jnp.max))
        return jnp.maximum(m, jnp.concatenate(parts, axis=1))

    m8 = lax.fori_loop(0, nch, max_chunk, jnp.full((8, N_HEADS * BLOCK), neg_inf, jnp.float32))
    m = jnp.max(m8, axis=0, keepdims=True)

    acc_ref[...] = jnp.zeros(acc_ref.shape, jnp.float32)

    def attend_chunk(c, l):
        r0 = chunk_start(c)
        kc = kbd_ref[0, pl.ds(r0, KEY_CHUNK), :]
        vt = vbt_ref[0, c]
        sel = sc_ref[pl.ds(r0, KEY_CHUNK), :] >= thr
        sums = []
        s_next = _nt_dot(kc, pair_q(qbs_ref, 0))
        for p in range(N_HEADS // 2):
            s = s_next
            if p + 1 < N_HEADS // 2:
                s_next = _nt_dot(kc, pair_q(qbs_ref, p + 1))
            probs = []
            for hh in range(2):
                cols = slice((2 * p + hh) * BLOCK, (2 * p + hh + 1) * BLOCK)
                e = jnp.exp(jnp.where(sel, s[:, hh * BLOCK:(hh + 1) * BLOCK], neg_inf) - m[:, cols])
                sums.append(_fold_rows(e, jnp.sum))
                probs.append(e.astype(MXU_DTYPE))
            pcols = slice(p * 2 * BLOCK, (p + 1) * 2 * BLOCK)
            acc_ref[:, pcols] += jnp.dot(vt, jnp.concatenate(probs, axis=1), preferred_element_type=jnp.float32)
        return l + jnp.concatenate(sums, axis=1)

    l8 = lax.fori_loop(0, nch, attend_chunk, jnp.zeros((8, N_HEADS * BLOCK), jnp.float32))
    o_t = acc_ref[...] / jnp.sum(l8, axis=0, keepdims=True)
    for p in range(N_HEADS // 2):
        pair = jnp.concatenate([o_t[:, (2 * p) * BLOCK:(2 * p + 1) * BLOCK],
                                o_t[:, (2 * p + 1) * BLOCK:(2 * p + 2) * BLOCK]], axis=0)
        out_ref[0, :, p * LANES:(p + 1) * LANES] = pair.T.astype(out_ref.dtype)


def _sparse(qi, qb, wt, kid, kbd, vbt, batch, seq):
    topk = min(TOPK_MAX, seq // 4)
    nq = seq // BLOCK
    qblk = pl.BlockSpec((1, BLOCK, D_HEADS), lambda b, i: (b, i, 0))
    full = pl.BlockSpec((1, seq, LANES), lambda b, i: (b, 0, 0))
    kern = functools.partial(_sparse_kernel, topk=topk, index_bits=max(1, (seq - 1).bit_length()))
    return pl.pallas_call(
        kern,
        grid=(batch, nq),
        in_specs=[qblk, qblk,
                  pl.BlockSpec((1, IDX_HEADS, BLOCK), lambda b, i: (b, 0, i)),
                  full, full,
                  pl.BlockSpec((1, seq // KEY_CHUNK, HEAD_DIM, KEY_CHUNK), lambda b, i: (b, 0, 0, 0))],
        out_specs=qblk,
        out_shape=jax.ShapeDtypeStruct((batch, seq, D_HEADS), MXU_DTYPE),
        scratch_shapes=[pltpu.VMEM((seq, LANES), jnp.float32),
                        pltpu.VMEM((N_HEADS * BLOCK, LANES), MXU_DTYPE),
                        pltpu.VMEM((N_HEADS * BLOCK, LANES), MXU_DTYPE),
                        pltpu.VMEM((HEAD_DIM, N_HEADS * BLOCK), jnp.float32)],
        compiler_params=pltpu.CompilerParams(dimension_semantics=("arbitrary", "arbitrary"),
                                             vmem_limit_bytes=VMEM_LIMIT),
        name="indexer_sparse_attn",
    )(qi, qb, wt, kid, kbd, vbt)


def _merge_kernel(x_ref, g_ref, o1_ref, o2_ref, o3_ref, l1_ref, l2_ref, l3_ref, yb_ref,
                  wg_ref, wua_ref, wub_ref, wo_ref, x1_ref):
    x = x_ref[...]
    h = _rms(x, g_ref[...]).astype(MXU_DTYPE)
    l1, l2, l3 = l1_ref[...], l2_ref[...], l3_ref[...]
    m = jnp.maximum(jnp.maximum(l1, l2), l3)
    e1, e2, e3 = jnp.exp(l1 - m), jnp.exp(l2 - m), jnp.exp(l3 - m)
    ya = (e1 * o1_ref[...].astype(jnp.float32) + e2 * o2_ref[...].astype(jnp.float32)
          + e3 * o3_ref[...].astype(jnp.float32)) / (e1 + e2 + e3)
    ua = jnp.dot(ya.astype(MXU_DTYPE), wua_ref[...], preferred_element_type=jnp.float32)
    ub = jnp.dot(yb_ref[...], wub_ref[...], preferred_element_type=jnp.float32)
    ga = jnp.dot(h, wg_ref[:, :D_MODEL], preferred_element_type=jnp.float32)
    gb = jnp.dot(h, wg_ref[:, D_MODEL:], preferred_element_type=jnp.float32)
    merged = jax.nn.sigmoid(ga) * ua + jax.nn.sigmoid(gb) * ub
    x1_ref[...] = x + jnp.dot(merged.astype(MXU_DTYPE), wo_ref[...], preferred_element_type=jnp.float32)


def _merge(x2, g, os_, lses, yb, wg, wua, wub, wo):
    n = x2.shape[0]
    tm = ROW_TILE
    row = lambda i: (i, 0)
    const = lambda i: (0, 0)
    half = pl.BlockSpec((tm, D_HEADS), row)
    return pl.pallas_call(
        _merge_kernel,
        grid=(n // tm,),
        in_specs=[pl.BlockSpec((tm, D_MODEL), row), pl.BlockSpec((1, D_MODEL), const)] + [half] * 7 + [
            pl.BlockSpec((D_MODEL, 2 * D_MODEL), const),
            pl.BlockSpec((D_HEADS, D_MODEL), const),
            pl.BlockSpec((D_HEADS, D_MODEL), const),
            pl.BlockSpec((D_MODEL, D_MODEL), const)],
        out_specs=pl.BlockSpec((tm, D_MODEL), row),
        out_shape=jax.ShapeDtypeStruct((n, D_MODEL), jnp.float32),
        compiler_params=pltpu.CompilerParams(dimension_semantics=("arbitrary",), vmem_limit_bytes=VMEM_LIMIT),
        name="mix_gate_out",
    )(x2, g, *os_, *lses, yb, wg, wua, wub, wo)


def _ffn_kernel(x_ref, g_ref, wgate_ref, wup_ref, wdown_ref, gf_ref, out_ref):
    x = x_ref[...]
    h = _rms(x, g_ref[...]).astype(MXU_DTYPE)
    y = x
    for c0 in range(0, D_FF, FF_CHUNK):
        a = jnp.dot(h, wgate_ref[:, c0:c0 + FF_CHUNK], preferred_element_type=jnp.float32)
        u = jnp.dot(h, wup_ref[:, c0:c0 + FF_CHUNK], preferred_element_type=jnp.float32)
        act = (a * jax.nn.sigmoid(a) * u).astype(MXU_DTYPE)
        y = y + jnp.dot(act, wdown_ref[c0:c0 + FF_CHUNK, :], preferred_element_type=jnp.float32)
    out_ref[...] = _rms(y, gf_ref[...])


def _ffn(x1, g, wgate, wup, wdown, gf):
    n = x1.shape[0]
    tm = ROW_TILE
    row = lambda i: (i, 0)
    const = lambda i: (0, 0)
    return pl.pallas_call(
        _ffn_kernel,
        grid=(n // tm,),
        in_specs=[pl.BlockSpec((tm, D_MODEL), row), pl.BlockSpec((1, D_MODEL), const),
                  pl.BlockSpec((D_MODEL, D_FF), const), pl.BlockSpec((D_MODEL, D_FF), const),
                  pl.BlockSpec((D_FF, D_MODEL), const), pl.BlockSpec((1, D_MODEL), const)],
        out_specs=pl.BlockSpec((tm, D_MODEL), row),
        out_shape=jax.ShapeDtypeStruct((n, D_MODEL), jnp.float32),
        compiler_params=pltpu.CompilerParams(dimension_semantics=("arbitrary",), vmem_limit_bytes=VMEM_LIMIT),
        name="ffn_norm",
    )(x1, g, wgate, wup, wdown, gf)


def _rope_tables(seq):
    inv_freq = ROPE_THETA ** (-jnp.arange(HALF, dtype=jnp.float32) / HALF)
    ang = jnp.arange(seq, dtype=jnp.int32).astype(jnp.float32)[:, None] * inv_freq[None, :]
    cos, sin = jnp.cos(ang), jnp.sin(ang)
    return jnp.tile(cos, (1, 4)), jnp.concatenate([-sin, -sin, sin, sin], axis=1)


def kernel(x, norm_mix, w_in, w_up_a, w_up_b, w_out, norm_ffn, w_gate, w_up, w_down, norm_final):
    batch, seq, _ = x.shape
    assert seq % max(d * BLOCK for _, d in DILATED_PATTERNS) == 0 and seq % KEY_CHUNK == 0
    assert all(w // d == BLOCK for w, d in DILATED_PATTERNS)
    n = batch * seq
    bf = MXU_DTYPE
    xf = x.reshape(n, D_MODEL)
    cos_t, sin_t = _rope_tables(seq)
    for layer in range(w_in.shape[0]):
        w = w_in[layer]
        w_pack = jnp.pad(w[:, _packed_columns()], ((0, 0), (0, _P_TOTAL - _P_MISC - HEAD_DIM - IDX_HEADS))).astype(bf)
        w_gates = w[:, _GA:].astype(bf)
        qa, ka, va, qb, qi, kbd, kid, misc = _project(xf, norm_mix[layer][None], w_pack, cos_t, sin_t, seq)

        dil = [_dilated(qa, ka, va, batch, seq, d) for _, d in DILATED_PATTERNS]

        vb = misc[:, :HEAD_DIM].astype(bf).reshape(batch, seq // KEY_CHUNK, KEY_CHUNK, HEAD_DIM)
        vbt = jnp.swapaxes(vb, 2, 3)
        wt = jnp.swapaxes(misc[:, HEAD_DIM:HEAD_DIM + IDX_HEADS].reshape(batch, seq, IDX_HEADS), 1, 2)
        r3 = lambda z: z.reshape(batch, seq, z.shape[-1])
        yb = _sparse(r3(qi), r3(qb), wt, r3(kid), r3(kbd), vbt, batch, seq).reshape(n, D_HEADS)

        x1 = _merge(xf, norm_mix[layer][None], [o for o, _ in dil], [l for _, l in dil], yb,
                    w_gates, w_up_a[layer].astype(bf), w_up_b[layer].astype(bf), w_out[layer].astype(bf))
        last = layer == w_in.shape[0] - 1
        assert last, "the final norm is fused into the FFN kernel of the last layer"
        xf = _ffn(x1, norm_ffn[layer][None], w_gate[layer].astype(bf), w_up[layer].astype(bf),
                  w_down[layer].astype(bf), norm_final[None])
    return xf.reshape(batch, seq, D_MODEL)
```

```python
import functools

import numpy as np
import jax
import jax.numpy as jnp
from jax import lax
from jax.experimental import pallas as pl
from jax.experimental.pallas import tpu as pltpu

D_MODEL = 1024
HEAD_DIM = 64
HALF = HEAD_DIM // 2
N_HEADS = 8
D_HEADS = N_HEADS * HEAD_DIM
IDX_HEADS = 8
DILATED_PATTERNS = ((128, 1), (512, 4), (2048, 16))
RESIDUE_DILATIONS = tuple(d for _, d in DILATED_PATTERNS if d > 1)
TOPK_MAX = 256
D_FF = 2816
ROPE_THETA = 10000.0
RMS_EPS = 1e-6
BLOCK = 128
ATTN_SCALE = HEAD_DIM ** -0.5
IDX_SCALE = (HEAD_DIM ** -0.5) * (IDX_HEADS ** -0.5)

LANES = 128
INT_MIN = -(2 ** 31)
KEY_CHUNK = 1024
COUNT_CHUNK = 512
L_ROWS = 8
ROW_TILE = 512
FF_CHUNK = 1408
VMEM_LIMIT = 56 * 1024 * 1024
MXU_DTYPE = jnp.bfloat16

_SPLITS = (D_HEADS, D_HEADS, D_HEADS, D_HEADS, HEAD_DIM, HEAD_DIM, IDX_HEADS * HEAD_DIM, HEAD_DIM, IDX_HEADS,
           D_MODEL, D_MODEL)
_OFF = np.concatenate([[0], np.cumsum(_SPLITS)])
(_QA, _KA, _VA, _QB, _KB, _VB, _QI, _KI, _WI, _GA, _GB) = (int(o) for o in _OFF[:-1])

_P_QA, _P_KA, _P_VA, _P_QB, _P_QI = 0, 512, 1024, 1536, 2048
_P_KBD, _P_KID, _P_MISC = 2560, 2688, 2816
_P_TOTAL = 2944


def _pair_perm():
    idx = np.empty(D_HEADS, np.int64)
    for j in range(D_HEADS):
        g, l = divmod(j, LANES)
        quarter, e = divmod(l, HALF)
        head = 2 * g + (quarter % 2)
        idx[j] = head * HEAD_DIM + (quarter // 2) * HALF + e
    return idx


def _dup_perm():
    idx = np.empty(LANES, np.int64)
    for l in range(LANES):
        quarter, e = divmod(l, HALF)
        idx[l] = (quarter // 2) * HALF + e
    return idx


def _packed_columns():
    pp, dp = _pair_perm(), _dup_perm()
    return np.concatenate([
        _QA + pp, _KA + pp, _VA + np.arange(D_HEADS), _QB + pp, _QI + pp,
        _KB + dp, _KI + dp, _VB + np.arange(HEAD_DIM), _WI + np.arange(IDX_HEADS)])


def _rms(x, g):
    ms = jnp.mean(x * x, axis=-1, keepdims=True)
    return x * lax.rsqrt(ms + RMS_EPS) * g


def _nt_dot(a, b):
    return lax.dot_general(a, b, (((1,), (1,)), ((), ())), preferred_element_type=jnp.float32)


def _proj_kernel(x_ref, g_ref, w_ref, cos_ref, sin_ref, *refs):
    n_lay = 1 + len(RESIDUE_DILATIONS)
    qa_refs, ka_refs, va_refs = refs[:n_lay], refs[n_lay:2 * n_lay], refs[2 * n_lay:3 * n_lay]
    qb_ref, qi_ref, kbd_ref, kid_ref, misc_ref, h_ref, slab_ref = refs[3 * n_lay:]
    h_ref[...] = _rms(x_ref[...], g_ref[...]).astype(MXU_DTYPE)
    cos = cos_ref[...]
    sin = sin_ref[...]

    def mm(c0, width):
        return jnp.dot(h_ref[...], w_ref[:, c0:c0 + width], preferred_element_type=jnp.float32)

    def rope(z):
        parts = []
        for g in range(z.shape[1] // LANES):
            zg = z[:, g * LANES:(g + 1) * LANES]
            parts.append(zg * cos + pltpu.roll(zg, 2 * HALF, axis=1) * sin)
        return parts[0] if len(parts) == 1 else jnp.concatenate(parts, axis=1)

    def emit(y, out_refs):
        out_refs[0][...] = y.astype(MXU_DTYPE)
        for g in range(D_HEADS // LANES):
            slab_ref[g] = y[:, g * LANES:(g + 1) * LANES]
        for d, ref in zip(RESIDUE_DILATIONS, out_refs[1:]):
            rows = y.shape[0] // d
            for r in range(d):
                for g in range(D_HEADS // LANES):
                    ref[0, r, :, g * LANES:(g + 1) * LANES] = (
                        slab_ref[g, pl.ds(r, rows, stride=d), :].astype(MXU_DTYPE))

    emit(rope(mm(_P_QA, D_HEADS)) * ATTN_SCALE, qa_refs)
    emit(rope(mm(_P_KA, D_HEADS)), ka_refs)
    emit(mm(_P_VA, D_HEADS), va_refs)
    qb_ref[...] = (rope(mm(_P_QB, D_HEADS)) * ATTN_SCALE).astype(MXU_DTYPE)
    qi_ref[...] = rope(mm(_P_QI, D_HEADS)).astype(MXU_DTYPE)
    kbd_ref[...] = rope(mm(_P_KBD, LANES)).astype(MXU_DTYPE)
    kid_ref[...] = rope(mm(_P_KID, LANES)).astype(MXU_DTYPE)
    misc_ref[...] = mm(_P_MISC, LANES)


def _project(x2, g, w_pack, cos_t, sin_t, seq):
    n = x2.shape[0]
    tm = ROW_TILE
    tiles_per_seq = seq // tm
    row = lambda i: (i, 0)
    const = lambda i: (0, 0)
    pos = lambda i: (i % tiles_per_seq, 0)
    batch = n // seq
    wide = jax.ShapeDtypeStruct((n, D_HEADS), MXU_DTYPE)
    narrow = jax.ShapeDtypeStruct((n, LANES), MXU_DTYPE)
    wide_spec = pl.BlockSpec((tm, D_HEADS), row)
    lay_shapes = [wide] + [jax.ShapeDtypeStruct((batch, d, seq // d, D_HEADS), MXU_DTYPE) for d in RESIDUE_DILATIONS]
    lay_specs = [wide_spec] + [
        pl.BlockSpec((1, d, tm // d, D_HEADS), lambda i: (i // tiles_per_seq, 0, i % tiles_per_seq, 0))
        for d in RESIDUE_DILATIONS]
    n_lay = len(lay_shapes)
    outs = pl.pallas_call(
        _proj_kernel,
        grid=(n // tm,),
        in_specs=[
            pl.BlockSpec((tm, D_MODEL), row),
            pl.BlockSpec((1, D_MODEL), const),
            pl.BlockSpec((D_MODEL, _P_TOTAL), const),
            pl.BlockSpec((tm, LANES), pos),
            pl.BlockSpec((tm, LANES), pos),
        ],
        out_specs=lay_specs * 3 + [wide_spec] * 2 + [pl.BlockSpec((tm, LANES), row)] * 3,
        out_shape=lay_shapes * 3 + [wide] * 2 + [narrow, narrow, jax.ShapeDtypeStruct((n, LANES), jnp.float32)],
        scratch_shapes=[pltpu.VMEM((tm, D_MODEL), MXU_DTYPE),
                        pltpu.VMEM((D_HEADS // LANES, tm, LANES), jnp.float32)],
        compiler_params=pltpu.CompilerParams(dimension_semantics=("arbitrary",), vmem_limit_bytes=VMEM_LIMIT),
        name="proj_rope",
    )(x2, g, w_pack, cos_t, sin_t)
    return (outs[:n_lay], outs[n_lay:2 * n_lay], outs[2 * n_lay:3 * n_lay]) + tuple(outs[3 * n_lay:])


def _dil_kernel(q_ref, k_ref, v_ref, kp_ref, vp_ref, o_ref, lse_ref):
    n = pl.program_id(2)
    qi = lax.broadcasted_iota(jnp.int32, (BLOCK, 2 * BLOCK), 0)
    kj = lax.broadcasted_iota(jnp.int32, (BLOCK, 2 * BLOCK), 1)
    band = (kj >= qi) & (kj <= qi + BLOCK) & ((kj >= BLOCK) | (n > 0))
    lane = lax.broadcasted_iota(jnp.int32, (1, LANES), 1)
    first_v = lane < HEAD_DIM
    for p in range(N_HEADS // 2):
        cols = slice(p * LANES, (p + 1) * LANES)
        qp = q_ref[0, 0, :, cols]
        k2 = jnp.concatenate([kp_ref[0, 0, :, cols], k_ref[0, 0, :, cols]], axis=0)
        v2 = jnp.concatenate([vp_ref[0, 0, :, cols], v_ref[0, 0, :, cols]], axis=0)
        outs, lses = [], []
        for hh in range(2):
            head_lanes = ((lane // HALF) % 2) == hh
            qm = jnp.where(head_lanes, qp, jnp.zeros_like(qp))
            s = jnp.where(band, _nt_dot(qm, k2), -jnp.inf)
            m = jnp.max(s, axis=-1, keepdims=True)
            e = jnp.exp(s - m)
            den = jnp.sum(e, axis=-1, keepdims=True)
            o = jnp.dot(e.astype(MXU_DTYPE), v2, preferred_element_type=jnp.float32) / den
            outs.append(o)
            lses.append(m + jnp.log(den))
        o_ref[0, 0, :, cols] = jnp.where(first_v, outs[0], outs[1]).astype(o_ref.dtype)
        lse_ref[0, 0, :, cols] = jnp.where(first_v, lses[0], lses[1])


def _dilated(q, k, v, dilation):
    batch, _, m_len, _ = q.shape
    nb = m_len // BLOCK
    cur = lambda b, r, n: (b, r, n, 0)
    prev = lambda b, r, n: (b, r, jnp.maximum(n - 1, 0), 0)
    blk = (1, 1, BLOCK, D_HEADS)
    return pl.pallas_call(
        _dil_kernel,
        grid=(batch, dilation, nb),
        in_specs=[pl.BlockSpec(blk, cur), pl.BlockSpec(blk, cur), pl.BlockSpec(blk, cur),
                  pl.BlockSpec(blk, prev), pl.BlockSpec(blk, prev)],
        out_specs=[pl.BlockSpec(blk, cur), pl.BlockSpec(blk, cur)],
        out_shape=[jax.ShapeDtypeStruct(q.shape, MXU_DTYPE), jax.ShapeDtypeStruct(q.shape, jnp.float32)],
        compiler_params=pltpu.CompilerParams(dimension_semantics=("arbitrary",) * 3),
        name=f"dilated_d{dilation}",
    )(q, k, v, k, v)


def _key_to_f32(key):
    bits = key ^ ((key >> 31) & jnp.int32(0x7FFFFFFF))
    return lax.bitcast_convert_type(bits, jnp.float32)


def _fold_rows(x, op):
    rows = x.shape[0]
    y = op(x.reshape(rows // 64, 64, LANES), axis=0)
    return op(y.reshape(8, 8, LANES), axis=0)


def _sparse_kernel(qi_ref, qb_ref, wt_ref, kid_ref, kbd_ref, vbt_ref, out_ref,
                   sc_ref, qis_ref, qbs_ref, acc_ref, *, topk, index_bits):
    i = pl.program_id(1)
    nch = i // (KEY_CHUNK // BLOCK) + 1
    ncc = i // (COUNT_CHUNK // BLOCK) + 1
    lane = lax.broadcasted_iota(jnp.int32, (1, LANES), 1)
    t_idx = i * BLOCK + lane
    row_iota = lax.broadcasted_iota(jnp.int32, (KEY_CHUNK, LANES), 0)
    crow_iota = lax.broadcasted_iota(jnp.int32, (COUNT_CHUNK, LANES), 0)
    neg_inf = jnp.float32(-jnp.inf)
    f32_lowest = jnp.float32(jnp.finfo(jnp.float32).min)

    for h in range(N_HEADS):
        cols = slice((h // 2) * LANES, (h // 2 + 1) * LANES)
        head_lanes = ((lane // HALF) % 2) == (h % 2)
        rows = slice(h * BLOCK, (h + 1) * BLOCK)
        qis_ref[rows, :] = jnp.where(head_lanes, qi_ref[0, :, cols], jnp.zeros((), MXU_DTYPE))
        qbs_ref[rows, :] = jnp.where(head_lanes, qb_ref[0, :, cols], jnp.zeros((), MXU_DTYPE))

    def chunk_start(c):
        return pl.multiple_of(c * KEY_CHUNK, KEY_CHUNK)

    def pair_q(ref, p):
        return ref[p * 2 * BLOCK:(p + 1) * 2 * BLOCK, :]

    def score_chunk(c, carry):
        r0 = chunk_start(c)
        kc = kid_ref[0, pl.ds(r0, KEY_CHUNK), :]
        acc = jnp.zeros((KEY_CHUNK, LANES), jnp.float32)
        for p in range(N_HEADS // 2):
            d = _nt_dot(kc, pair_q(qis_ref, p))
            for hh in range(2):
                h = 2 * p + hh
                acc = acc + jnp.maximum(d[:, hh * BLOCK:(hh + 1) * BLOCK], 0.0) * wt_ref[0, h:h + 1, :]
        sc_ref[pl.ds(r0, KEY_CHUNK), :] = jnp.where(r0 + row_iota <= t_idx, acc * IDX_SCALE, neg_inf)
        return carry

    lax.fori_loop(0, nch, score_chunk, 0)

    def count(pred):
        def body(c, acc):
            r0 = pl.multiple_of(c * COUNT_CHUNK, COUNT_CHUNK)
            hit = jnp.where(pred(sc_ref[pl.ds(r0, COUNT_CHUNK), :], r0), 1.0, 0.0)
            return acc + jnp.sum(hit.reshape(8, COUNT_CHUNK // 8, LANES), axis=0)
        acc = lax.fori_loop(0, ncc, body, jnp.zeros((COUNT_CHUNK // 8, LANES), jnp.float32))
        return jnp.sum(acc, axis=0, keepdims=True)

    def bit_step(b, carry):
        res, cnt_res = carry
        cand = res | (jnp.int32(1) << (31 - b))
        thr_c = _key_to_f32(cand ^ jnp.int32(INT_MIN))
        cnt = count(lambda x, r0: x >= thr_c)
        take = cnt >= topk
        return jnp.where(take, cand, res), jnp.where(take, cnt, cnt_res)

    res, cnt_ge = lax.fori_loop(0, 32, bit_step, (jnp.zeros((1, LANES), jnp.int32),
                                                  jnp.zeros((1, LANES), jnp.float32)))
    thr_raw = _key_to_f32(res ^ jnp.int32(INT_MIN))
    enough = thr_raw > f32_lowest
    thr = jnp.where(enough, thr_raw, f32_lowest)

    tie_lane = enough & (cnt_ge > topk)

    @pl.when(jnp.max(jnp.where(tie_lane, 1.0, 0.0)) > 0.0)
    def _():
        need = topk - count(lambda x, r0: x > thr)

        def idx_step(b, lim):
            cand = lim | (jnp.int32(1) << (index_bits - 1 - b))
            below = count(lambda x, r0: (x == thr) & (r0 + crow_iota < cand))
            return jnp.where(below < need, cand, lim)

        lim = lax.fori_loop(0, index_bits, idx_step, jnp.zeros((1, LANES), jnp.int32))

        def demote(c, carry):
            r0 = pl.multiple_of(c * COUNT_CHUNK, COUNT_CHUNK)
            x = sc_ref[pl.ds(r0, COUNT_CHUNK), :]
            drop = (x == thr) & (r0 + crow_iota > lim) & tie_lane
            sc_ref[pl.ds(r0, COUNT_CHUNK), :] = jnp.where(drop, neg_inf, x)
            return carry

        lax.fori_loop(0, ncc, demote, 0)

    def select_bias(r0):
        return jnp.where(sc_ref[pl.ds(r0, KEY_CHUNK), :] >= thr, 0.0, neg_inf)

    def max_chunk(c, m):
        r0 = chunk_start(c)
        kc = kbd_ref[0, pl.ds(r0, KEY_CHUNK), :]
        bias = select_bias(r0)
        parts = []
        for p in range(N_HEADS // 2):
            s = _nt_dot(kc, pair_q(qbs_ref, p))
            for hh in range(2):
                parts.append(_fold_rows(s[:, hh * BLOCK:(hh + 1) * BLOCK] + bias, jnp.max))
        return jnp.maximum(m, jnp.concatenate(parts, axis=1))

    m8 = lax.fori_loop(0, nch, max_chunk, jnp.full((8, N_HEADS * BLOCK), neg_inf, jnp.float32))
    m = jnp.max(m8, axis=0, keepdims=True)

    acc_ref[...] = jnp.zeros(acc_ref.shape, jnp.float32)

    def attend_chunk(c, carry):
        r0 = chunk_start(c)
        kc = kbd_ref[0, pl.ds(r0, KEY_CHUNK), :]
        vt = vbt_ref[0, c]
        bias = select_bias(r0)
        s_next = _nt_dot(kc, pair_q(qbs_ref, 0))
        for p in range(N_HEADS // 2):
            s = s_next
            if p + 1 < N_HEADS // 2:
                s_next = _nt_dot(kc, pair_q(qbs_ref, p + 1))
            probs = []
            for hh in range(2):
                cols = slice((2 * p + hh) * BLOCK, (2 * p + hh + 1) * BLOCK)
                e = jnp.exp(s[:, hh * BLOCK:(hh + 1) * BLOCK] + bias - m[:, cols])
                probs.append(e.astype(MXU_DTYPE))
            pcols = slice(p * 2 * BLOCK, (p + 1) * 2 * BLOCK)
            acc_ref[:, pcols] += jnp.dot(vt, jnp.concatenate(probs, axis=1), preferred_element_type=jnp.float32)
        return carry

    lax.fori_loop(0, nch, attend_chunk, 0)
    o_t = acc_ref[:HEAD_DIM, :] / acc_ref[HEAD_DIM:HEAD_DIM + 1, :]
    for p in range(N_HEADS // 2):
        pair = jnp.concatenate([o_t[:, (2 * p) * BLOCK:(2 * p + 1) * BLOCK],
                                o_t[:, (2 * p + 1) * BLOCK:(2 * p + 2) * BLOCK]], axis=0)
        out_ref[0, :, p * LANES:(p + 1) * LANES] = pair.T.astype(out_ref.dtype)


def _sparse(qi, qb, wt, kid, kbd, vbt, batch, seq):
    topk = min(TOPK_MAX, seq // 4)
    nq = seq // BLOCK
    qblk = pl.BlockSpec((1, BLOCK, D_HEADS), lambda b, i: (b, i, 0))
    full = pl.BlockSpec((1, seq, LANES), lambda b, i: (b, 0, 0))
    kern = functools.partial(_sparse_kernel, topk=topk, index_bits=max(1, (seq - 1).bit_length()))
    return pl.pallas_call(
        kern,
        grid=(batch, nq),
        in_specs=[qblk, qblk,
                  pl.BlockSpec((1, IDX_HEADS, BLOCK), lambda b, i: (b, 0, i)),
                  full, full,
                  pl.BlockSpec((1, seq // KEY_CHUNK, HEAD_DIM + L_ROWS, KEY_CHUNK), lambda b, i: (b, 0, 0, 0))],
        out_specs=qblk,
        out_shape=jax.ShapeDtypeStruct((batch, seq, D_HEADS), MXU_DTYPE),
        scratch_shapes=[pltpu.VMEM((seq, LANES), jnp.float32),
                        pltpu.VMEM((N_HEADS * BLOCK, LANES), MXU_DTYPE),
                        pltpu.VMEM((N_HEADS * BLOCK, LANES), MXU_DTYPE),
                        pltpu.VMEM((HEAD_DIM + L_ROWS, N_HEADS * BLOCK), jnp.float32)],
        compiler_params=pltpu.CompilerParams(dimension_semantics=("arbitrary", "arbitrary"),
                                             vmem_limit_bytes=VMEM_LIMIT),
        name="indexer_sparse_attn",
    )(qi, qb, wt, kid, kbd, vbt)


def _merge_kernel(x_ref, g_ref, *refs):
    n_pat = len(DILATED_PATTERNS)
    o_refs, l_refs = refs[:n_pat], refs[n_pat:2 * n_pat]
    yb_ref, wg_ref, wua_ref, wub_ref, wo_ref, x1_ref = refs[2 * n_pat:2 * n_pat + 6]
    slabs = refs[2 * n_pat + 6:]
    x = x_ref[...]
    h = _rms(x, g_ref[...]).astype(MXU_DTYPE)
    tm = x.shape[0]
    n_grp = D_HEADS // LANES

    o_src, l_src, k = [], [], 0
    for (_, d), o_ref, l_ref in zip(DILATED_PATTERNS, o_refs, l_refs):
        if d == 1:
            o_src.append(lambda g, r=o_ref: r[:, g * LANES:(g + 1) * LANES].astype(jnp.float32))
            l_src.append(lambda g, r=l_ref: r[:, g * LANES:(g + 1) * LANES])
            continue
        o_slab, l_slab = slabs[2 * k], slabs[2 * k + 1]
        k += 1
        for r in range(d):
            for g in range(n_grp):
                cols = slice(g * LANES, (g + 1) * LANES)
                o_slab[g, pl.ds(r, tm // d, stride=d), :] = o_ref[0, r, :, cols].astype(jnp.float32)
                l_slab[g, pl.ds(r, tm // d, stride=d), :] = l_ref[0, r, :, cols]
        o_src.append(lambda g, s=o_slab: s[g])
        l_src.append(lambda g, s=l_slab: s[g])

    parts = []
    for g in range(n_grp):
        ls = [f(g) for f in l_src]
        m = functools.reduce(jnp.maximum, ls)
        es = [jnp.exp(l - m) for l in ls]
        num = functools.reduce(jnp.add, [e * f(g) for e, f in zip(es, o_src)])
        parts.append((num / functools.reduce(jnp.add, es)).astype(MXU_DTYPE))
    ya = jnp.concatenate(parts, axis=1)
    ua = jnp.dot(ya, wua_ref[...], preferred_element_type=jnp.float32)
    ub = jnp.dot(yb_ref[...], wub_ref[...], preferred_element_type=jnp.float32)
    ga = jnp.dot(h, wg_ref[:, :D_MODEL], preferred_element_type=jnp.float32)
    gb = jnp.dot(h, wg_ref[:, D_MODEL:], preferred_element_type=jnp.float32)
    merged = jax.nn.sigmoid(ga) * ua + jax.nn.sigmoid(gb) * ub
    x1_ref[...] = x + jnp.dot(merged.astype(MXU_DTYPE), wo_ref[...], preferred_element_type=jnp.float32)


def _merge(x2, g, os_, lses, yb, wg, wua, wub, wo, seq):
    n = x2.shape[0]
    tm = ROW_TILE
    tiles_per_seq = seq // tm
    row = lambda i: (i, 0)
    const = lambda i: (0, 0)
    half = pl.BlockSpec((tm, D_HEADS), row)
    pat_specs = [half if d == 1 else
                 pl.BlockSpec((1, d, tm // d, D_HEADS), lambda i: (i // tiles_per_seq, 0, i % tiles_per_seq, 0))
                 for _, d in DILATED_PATTERNS]
    return pl.pallas_call(
        _merge_kernel,
        grid=(n // tm,),
        in_specs=[pl.BlockSpec((tm, D_MODEL), row), pl.BlockSpec((1, D_MODEL), const)] + pat_specs * 2 + [half] + [
            pl.BlockSpec((D_MODEL, 2 * D_MODEL), const),
            pl.BlockSpec((D_HEADS, D_MODEL), const),
            pl.BlockSpec((D_HEADS, D_MODEL), const),
            pl.BlockSpec((D_MODEL, D_MODEL), const)],
        out_specs=pl.BlockSpec((tm, D_MODEL), row),
        out_shape=jax.ShapeDtypeStruct((n, D_MODEL), jnp.float32),
        scratch_shapes=[pltpu.VMEM((D_HEADS // LANES, tm, LANES), jnp.float32)] * (2 * len(RESIDUE_DILATIONS)),
        compiler_params=pltpu.CompilerParams(dimension_semantics=("arbitrary",), vmem_limit_bytes=VMEM_LIMIT),
        name="mix_gate_out",
    )(x2, g, *os_, *lses, yb, wg, wua, wub, wo)


def _ffn_kernel(x_ref, g_ref, wgate_ref, wup_ref, wdown_ref, gf_ref, out_ref):
    x = x_ref[...]
    h = _rms(x, g_ref[...]).astype(MXU_DTYPE)
    y = x
    for c0 in range(0, D_FF, FF_CHUNK):
        a = jnp.dot(h, wgate_ref[:, c0:c0 + FF_CHUNK], preferred_element_type=jnp.float32)
        u = jnp.dot(h, wup_ref[:, c0:c0 + FF_CHUNK], preferred_element_type=jnp.float32)
        act = (a * jax.nn.sigmoid(a) * u).astype(MXU_DTYPE)
        y = y + jnp.dot(act, wdown_ref[c0:c0 + FF_CHUNK, :], preferred_element_type=jnp.float32)
    out_ref[...] = _rms(y, gf_ref[...])


def _ffn(x1, g, wgate, wup, wdown, gf):
    n = x1.shape[0]
    tm = ROW_TILE
    row = lambda i: (i, 0)
    const = lambda i: (0, 0)
    return pl.pallas_call(
        _ffn_kernel,
        grid=(n // tm,),
        in_specs=[pl.BlockSpec((tm, D_MODEL), row), pl.BlockSpec((1, D_MODEL), const),
                  pl.BlockSpec((D_MODEL, D_FF), const), pl.BlockSpec((D_MODEL, D_FF), const),
                  pl.BlockSpec((D_FF, D_MODEL), const), pl.BlockSpec((1, D_MODEL), const)],
        out_specs=pl.BlockSpec((tm, D_MODEL), row),
        out_shape=jax.ShapeDtypeStruct((n, D_MODEL), jnp.float32),
        compiler_params=pltpu.CompilerParams(dimension_semantics=("arbitrary",), vmem_limit_bytes=VMEM_LIMIT),
        name="ffn_norm",
    )(x1, g, wgate, wup, wdown, gf)


def _rope_tables(seq):
    inv_freq = ROPE_THETA ** (-jnp.arange(HALF, dtype=jnp.float32) / HALF)
    ang = jnp.arange(seq, dtype=jnp.int32).astype(jnp.float32)[:, None] * inv_freq[None, :]
    cos, sin = jnp.cos(ang), jnp.sin(ang)
    return jnp.tile(cos, (1, 4)), jnp.concatenate([-sin, -sin, sin, sin], axis=1)


def kernel(x, norm_mix, w_in, w_up_a, w_up_b, w_out, norm_ffn, w_gate, w_up, w_down, norm_final):
    batch, seq, _ = x.shape
    assert seq % max(d * BLOCK for _, d in DILATED_PATTERNS) == 0 and seq % KEY_CHUNK == 0
    assert all(w // d == BLOCK for w, d in DILATED_PATTERNS)
    n = batch * seq
    bf = MXU_DTYPE
    xf = x.reshape(n, D_MODEL)
    cos_t, sin_t = _rope_tables(seq)
    for layer in range(w_in.shape[0]):
        w = w_in[layer]
        w_pack = jnp.pad(w[:, _packed_columns()], ((0, 0), (0, _P_TOTAL - _P_MISC - HEAD_DIM - IDX_HEADS))).astype(bf)
        w_gates = w[:, _GA:].astype(bf)
        qas, kas, vas, qb, qi, kbd, kid, misc = _project(xf, norm_mix[layer][None], w_pack, cos_t, sin_t, seq)

        dil = []
        for (_, d), q, k, v in zip(DILATED_PATTERNS, qas, kas, vas):
            if d == 1:
                o, lse = _dilated(*(z.reshape(batch, 1, seq, D_HEADS) for z in (q, k, v)), d)
                dil.append((o.reshape(n, D_HEADS), lse.reshape(n, D_HEADS)))
            else:
                dil.append(_dilated(q, k, v, d))

        ones_col = (jnp.arange(L_ROWS) == 0).astype(bf)[None, :]
        vb = jnp.concatenate([misc[:, :HEAD_DIM].astype(bf), jnp.broadcast_to(ones_col, (n, L_ROWS))], axis=1)
        vbt = jnp.swapaxes(vb.reshape(batch, seq // KEY_CHUNK, KEY_CHUNK, HEAD_DIM + L_ROWS), 2, 3)
        wt = jnp.swapaxes(misc[:, HEAD_DIM:HEAD_DIM + IDX_HEADS].reshape(batch, seq, IDX_HEADS), 1, 2)
        r3 = lambda z: z.reshape(batch, seq, z.shape[-1])
        yb = _sparse(r3(qi), r3(qb), wt, r3(kid), r3(kbd), vbt, batch, seq).reshape(n, D_HEADS)

        x1 = _merge(xf, norm_mix[layer][None], [o for o, _ in dil], [l for _, l in dil], yb,
                    w_gates, w_up_a[layer].astype(bf), w_up_b[layer].astype(bf), w_out[layer].astype(bf), seq)
        last = layer == w_in.shape[0] - 1
        assert last, "the final norm is fused into the FFN kernel of the last layer"
        xf = _ffn(x1, norm_ffn[layer][None], w_gate[layer].astype(bf), w_up[layer].astype(bf),
                  w_down[layer].astype(bf), norm_final[None])
    return xf.reshape(batch, seq, D_MODEL)
```

```python
import functools

import numpy as np
import jax
import jax.numpy as jnp
from jax import lax
from jax.experimental import pallas as pl
from jax.experimental.pallas import tpu as pltpu

D_MODEL = 1024
HEAD_DIM = 64
HALF = HEAD_DIM // 2
N_HEADS = 8
D_HEADS = N_HEADS * HEAD_DIM
IDX_HEADS = 8
DILATED_PATTERNS = ((128, 1), (512, 4), (2048, 16))
RESIDUE_DILATIONS = tuple(d for _, d in DILATED_PATTERNS if d > 1)
TOPK_MAX = 256
D_FF = 2816
ROPE_THETA = 10000.0
RMS_EPS = 1e-6
BLOCK = 128
ATTN_SCALE = HEAD_DIM ** -0.5
IDX_SCALE = (HEAD_DIM ** -0.5) * (IDX_HEADS ** -0.5)

LANES = 128
INT_MIN = -(2 ** 31)
KEY_CHUNK = 1024
COUNT_CHUNK = 512
UNCONDITIONAL_BITS = 20
L_ROWS = 8
ROW_TILE = 512
FF_CHUNK = 1408
VMEM_LIMIT = 56 * 1024 * 1024
MXU_DTYPE = jnp.bfloat16

_SPLITS = (D_HEADS, D_HEADS, D_HEADS, D_HEADS, HEAD_DIM, HEAD_DIM, IDX_HEADS * HEAD_DIM, HEAD_DIM, IDX_HEADS,
           D_MODEL, D_MODEL)
_OFF = np.concatenate([[0], np.cumsum(_SPLITS)])
(_QA, _KA, _VA, _QB, _KB, _VB, _QI, _KI, _WI, _GA, _GB) = (int(o) for o in _OFF[:-1])

_P_QA, _P_KA, _P_VA, _P_QB, _P_QI = 0, 512, 1024, 1536, 2048
_P_KBD, _P_KID, _P_MISC = 2560, 2688, 2816
_P_TOTAL = 2944


def _pair_perm():
    idx = np.empty(D_HEADS, np.int64)
    for j in range(D_HEADS):
        g, l = divmod(j, LANES)
        quarter, e = divmod(l, HALF)
        head = 2 * g + (quarter % 2)
        idx[j] = head * HEAD_DIM + (quarter // 2) * HALF + e
    return idx


def _dup_perm():
    idx = np.empty(LANES, np.int64)
    for l in range(LANES):
        quarter, e = divmod(l, HALF)
        idx[l] = (quarter // 2) * HALF + e
    return idx


def _packed_columns():
    pp, dp = _pair_perm(), _dup_perm()
    return np.concatenate([
        _QA + pp, _KA + pp, _VA + np.arange(D_HEADS), _QB + pp, _QI + pp,
        _KB + dp, _KI + dp, _VB + np.arange(HEAD_DIM), _WI + np.arange(IDX_HEADS)])


def _rms(x, g):
    ms = jnp.mean(x * x, axis=-1, keepdims=True)
    return x * lax.rsqrt(ms + RMS_EPS) * g


def _nt_dot(a, b):
    return lax.dot_general(a, b, (((1,), (1,)), ((), ())), preferred_element_type=jnp.float32)


def _proj_kernel(x_ref, g_ref, w_ref, cos_ref, sin_ref, *refs):
    n_lay = 1 + len(RESIDUE_DILATIONS)
    qa_refs, ka_refs, va_refs = refs[:n_lay], refs[n_lay:2 * n_lay], refs[2 * n_lay:3 * n_lay]
    qb_ref, qi_ref, kbd_ref, kid_ref, misc_ref, h_ref, slab_ref = refs[3 * n_lay:]
    h_ref[...] = _rms(x_ref[...], g_ref[...]).astype(MXU_DTYPE)
    cos = cos_ref[...]
    sin = sin_ref[...]

    def mm(c0, width):
        return jnp.dot(h_ref[...], w_ref[:, c0:c0 + width], preferred_element_type=jnp.float32)

    def rope(z):
        parts = []
        for g in range(z.shape[1] // LANES):
            zg = z[:, g * LANES:(g + 1) * LANES]
            parts.append(zg * cos + pltpu.roll(zg, 2 * HALF, axis=1) * sin)
        return parts[0] if len(parts) == 1 else jnp.concatenate(parts, axis=1)

    def emit(y, out_refs):
        out_refs[0][...] = y.astype(MXU_DTYPE)
        for g in range(D_HEADS // LANES):
            slab_ref[g] = y[:, g * LANES:(g + 1) * LANES]
        for d, ref in zip(RESIDUE_DILATIONS, out_refs[1:]):
            rows = y.shape[0] // d
            for r in range(d):
                for g in range(D_HEADS // LANES):
                    ref[0, r, :, g * LANES:(g + 1) * LANES] = (
                        slab_ref[g, pl.ds(r, rows, stride=d), :].astype(MXU_DTYPE))

    emit(rope(mm(_P_QA, D_HEADS)) * ATTN_SCALE, qa_refs)
    emit(rope(mm(_P_KA, D_HEADS)), ka_refs)
    emit(mm(_P_VA, D_HEADS), va_refs)
    qb_ref[...] = (rope(mm(_P_QB, D_HEADS)) * ATTN_SCALE).astype(MXU_DTYPE)
    qi_ref[...] = rope(mm(_P_QI, D_HEADS)).astype(MXU_DTYPE)
    kbd_ref[...] = rope(mm(_P_KBD, LANES)).astype(MXU_DTYPE)
    kid_ref[...] = rope(mm(_P_KID, LANES)).astype(MXU_DTYPE)
    misc_ref[...] = mm(_P_MISC, LANES)


def _project(x2, g, w_pack, cos_t, sin_t, seq):
    n = x2.shape[0]
    tm = ROW_TILE
    tiles_per_seq = seq // tm
    row = lambda i: (i, 0)
    const = lambda i: (0, 0)
    pos = lambda i: (i % tiles_per_seq, 0)
    batch = n // seq
    wide = jax.ShapeDtypeStruct((n, D_HEADS), MXU_DTYPE)
    narrow = jax.ShapeDtypeStruct((n, LANES), MXU_DTYPE)
    wide_spec = pl.BlockSpec((tm, D_HEADS), row)
    lay_shapes = [wide] + [jax.ShapeDtypeStruct((batch, d, seq // d, D_HEADS), MXU_DTYPE) for d in RESIDUE_DILATIONS]
    lay_specs = [wide_spec] + [
        pl.BlockSpec((1, d, tm // d, D_HEADS), lambda i: (i // tiles_per_seq, 0, i % tiles_per_seq, 0))
        for d in RESIDUE_DILATIONS]
    n_lay = len(lay_shapes)
    outs = pl.pallas_call(
        _proj_kernel,
        grid=(n // tm,),
        in_specs=[
            pl.BlockSpec((tm, D_MODEL), row),
            pl.BlockSpec((1, D_MODEL), const),
            pl.BlockSpec((D_MODEL, _P_TOTAL), const),
            pl.BlockSpec((tm, LANES), pos),
            pl.BlockSpec((tm, LANES), pos),
        ],
        out_specs=lay_specs * 3 + [wide_spec] * 2 + [pl.BlockSpec((tm, LANES), row)] * 3,
        out_shape=lay_shapes * 3 + [wide] * 2 + [narrow, narrow, jax.ShapeDtypeStruct((n, LANES), jnp.float32)],
        scratch_shapes=[pltpu.VMEM((tm, D_MODEL), MXU_DTYPE),
                        pltpu.VMEM((D_HEADS // LANES, tm, LANES), jnp.float32)],
        compiler_params=pltpu.CompilerParams(dimension_semantics=("arbitrary",), vmem_limit_bytes=VMEM_LIMIT),
        name="proj_rope",
    )(x2, g, w_pack, cos_t, sin_t)
    return (outs[:n_lay], outs[n_lay:2 * n_lay], outs[2 * n_lay:3 * n_lay]) + tuple(outs[3 * n_lay:])


def _dil_kernel(q_ref, k_ref, v_ref, kp_ref, vp_ref, o_ref, lse_ref):
    n = pl.program_id(2)
    qi = lax.broadcasted_iota(jnp.int32, (BLOCK, 2 * BLOCK), 0)
    kj = lax.broadcasted_iota(jnp.int32, (BLOCK, 2 * BLOCK), 1)
    band = (kj >= qi) & (kj <= qi + BLOCK) & ((kj >= BLOCK) | (n > 0))
    lane = lax.broadcasted_iota(jnp.int32, (1, LANES), 1)
    first_v = lane < HEAD_DIM
    for p in range(N_HEADS // 2):
        cols = slice(p * LANES, (p + 1) * LANES)
        qp = q_ref[0, 0, :, cols]
        k2 = jnp.concatenate([kp_ref[0, 0, :, cols], k_ref[0, 0, :, cols]], axis=0)
        v2 = jnp.concatenate([vp_ref[0, 0, :, cols], v_ref[0, 0, :, cols]], axis=0)
        outs, lses = [], []
        for hh in range(2):
            head_lanes = ((lane // HALF) % 2) == hh
            qm = jnp.where(head_lanes, qp, jnp.zeros_like(qp))
            s = jnp.where(band, _nt_dot(qm, k2), -jnp.inf)
            m = jnp.max(s, axis=-1, keepdims=True)
            e = jnp.exp(s - m)
            den = jnp.sum(e, axis=-1, keepdims=True)
            o = jnp.dot(e.astype(MXU_DTYPE), v2, preferred_element_type=jnp.float32) / den
            outs.append(o)
            lses.append(m + jnp.log(den))
        o_ref[0, 0, :, cols] = jnp.where(first_v, outs[0], outs[1]).astype(o_ref.dtype)
        lse_ref[0, 0, :, cols] = jnp.where(first_v, lses[0], lses[1])


def _dilated(q, k, v, dilation):
    batch, _, m_len, _ = q.shape
    nb = m_len // BLOCK
    cur = lambda b, r, n: (b, r, n, 0)
    prev = lambda b, r, n: (b, r, jnp.maximum(n - 1, 0), 0)
    blk = (1, 1, BLOCK, D_HEADS)
    return pl.pallas_call(
        _dil_kernel,
        grid=(batch, dilation, nb),
        in_specs=[pl.BlockSpec(blk, cur), pl.BlockSpec(blk, cur), pl.BlockSpec(blk, cur),
                  pl.BlockSpec(blk, prev), pl.BlockSpec(blk, prev)],
        out_specs=[pl.BlockSpec(blk, cur), pl.BlockSpec(blk, cur)],
        out_shape=[jax.ShapeDtypeStruct(q.shape, MXU_DTYPE), jax.ShapeDtypeStruct(q.shape, jnp.float32)],
        compiler_params=pltpu.CompilerParams(dimension_semantics=("arbitrary",) * 3),
        name=f"dilated_d{dilation}",
    )(q, k, v, k, v)


def _key_to_f32(key):
    bits = key ^ ((key >> 31) & jnp.int32(0x7FFFFFFF))
    return lax.bitcast_convert_type(bits, jnp.float32)


def _fold_rows(x, op):
    rows = x.shape[0]
    y = op(x.reshape(rows // 64, 64, LANES), axis=0)
    return op(y.reshape(8, 8, LANES), axis=0)


def _sparse_kernel(qi_ref, qb_ref, wt_ref, kid_ref, kbd_ref, vbt_ref, out_ref,
                   sc_ref, qis_ref, qbs_ref, acc_ref, m_ref, sqa_ref, sqb_ref, mxa_ref, mxb_ref,
                   *, topk, index_bits):
    i = pl.program_id(1)
    nch = i // (KEY_CHUNK // BLOCK) + 1
    ncc = i // (COUNT_CHUNK // BLOCK) + 1
    lane = lax.broadcasted_iota(jnp.int32, (1, LANES), 1)
    t_idx = i * BLOCK + lane
    row_iota = lax.broadcasted_iota(jnp.int32, (KEY_CHUNK, LANES), 0)
    crow_iota = lax.broadcasted_iota(jnp.int32, (COUNT_CHUNK, LANES), 0)
    neg_inf = jnp.float32(-jnp.inf)
    f32_lowest = jnp.float32(jnp.finfo(jnp.float32).min)

    for h in range(N_HEADS):
        cols = slice((h // 2) * LANES, (h // 2 + 1) * LANES)
        head_lanes = ((lane // HALF) % 2) == (h % 2)
        rows = slice(h * BLOCK, (h + 1) * BLOCK)
        qis_ref[rows, :] = jnp.where(head_lanes, qi_ref[0, :, cols], jnp.zeros((), MXU_DTYPE))
        qbs_ref[rows, :] = jnp.where(head_lanes, qb_ref[0, :, cols], jnp.zeros((), MXU_DTYPE))

    def chunk_start(c):
        return pl.multiple_of(c * KEY_CHUNK, KEY_CHUNK)

    def pair_q(ref, p):
        return ref[p * 2 * BLOCK:(p + 1) * 2 * BLOCK, :]

    def score_chunk(c, carry):
        r0 = chunk_start(c)
        kc = kid_ref[0, pl.ds(r0, KEY_CHUNK), :]
        acc = jnp.zeros((KEY_CHUNK, LANES), jnp.float32)
        for p in range(N_HEADS // 2):
            d = _nt_dot(kc, pair_q(qis_ref, p))
            for hh in range(2):
                h = 2 * p + hh
                acc = acc + jnp.maximum(d[:, hh * BLOCK:(hh + 1) * BLOCK], 0.0) * wt_ref[0, h:h + 1, :]
        sc_ref[pl.ds(r0, KEY_CHUNK), :] = jnp.where(r0 + row_iota <= t_idx, acc * IDX_SCALE, neg_inf)
        return carry

    lax.fori_loop(0, nch, score_chunk, 0)

    def count(pred):
        def body(c, acc):
            r0 = pl.multiple_of(c * COUNT_CHUNK, COUNT_CHUNK)
            hit = jnp.where(pred(sc_ref[pl.ds(r0, COUNT_CHUNK), :], r0), 1.0, 0.0)
            return acc + jnp.sum(hit.reshape(8, COUNT_CHUNK // 8, LANES), axis=0)
        acc = lax.fori_loop(0, ncc, body, jnp.zeros((COUNT_CHUNK // 8, LANES), jnp.float32))
        return jnp.sum(acc, axis=0, keepdims=True)

    def bit_step(b, carry):
        res, cnt_res = carry
        cand = res | (jnp.int32(1) << (31 - b))
        thr_c = _key_to_f32(cand ^ jnp.int32(INT_MIN))
        cnt = count(lambda x, r0: x >= thr_c)
        take = cnt >= topk
        return jnp.where(take, cand, res), jnp.where(take, cnt, cnt_res)

    def pending(cnt_res):
        settled = (cnt_res == topk) | (t_idx < topk)
        return jnp.max(jnp.where(settled, 0.0, 1.0))

    res, cnt_ge = lax.fori_loop(0, UNCONDITIONAL_BITS, bit_step, (jnp.zeros((1, LANES), jnp.int32),
                                                                  jnp.zeros((1, LANES), jnp.float32)))

    def more_bits(carry):
        b, res, cnt_res, _ = carry
        res, cnt_res = bit_step(b + 1, bit_step(b, (res, cnt_res)))
        return b + 2, res, cnt_res, pending(cnt_res)

    _, res, cnt_ge, _ = lax.while_loop(lambda c: (c[0] < 32) & (c[3] > 0.0), more_bits,
                                       (jnp.int32(UNCONDITIONAL_BITS), res, cnt_ge, pending(cnt_ge)))
    thr_raw = _key_to_f32(res ^ jnp.int32(INT_MIN))
    enough = thr_raw > f32_lowest
    thr = jnp.where(enough, thr_raw, f32_lowest)

    tie_lane = enough & (cnt_ge > topk)

    @pl.when(jnp.max(jnp.where(tie_lane, 1.0, 0.0)) > 0.0)
    def _():
        need = topk - count(lambda x, r0: x > thr)

        def idx_step(b, lim):
            cand = lim | (jnp.int32(1) << (index_bits - 1 - b))
            below = count(lambda x, r0: (x == thr) & (r0 + crow_iota < cand))
            return jnp.where(below < need, cand, lim)

        lim = lax.fori_loop(0, index_bits, idx_step, jnp.zeros((1, LANES), jnp.int32))

        def demote(c, carry):
            r0 = pl.multiple_of(c * COUNT_CHUNK, COUNT_CHUNK)
            x = sc_ref[pl.ds(r0, COUNT_CHUNK), :]
            drop = (x == thr) & (r0 + crow_iota > lim) & tie_lane
            sc_ref[pl.ds(r0, COUNT_CHUNK), :] = jnp.where(drop, neg_inf, x)
            return carry

        lax.fori_loop(0, ncc, demote, 0)

    def select_bias(r0):
        return jnp.where(sc_ref[pl.ds(r0, KEY_CHUNK), :] >= thr, 0.0, neg_inf)

    def score_into(c, buf):
        sq_ref, mx_ref = buf
        r0 = chunk_start(c)
        kc = kbd_ref[0, pl.ds(r0, KEY_CHUNK), :]
        bias = select_bias(r0)
        for p in range(N_HEADS // 2):
            s = _nt_dot(kc, pair_q(qbs_ref, p))
            for hh in range(2):
                cols = slice((2 * p + hh) * BLOCK, (2 * p + hh + 1) * BLOCK)
                sb = s[:, hh * BLOCK:(hh + 1) * BLOCK] + bias
                sq_ref[:, cols] = sb
                mx_ref[:, cols] = _fold_rows(sb, jnp.max)

    def accumulate(c, buf):
        sq_ref, mx_ref = buf
        m_run = m_ref[...]
        m_new = jnp.maximum(m_run, jnp.max(mx_ref[...], axis=0, keepdims=True))
        m_ref[...] = m_new
        acc_ref[...] = acc_ref[...] * jnp.exp(m_run - m_new)
        vt = vbt_ref[0, c]
        for p in range(N_HEADS // 2):
            pcols = slice(p * 2 * BLOCK, (p + 1) * 2 * BLOCK)
            e = jnp.exp(sq_ref[:, pcols] - m_new[:, pcols])
            acc_ref[:, pcols] += jnp.dot(vt, e.astype(MXU_DTYPE), preferred_element_type=jnp.float32)

    acc_ref[...] = jnp.zeros(acc_ref.shape, jnp.float32)
    m_ref[...] = jnp.full(m_ref.shape, f32_lowest, jnp.float32)
    buf_a, buf_b = (sqa_ref, mxa_ref), (sqb_ref, mxb_ref)
    score_into(0, buf_a)

    def attend_two(j, carry):
        c = 2 * j
        score_into(c + 1, buf_b)
        accumulate(c, buf_a)
        score_into(c + 2, buf_a)
        accumulate(c + 1, buf_b)
        return carry

    lax.fori_loop(0, (nch - 1) // 2, attend_two, 0)
    last = nch - 1

    @pl.when(last % 2 == 1)
    def _():
        score_into(last, buf_b)
        accumulate(last - 1, buf_a)
        accumulate(last, buf_b)

    @pl.when(last % 2 == 0)
    def _():
        accumulate(last, buf_a)

    o_t = acc_ref[:HEAD_DIM, :] / acc_ref[HEAD_DIM:HEAD_DIM + 1, :]
    for p in range(N_HEADS // 2):
        pair = jnp.concatenate([o_t[:, (2 * p) * BLOCK:(2 * p + 1) * BLOCK],
                                o_t[:, (2 * p + 1) * BLOCK:(2 * p + 2) * BLOCK]], axis=0)
        out_ref[0, :, p * LANES:(p + 1) * LANES] = pair.T.astype(out_ref.dtype)


def _sparse(qi, qb, wt, kid, kbd, vbt, batch, seq):
    topk = min(TOPK_MAX, seq // 4)
    nq = seq // BLOCK
    qblk = pl.BlockSpec((1, BLOCK, D_HEADS), lambda b, i: (b, i, 0))
    full = pl.BlockSpec((1, seq, LANES), lambda b, i: (b, 0, 0))
    kern = functools.partial(_sparse_kernel, topk=topk, index_bits=max(1, (seq - 1).bit_length()))
    return pl.pallas_call(
        kern,
        grid=(batch, nq),
        in_specs=[qblk, qblk,
                  pl.BlockSpec((1, IDX_HEADS, BLOCK), lambda b, i: (b, 0, i)),
                  full, full,
                  pl.BlockSpec((1, seq // KEY_CHUNK, HEAD_DIM + L_ROWS, KEY_CHUNK), lambda b, i: (b, 0, 0, 0))],
        out_specs=qblk,
        out_shape=jax.ShapeDtypeStruct((batch, seq, D_HEADS), MXU_DTYPE),
        scratch_shapes=[pltpu.VMEM((seq, LANES), jnp.float32),
                        pltpu.VMEM((N_HEADS * BLOCK, LANES), MXU_DTYPE),
                        pltpu.VMEM((N_HEADS * BLOCK, LANES), MXU_DTYPE),
                        pltpu.VMEM((HEAD_DIM + L_ROWS, N_HEADS * BLOCK), jnp.float32),
                        pltpu.VMEM((1, N_HEADS * BLOCK), jnp.float32),
                        pltpu.VMEM((KEY_CHUNK, N_HEADS * BLOCK), jnp.float32),
                        pltpu.VMEM((KEY_CHUNK, N_HEADS * BLOCK), jnp.float32),
                        pltpu.VMEM((8, N_HEADS * BLOCK), jnp.float32),
                        pltpu.VMEM((8, N_HEADS * BLOCK), jnp.float32)],
        compiler_params=pltpu.CompilerParams(dimension_semantics=("arbitrary", "arbitrary"),
                                             vmem_limit_bytes=VMEM_LIMIT),
        name="indexer_sparse_attn",
    )(qi, qb, wt, kid, kbd, vbt)


def _merge_kernel(x_ref, g_ref, *refs):
    n_pat = len(DILATED_PATTERNS)
    o_refs, l_refs = refs[:n_pat], refs[n_pat:2 * n_pat]
    yb_ref, wg_ref, wua_ref, wub_ref, wo_ref, x1_ref = refs[2 * n_pat:2 * n_pat + 6]
    slabs = refs[2 * n_pat + 6:]
    x = x_ref[...]
    h = _rms(x, g_ref[...]).astype(MXU_DTYPE)
    tm = x.shape[0]
    n_grp = D_HEADS // LANES

    o_src, l_src, k = [], [], 0
    for (_, d), o_ref, l_ref in zip(DILATED_PATTERNS, o_refs, l_refs):
        if d == 1:
            o_src.append(lambda g, r=o_ref: r[:, g * LANES:(g + 1) * LANES].astype(jnp.float32))
            l_src.append(lambda g, r=l_ref: r[:, g * LANES:(g + 1) * LANES])
            continue
        o_slab, l_slab = slabs[2 * k], slabs[2 * k + 1]
        k += 1
        for r in range(d):
            for g in range(n_grp):
                cols = slice(g * LANES, (g + 1) * LANES)
                o_slab[g, pl.ds(r, tm // d, stride=d), :] = o_ref[0, r, :, cols].astype(jnp.float32)
                l_slab[g, pl.ds(r, tm // d, stride=d), :] = l_ref[0, r, :, cols]
        o_src.append(lambda g, s=o_slab: s[g])
        l_src.append(lambda g, s=l_slab: s[g])

    parts = []
    for g in range(n_grp):
        ls = [f(g) for f in l_src]
        m = functools.reduce(jnp.maximum, ls)
        es = [jnp.exp(l - m) for l in ls]
        num = functools.reduce(jnp.add, [e * f(g) for e, f in zip(es, o_src)])
        parts.append((num / functools.reduce(jnp.add, es)).astype(MXU_DTYPE))
    ya = jnp.concatenate(parts, axis=1)
    ua = jnp.dot(ya, wua_ref[...], preferred_element_type=jnp.float32)
    ub = jnp.dot(yb_ref[...], wub_ref[...], preferred_element_type=jnp.float32)
    ga = jnp.dot(h, wg_ref[:, :D_MODEL], preferred_element_type=jnp.float32)
    gb = jnp.dot(h, wg_ref[:, D_MODEL:], preferred_element_type=jnp.float32)
    merged = jax.nn.sigmoid(ga) * ua + jax.nn.sigmoid(gb) * ub
    x1_ref[...] = x + jnp.dot(merged.astype(MXU_DTYPE), wo_ref[...], preferred_element_type=jnp.float32)


def _merge(x2, g, os_, lses, yb, wg, wua, wub, wo, seq):
    n = x2.shape[0]
    tm = ROW_TILE
    tiles_per_seq = seq // tm
    row = lambda i: (i, 0)
    const = lambda i: (0, 0)
    half = pl.BlockSpec((tm, D_HEADS), row)
    pat_specs = [half if d == 1 else
                 pl.BlockSpec((1, d, tm // d, D_HEADS), lambda i: (i // tiles_per_seq, 0, i % tiles_per_seq, 0))
                 for _, d in DILATED_PATTERNS]
    return pl.pallas_call(
        _merge_kernel,
        grid=(n // tm,),
        in_specs=[pl.BlockSpec((tm, D_MODEL), row), pl.BlockSpec((1, D_MODEL), const)] + pat_specs * 2 + [half] + [
            pl.BlockSpec((D_MODEL, 2 * D_MODEL), const),
            pl.BlockSpec((D_HEADS, D_MODEL), const),
            pl.BlockSpec((D_HEADS, D_MODEL), const),
            pl.BlockSpec((D_MODEL, D_MODEL), const)],
        out_specs=pl.BlockSpec((tm, D_MODEL), row),
        out_shape=jax.ShapeDtypeStruct((n, D_MODEL), jnp.float32),
        scratch_shapes=[pltpu.VMEM((D_HEADS // LANES, tm, LANES), jnp.float32)] * (2 * len(RESIDUE_DILATIONS)),
        compiler_params=pltpu.CompilerParams(dimension_semantics=("arbitrary",), vmem_limit_bytes=VMEM_LIMIT),
        name="mix_gate_out",
    )(x2, g, *os_, *lses, yb, wg, wua, wub, wo)


def _ffn_kernel(x_ref, g_ref, wgate_ref, wup_ref, wdown_ref, gf_ref, out_ref):
    x = x_ref[...]
    h = _rms(x, g_ref[...]).astype(MXU_DTYPE)
    y = x
    for c0 in range(0, D_FF, FF_CHUNK):
        a = jnp.dot(h, wgate_ref[:, c0:c0 + FF_CHUNK], preferred_element_type=jnp.float32)
        u = jnp.dot(h, wup_ref[:, c0:c0 + FF_CHUNK], preferred_element_type=jnp.float32)
        act = (a * jax.nn.sigmoid(a) * u).astype(MXU_DTYPE)
        y = y + jnp.dot(act, wdown_ref[c0:c0 + FF_CHUNK, :], preferred_element_type=jnp.float32)
    out_ref[...] = _rms(y, gf_ref[...])


def _ffn(x1, g, wgate, wup, wdown, gf):
    n = x1.shape[0]
    tm = ROW_TILE
    row = lambda i: (i, 0)
    const = lambda i: (0, 0)
    return pl.pallas_call(
        _ffn_kernel,
        grid=(n // tm,),
        in_specs=[pl.BlockSpec((tm, D_MODEL), row), pl.BlockSpec((1, D_MODEL), const),
                  pl.BlockSpec((D_MODEL, D_FF), const), pl.BlockSpec((D_MODEL, D_FF), const),
                  pl.BlockSpec((D_FF, D_MODEL), const), pl.BlockSpec((1, D_MODEL), const)],
        out_specs=pl.BlockSpec((tm, D_MODEL), row),
        out_shape=jax.ShapeDtypeStruct((n, D_MODEL), jnp.float32),
        compiler_params=pltpu.CompilerParams(dimension_semantics=("arbitrary",), vmem_limit_bytes=VMEM_LIMIT),
        name="ffn_norm",
    )(x1, g, wgate, wup, wdown, gf)


def _rope_tables(seq):
    inv_freq = ROPE_THETA ** (-jnp.arange(HALF, dtype=jnp.float32) / HALF)
    ang = jnp.arange(seq, dtype=jnp.int32).astype(jnp.float32)[:, None] * inv_freq[None, :]
    cos, sin = jnp.cos(ang), jnp.sin(ang)
    return jnp.tile(cos, (1, 4)), jnp.concatenate([-sin, -sin, sin, sin], axis=1)


def kernel(x, norm_mix, w_in, w_up_a, w_up_b, w_out, norm_ffn, w_gate, w_up, w_down, norm_final):
    batch, seq, _ = x.shape
    assert seq % max(d * BLOCK for _, d in DILATED_PATTERNS) == 0 and seq % KEY_CHUNK == 0
    assert all(w // d == BLOCK for w, d in DILATED_PATTERNS)
    n = batch * seq
    bf = MXU_DTYPE
    xf = x.reshape(n, D_MODEL)
    cos_t, sin_t = _rope_tables(seq)
    for layer in range(w_in.shape[0]):
        w = w_in[layer]
        w_pack = jnp.pad(w[:, _packed_columns()], ((0, 0), (0, _P_TOTAL - _P_MISC - HEAD_DIM - IDX_HEADS))).astype(bf)
        w_gates = w[:, _GA:].astype(bf)
        qas, kas, vas, qb, qi, kbd, kid, misc = _project(xf, norm_mix[layer][None], w_pack, cos_t, sin_t, seq)

        dil = []
        for (_, d), q, k, v in zip(DILATED_PATTERNS, qas, kas, vas):
            if d == 1:
                o, lse = _dilated(*(z.reshape(batch, 1, seq, D_HEADS) for z in (q, k, v)), d)
                dil.append((o.reshape(n, D_HEADS), lse.reshape(n, D_HEADS)))
            else:
                dil.append(_dilated(q, k, v, d))

        ones_col = (jnp.arange(L_ROWS) == 0).astype(bf)[None, :]
        vb = jnp.concatenate([misc[:, :HEAD_DIM].astype(bf), jnp.broadcast_to(ones_col, (n, L_ROWS))], axis=1)
        vbt = jnp.swapaxes(vb.reshape(batch, seq // KEY_CHUNK, KEY_CHUNK, HEAD_DIM + L_ROWS), 2, 3)
        wt = jnp.swapaxes(misc[:, HEAD_DIM:HEAD_DIM + IDX_HEADS].reshape(batch, seq, IDX_HEADS), 1, 2)
        r3 = lambda z: z.reshape(batch, seq, z.shape[-1])
        yb = _sparse(r3(qi), r3(qb), wt, r3(kid), r3(kbd), vbt, batch, seq).reshape(n, D_HEADS)

        x1 = _merge(xf, norm_mix[layer][None], [o for o, _ in dil], [l for _, l in dil], yb,
                    w_gates, w_up_a[layer].astype(bf), w_up_b[layer].astype(bf), w_out[layer].astype(bf), seq)
        last = layer == w_in.shape[0] - 1
        assert last, "the final norm is fused into the FFN kernel of the last layer"
        xf = _ffn(x1, norm_ffn[layer][None], w_gate[layer].astype(bf), w_up[layer].astype(bf),
                  w_down[layer].astype(bf), norm_final[None])
    return xf.reshape(batch, seq, D_MODEL)
```

```python
import functools

import numpy as np
import jax
import jax.numpy as jnp
from jax import lax
from jax.experimental import pallas as pl
from jax.experimental.pallas import tpu as pltpu

D_MODEL = 1024
HEAD_DIM = 64
HALF = HEAD_DIM // 2
N_HEADS = 8
D_HEADS = N_HEADS * HEAD_DIM
IDX_HEADS = 8
DILATED_PATTERNS = ((128, 1), (512, 4), (2048, 16))
RESIDUE_DILATIONS = tuple(d for _, d in DILATED_PATTERNS if d > 1)
TOPK_MAX = 256
D_FF = 2816
ROPE_THETA = 10000.0
RMS_EPS = 1e-6
BLOCK = 128
ATTN_SCALE = HEAD_DIM ** -0.5
IDX_SCALE = (HEAD_DIM ** -0.5) * (IDX_HEADS ** -0.5)
LOG2_E = float(np.log2(np.e))

LANES = 128
INT_MIN = -(2 ** 31)
KEY_CHUNK = 1024
COUNT_CHUNK = 512
COARSE_BITS = 16
FINE_BITS = 17
UNCONDITIONAL_BITS = 5
L_ROWS = 8
ROW_TILE = 512
FF_CHUNK = 1408
VMEM_LIMIT = 56 * 1024 * 1024
MXU_DTYPE = jnp.bfloat16

_SPLITS = (D_HEADS, D_HEADS, D_HEADS, D_HEADS, HEAD_DIM, HEAD_DIM, IDX_HEADS * HEAD_DIM, HEAD_DIM, IDX_HEADS,
           D_MODEL, D_MODEL)
_OFF = np.concatenate([[0], np.cumsum(_SPLITS)])
(_QA, _KA, _VA, _QB, _KB, _VB, _QI, _KI, _WI, _GA, _GB) = (int(o) for o in _OFF[:-1])

_P_QA, _P_KA, _P_VA, _P_QB, _P_QI = 0, 512, 1024, 1536, 2048
_P_KBD, _P_KID, _P_MISC = 2560, 2688, 2816
_P_TOTAL = 2944


def _pair_perm():
    idx = np.empty(D_HEADS, np.int64)
    for j in range(D_HEADS):
        g, l = divmod(j, LANES)
        quarter, e = divmod(l, HALF)
        head = 2 * g + (quarter % 2)
        idx[j] = head * HEAD_DIM + (quarter // 2) * HALF + e
    return idx


def _dup_perm():
    idx = np.empty(LANES, np.int64)
    for l in range(LANES):
        quarter, e = divmod(l, HALF)
        idx[l] = (quarter // 2) * HALF + e
    return idx


def _packed_columns():
    pp, dp = _pair_perm(), _dup_perm()
    return np.concatenate([
        _QA + pp, _KA + pp, _VA + np.arange(D_HEADS), _QB + pp, _QI + pp,
        _KB + dp, _KI + dp, _VB + np.arange(HEAD_DIM), _WI + np.arange(IDX_HEADS)])


def _rms(x, g):
    ms = jnp.mean(x * x, axis=-1, keepdims=True)
    return x * lax.rsqrt(ms + RMS_EPS) * g


def _nt_dot(a, b):
    return lax.dot_general(a, b, (((1,), (1,)), ((), ())), preferred_element_type=jnp.float32)


def _proj_kernel(x_ref, g_ref, w_ref, cos_ref, sin_ref, *refs):
    n_lay = 1 + len(RESIDUE_DILATIONS)
    qa_refs, ka_refs, va_refs = refs[:n_lay], refs[n_lay:2 * n_lay], refs[2 * n_lay:3 * n_lay]
    qb_ref, qi_ref, kbd_ref, kid_ref, misc_ref, h_ref, slab_ref = refs[3 * n_lay:]
    h_ref[...] = _rms(x_ref[...], g_ref[...]).astype(MXU_DTYPE)
    cos = cos_ref[...]
    sin = sin_ref[...]

    def mm(c0, width):
        return jnp.dot(h_ref[...], w_ref[:, c0:c0 + width], preferred_element_type=jnp.float32)

    def rope(z):
        parts = []
        for g in range(z.shape[1] // LANES):
            zg = z[:, g * LANES:(g + 1) * LANES]
            parts.append(zg * cos + pltpu.roll(zg, 2 * HALF, axis=1) * sin)
        return parts[0] if len(parts) == 1 else jnp.concatenate(parts, axis=1)

    def emit(y, out_refs):
        out_refs[0][...] = y.astype(MXU_DTYPE)
        for g in range(D_HEADS // LANES):
            slab_ref[g] = y[:, g * LANES:(g + 1) * LANES]
        for d, ref in zip(RESIDUE_DILATIONS, out_refs[1:]):
            rows = y.shape[0] // d
            for r in range(d):
                for g in range(D_HEADS // LANES):
                    ref[0, r, :, g * LANES:(g + 1) * LANES] = (
                        slab_ref[g, pl.ds(r, rows, stride=d), :].astype(MXU_DTYPE))

    emit(rope(mm(_P_QA, D_HEADS)) * ATTN_SCALE, qa_refs)
    emit(rope(mm(_P_KA, D_HEADS)), ka_refs)
    emit(mm(_P_VA, D_HEADS), va_refs)
    qb_ref[...] = (rope(mm(_P_QB, D_HEADS)) * (ATTN_SCALE * LOG2_E)).astype(MXU_DTYPE)
    qi_ref[...] = rope(mm(_P_QI, D_HEADS)).astype(MXU_DTYPE)
    kbd_ref[...] = rope(mm(_P_KBD, LANES)).astype(MXU_DTYPE)
    kid_ref[...] = rope(mm(_P_KID, LANES)).astype(MXU_DTYPE)
    misc_ref[...] = mm(_P_MISC, LANES)


def _project(x2, g, w_pack, cos_t, sin_t, seq):
    n = x2.shape[0]
    tm = ROW_TILE
    tiles_per_seq = seq // tm
    row = lambda i: (i, 0)
    const = lambda i: (0, 0)
    pos = lambda i: (i % tiles_per_seq, 0)
    batch = n // seq
    wide = jax.ShapeDtypeStruct((n, D_HEADS), MXU_DTYPE)
    narrow = jax.ShapeDtypeStruct((n, LANES), MXU_DTYPE)
    wide_spec = pl.BlockSpec((tm, D_HEADS), row)
    lay_shapes = [wide] + [jax.ShapeDtypeStruct((batch, d, seq // d, D_HEADS), MXU_DTYPE) for d in RESIDUE_DILATIONS]
    lay_specs = [wide_spec] + [
        pl.BlockSpec((1, d, tm // d, D_HEADS), lambda i: (i // tiles_per_seq, 0, i % tiles_per_seq, 0))
        for d in RESIDUE_DILATIONS]
    n_lay = len(lay_shapes)
    outs = pl.pallas_call(
        _proj_kernel,
        grid=(n // tm,),
        in_specs=[
            pl.BlockSpec((tm, D_MODEL), row),
            pl.BlockSpec((1, D_MODEL), const),
            pl.BlockSpec((D_MODEL, _P_TOTAL), const),
            pl.BlockSpec((tm, LANES), pos),
            pl.BlockSpec((tm, LANES), pos),
        ],
        out_specs=lay_specs * 3 + [wide_spec] * 2 + [pl.BlockSpec((tm, LANES), row)] * 3,
        out_shape=lay_shapes * 3 + [wide] * 2 + [narrow, narrow, jax.ShapeDtypeStruct((n, LANES), jnp.float32)],
        scratch_shapes=[pltpu.VMEM((tm, D_MODEL), MXU_DTYPE),
                        pltpu.VMEM((D_HEADS // LANES, tm, LANES), jnp.float32)],
        compiler_params=pltpu.CompilerParams(dimension_semantics=("arbitrary",), vmem_limit_bytes=VMEM_LIMIT),
        name="proj_rope",
    )(x2, g, w_pack, cos_t, sin_t)
    return (outs[:n_lay], outs[n_lay:2 * n_lay], outs[2 * n_lay:3 * n_lay]) + tuple(outs[3 * n_lay:])


def _dil_kernel(q_ref, k_ref, v_ref, kp_ref, vp_ref, o_ref, lse_ref):
    n = pl.program_id(2)
    qi = lax.broadcasted_iota(jnp.int32, (BLOCK, 2 * BLOCK), 0)
    kj = lax.broadcasted_iota(jnp.int32, (BLOCK, 2 * BLOCK), 1)
    band = (kj >= qi) & (kj <= qi + BLOCK) & ((kj >= BLOCK) | (n > 0))
    lane = lax.broadcasted_iota(jnp.int32, (1, LANES), 1)
    first_v = lane < HEAD_DIM
    for p in range(N_HEADS // 2):
        cols = slice(p * LANES, (p + 1) * LANES)
        qp = q_ref[0, 0, :, cols]
        k2 = jnp.concatenate([kp_ref[0, 0, :, cols], k_ref[0, 0, :, cols]], axis=0)
        v2 = jnp.concatenate([vp_ref[0, 0, :, cols], v_ref[0, 0, :, cols]], axis=0)
        outs, lses = [], []
        for hh in range(2):
            head_lanes = ((lane // HALF) % 2) == hh
            qm = jnp.where(head_lanes, qp, jnp.zeros_like(qp))
            s = jnp.where(band, _nt_dot(qm, k2), -jnp.inf)
            m = jnp.max(s, axis=-1, keepdims=True)
            e = jnp.exp(s - m)
            den = jnp.sum(e, axis=-1, keepdims=True)
            o = jnp.dot(e.astype(MXU_DTYPE), v2, preferred_element_type=jnp.float32) / den
            outs.append(o)
            lses.append(m + jnp.log(den))
        o_ref[0, 0, :, cols] = jnp.where(first_v, outs[0], outs[1]).astype(o_ref.dtype)
        lse_ref[0, 0, :, cols] = jnp.where(first_v, lses[0], lses[1])


def _dilated(q, k, v, dilation):
    batch, _, m_len, _ = q.shape
    nb = m_len // BLOCK
    cur = lambda b, r, n: (b, r, n, 0)
    prev = lambda b, r, n: (b, r, jnp.maximum(n - 1, 0), 0)
    blk = (1, 1, BLOCK, D_HEADS)
    return pl.pallas_call(
        _dil_kernel,
        grid=(batch, dilation, nb),
        in_specs=[pl.BlockSpec(blk, cur), pl.BlockSpec(blk, cur), pl.BlockSpec(blk, cur),
                  pl.BlockSpec(blk, prev), pl.BlockSpec(blk, prev)],
        out_specs=[pl.BlockSpec(blk, cur), pl.BlockSpec(blk, cur)],
        out_shape=[jax.ShapeDtypeStruct(q.shape, MXU_DTYPE), jax.ShapeDtypeStruct(q.shape, jnp.float32)],
        compiler_params=pltpu.CompilerParams(dimension_semantics=("arbitrary",) * 3),
        name=f"dilated_d{dilation}",
    )(q, k, v, k, v)


def _key_to_f32(key):
    bits = key ^ ((key >> 31) & jnp.int32(0x7FFFFFFF))
    return lax.bitcast_convert_type(bits, jnp.float32)


def _fold_rows(x, op):
    rows = x.shape[0]
    y = op(x.reshape(rows // 64, 64, LANES), axis=0)
    return op(y.reshape(8, 8, LANES), axis=0)


def _sparse_kernel(qi_ref, qb_ref, wt_ref, kid_ref, kbd_ref, vbt_ref, out_ref,
                   sc_ref, scb_ref, mb_ref, qis_ref, qbs_ref, acc_ref, m_ref, sqa_ref, sqb_ref, mxa_ref, mxb_ref,
                   *, topk, index_bits):
    i = pl.program_id(1)
    nch = i // (KEY_CHUNK // BLOCK) + 1
    ncc = i // (COUNT_CHUNK // BLOCK) + 1
    lane = lax.broadcasted_iota(jnp.int32, (1, LANES), 1)
    t_idx = i * BLOCK + lane
    row_iota = lax.broadcasted_iota(jnp.int32, (KEY_CHUNK, LANES), 0)
    crow_iota = lax.broadcasted_iota(jnp.int32, (COUNT_CHUNK, LANES), 0)
    neg_inf = jnp.float32(-jnp.inf)
    f32_lowest = jnp.float32(jnp.finfo(jnp.float32).min)

    eye = jnp.where(lax.broadcasted_iota(jnp.int32, (BLOCK, LANES), 0) == lane, 1.0, 0.0).astype(MXU_DTYPE)
    for h in range(N_HEADS):
        cols = slice((h // 2) * LANES, (h // 2 + 1) * LANES)
        head_lanes = ((lane // HALF) % 2) == (h % 2)
        rows = slice(h * BLOCK, (h + 1) * BLOCK)
        qis_ref[rows, :] = jnp.where(head_lanes, qi_ref[0, :, cols], jnp.zeros((), MXU_DTYPE))
        qbs_ref[rows, :LANES] = jnp.where(head_lanes, qb_ref[0, :, cols], jnp.zeros((), MXU_DTYPE))
        qbs_ref[rows, LANES:] = eye

    def chunk_start(c):
        return pl.multiple_of(c * KEY_CHUNK, KEY_CHUNK)

    def pair_q(ref, p):
        return ref[p * 2 * BLOCK:(p + 1) * 2 * BLOCK, :]

    buf_a, buf_b = (sqa_ref, mxa_ref), (sqb_ref, mxb_ref)

    def pipeline(produce, consume):
        produce(0, buf_a)

        def two_chunks(j, carry):
            c = 2 * j
            produce(c + 1, buf_b)
            consume(c, buf_a)
            produce(c + 2, buf_a)
            consume(c + 1, buf_b)
            return carry

        lax.fori_loop(0, (nch - 1) // 2, two_chunks, 0)
        last = nch - 1

        @pl.when(last % 2 == 1)
        def _():
            produce(last, buf_b)
            consume(last - 1, buf_a)
            consume(last, buf_b)

        @pl.when(last % 2 == 0)
        def _():
            consume(last, buf_a)

    def index_dots(c, buf):
        kc = kid_ref[0, pl.ds(chunk_start(c), KEY_CHUNK), :]
        for p in range(N_HEADS // 2):
            buf[0][:, p * 2 * BLOCK:(p + 1) * 2 * BLOCK] = _nt_dot(kc, pair_q(qis_ref, p))

    def index_scores(c, buf):
        r0 = chunk_start(c)
        acc = jnp.zeros((KEY_CHUNK, LANES), jnp.float32)
        for h in range(N_HEADS):
            acc = acc + jnp.maximum(buf[0][:, h * BLOCK:(h + 1) * BLOCK], 0.0) * wt_ref[0, h:h + 1, :]
        scores = jnp.where(r0 + row_iota <= t_idx, acc * IDX_SCALE, neg_inf)
        sc_ref[pl.ds(r0, KEY_CHUNK), :] = scores
        scb_ref[pl.ds(r0, KEY_CHUNK), :] = scores.astype(MXU_DTYPE)

    pipeline(index_dots, index_scores)

    def count(pred):
        def body(c, acc):
            r0 = pl.multiple_of(c * COUNT_CHUNK, COUNT_CHUNK)
            hit = jnp.where(pred(sc_ref[pl.ds(r0, COUNT_CHUNK), :], r0), 1.0, 0.0)
            return acc + jnp.sum(hit.reshape(8, COUNT_CHUNK // 8, LANES), axis=0)
        acc = lax.fori_loop(0, ncc, body, jnp.zeros((COUNT_CHUNK // 8, LANES), jnp.float32))
        return jnp.sum(acc, axis=0, keepdims=True)

    def count_coarse(thr_c):
        one, zero = jnp.ones((), MXU_DTYPE), jnp.zeros((), MXU_DTYPE)
        q = COUNT_CHUNK // 4

        def body(c, acc):
            r0 = pl.multiple_of(c * COUNT_CHUNK, COUNT_CHUNK)
            hit = jnp.where(scb_ref[pl.ds(r0, COUNT_CHUNK), :] >= thr_c, one, zero)
            return acc + ((hit[:q] + hit[q:2 * q]) + (hit[2 * q:3 * q] + hit[3 * q:]))
        acc = lax.fori_loop(0, ncc, body, jnp.zeros((q, LANES), MXU_DTYPE))
        return jnp.sum(acc.astype(jnp.float32), axis=0, keepdims=True)

    def coarse_step(b, res):
        cand = res | (jnp.int32(1) << (31 - b))
        thr_c = _key_to_f32(cand ^ jnp.int32(INT_MIN)).astype(MXU_DTYPE)
        return jnp.where(count_coarse(thr_c) >= topk, cand, res)

    res_hi = lax.fori_loop(0, COARSE_BITS, coarse_step, jnp.zeros((1, LANES), jnp.int32))
    key_lo = (res_hi ^ jnp.int32(INT_MIN)) - jnp.int32(1 << (31 - COARSE_BITS))

    def fine_step(b, carry):
        d, cnt_d = carry
        cand = d | (jnp.int32(1) << (FINE_BITS - 1 - b))
        thr_c = _key_to_f32(key_lo + cand)
        cnt = count(lambda x, r0: x >= thr_c)
        take = cnt >= topk
        return jnp.where(take, cand, d), jnp.where(take, cnt, cnt_d)

    def pending(cnt_d):
        settled = (cnt_d == topk) | (t_idx < topk)
        return jnp.max(jnp.where(settled, 0.0, 1.0))

    d, cnt_ge = lax.fori_loop(0, UNCONDITIONAL_BITS, fine_step,
                              (jnp.zeros((1, LANES), jnp.int32), jnp.full((1, LANES), 2.0 * sc_ref.shape[0])))

    def more_bits(carry):
        b, d, cnt_d, _ = carry
        d, cnt_d = fine_step(b + 1, fine_step(b, (d, cnt_d)))
        return b + 2, d, cnt_d, pending(cnt_d)

    _, d, cnt_ge, _ = lax.while_loop(lambda c: (c[0] < FINE_BITS) & (c[3] > 0.0), more_bits,
                                     (jnp.int32(UNCONDITIONAL_BITS), d, cnt_ge, pending(cnt_ge)))
    thr_raw = _key_to_f32(key_lo + d)
    enough = thr_raw > f32_lowest
    thr = jnp.where(enough, thr_raw, f32_lowest)

    tie_lane = enough & (cnt_ge > topk)

    @pl.when(jnp.max(jnp.where(tie_lane, 1.0, 0.0)) > 0.0)
    def _():
        need = topk - count(lambda x, r0: x > thr)

        def idx_step(b, lim):
            cand = lim | (jnp.int32(1) << (index_bits - 1 - b))
            below = count(lambda x, r0: (x == thr) & (r0 + crow_iota < cand))
            return jnp.where(below < need, cand, lim)

        lim = lax.fori_loop(0, index_bits, idx_step, jnp.zeros((1, LANES), jnp.int32))

        def demote(c, carry):
            r0 = pl.multiple_of(c * COUNT_CHUNK, COUNT_CHUNK)
            x = sc_ref[pl.ds(r0, COUNT_CHUNK), :]
            drop = (x == thr) & (r0 + crow_iota > lim) & tie_lane
            sc_ref[pl.ds(r0, COUNT_CHUNK), :] = jnp.where(drop, neg_inf, x)
            return carry

        lax.fori_loop(0, ncc, demote, 0)

    mask_off = float(jnp.finfo(MXU_DTYPE).min)

    def build_mask(c, carry):
        r0 = chunk_start(c)
        selected = sc_ref[pl.ds(r0, KEY_CHUNK), :] >= thr
        mb_ref[pl.ds(r0, KEY_CHUNK), :] = jnp.where(selected, 0.0, mask_off).astype(MXU_DTYPE)
        return carry

    lax.fori_loop(0, nch, build_mask, 0)

    def attn_scores(c, buf):
        sq_ref, mx_ref = buf
        r0 = chunk_start(c)
        kaug = jnp.concatenate([kbd_ref[0, pl.ds(r0, KEY_CHUNK), :], mb_ref[pl.ds(r0, KEY_CHUNK), :]], axis=1)
        for p in range(N_HEADS // 2):
            s = _nt_dot(kaug, pair_q(qbs_ref, p))
            sq_ref[:, p * 2 * BLOCK:(p + 1) * 2 * BLOCK] = s
            for hh in range(2):
                cols = slice((2 * p + hh) * BLOCK, (2 * p + hh + 1) * BLOCK)
                mx_ref[:, cols] = _fold_rows(s[:, hh * BLOCK:(hh + 1) * BLOCK], jnp.max)

    def accumulate(c, buf):
        sq_ref, mx_ref = buf
        m_run = m_ref[...]
        m_new = jnp.maximum(m_run, jnp.max(mx_ref[...], axis=0, keepdims=True))
        m_ref[...] = m_new
        acc_ref[...] = acc_ref[...] * jnp.exp2(m_run - m_new)
        vt = vbt_ref[0, c]
        for p in range(N_HEADS // 2):
            pcols = slice(p * 2 * BLOCK, (p + 1) * 2 * BLOCK)
            e = jnp.exp2(sq_ref[:, pcols] - m_new[:, pcols])
            acc_ref[:, pcols] += jnp.dot(vt, e.astype(MXU_DTYPE), preferred_element_type=jnp.float32)

    acc_ref[...] = jnp.zeros(acc_ref.shape, jnp.float32)
    m_ref[...] = jnp.full(m_ref.shape, f32_lowest, jnp.float32)
    pipeline(attn_scores, accumulate)

    o_t = acc_ref[:HEAD_DIM, :] / acc_ref[HEAD_DIM:HEAD_DIM + 1, :]
    for p in range(N_HEADS // 2):
        pair = jnp.concatenate([o_t[:, (2 * p) * BLOCK:(2 * p + 1) * BLOCK],
                                o_t[:, (2 * p + 1) * BLOCK:(2 * p + 2) * BLOCK]], axis=0)
        out_ref[0, :, p * LANES:(p + 1) * LANES] = pair.T.astype(out_ref.dtype)


def _sparse(qi, qb, wt, kid, kbd, vbt, batch, seq):
    topk = min(TOPK_MAX, seq // 4)
    nq = seq // BLOCK
    qblk = pl.BlockSpec((1, BLOCK, D_HEADS), lambda b, i: (b, i, 0))
    full = pl.BlockSpec((1, seq, LANES), lambda b, i: (b, 0, 0))
    kern = functools.partial(_sparse_kernel, topk=topk, index_bits=max(1, (seq - 1).bit_length()))
    return pl.pallas_call(
        kern,
        grid=(batch, nq),
        in_specs=[qblk, qblk,
                  pl.BlockSpec((1, IDX_HEADS, BLOCK), lambda b, i: (b, 0, i)),
                  full, full,
                  pl.BlockSpec((1, seq // KEY_CHUNK, HEAD_DIM + L_ROWS, KEY_CHUNK), lambda b, i: (b, 0, 0, 0))],
        out_specs=qblk,
        out_shape=jax.ShapeDtypeStruct((batch, seq, D_HEADS), MXU_DTYPE),
        scratch_shapes=[pltpu.VMEM((seq, LANES), jnp.float32),
                        pltpu.VMEM((seq, LANES), MXU_DTYPE),
                        pltpu.VMEM((seq, LANES), MXU_DTYPE),
                        pltpu.VMEM((N_HEADS * BLOCK, LANES), MXU_DTYPE),
                        pltpu.VMEM((N_HEADS * BLOCK, 2 * LANES), MXU_DTYPE),
                        pltpu.VMEM((HEAD_DIM + L_ROWS, N_HEADS * BLOCK), jnp.float32),
                        pltpu.VMEM((1, N_HEADS * BLOCK), jnp.float32),
                        pltpu.VMEM((KEY_CHUNK, N_HEADS * BLOCK), jnp.float32),
                        pltpu.VMEM((KEY_CHUNK, N_HEADS * BLOCK), jnp.float32),
                        pltpu.VMEM((8, N_HEADS * BLOCK), jnp.float32),
                        pltpu.VMEM((8, N_HEADS * BLOCK), jnp.float32)],
        compiler_params=pltpu.CompilerParams(dimension_semantics=("arbitrary", "arbitrary"),
                                             vmem_limit_bytes=VMEM_LIMIT),
        name="indexer_sparse_attn",
    )(qi, qb, wt, kid, kbd, vbt)


def _merge_kernel(x_ref, g_ref, *refs):
    n_pat = len(DILATED_PATTERNS)
    o_refs, l_refs = refs[:n_pat], refs[n_pat:2 * n_pat]
    yb_ref, wg_ref, wua_ref, wub_ref, wo_ref, x1_ref = refs[2 * n_pat:2 * n_pat + 6]
    slabs = refs[2 * n_pat + 6:]
    x = x_ref[...]
    h = _rms(x, g_ref[...]).astype(MXU_DTYPE)
    tm = x.shape[0]
    n_grp = D_HEADS // LANES

    o_src, l_src, k = [], [], 0
    for (_, d), o_ref, l_ref in zip(DILATED_PATTERNS, o_refs, l_refs):
        if d == 1:
            o_src.append(lambda g, r=o_ref: r[:, g * LANES:(g + 1) * LANES].astype(jnp.float32))
            l_src.append(lambda g, r=l_ref: r[:, g * LANES:(g + 1) * LANES])
            continue
        o_slab, l_slab = slabs[2 * k], slabs[2 * k + 1]
        k += 1
        for r in range(d):
            for g in range(n_grp):
                cols = slice(g * LANES, (g + 1) * LANES)
                o_slab[g, pl.ds(r, tm // d, stride=d), :] = o_ref[0, r, :, cols].astype(jnp.float32)
                l_slab[g, pl.ds(r, tm // d, stride=d), :] = l_ref[0, r, :, cols]
        o_src.append(lambda g, s=o_slab: s[g])
        l_src.append(lambda g, s=l_slab: s[g])

    parts = []
    for g in range(n_grp):
        ls = [f(g) for f in l_src]
        m = functools.reduce(jnp.maximum, ls)
        es = [jnp.exp(l - m) for l in ls]
        num = functools.reduce(jnp.add, [e * f(g) for e, f in zip(es, o_src)])
        parts.append((num / functools.reduce(jnp.add, es)).astype(MXU_DTYPE))
    ya = jnp.concatenate(parts, axis=1)
    ua = jnp.dot(ya, wua_ref[...], preferred_element_type=jnp.float32)
    ub = jnp.dot(yb_ref[...], wub_ref[...], preferred_element_type=jnp.float32)
    ga = jnp.dot(h, wg_ref[:, :D_MODEL], preferred_element_type=jnp.float32)
    gb = jnp.dot(h, wg_ref[:, D_MODEL:], preferred_element_type=jnp.float32)
    merged = jax.nn.sigmoid(ga) * ua + jax.nn.sigmoid(gb) * ub
    x1_ref[...] = x + jnp.dot(merged.astype(MXU_DTYPE), wo_ref[...], preferred_element_type=jnp.float32)


def _merge(x2, g, os_, lses, yb, wg, wua, wub, wo, seq):
    n = x2.shape[0]
    tm = ROW_TILE
    tiles_per_seq = seq // tm
    row = lambda i: (i, 0)
    const = lambda i: (0, 0)
    half = pl.BlockSpec((tm, D_HEADS), row)
    pat_specs = [half if d == 1 else
                 pl.BlockSpec((1, d, tm // d, D_HEADS), lambda i: (i // tiles_per_seq, 0, i % tiles_per_seq, 0))
                 for _, d in DILATED_PATTERNS]
    return pl.pallas_call(
        _merge_kernel,
        grid=(n // tm,),
        in_specs=[pl.BlockSpec((tm, D_MODEL), row), pl.BlockSpec((1, D_MODEL), const)] + pat_specs * 2 + [half] + [
            pl.BlockSpec((D_MODEL, 2 * D_MODEL), const),
            pl.BlockSpec((D_HEADS, D_MODEL), const),
            pl.BlockSpec((D_HEADS, D_MODEL), const),
            pl.BlockSpec((D_MODEL, D_MODEL), const)],
        out_specs=pl.BlockSpec((tm, D_MODEL), row),
        out_shape=jax.ShapeDtypeStruct((n, D_MODEL), jnp.float32),
        scratch_shapes=[pltpu.VMEM((D_HEADS // LANES, tm, LANES), jnp.float32)] * (2 * len(RESIDUE_DILATIONS)),
        compiler_params=pltpu.CompilerParams(dimension_semantics=("arbitrary",), vmem_limit_bytes=VMEM_LIMIT),
        name="mix_gate_out",
    )(x2, g, *os_, *lses, yb, wg, wua, wub, wo)


def _ffn_kernel(x_ref, g_ref, wgate_ref, wup_ref, wdown_ref, gf_ref, out_ref):
    x = x_ref[...]
    h = _rms(x, g_ref[...]).astype(MXU_DTYPE)
    y = x
    for c0 in range(0, D_FF, FF_CHUNK):
        a = jnp.dot(h, wgate_ref[:, c0:c0 + FF_CHUNK], preferred_element_type=jnp.float32)
        u = jnp.dot(h, wup_ref[:, c0:c0 + FF_CHUNK], preferred_element_type=jnp.float32)
        act = (a * jax.nn.sigmoid(a) * u).astype(MXU_DTYPE)
        y = y + jnp.dot(act, wdown_ref[c0:c0 + FF_CHUNK, :], preferred_element_type=jnp.float32)
    out_ref[...] = _rms(y, gf_ref[...])


def _ffn(x1, g, wgate, wup, wdown, gf):
    n = x1.shape[0]
    tm = ROW_TILE
    row = lambda i: (i, 0)
    const = lambda i: (0, 0)
    return pl.pallas_call(
        _ffn_kernel,
        grid=(n // tm,),
        in_specs=[pl.BlockSpec((tm, D_MODEL), row), pl.BlockSpec((1, D_MODEL), const),
                  pl.BlockSpec((D_MODEL, D_FF), const), pl.BlockSpec((D_MODEL, D_FF), const),
                  pl.BlockSpec((D_FF, D_MODEL), const), pl.BlockSpec((1, D_MODEL), const)],
        out_specs=pl.BlockSpec((tm, D_MODEL), row),
        out_shape=jax.ShapeDtypeStruct((n, D_MODEL), jnp.float32),
        compiler_params=pltpu.CompilerParams(dimension_semantics=("arbitrary",), vmem_limit_bytes=VMEM_LIMIT),
        name="ffn_norm",
    )(x1, g, wgate, wup, wdown, gf)


def _rope_tables(seq):
    inv_freq = ROPE_THETA ** (-jnp.arange(HALF, dtype=jnp.float32) / HALF)
    ang = jnp.arange(seq, dtype=jnp.int32).astype(jnp.float32)[:, None] * inv_freq[None, :]
    cos, sin = jnp.cos(ang), jnp.sin(ang)
    return jnp.tile(cos, (1, 4)), jnp.concatenate([-sin, -sin, sin, sin], axis=1)


def kernel(x, norm_mix, w_in, w_up_a, w_up_b, w_out, norm_ffn, w_gate, w_up, w_down, norm_final):
    batch, seq, _ = x.shape
    assert seq % max(d * BLOCK for _, d in DILATED_PATTERNS) == 0 and seq % KEY_CHUNK == 0
    assert all(w // d == BLOCK for w, d in DILATED_PATTERNS)
    n = batch * seq
    bf = MXU_DTYPE
    xf = x.reshape(n, D_MODEL)
    cos_t, sin_t = _rope_tables(seq)
    for layer in range(w_in.shape[0]):
        w = w_in[layer]
        w_pack = jnp.pad(w[:, _packed_columns()], ((0, 0), (0, _P_TOTAL - _P_MISC - HEAD_DIM - IDX_HEADS))).astype(bf)
        w_gates = w[:, _GA:].astype(bf)
        qas, kas, vas, qb, qi, kbd, kid, misc = _project(xf, norm_mix[layer][None], w_pack, cos_t, sin_t, seq)

        dil = []
        for (_, d), q, k, v in zip(DILATED_PATTERNS, qas, kas, vas):
            if d == 1:
                o, lse = _dilated(*(z.reshape(batch, 1, seq, D_HEADS) for z in (q, k, v)), d)
                dil.append((o.reshape(n, D_HEADS), lse.reshape(n, D_HEADS)))
            else:
                dil.append(_dilated(q, k, v, d))

        ones_col = (jnp.arange(L_ROWS) == 0).astype(bf)[None, :]
        vb = jnp.concatenate([misc[:, :HEAD_DIM].astype(bf), jnp.broadcast_to(ones_col, (n, L_ROWS))], axis=1)
        vbt = jnp.swapaxes(vb.reshape(batch, seq // KEY_CHUNK, KEY_CHUNK, HEAD_DIM + L_ROWS), 2, 3)
        wt = jnp.swapaxes(misc[:, HEAD_DIM:HEAD_DIM + IDX_HEADS].reshape(batch, seq, IDX_HEADS), 1, 2)
        r3 = lambda z: z.reshape(batch, seq, z.shape[-1])
        yb = _sparse(r3(qi), r3(qb), wt, r3(kid), r3(kbd), vbt, batch, seq).reshape(n, D_HEADS)

        x1 = _merge(xf, norm_mix[layer][None], [o for o, _ in dil], [l for _, l in dil], yb,
                    w_gates, w_up_a[layer].astype(bf), w_up_b[layer].astype(bf), w_out[layer].astype(bf), seq)
        last = layer == w_in.shape[0] - 1
        assert last, "the final norm is fused into the FFN kernel of the last layer"
        xf = _ffn(x1, norm_ffn[layer][None], w_gate[layer].astype(bf), w_up[layer].astype(bf),
                  w_down[layer].astype(bf), norm_final[None])
    return xf.reshape(batch, seq, D_MODEL)
```

```python
import functools

import numpy as np
import jax
import jax.numpy as jnp
from jax import lax
from jax.experimental import pallas as pl
from jax.experimental.pallas import tpu as pltpu

D_MODEL = 1024
HEAD_DIM = 64
HALF = HEAD_DIM // 2
N_HEADS = 8
D_HEADS = N_HEADS * HEAD_DIM
IDX_HEADS = 8
DILATED_PATTERNS = ((128, 1), (512, 4), (2048, 16))
RESIDUE_DILATIONS = tuple(d for _, d in DILATED_PATTERNS if d > 1)
TOPK_MAX = 256
D_FF = 2816
ROPE_THETA = 10000.0
RMS_EPS = 1e-6
BLOCK = 128
ATTN_SCALE = HEAD_DIM ** -0.5
IDX_SCALE = (HEAD_DIM ** -0.5) * (IDX_HEADS ** -0.5)
LOG2_E = float(np.log2(np.e))

LANES = 128
INT_MIN = -(2 ** 31)
KEY_CHUNK = 1024
COUNT_CHUNK = 512
UNCONDITIONAL_BITS = 20
L_ROWS = 8
ROW_TILE = 512
FF_CHUNK = 1408
VMEM_LIMIT = 56 * 1024 * 1024
MXU_DTYPE = jnp.bfloat16

_SPLITS = (D_HEADS, D_HEADS, D_HEADS, D_HEADS, HEAD_DIM, HEAD_DIM, IDX_HEADS * HEAD_DIM, HEAD_DIM, IDX_HEADS,
           D_MODEL, D_MODEL)
_OFF = np.concatenate([[0], np.cumsum(_SPLITS)])
(_QA, _KA, _VA, _QB, _KB, _VB, _QI, _KI, _WI, _GA, _GB) = (int(o) for o in _OFF[:-1])

_P_QA, _P_KA, _P_VA, _P_QB, _P_QI = 0, 512, 1024, 1536, 2048
_P_KBD, _P_KID, _P_MISC = 2560, 2688, 2816
_P_TOTAL = 2944


def _pair_perm():
    idx = np.empty(D_HEADS, np.int64)
    for j in range(D_HEADS):
        g, l = divmod(j, LANES)
        quarter, e = divmod(l, HALF)
        head = 2 * g + (quarter % 2)
        idx[j] = head * HEAD_DIM + (quarter // 2) * HALF + e
    return idx


def _dup_perm():
    idx = np.empty(LANES, np.int64)
    for l in range(LANES):
        quarter, e = divmod(l, HALF)
        idx[l] = (quarter // 2) * HALF + e
    return idx


def _packed_columns():
    pp, dp = _pair_perm(), _dup_perm()
    return np.concatenate([
        _QA + pp, _KA + pp, _VA + np.arange(D_HEADS), _QB + pp, _QI + pp,
        _KB + dp, _KI + dp, _VB + np.arange(HEAD_DIM), _WI + np.arange(IDX_HEADS)])


def _rms(x, g):
    ms = jnp.mean(x * x, axis=-1, keepdims=True)
    return x * lax.rsqrt(ms + RMS_EPS) * g


def _nt_dot(a, b):
    return lax.dot_general(a, b, (((1,), (1,)), ((), ())), preferred_element_type=jnp.float32)


def _proj_kernel(x_ref, g_ref, w_ref, cos_ref, sin_ref, *refs):
    n_lay = 1 + len(RESIDUE_DILATIONS)
    qa_refs, ka_refs, va_refs = refs[:n_lay], refs[n_lay:2 * n_lay], refs[2 * n_lay:3 * n_lay]
    qb_ref, qi_ref, kbd_ref, kid_ref, misc_ref, h_ref, slab_ref = refs[3 * n_lay:]
    h_ref[...] = _rms(x_ref[...], g_ref[...]).astype(MXU_DTYPE)
    cos = cos_ref[...]
    sin = sin_ref[...]

    def mm(c0, width):
        return jnp.dot(h_ref[...], w_ref[:, c0:c0 + width], preferred_element_type=jnp.float32)

    def rope(z):
        parts = []
        for g in range(z.shape[1] // LANES):
            zg = z[:, g * LANES:(g + 1) * LANES]
            parts.append(zg * cos + pltpu.roll(zg, 2 * HALF, axis=1) * sin)
        return parts[0] if len(parts) == 1 else jnp.concatenate(parts, axis=1)

    def emit(y, out_refs):
        out_refs[0][...] = y.astype(MXU_DTYPE)
        for g in range(D_HEADS // LANES):
            slab_ref[g] = y[:, g * LANES:(g + 1) * LANES]
        for d, ref in zip(RESIDUE_DILATIONS, out_refs[1:]):
            rows = y.shape[0] // d
            for r in range(d):
                for g in range(D_HEADS // LANES):
                    ref[0, r, :, g * LANES:(g + 1) * LANES] = (
                        slab_ref[g, pl.ds(r, rows, stride=d), :].astype(MXU_DTYPE))

    emit(rope(mm(_P_QA, D_HEADS)) * ATTN_SCALE, qa_refs)
    emit(rope(mm(_P_KA, D_HEADS)), ka_refs)
    emit(mm(_P_VA, D_HEADS), va_refs)
    qb_ref[...] = (rope(mm(_P_QB, D_HEADS)) * (ATTN_SCALE * LOG2_E)).astype(MXU_DTYPE)
    qi_ref[...] = rope(mm(_P_QI, D_HEADS)).astype(MXU_DTYPE)
    kbd_ref[...] = rope(mm(_P_KBD, LANES)).astype(MXU_DTYPE)
    kid_ref[...] = rope(mm(_P_KID, LANES)).astype(MXU_DTYPE)
    misc_ref[...] = mm(_P_MISC, LANES)


def _project(x2, g, w_pack, cos_t, sin_t, seq):
    n = x2.shape[0]
    tm = ROW_TILE
    tiles_per_seq = seq // tm
    row = lambda i: (i, 0)
    const = lambda i: (0, 0)
    pos = lambda i: (i % tiles_per_seq, 0)
    batch = n // seq
    wide = jax.ShapeDtypeStruct((n, D_HEADS), MXU_DTYPE)
    narrow = jax.ShapeDtypeStruct((n, LANES), MXU_DTYPE)
    wide_spec = pl.BlockSpec((tm, D_HEADS), row)
    lay_shapes = [wide] + [jax.ShapeDtypeStruct((batch, d, seq // d, D_HEADS), MXU_DTYPE) for d in RESIDUE_DILATIONS]
    lay_specs = [wide_spec] + [
        pl.BlockSpec((1, d, tm // d, D_HEADS), lambda i: (i // tiles_per_seq, 0, i % tiles_per_seq, 0))
        for d in RESIDUE_DILATIONS]
    n_lay = len(lay_shapes)
    outs = pl.pallas_call(
        _proj_kernel,
        grid=(n // tm,),
        in_specs=[
            pl.BlockSpec((tm, D_MODEL), row),
            pl.BlockSpec((1, D_MODEL), const),
            pl.BlockSpec((D_MODEL, _P_TOTAL), const),
            pl.BlockSpec((tm, LANES), pos),
            pl.BlockSpec((tm, LANES), pos),
        ],
        out_specs=lay_specs * 3 + [wide_spec] * 2 + [pl.BlockSpec((tm, LANES), row)] * 3,
        out_shape=lay_shapes * 3 + [wide] * 2 + [narrow, narrow, jax.ShapeDtypeStruct((n, LANES), jnp.float32)],
        scratch_shapes=[pltpu.VMEM((tm, D_MODEL), MXU_DTYPE),
                        pltpu.VMEM((D_HEADS // LANES, tm, LANES), jnp.float32)],
        compiler_params=pltpu.CompilerParams(dimension_semantics=("arbitrary",), vmem_limit_bytes=VMEM_LIMIT),
        name="proj_rope",
    )(x2, g, w_pack, cos_t, sin_t)
    return (outs[:n_lay], outs[n_lay:2 * n_lay], outs[2 * n_lay:3 * n_lay]) + tuple(outs[3 * n_lay:])


def _dil_kernel(q_ref, k_ref, v_ref, kp_ref, vp_ref, o_ref, lse_ref):
    n = pl.program_id(2)
    qi = lax.broadcasted_iota(jnp.int32, (BLOCK, 2 * BLOCK), 0)
    kj = lax.broadcasted_iota(jnp.int32, (BLOCK, 2 * BLOCK), 1)
    band = (kj >= qi) & (kj <= qi + BLOCK) & ((kj >= BLOCK) | (n > 0))
    lane = lax.broadcasted_iota(jnp.int32, (1, LANES), 1)
    first_v = lane < HEAD_DIM
    for p in range(N_HEADS // 2):
        cols = slice(p * LANES, (p + 1) * LANES)
        qp = q_ref[0, 0, :, cols]
        k2 = jnp.concatenate([kp_ref[0, 0, :, cols], k_ref[0, 0, :, cols]], axis=0)
        v2 = jnp.concatenate([vp_ref[0, 0, :, cols], v_ref[0, 0, :, cols]], axis=0)
        outs, lses = [], []
        for hh in range(2):
            head_lanes = ((lane // HALF) % 2) == hh
            qm = jnp.where(head_lanes, qp, jnp.zeros_like(qp))
            s = jnp.where(band, _nt_dot(qm, k2), -jnp.inf)
            m = jnp.max(s, axis=-1, keepdims=True)
            e = jnp.exp(s - m)
            den = jnp.sum(e, axis=-1, keepdims=True)
            o = jnp.dot(e.astype(MXU_DTYPE), v2, preferred_element_type=jnp.float32) / den
            outs.append(o)
            lses.append(m + jnp.log(den))
        o_ref[0, 0, :, cols] = jnp.where(first_v, outs[0], outs[1]).astype(o_ref.dtype)
        lse_ref[0, 0, :, cols] = jnp.where(first_v, lses[0], lses[1])


def _dilated(q, k, v, dilation):
    batch, _, m_len, _ = q.shape
    nb = m_len // BLOCK
    cur = lambda b, r, n: (b, r, n, 0)
    prev = lambda b, r, n: (b, r, jnp.maximum(n - 1, 0), 0)
    blk = (1, 1, BLOCK, D_HEADS)
    return pl.pallas_call(
        _dil_kernel,
        grid=(batch, dilation, nb),
        in_specs=[pl.BlockSpec(blk, cur), pl.BlockSpec(blk, cur), pl.BlockSpec(blk, cur),
                  pl.BlockSpec(blk, prev), pl.BlockSpec(blk, prev)],
        out_specs=[pl.BlockSpec(blk, cur), pl.BlockSpec(blk, cur)],
        out_shape=[jax.ShapeDtypeStruct(q.shape, MXU_DTYPE), jax.ShapeDtypeStruct(q.shape, jnp.float32)],
        compiler_params=pltpu.CompilerParams(dimension_semantics=("arbitrary",) * 3),
        name=f"dilated_d{dilation}",
    )(q, k, v, k, v)


def _key_to_f32(key):
    bits = key ^ ((key >> 31) & jnp.int32(0x7FFFFFFF))
    return lax.bitcast_convert_type(bits, jnp.float32)


def _fold_rows(x, op):
    rows = x.shape[0]
    y = op(x.reshape(rows // 64, 64, LANES), axis=0)
    return op(y.reshape(8, 8, LANES), axis=0)


def _sparse_kernel(qi_ref, qb_ref, wt_ref, kid_ref, kbd_ref, vbt_ref, out_ref,
                   sc_ref, mb_ref, qis_ref, qbs_ref, acc_ref, m_ref, sqa_ref, sqb_ref, mxa_ref, mxb_ref,
                   *, topk, index_bits):
    i = pl.program_id(1)
    nch = i // (KEY_CHUNK // BLOCK) + 1
    ncc = i // (COUNT_CHUNK // BLOCK) + 1
    lane = lax.broadcasted_iota(jnp.int32, (1, LANES), 1)
    t_idx = i * BLOCK + lane
    row_iota = lax.broadcasted_iota(jnp.int32, (KEY_CHUNK, LANES), 0)
    crow_iota = lax.broadcasted_iota(jnp.int32, (COUNT_CHUNK, LANES), 0)
    neg_inf = jnp.float32(-jnp.inf)
    f32_lowest = jnp.float32(jnp.finfo(jnp.float32).min)

    eye = jnp.where(lax.broadcasted_iota(jnp.int32, (BLOCK, LANES), 0) == lane, 1.0, 0.0).astype(MXU_DTYPE)
    for h in range(N_HEADS):
        cols = slice((h // 2) * LANES, (h // 2 + 1) * LANES)
        head_lanes = ((lane // HALF) % 2) == (h % 2)
        rows = slice(h * BLOCK, (h + 1) * BLOCK)
        qis_ref[rows, :] = jnp.where(head_lanes, qi_ref[0, :, cols], jnp.zeros((), MXU_DTYPE))
        qbs_ref[rows, :LANES] = jnp.where(head_lanes, qb_ref[0, :, cols], jnp.zeros((), MXU_DTYPE))
        qbs_ref[rows, LANES:] = eye

    def chunk_start(c):
        return pl.multiple_of(c * KEY_CHUNK, KEY_CHUNK)

    def pair_q(ref, p):
        return ref[p * 2 * BLOCK:(p + 1) * 2 * BLOCK, :]

    buf_a, buf_b = (sqa_ref, mxa_ref), (sqb_ref, mxb_ref)

    def pipeline(produce, consume):
        produce(0, buf_a)

        def two_chunks(j, carry):
            c = 2 * j
            produce(c + 1, buf_b)
            consume(c, buf_a)
            produce(c + 2, buf_a)
            consume(c + 1, buf_b)
            return carry

        lax.fori_loop(0, (nch - 1) // 2, two_chunks, 0)
        last = nch - 1

        @pl.when(last % 2 == 1)
        def _():
            produce(last, buf_b)
            consume(last - 1, buf_a)
            consume(last, buf_b)

        @pl.when(last % 2 == 0)
        def _():
            consume(last, buf_a)

    def index_dots(c, buf):
        kc = kid_ref[0, pl.ds(chunk_start(c), KEY_CHUNK), :]
        for p in range(N_HEADS // 2):
            buf[0][:, p * 2 * BLOCK:(p + 1) * 2 * BLOCK] = _nt_dot(kc, pair_q(qis_ref, p))

    def index_scores(c, buf):
        r0 = chunk_start(c)
        acc = jnp.zeros((KEY_CHUNK, LANES), jnp.float32)
        for h in range(N_HEADS):
            acc = acc + jnp.maximum(buf[0][:, h * BLOCK:(h + 1) * BLOCK], 0.0) * wt_ref[0, h:h + 1, :]
        sc_ref[pl.ds(r0, KEY_CHUNK), :] = jnp.where(r0 + row_iota <= t_idx, acc * IDX_SCALE, neg_inf)

    pipeline(index_dots, index_scores)

    def count(pred):
        def body(c, acc):
            r0 = pl.multiple_of(c * COUNT_CHUNK, COUNT_CHUNK)
            hit = jnp.where(pred(sc_ref[pl.ds(r0, COUNT_CHUNK), :], r0), 1.0, 0.0)
            return acc + jnp.sum(hit.reshape(8, COUNT_CHUNK // 8, LANES), axis=0)
        acc = lax.fori_loop(0, ncc, body, jnp.zeros((COUNT_CHUNK // 8, LANES), jnp.float32))
        return jnp.sum(acc, axis=0, keepdims=True)

    def bit_step(b, carry):
        res, cnt_res = carry
        cand = res | (jnp.int32(1) << (31 - b))
        thr_c = _key_to_f32(cand ^ jnp.int32(INT_MIN))
        cnt = count(lambda x, r0: x >= thr_c)
        take = cnt >= topk
        return jnp.where(take, cand, res), jnp.where(take, cnt, cnt_res)

    def pending(cnt_res):
        settled = (cnt_res == topk) | (t_idx < topk)
        return jnp.max(jnp.where(settled, 0.0, 1.0))

    res, cnt_ge = lax.fori_loop(0, UNCONDITIONAL_BITS, bit_step, (jnp.zeros((1, LANES), jnp.int32),
                                                                  jnp.zeros((1, LANES), jnp.float32)))

    def more_bits(carry):
        b, res, cnt_res, _ = carry
        res, cnt_res = bit_step(b + 1, bit_step(b, (res, cnt_res)))
        return b + 2, res, cnt_res, pending(cnt_res)

    _, res, cnt_ge, _ = lax.while_loop(lambda c: (c[0] < 32) & (c[3] > 0.0), more_bits,
                                       (jnp.int32(UNCONDITIONAL_BITS), res, cnt_ge, pending(cnt_ge)))
    thr_raw = _key_to_f32(res ^ jnp.int32(INT_MIN))
    enough = thr_raw > f32_lowest
    thr = jnp.where(enough, thr_raw, f32_lowest)

    tie_lane = enough & (cnt_ge > topk)

    @pl.when(jnp.max(jnp.where(tie_lane, 1.0, 0.0)) > 0.0)
    def _():
        need = topk - count(lambda x, r0: x > thr)

        def idx_step(b, lim):
            cand = lim | (jnp.int32(1) << (index_bits - 1 - b))
            below = count(lambda x, r0: (x == thr) & (r0 + crow_iota < cand))
            return jnp.where(below < need, cand, lim)

        lim = lax.fori_loop(0, index_bits, idx_step, jnp.zeros((1, LANES), jnp.int32))

        def demote(c, carry):
            r0 = pl.multiple_of(c * COUNT_CHUNK, COUNT_CHUNK)
            x = sc_ref[pl.ds(r0, COUNT_CHUNK), :]
            drop = (x == thr) & (r0 + crow_iota > lim) & tie_lane
            sc_ref[pl.ds(r0, COUNT_CHUNK), :] = jnp.where(drop, neg_inf, x)
            return carry

        lax.fori_loop(0, ncc, demote, 0)

    mask_off = float(jnp.finfo(MXU_DTYPE).min)

    def build_mask(c, carry):
        r0 = chunk_start(c)
        selected = sc_ref[pl.ds(r0, KEY_CHUNK), :] >= thr
        mb_ref[pl.ds(r0, KEY_CHUNK), :] = jnp.where(selected, 0.0, mask_off).astype(MXU_DTYPE)
        return carry

    lax.fori_loop(0, nch, build_mask, 0)

    def attn_scores(c, buf):
        sq_ref, mx_ref = buf
        r0 = chunk_start(c)
        kaug = jnp.concatenate([kbd_ref[0, pl.ds(r0, KEY_CHUNK), :], mb_ref[pl.ds(r0, KEY_CHUNK), :]], axis=1)
        for p in range(N_HEADS // 2):
            s = _nt_dot(kaug, pair_q(qbs_ref, p))
            sq_ref[:, p * 2 * BLOCK:(p + 1) * 2 * BLOCK] = s
            for hh in range(2):
                cols = slice((2 * p + hh) * BLOCK, (2 * p + hh + 1) * BLOCK)
                mx_ref[:, cols] = _fold_rows(s[:, hh * BLOCK:(hh + 1) * BLOCK], jnp.max)

    def accumulate(c, buf):
        sq_ref, mx_ref = buf
        m_run = m_ref[...]
        m_new = jnp.maximum(m_run, jnp.max(mx_ref[...], axis=0, keepdims=True))
        m_ref[...] = m_new
        acc_ref[...] = acc_ref[...] * jnp.exp2(m_run - m_new)
        vt = vbt_ref[0, c]
        for p in range(N_HEADS // 2):
            pcols = slice(p * 2 * BLOCK, (p + 1) * 2 * BLOCK)
            e = jnp.exp2(sq_ref[:, pcols] - m_new[:, pcols])
            acc_ref[:, pcols] += jnp.dot(vt, e.astype(MXU_DTYPE), preferred_element_type=jnp.float32)

    acc_ref[...] = jnp.zeros(acc_ref.shape, jnp.float32)
    m_ref[...] = jnp.full(m_ref.shape, f32_lowest, jnp.float32)
    pipeline(attn_scores, accumulate)

    o_t = acc_ref[:HEAD_DIM, :] / acc_ref[HEAD_DIM:HEAD_DIM + 1, :]
    for p in range(N_HEADS // 2):
        pair = jnp.concatenate([o_t[:, (2 * p) * BLOCK:(2 * p + 1) * BLOCK],
                                o_t[:, (2 * p + 1) * BLOCK:(2 * p + 2) * BLOCK]], axis=0)
        out_ref[0, :, p * LANES:(p + 1) * LANES] = pair.T.astype(out_ref.dtype)


def _sparse(qi, qb, wt, kid, kbd, vbt, batch, seq):
    topk = min(TOPK_MAX, seq // 4)
    nq = seq // BLOCK
    qblk = pl.BlockSpec((1, BLOCK, D_HEADS), lambda b, i: (b, i, 0))
    full = pl.BlockSpec((1, seq, LANES), lambda b, i: (b, 0, 0))
    kern = functools.partial(_sparse_kernel, topk=topk, index_bits=max(1, (seq - 1).bit_length()))
    return pl.pallas_call(
        kern,
        grid=(batch, nq),
        in_specs=[qblk, qblk,
                  pl.BlockSpec((1, IDX_HEADS, BLOCK), lambda b, i: (b, 0, i)),
                  full, full,
                  pl.BlockSpec((1, seq // KEY_CHUNK, HEAD_DIM + L_ROWS, KEY_CHUNK), lambda b, i: (b, 0, 0, 0))],
        out_specs=qblk,
        out_shape=jax.ShapeDtypeStruct((batch, seq, D_HEADS), MXU_DTYPE),
        scratch_shapes=[pltpu.VMEM((seq, LANES), jnp.float32),
                        pltpu.VMEM((seq, LANES), MXU_DTYPE),
                        pltpu.VMEM((N_HEADS * BLOCK, LANES), MXU_DTYPE),
                        pltpu.VMEM((N_HEADS * BLOCK, 2 * LANES), MXU_DTYPE),
                        pltpu.VMEM((HEAD_DIM + L_ROWS, N_HEADS * BLOCK), jnp.float32),
                        pltpu.VMEM((1, N_HEADS * BLOCK), jnp.float32),
                        pltpu.VMEM((KEY_CHUNK, N_HEADS * BLOCK), jnp.float32),
                        pltpu.VMEM((KEY_CHUNK, N_HEADS * BLOCK), jnp.float32),
                        pltpu.VMEM((8, N_HEADS * BLOCK), jnp.float32),
                        pltpu.VMEM((8, N_HEADS * BLOCK), jnp.float32)],
        compiler_params=pltpu.CompilerParams(dimension_semantics=("arbitrary", "arbitrary"),
                                             vmem_limit_bytes=VMEM_LIMIT),
        name="indexer_sparse_attn",
    )(qi, qb, wt, kid, kbd, vbt)


def _merge_kernel(x_ref, g_ref, *refs):
    n_pat = len(DILATED_PATTERNS)
    o_refs, l_refs = refs[:n_pat], refs[n_pat:2 * n_pat]
    yb_ref, wg_ref, wua_ref, wub_ref, wo_ref, x1_ref = refs[2 * n_pat:2 * n_pat + 6]
    slabs = refs[2 * n_pat + 6:]
    x = x_ref[...]
    h = _rms(x, g_ref[...]).astype(MXU_DTYPE)
    tm = x.shape[0]
    n_grp = D_HEADS // LANES

    o_src, l_src, k = [], [], 0
    for (_, d), o_ref, l_ref in zip(DILATED_PATTERNS, o_refs, l_refs):
        if d == 1:
            o_src.append(lambda g, r=o_ref: r[:, g * LANES:(g + 1) * LANES].astype(jnp.float32))
            l_src.append(lambda g, r=l_ref: r[:, g * LANES:(g + 1) * LANES])
            continue
        o_slab, l_slab = slabs[2 * k], slabs[2 * k + 1]
        k += 1
        for r in range(d):
            for g in range(n_grp):
                cols = slice(g * LANES, (g + 1) * LANES)
                o_slab[g, pl.ds(r, tm // d, stride=d), :] = o_ref[0, r, :, cols].astype(jnp.float32)
                l_slab[g, pl.ds(r, tm // d, stride=d), :] = l_ref[0, r, :, cols]
        o_src.append(lambda g, s=o_slab: s[g])
        l_src.append(lambda g, s=l_slab: s[g])

    parts = []
    for g in range(n_grp):
        ls = [f(g) for f in l_src]
        m = functools.reduce(jnp.maximum, ls)
        es = [jnp.exp(l - m) for l in ls]
        num = functools.reduce(jnp.add, [e * f(g) for e, f in zip(es, o_src)])
        parts.append((num / functools.reduce(jnp.add, es)).astype(MXU_DTYPE))
    ya = jnp.concatenate(parts, axis=1)
    ua = jnp.dot(ya, wua_ref[...], preferred_element_type=jnp.float32)
    ub = jnp.dot(yb_ref[...], wub_ref[...], preferred_element_type=jnp.float32)
    ga = jnp.dot(h, wg_ref[:, :D_MODEL], preferred_element_type=jnp.float32)
    gb = jnp.dot(h, wg_ref[:, D_MODEL:], preferred_element_type=jnp.float32)
    merged = jax.nn.sigmoid(ga) * ua + jax.nn.sigmoid(gb) * ub
    x1_ref[...] = x + jnp.dot(merged.astype(MXU_DTYPE), wo_ref[...], preferred_element_type=jnp.float32)


def _merge(x2, g, os_, lses, yb, wg, wua, wub, wo, seq):
    n = x2.shape[0]
    tm = ROW_TILE
    tiles_per_seq = seq // tm
    row = lambda i: (i, 0)
    const = lambda i: (0, 0)
    half = pl.BlockSpec((tm, D_HEADS), row)
    pat_specs = [half if d == 1 else
                 pl.BlockSpec((1, d, tm // d, D_HEADS), lambda i: (i // tiles_per_seq, 0, i % tiles_per_seq, 0))
                 for _, d in DILATED_PATTERNS]
    return pl.pallas_call(
        _merge_kernel,
        grid=(n // tm,),
        in_specs=[pl.BlockSpec((tm, D_MODEL), row), pl.BlockSpec((1, D_MODEL), const)] + pat_specs * 2 + [half] + [
            pl.BlockSpec((D_MODEL, 2 * D_MODEL), const),
            pl.BlockSpec((D_HEADS, D_MODEL), const),
            pl.BlockSpec((D_HEADS, D_MODEL), const),
            pl.BlockSpec((D_MODEL, D_MODEL), const)],
        out_specs=pl.BlockSpec((tm, D_MODEL), row),
        out_shape=jax.ShapeDtypeStruct((n, D_MODEL), jnp.float32),
        scratch_shapes=[pltpu.VMEM((D_HEADS // LANES, tm, LANES), jnp.float32)] * (2 * len(RESIDUE_DILATIONS)),
        compiler_params=pltpu.CompilerParams(dimension_semantics=("arbitrary",), vmem_limit_bytes=VMEM_LIMIT),
        name="mix_gate_out",
    )(x2, g, *os_, *lses, yb, wg, wua, wub, wo)


def _ffn_kernel(x_ref, g_ref, wgate_ref, wup_ref, wdown_ref, gf_ref, out_ref):
    x = x_ref[...]
    h = _rms(x, g_ref[...]).astype(MXU_DTYPE)
    y = x
    for c0 in range(0, D_FF, FF_CHUNK):
        a = jnp.dot(h, wgate_ref[:, c0:c0 + FF_CHUNK], preferred_element_type=jnp.float32)
        u = jnp.dot(h, wup_ref[:, c0:c0 + FF_CHUNK], preferred_element_type=jnp.float32)
        act = (a * jax.nn.sigmoid(a) * u).astype(MXU_DTYPE)
        y = y + jnp.dot(act, wdown_ref[c0:c0 + FF_CHUNK, :], preferred_element_type=jnp.float32)
    out_ref[...] = _rms(y, gf_ref[...])


def _ffn(x1, g, wgate, wup, wdown, gf):
    n = x1.shape[0]
    tm = ROW_TILE
    row = lambda i: (i, 0)
    const = lambda i: (0, 0)
    return pl.pallas_call(
        _ffn_kernel,
        grid=(n // tm,),
        in_specs=[pl.BlockSpec((tm, D_MODEL), row), pl.BlockSpec((1, D_MODEL), const),
                  pl.BlockSpec((D_MODEL, D_FF), const), pl.BlockSpec((D_MODEL, D_FF), const),
                  pl.BlockSpec((D_FF, D_MODEL), const), pl.BlockSpec((1, D_MODEL), const)],
        out_specs=pl.BlockSpec((tm, D_MODEL), row),
        out_shape=jax.ShapeDtypeStruct((n, D_MODEL), jnp.float32),
        compiler_params=pltpu.CompilerParams(dimension_semantics=("arbitrary",), vmem_limit_bytes=VMEM_LIMIT),
        name="ffn_norm",
    )(x1, g, wgate, wup, wdown, gf)


def _rope_tables(seq):
    inv_freq = ROPE_THETA ** (-jnp.arange(HALF, dtype=jnp.float32) / HALF)
    ang = jnp.arange(seq, dtype=jnp.int32).astype(jnp.float32)[:, None] * inv_freq[None, :]
    cos, sin = jnp.cos(ang), jnp.sin(ang)
    return jnp.tile(cos, (1, 4)), jnp.concatenate([-sin, -sin, sin, sin], axis=1)


def kernel(x, norm_mix, w_in, w_up_a, w_up_b, w_out, norm_ffn, w_gate, w_up, w_down, norm_final):
    batch, seq, _ = x.shape
    assert seq % max(d * BLOCK for _, d in DILATED_PATTERNS) == 0 and seq % KEY_CHUNK == 0
    assert all(w // d == BLOCK for w, d in DILATED_PATTERNS)
    n = batch * seq
    bf = MXU_DTYPE
    xf = x.reshape(n, D_MODEL)
    cos_t, sin_t = _rope_tables(seq)
    for layer in range(w_in.shape[0]):
        w = w_in[layer]
        w_pack = jnp.pad(w[:, _packed_columns()], ((0, 0), (0, _P_TOTAL - _P_MISC - HEAD_DIM - IDX_HEADS))).astype(bf)
        w_gates = w[:, _GA:].astype(bf)
        qas, kas, vas, qb, qi, kbd, kid, misc = _project(xf, norm_mix[layer][None], w_pack, cos_t, sin_t, seq)

        dil = []
        for (_, d), q, k, v in zip(DILATED_PATTERNS, qas, kas, vas):
            if d == 1:
                o, lse = _dilated(*(z.reshape(batch, 1, seq, D_HEADS) for z in (q, k, v)), d)
                dil.append((o.reshape(n, D_HEADS), lse.reshape(n, D_HEADS)))
            else:
                dil.append(_dilated(q, k, v, d))

        ones_col = (jnp.arange(L_ROWS) == 0).astype(bf)[None, :]
        vb = jnp.concatenate([misc[:, :HEAD_DIM].astype(bf), jnp.broadcast_to(ones_col, (n, L_ROWS))], axis=1)
        vbt = jnp.swapaxes(vb.reshape(batch, seq // KEY_CHUNK, KEY_CHUNK, HEAD_DIM + L_ROWS), 2, 3)
        wt = jnp.swapaxes(misc[:, HEAD_DIM:HEAD_DIM + IDX_HEADS].reshape(batch, seq, IDX_HEADS), 1, 2)
        r3 = lambda z: z.reshape(batch, seq, z.shape[-1])
        yb = _sparse(r3(qi), r3(qb), wt, r3(kid), r3(kbd), vbt, batch, seq).reshape(n, D_HEADS)

        x1 = _merge(xf, norm_mix[layer][None], [o for o, _ in dil], [l for _, l in dil], yb,
                    w_gates, w_up_a[layer].astype(bf), w_up_b[layer].astype(bf), w_out[layer].astype(bf), seq)
        last = layer == w_in.shape[0] - 1
        assert last, "the final norm is fused into the FFN kernel of the last layer"
        xf = _ffn(x1, norm_ffn[layer][None], w_gate[layer].astype(bf), w_up[layer].astype(bf),
                  w_down[layer].astype(bf), norm_final[None])
    return xf.reshape(batch, seq, D_MODEL)
```

```python
import functools

import numpy as np
import jax
import jax.numpy as jnp
from jax import lax
from jax.experimental import pallas as pl
from jax.experimental.pallas import tpu as pltpu

D_MODEL = 1024
HEAD_DIM = 64
HALF = HEAD_DIM // 2
N_HEADS = 8
D_HEADS = N_HEADS * HEAD_DIM
IDX_HEADS = 8
DILATED_PATTERNS = ((128, 1), (512, 4), (2048, 16))
RESIDUE_DILATIONS = tuple(d for _, d in DILATED_PATTERNS if d > 1)
TOPK_MAX = 256
D_FF = 2816
ROPE_THETA = 10000.0
RMS_EPS = 1e-6
BLOCK = 128
ATTN_SCALE = HEAD_DIM ** -0.5
IDX_SCALE = (HEAD_DIM ** -0.5) * (IDX_HEADS ** -0.5)
LOG2_E = float(np.log2(np.e))

LANES = 128
INT_MIN = -(2 ** 31)
KEY_CHUNK = 1024
COUNT_CHUNK = 512
UNCONDITIONAL_BITS = 20
L_ROWS = 8
ROW_TILE = 512
FF_CHUNK = 1408
VMEM_LIMIT = 56 * 1024 * 1024
MXU_DTYPE = jnp.bfloat16

_SPLITS = (D_HEADS, D_HEADS, D_HEADS, D_HEADS, HEAD_DIM, HEAD_DIM, IDX_HEADS * HEAD_DIM, HEAD_DIM, IDX_HEADS,
           D_MODEL, D_MODEL)
_OFF = np.concatenate([[0], np.cumsum(_SPLITS)])
(_QA, _KA, _VA, _QB, _KB, _VB, _QI, _KI, _WI, _GA, _GB) = (int(o) for o in _OFF[:-1])

_P_QA, _P_KA, _P_VA, _P_QB, _P_QI = 0, 512, 1024, 1536, 2048
_P_KBD, _P_KID, _P_MISC = 2560, 2688, 2816
_P_TOTAL = 2944


def _pair_perm():
    idx = np.empty(D_HEADS, np.int64)
    for j in range(D_HEADS):
        g, l = divmod(j, LANES)
        quarter, e = divmod(l, HALF)
        head = 2 * g + (quarter % 2)
        idx[j] = head * HEAD_DIM + (quarter // 2) * HALF + e
    return idx


def _dup_perm():
    idx = np.empty(LANES, np.int64)
    for l in range(LANES):
        quarter, e = divmod(l, HALF)
        idx[l] = (quarter // 2) * HALF + e
    return idx


def _packed_columns():
    pp, dp = _pair_perm(), _dup_perm()
    return np.concatenate([
        _QA + pp, _KA + pp, _VA + np.arange(D_HEADS), _QB + pp, _QI + pp,
        _KB + dp, _KI + dp, _VB + np.arange(HEAD_DIM), _WI + np.arange(IDX_HEADS)])


def _rms(x, g):
    ms = jnp.mean(x * x, axis=-1, keepdims=True)
    return x * lax.rsqrt(ms + RMS_EPS) * g


def _nt_dot(a, b):
    return lax.dot_general(a, b, (((1,), (1,)), ((), ())), preferred_element_type=jnp.float32)


def _proj_kernel(x_ref, g_ref, w_ref, cos_ref, sin_ref, *refs):
    n_lay = 1 + len(RESIDUE_DILATIONS)
    qa_refs, ka_refs, va_refs = refs[:n_lay], refs[n_lay:2 * n_lay], refs[2 * n_lay:3 * n_lay]
    qb_ref, qi_ref, kbd_ref, kid_ref, misc_ref, h_ref, slab_ref = refs[3 * n_lay:]
    h_ref[...] = _rms(x_ref[...], g_ref[...]).astype(MXU_DTYPE)
    cos = cos_ref[...]
    sin = sin_ref[...]

    def mm(c0, width):
        return jnp.dot(h_ref[...], w_ref[:, c0:c0 + width], preferred_element_type=jnp.float32)

    def rope(z):
        parts = []
        for g in range(z.shape[1] // LANES):
            zg = z[:, g * LANES:(g + 1) * LANES]
            parts.append(zg * cos + pltpu.roll(zg, 2 * HALF, axis=1) * sin)
        return parts[0] if len(parts) == 1 else jnp.concatenate(parts, axis=1)

    def emit(y, out_refs):
        out_refs[0][...] = y.astype(MXU_DTYPE)
        for g in range(D_HEADS // LANES):
            slab_ref[g] = y[:, g * LANES:(g + 1) * LANES]
        for d, ref in zip(RESIDUE_DILATIONS, out_refs[1:]):
            rows = y.shape[0] // d
            for r in range(d):
                for g in range(D_HEADS // LANES):
                    ref[0, r, :, g * LANES:(g + 1) * LANES] = (
                        slab_ref[g, pl.ds(r, rows, stride=d), :].astype(MXU_DTYPE))

    emit(rope(mm(_P_QA, D_HEADS)) * ATTN_SCALE, qa_refs)
    emit(rope(mm(_P_KA, D_HEADS)), ka_refs)
    emit(mm(_P_VA, D_HEADS), va_refs)
    qb_ref[...] = (rope(mm(_P_QB, D_HEADS)) * (ATTN_SCALE * LOG2_E)).astype(MXU_DTYPE)
    qi_ref[...] = rope(mm(_P_QI, D_HEADS)).astype(MXU_DTYPE)
    kbd_ref[...] = rope(mm(_P_KBD, LANES)).astype(MXU_DTYPE)
    kid_ref[...] = rope(mm(_P_KID, LANES)).astype(MXU_DTYPE)
    misc_ref[...] = mm(_P_MISC, LANES)


def _project(x2, g, w_pack, cos_t, sin_t, seq):
    n = x2.shape[0]
    tm = ROW_TILE
    tiles_per_seq = seq // tm
    row = lambda i: (i, 0)
    const = lambda i: (0, 0)
    pos = lambda i: (i % tiles_per_seq, 0)
    batch = n // seq
    wide = jax.ShapeDtypeStruct((n, D_HEADS), MXU_DTYPE)
    narrow = jax.ShapeDtypeStruct((n, LANES), MXU_DTYPE)
    wide_spec = pl.BlockSpec((tm, D_HEADS), row)
    lay_shapes = [wide] + [jax.ShapeDtypeStruct((batch, d, seq // d, D_HEADS), MXU_DTYPE) for d in RESIDUE_DILATIONS]
    lay_specs = [wide_spec] + [
        pl.BlockSpec((1, d, tm // d, D_HEADS), lambda i: (i // tiles_per_seq, 0, i % tiles_per_seq, 0))
        for d in RESIDUE_DILATIONS]
    n_lay = len(lay_shapes)
    outs = pl.pallas_call(
        _proj_kernel,
        grid=(n // tm,),
        in_specs=[
            pl.BlockSpec((tm, D_MODEL), row),
            pl.BlockSpec((1, D_MODEL), const),
            pl.BlockSpec((D_MODEL, _P_TOTAL), const),
            pl.BlockSpec((tm, LANES), pos),
            pl.BlockSpec((tm, LANES), pos),
        ],
        out_specs=lay_specs * 3 + [wide_spec] * 2 + [pl.BlockSpec((tm, LANES), row)] * 3,
        out_shape=lay_shapes * 3 + [wide] * 2 + [narrow, narrow, jax.ShapeDtypeStruct((n, LANES), jnp.float32)],
        scratch_shapes=[pltpu.VMEM((tm, D_MODEL), MXU_DTYPE),
                        pltpu.VMEM((D_HEADS // LANES, tm, LANES), jnp.float32)],
        compiler_params=pltpu.CompilerParams(dimension_semantics=("arbitrary",), vmem_limit_bytes=VMEM_LIMIT),
        name="proj_rope",
    )(x2, g, w_pack, cos_t, sin_t)
    return (outs[:n_lay], outs[n_lay:2 * n_lay], outs[2 * n_lay:3 * n_lay]) + tuple(outs[3 * n_lay:])


def _dil_kernel(q_ref, k_ref, v_ref, kp_ref, vp_ref, o_ref, lse_ref):
    n = pl.program_id(2)
    qi = lax.broadcasted_iota(jnp.int32, (BLOCK, 2 * BLOCK), 0)
    kj = lax.broadcasted_iota(jnp.int32, (BLOCK, 2 * BLOCK), 1)
    band = (kj >= qi) & (kj <= qi + BLOCK) & ((kj >= BLOCK) | (n > 0))
    lane = lax.broadcasted_iota(jnp.int32, (1, LANES), 1)
    first_v = lane < HEAD_DIM
    for p in range(N_HEADS // 2):
        cols = slice(p * LANES, (p + 1) * LANES)
        qp = q_ref[0, 0, :, cols]
        k2 = jnp.concatenate([kp_ref[0, 0, :, cols], k_ref[0, 0, :, cols]], axis=0)
        v2 = jnp.concatenate([vp_ref[0, 0, :, cols], v_ref[0, 0, :, cols]], axis=0)
        outs, lses = [], []
        for hh in range(2):
            head_lanes = ((lane // HALF) % 2) == hh
            qm = jnp.where(head_lanes, qp, jnp.zeros_like(qp))
            s = jnp.where(band, _nt_dot(qm, k2), -jnp.inf)
            m = jnp.max(s, axis=-1, keepdims=True)
            e = jnp.exp(s - m)
            den = jnp.sum(e, axis=-1, keepdims=True)
            o = jnp.dot(e.astype(MXU_DTYPE), v2, preferred_element_type=jnp.float32) / den
            outs.append(o)
            lses.append(m + jnp.log(den))
        o_ref[0, 0, :, cols] = jnp.where(first_v, outs[0], outs[1]).astype(o_ref.dtype)
        lse_ref[0, 0, :, cols] = jnp.where(first_v, lses[0], lses[1])


def _dilated(q, k, v, dilation):
    batch, _, m_len, _ = q.shape
    nb = m_len // BLOCK
    cur = lambda b, r, n: (b, r, n, 0)
    prev = lambda b, r, n: (b, r, jnp.maximum(n - 1, 0), 0)
    blk = (1, 1, BLOCK, D_HEADS)
    return pl.pallas_call(
        _dil_kernel,
        grid=(batch, dilation, nb),
        in_specs=[pl.BlockSpec(blk, cur), pl.BlockSpec(blk, cur), pl.BlockSpec(blk, cur),
                  pl.BlockSpec(blk, prev), pl.BlockSpec(blk, prev)],
        out_specs=[pl.BlockSpec(blk, cur), pl.BlockSpec(blk, cur)],
        out_shape=[jax.ShapeDtypeStruct(q.shape, MXU_DTYPE), jax.ShapeDtypeStruct(q.shape, jnp.float32)],
        compiler_params=pltpu.CompilerParams(dimension_semantics=("arbitrary",) * 3),
        name=f"dilated_d{dilation}",
    )(q, k, v, k, v)


def _key_to_f32(key):
    bits = key ^ ((key >> 31) & jnp.int32(0x7FFFFFFF))
    return lax.bitcast_convert_type(bits, jnp.float32)


def _fold_rows(x, op):
    rows = x.shape[0]
    y = op(x.reshape(rows // 64, 64, LANES), axis=0)
    return op(y.reshape(8, 8, LANES), axis=0)


def _sparse_kernel(qi_ref, qb_ref, wt_ref, kid_ref, kbd_ref, vbt_ref, out_ref,
                   sc_ref, mb_ref, qis_ref, qbs_ref, acc_ref, m_ref, sqa_ref, sqb_ref, mxa_ref, mxb_ref,
                   *, topk, index_bits):
    i = pl.program_id(1)
    nch = i // (KEY_CHUNK // BLOCK) + 1
    ncc = i // (COUNT_CHUNK // BLOCK) + 1
    lane = lax.broadcasted_iota(jnp.int32, (1, LANES), 1)
    t_idx = i * BLOCK + lane
    row_iota = lax.broadcasted_iota(jnp.int32, (KEY_CHUNK, LANES), 0)
    crow_iota = lax.broadcasted_iota(jnp.int32, (COUNT_CHUNK, LANES), 0)
    neg_inf = jnp.float32(-jnp.inf)
    f32_lowest = jnp.float32(jnp.finfo(jnp.float32).min)

    eye = jnp.where(lax.broadcasted_iota(jnp.int32, (BLOCK, LANES), 0) == lane, 1.0, 0.0).astype(MXU_DTYPE)
    for h in range(N_HEADS):
        cols = slice((h // 2) * LANES, (h // 2 + 1) * LANES)
        head_lanes = ((lane // HALF) % 2) == (h % 2)
        rows = slice(h * BLOCK, (h + 1) * BLOCK)
        qis_ref[rows, :] = jnp.where(head_lanes, qi_ref[0, :, cols], jnp.zeros((), MXU_DTYPE))
        qbs_ref[rows, :LANES] = jnp.where(head_lanes, qb_ref[0, :, cols], jnp.zeros((), MXU_DTYPE))
        qbs_ref[rows, LANES:] = eye

    def chunk_start(c):
        return pl.multiple_of(c * KEY_CHUNK, KEY_CHUNK)

    def pair_q(ref, p):
        return ref[p * 2 * BLOCK:(p + 1) * 2 * BLOCK, :]

    buf_a, buf_b = (sqa_ref, mxa_ref), (sqb_ref, mxb_ref)

    def pipeline(produce, consume):
        produce(0, buf_a)

        def two_chunks(j, carry):
            c = 2 * j
            produce(c + 1, buf_b)
            consume(c, buf_a)
            produce(c + 2, buf_a)
            consume(c + 1, buf_b)
            return carry

        lax.fori_loop(0, (nch - 1) // 2, two_chunks, 0)
        last = nch - 1

        @pl.when(last % 2 == 1)
        def _():
            produce(last, buf_b)
            consume(last - 1, buf_a)
            consume(last, buf_b)

        @pl.when(last % 2 == 0)
        def _():
            consume(last, buf_a)

    def score_chunk(c, carry):
        r0 = chunk_start(c)
        kc = kid_ref[0, pl.ds(r0, KEY_CHUNK), :]
        acc = jnp.zeros((KEY_CHUNK, LANES), jnp.float32)
        for p in range(N_HEADS // 2):
            d = _nt_dot(kc, pair_q(qis_ref, p))
            for hh in range(2):
                h = 2 * p + hh
                acc = acc + jnp.maximum(d[:, hh * BLOCK:(hh + 1) * BLOCK], 0.0) * wt_ref[0, h:h + 1, :]
        sc_ref[pl.ds(r0, KEY_CHUNK), :] = jnp.where(r0 + row_iota <= t_idx, acc * IDX_SCALE, neg_inf)
        return carry

    lax.fori_loop(0, nch, score_chunk, 0)

    def count(pred):
        def body(c, acc):
            r0 = pl.multiple_of(c * COUNT_CHUNK, COUNT_CHUNK)
            hit = jnp.where(pred(sc_ref[pl.ds(r0, COUNT_CHUNK), :], r0), 1.0, 0.0)
            return acc + jnp.sum(hit.reshape(8, COUNT_CHUNK // 8, LANES), axis=0)
        acc = lax.fori_loop(0, ncc, body, jnp.zeros((COUNT_CHUNK // 8, LANES), jnp.float32))
        return jnp.sum(acc, axis=0, keepdims=True)

    def bit_step(b, carry):
        res, cnt_res = carry
        cand = res | (jnp.int32(1) << (31 - b))
        thr_c = _key_to_f32(cand ^ jnp.int32(INT_MIN))
        cnt = count(lambda x, r0: x >= thr_c)
        take = cnt >= topk
        return jnp.where(take, cand, res), jnp.where(take, cnt, cnt_res)

    def pending(cnt_res):
        settled = (cnt_res == topk) | (t_idx < topk)
        return jnp.max(jnp.where(settled, 0.0, 1.0))

    res, cnt_ge = lax.fori_loop(0, UNCONDITIONAL_BITS, bit_step, (jnp.zeros((1, LANES), jnp.int32),
                                                                  jnp.zeros((1, LANES), jnp.float32)))

    def more_bits(carry):
        b, res, cnt_res, _ = carry
        res, cnt_res = bit_step(b + 1, bit_step(b, (res, cnt_res)))
        return b + 2, res, cnt_res, pending(cnt_res)

    _, res, cnt_ge, _ = lax.while_loop(lambda c: (c[0] < 32) & (c[3] > 0.0), more_bits,
                                       (jnp.int32(UNCONDITIONAL_BITS), res, cnt_ge, pending(cnt_ge)))
    thr_raw = _key_to_f32(res ^ jnp.int32(INT_MIN))
    enough = thr_raw > f32_lowest
    thr = jnp.where(enough, thr_raw, f32_lowest)

    tie_lane = enough & (cnt_ge > topk)

    @pl.when(jnp.max(jnp.where(tie_lane, 1.0, 0.0)) > 0.0)
    def _():
        need = topk - count(lambda x, r0: x > thr)

        def idx_step(b, lim):
            cand = lim | (jnp.int32(1) << (index_bits - 1 - b))
            below = count(lambda x, r0: (x == thr) & (r0 + crow_iota < cand))
            return jnp.where(below < need, cand, lim)

        lim = lax.fori_loop(0, index_bits, idx_step, jnp.zeros((1, LANES), jnp.int32))

        def demote(c, carry):
            r0 = pl.multiple_of(c * COUNT_CHUNK, COUNT_CHUNK)
            x = sc_ref[pl.ds(r0, COUNT_CHUNK), :]
            drop = (x == thr) & (r0 + crow_iota > lim) & tie_lane
            sc_ref[pl.ds(r0, COUNT_CHUNK), :] = jnp.where(drop, neg_inf, x)
            return carry

        lax.fori_loop(0, ncc, demote, 0)

    mask_off = float(jnp.finfo(MXU_DTYPE).min)

    def build_mask(c, carry):
        r0 = chunk_start(c)
        selected = sc_ref[pl.ds(r0, KEY_CHUNK), :] >= thr
        mb_ref[pl.ds(r0, KEY_CHUNK), :] = jnp.where(selected, 0.0, mask_off).astype(MXU_DTYPE)
        return carry

    lax.fori_loop(0, nch, build_mask, 0)

    def attn_scores(c, buf):
        sq_ref, mx_ref = buf
        r0 = chunk_start(c)
        kaug = jnp.concatenate([kbd_ref[0, pl.ds(r0, KEY_CHUNK), :], mb_ref[pl.ds(r0, KEY_CHUNK), :]], axis=1)
        for p in range(N_HEADS // 2):
            s = _nt_dot(kaug, pair_q(qbs_ref, p))
            sq_ref[:, p * 2 * BLOCK:(p + 1) * 2 * BLOCK] = s
            for hh in range(2):
                cols = slice((2 * p + hh) * BLOCK, (2 * p + hh + 1) * BLOCK)
                mx_ref[:, cols] = _fold_rows(s[:, hh * BLOCK:(hh + 1) * BLOCK], jnp.max)

    def accumulate(c, buf):
        sq_ref, mx_ref = buf
        m_run = m_ref[...]
        m_new = jnp.maximum(m_run, jnp.max(mx_ref[...], axis=0, keepdims=True))
        m_ref[...] = m_new
        acc_ref[...] = acc_ref[...] * jnp.exp2(m_run - m_new)
        vt = vbt_ref[0, c]
        for p in range(N_HEADS // 2):
            pcols = slice(p * 2 * BLOCK, (p + 1) * 2 * BLOCK)
            e = jnp.exp2(sq_ref[:, pcols] - m_new[:, pcols])
            acc_ref[:, pcols] += jnp.dot(vt, e.astype(MXU_DTYPE), preferred_element_type=jnp.float32)

    acc_ref[...] = jnp.zeros(acc_ref.shape, jnp.float32)
    m_ref[...] = jnp.full(m_ref.shape, f32_lowest, jnp.float32)
    pipeline(attn_scores, accumulate)

    o_t = acc_ref[:HEAD_DIM, :] / acc_ref[HEAD_DIM:HEAD_DIM + 1, :]
    for p in range(N_HEADS // 2):
        pair = jnp.concatenate([o_t[:, (2 * p) * BLOCK:(2 * p + 1) * BLOCK],
                                o_t[:, (2 * p + 1) * BLOCK:(2 * p + 2) * BLOCK]], axis=0)
        out_ref[0, :, p * LANES:(p + 1) * LANES] = pair.T.astype(out_ref.dtype)


def _sparse(qi, qb, wt, kid, kbd, vbt, batch, seq):
    topk = min(TOPK_MAX, seq // 4)
    nq = seq // BLOCK
    qblk = pl.BlockSpec((1, BLOCK, D_HEADS), lambda b, i: (b, i, 0))
    full = pl.BlockSpec((1, seq, LANES), lambda b, i: (b, 0, 0))
    kern = functools.partial(_sparse_kernel, topk=topk, index_bits=max(1, (seq - 1).bit_length()))
    return pl.pallas_call(
        kern,
        grid=(batch, nq),
        in_specs=[qblk, qblk,
                  pl.BlockSpec((1, IDX_HEADS, BLOCK), lambda b, i: (b, 0, i)),
                  full, full,
                  pl.BlockSpec((1, seq // KEY_CHUNK, HEAD_DIM + L_ROWS, KEY_CHUNK), lambda b, i: (b, 0, 0, 0))],
        out_specs=qblk,
        out_shape=jax.ShapeDtypeStruct((batch, seq, D_HEADS), MXU_DTYPE),
        scratch_shapes=[pltpu.VMEM((seq, LANES), jnp.float32),
                        pltpu.VMEM((seq, LANES), MXU_DTYPE),
                        pltpu.VMEM((N_HEADS * BLOCK, LANES), MXU_DTYPE),
                        pltpu.VMEM((N_HEADS * BLOCK, 2 * LANES), MXU_DTYPE),
                        pltpu.VMEM((HEAD_DIM + L_ROWS, N_HEADS * BLOCK), jnp.float32),
                        pltpu.VMEM((1, N_HEADS * BLOCK), jnp.float32),
                        pltpu.VMEM((KEY_CHUNK, N_HEADS * BLOCK), jnp.float32),
                        pltpu.VMEM((KEY_CHUNK, N_HEADS * BLOCK), jnp.float32),
                        pltpu.VMEM((8, N_HEADS * BLOCK), jnp.float32),
                        pltpu.VMEM((8, N_HEADS * BLOCK), jnp.float32)],
        compiler_params=pltpu.CompilerParams(dimension_semantics=("arbitrary", "arbitrary"),
                                             vmem_limit_bytes=VMEM_LIMIT),
        name="indexer_sparse_attn",
    )(qi, qb, wt, kid, kbd, vbt)


def _merge_kernel(x_ref, g_ref, *refs):
    n_pat = len(DILATED_PATTERNS)
    o_refs, l_refs = refs[:n_pat], refs[n_pat:2 * n_pat]
    yb_ref, wg_ref, wua_ref, wub_ref, wo_ref, x1_ref = refs[2 * n_pat:2 * n_pat + 6]
    slabs = refs[2 * n_pat + 6:]
    x = x_ref[...]
    h = _rms(x, g_ref[...]).astype(MXU_DTYPE)
    tm = x.shape[0]
    n_grp = D_HEADS // LANES

    o_src, l_src, k = [], [], 0
    for (_, d), o_ref, l_ref in zip(DILATED_PATTERNS, o_refs, l_refs):
        if d == 1:
            o_src.append(lambda g, r=o_ref: r[:, g * LANES:(g + 1) * LANES].astype(jnp.float32))
            l_src.append(lambda g, r=l_ref: r[:, g * LANES:(g + 1) * LANES])
            continue
        o_slab, l_slab = slabs[2 * k], slabs[2 * k + 1]
        k += 1
        for r in range(d):
            for g in range(n_grp):
                cols = slice(g * LANES, (g + 1) * LANES)
                o_slab[g, pl.ds(r, tm // d, stride=d), :] = o_ref[0, r, :, cols].astype(jnp.float32)
                l_slab[g, pl.ds(r, tm // d, stride=d), :] = l_ref[0, r, :, cols]
        o_src.append(lambda g, s=o_slab: s[g])
        l_src.append(lambda g, s=l_slab: s[g])

    parts = []
    for g in range(n_grp):
        ls = [f(g) for f in l_src]
        m = functools.reduce(jnp.maximum, ls)
        es = [jnp.exp(l - m) for l in ls]
        num = functools.reduce(jnp.add, [e * f(g) for e, f in zip(es, o_src)])
        parts.append((num / functools.reduce(jnp.add, es)).astype(MXU_DTYPE))
    ya = jnp.concatenate(parts, axis=1)
    ua = jnp.dot(ya, wua_ref[...], preferred_element_type=jnp.float32)
    ub = jnp.dot(yb_ref[...], wub_ref[...], preferred_element_type=jnp.float32)
    ga = jnp.dot(h, wg_ref[:, :D_MODEL], preferred_element_type=jnp.float32)
    gb = jnp.dot(h, wg_ref[:, D_MODEL:], preferred_element_type=jnp.float32)
    merged = jax.nn.sigmoid(ga) * ua + jax.nn.sigmoid(gb) * ub
    x1_ref[...] = x + jnp.dot(merged.astype(MXU_DTYPE), wo_ref[...], preferred_element_type=jnp.float32)


def _merge(x2, g, os_, lses, yb, wg, wua, wub, wo, seq):
    n = x2.shape[0]
    tm = ROW_TILE
    tiles_per_seq = seq // tm
    row = lambda i: (i, 0)
    const = lambda i: (0, 0)
    half = pl.BlockSpec((tm, D_HEADS), row)
    pat_specs = [half if d == 1 else
                 pl.BlockSpec((1, d, tm // d, D_HEADS), lambda i: (i // tiles_per_seq, 0, i % tiles_per_seq, 0))
                 for _, d in DILATED_PATTERNS]
    return pl.pallas_call(
        _merge_kernel,
        grid=(n // tm,),
        in_specs=[pl.BlockSpec((tm, D_MODEL), row), pl.BlockSpec((1, D_MODEL), const)] + pat_specs * 2 + [half] + [
            pl.BlockSpec((D_MODEL, 2 * D_MODEL), const),
            pl.BlockSpec((D_HEADS, D_MODEL), const),
            pl.BlockSpec((D_HEADS, D_MODEL), const),
            pl.BlockSpec((D_MODEL, D_MODEL), const)],
        out_specs=pl.BlockSpec((tm, D_MODEL), row),
        out_shape=jax.ShapeDtypeStruct((n, D_MODEL), jnp.float32),
        scratch_shapes=[pltpu.VMEM((D_HEADS // LANES, tm, LANES), jnp.float32)] * (2 * len(RESIDUE_DILATIONS)),
        compiler_params=pltpu.CompilerParams(dimension_semantics=("arbitrary",), vmem_limit_bytes=VMEM_LIMIT),
        name="mix_gate_out",
    )(x2, g, *os_, *lses, yb, wg, wua, wub, wo)


def _ffn_kernel(x_ref, g_ref, wgate_ref, wup_ref, wdown_ref, gf_ref, out_ref):
    x = x_ref[...]
    h = _rms(x, g_ref[...]).astype(MXU_DTYPE)
    y = x
    for c0 in range(0, D_FF, FF_CHUNK):
        a = jnp.dot(h, wgate_ref[:, c0:c0 + FF_CHUNK], preferred_element_type=jnp.float32)
        u = jnp.dot(h, wup_ref[:, c0:c0 + FF_CHUNK], preferred_element_type=jnp.float32)
        act = (a * jax.nn.sigmoid(a) * u).astype(MXU_DTYPE)
        y = y + jnp.dot(act, wdown_ref[c0:c0 + FF_CHUNK, :], preferred_element_type=jnp.float32)
    out_ref[...] = _rms(y, gf_ref[...])


def _ffn(x1, g, wgate, wup, wdown, gf):
    n = x1.shape[0]
    tm = ROW_TILE
    row = lambda i: (i, 0)
    const = lambda i: (0, 0)
    return pl.pallas_call(
        _ffn_kernel,
        grid=(n // tm,),
        in_specs=[pl.BlockSpec((tm, D_MODEL), row), pl.BlockSpec((1, D_MODEL), const),
                  pl.BlockSpec((D_MODEL, D_FF), const), pl.BlockSpec((D_MODEL, D_FF), const),
                  pl.BlockSpec((D_FF, D_MODEL), const), pl.BlockSpec((1, D_MODEL), const)],
        out_specs=pl.BlockSpec((tm, D_MODEL), row),
        out_shape=jax.ShapeDtypeStruct((n, D_MODEL), jnp.float32),
        compiler_params=pltpu.CompilerParams(dimension_semantics=("arbitrary",), vmem_limit_bytes=VMEM_LIMIT),
        name="ffn_norm",
    )(x1, g, wgate, wup, wdown, gf)


def _rope_tables(seq):
    inv_freq = ROPE_THETA ** (-jnp.arange(HALF, dtype=jnp.float32) / HALF)
    ang = jnp.arange(seq, dtype=jnp.int32).astype(jnp.float32)[:, None] * inv_freq[None, :]
    cos, sin = jnp.cos(ang), jnp.sin(ang)
    return jnp.tile(cos, (1, 4)), jnp.concatenate([-sin, -sin, sin, sin], axis=1)


def kernel(x, norm_mix, w_in, w_up_a, w_up_b, w_out, norm_ffn, w_gate, w_up, w_down, norm_final):
    batch, seq, _ = x.shape
    assert seq % max(d * BLOCK for _, d in DILATED_PATTERNS) == 0 and seq % KEY_CHUNK == 0
    assert all(w // d == BLOCK for w, d in DILATED_PATTERNS)
    n = batch * seq
    bf = MXU_DTYPE
    xf = x.reshape(n, D_MODEL)
    cos_t, sin_t = _rope_tables(seq)
    for layer in range(w_in.shape[0]):
        w = w_in[layer]
        w_pack = jnp.pad(w[:, _packed_columns()], ((0, 0), (0, _P_TOTAL - _P_MISC - HEAD_DIM - IDX_HEADS))).astype(bf)
        w_gates = w[:, _GA:].astype(bf)
        qas, kas, vas, qb, qi, kbd, kid, misc = _project(xf, norm_mix[layer][None], w_pack, cos_t, sin_t, seq)

        dil = []
        for (_, d), q, k, v in zip(DILATED_PATTERNS, qas, kas, vas):
            if d == 1:
                o, lse = _dilated(*(z.reshape(batch, 1, seq, D_HEADS) for z in (q, k, v)), d)
                dil.append((o.reshape(n, D_HEADS), lse.reshape(n, D_HEADS)))
            else:
                dil.append(_dilated(q, k, v, d))

        ones_col = (jnp.arange(L_ROWS) == 0).astype(bf)[None, :]
        vb = jnp.concatenate([misc[:, :HEAD_DIM].astype(bf), jnp.broadcast_to(ones_col, (n, L_ROWS))], axis=1)
        vbt = jnp.swapaxes(vb.reshape(batch, seq // KEY_CHUNK, KEY_CHUNK, HEAD_DIM + L_ROWS), 2, 3)
        wt = jnp.swapaxes(misc[:, HEAD_DIM:HEAD_DIM + IDX_HEADS].reshape(batch, seq, IDX_HEADS), 1, 2)
        r3 = lambda z: z.reshape(batch, seq, z.shape[-1])
        yb = _sparse(r3(qi), r3(qb), wt, r3(kid), r3(kbd), vbt, batch, seq).reshape(n, D_HEADS)

        x1 = _merge(xf, norm_mix[layer][None], [o for o, _ in dil], [l for _, l in dil], yb,
                    w_gates, w_up_a[layer].astype(bf), w_up_b[layer].astype(bf), w_out[layer].astype(bf), seq)
        last = layer == w_in.shape[0] - 1
        assert last, "the final norm is fused into the FFN kernel of the last layer"
        xf = _ffn(x1, norm_ffn[layer][None], w_gate[layer].astype(bf), w_up[layer].astype(bf),
                  w_down[layer].astype(bf), norm_final[None])
    return xf.reshape(batch, seq, D_MODEL)
```

```python
import functools

import numpy as np
import jax
import jax.numpy as jnp
from jax import lax
from jax.experimental import pallas as pl
from jax.experimental.pallas import tpu as pltpu

D_MODEL = 1024
HEAD_DIM = 64
HALF = HEAD_DIM // 2
N_HEADS = 8
D_HEADS = N_HEADS * HEAD_DIM
IDX_HEADS = 8
DILATED_PATTERNS = ((128, 1), (512, 4), (2048, 16))
RESIDUE_DILATIONS = tuple(d for _, d in DILATED_PATTERNS if d > 1)
TOPK_MAX = 256
D_FF = 2816
ROPE_THETA = 10000.0
RMS_EPS = 1e-6
BLOCK = 128
ATTN_SCALE = HEAD_DIM ** -0.5
IDX_SCALE = (HEAD_DIM ** -0.5) * (IDX_HEADS ** -0.5)
LOG2_E = float(np.log2(np.e))

LANES = 128
INT_MIN = -(2 ** 31)
KEY_CHUNK = 1024
COUNT_CHUNK = 512
UNCONDITIONAL_BITS = 20
L_ROWS = 8
ROW_TILE = 512
FF_CHUNK = 1408
VMEM_LIMIT = 56 * 1024 * 1024
MXU_DTYPE = jnp.bfloat16

_SPLITS = (D_HEADS, D_HEADS, D_HEADS, D_HEADS, HEAD_DIM, HEAD_DIM, IDX_HEADS * HEAD_DIM, HEAD_DIM, IDX_HEADS,
           D_MODEL, D_MODEL)
_OFF = np.concatenate([[0], np.cumsum(_SPLITS)])
(_QA, _KA, _VA, _QB, _KB, _VB, _QI, _KI, _WI, _GA, _GB) = (int(o) for o in _OFF[:-1])

_P_QA, _P_KA, _P_VA, _P_QB, _P_QI = 0, 512, 1024, 1536, 2048
_P_KBD, _P_KID, _P_MISC = 2560, 2688, 2816
_P_TOTAL = 2944


def _pair_perm():
    idx = np.empty(D_HEADS, np.int64)
    for j in range(D_HEADS):
        g, l = divmod(j, LANES)
        quarter, e = divmod(l, HALF)
        head = 2 * g + (quarter % 2)
        idx[j] = head * HEAD_DIM + (quarter // 2) * HALF + e
    return idx


def _dup_perm():
    idx = np.empty(LANES, np.int64)
    for l in range(LANES):
        quarter, e = divmod(l, HALF)
        idx[l] = (quarter // 2) * HALF + e
    return idx


def _packed_columns():
    pp, dp = _pair_perm(), _dup_perm()
    return np.concatenate([
        _QA + pp, _KA + pp, _VA + np.arange(D_HEADS), _QB + pp, _QI + pp,
        _KB + dp, _KI + dp, _VB + np.arange(HEAD_DIM), _WI + np.arange(IDX_HEADS)])


def _rms(x, g):
    ms = jnp.mean(x * x, axis=-1, keepdims=True)
    return x * lax.rsqrt(ms + RMS_EPS) * g


def _nt_dot(a, b):
    return lax.dot_general(a, b, (((1,), (1,)), ((), ())), preferred_element_type=jnp.float32)


def _proj_kernel(x_ref, g_ref, w_ref, cos_ref, sin_ref, *refs):
    n_lay = 1 + len(RESIDUE_DILATIONS)
    qa_refs, ka_refs, va_refs = refs[:n_lay], refs[n_lay:2 * n_lay], refs[2 * n_lay:3 * n_lay]
    qb_ref, qi_ref, kbd_ref, kid_ref, misc_ref, h_ref, slab_ref = refs[3 * n_lay:]
    h_ref[...] = _rms(x_ref[...], g_ref[...]).astype(MXU_DTYPE)
    cos = cos_ref[...]
    sin = sin_ref[...]

    def mm(c0, width):
        return jnp.dot(h_ref[...], w_ref[:, c0:c0 + width], preferred_element_type=jnp.float32)

    def rope(z):
        parts = []
        for g in range(z.shape[1] // LANES):
            zg = z[:, g * LANES:(g + 1) * LANES]
            parts.append(zg * cos + pltpu.roll(zg, 2 * HALF, axis=1) * sin)
        return parts[0] if len(parts) == 1 else jnp.concatenate(parts, axis=1)

    def emit(y, out_refs):
        out_refs[0][...] = y.astype(MXU_DTYPE)
        for g in range(D_HEADS // LANES):
            slab_ref[g] = y[:, g * LANES:(g + 1) * LANES]
        for d, ref in zip(RESIDUE_DILATIONS, out_refs[1:]):
            rows = y.shape[0] // d
            for r in range(d):
                for g in range(D_HEADS // LANES):
                    ref[0, r, :, g * LANES:(g + 1) * LANES] = (
                        slab_ref[g, pl.ds(r, rows, stride=d), :].astype(MXU_DTYPE))

    emit(rope(mm(_P_QA, D_HEADS)) * ATTN_SCALE, qa_refs)
    emit(rope(mm(_P_KA, D_HEADS)), ka_refs)
    emit(mm(_P_VA, D_HEADS), va_refs)
    qb_ref[...] = (rope(mm(_P_QB, D_HEADS)) * (ATTN_SCALE * LOG2_E)).astype(MXU_DTYPE)
    qi_ref[...] = rope(mm(_P_QI, D_HEADS)).astype(MXU_DTYPE)
    kbd_ref[...] = rope(mm(_P_KBD, LANES)).astype(MXU_DTYPE)
    kid_ref[...] = rope(mm(_P_KID, LANES)).astype(MXU_DTYPE)
    misc_ref[...] = mm(_P_MISC, LANES)


def _project(x2, g, w_pack, cos_t, sin_t, seq):
    n = x2.shape[0]
    tm = ROW_TILE
    tiles_per_seq = seq // tm
    row = lambda i: (i, 0)
    const = lambda i: (0, 0)
    pos = lambda i: (i % tiles_per_seq, 0)
    batch = n // seq
    wide = jax.ShapeDtypeStruct((n, D_HEADS), MXU_DTYPE)
    narrow = jax.ShapeDtypeStruct((n, LANES), MXU_DTYPE)
    wide_spec = pl.BlockSpec((tm, D_HEADS), row)
    lay_shapes = [wide] + [jax.ShapeDtypeStruct((batch, d, seq // d, D_HEADS), MXU_DTYPE) for d in RESIDUE_DILATIONS]
    lay_specs = [wide_spec] + [
        pl.BlockSpec((1, d, tm // d, D_HEADS), lambda i: (i // tiles_per_seq, 0, i % tiles_per_seq, 0))
        for d in RESIDUE_DILATIONS]
    n_lay = len(lay_shapes)
    outs = pl.pallas_call(
        _proj_kernel,
        grid=(n // tm,),
        in_specs=[
            pl.BlockSpec((tm, D_MODEL), row),
            pl.BlockSpec((1, D_MODEL), const),
            pl.BlockSpec((D_MODEL, _P_TOTAL), const),
            pl.BlockSpec((tm, LANES), pos),
            pl.BlockSpec((tm, LANES), pos),
        ],
        out_specs=lay_specs * 3 + [wide_spec] * 2 + [pl.BlockSpec((tm, LANES), row)] * 3,
        out_shape=lay_shapes * 3 + [wide] * 2 + [narrow, narrow, jax.ShapeDtypeStruct((n, LANES), jnp.float32)],
        scratch_shapes=[pltpu.VMEM((tm, D_MODEL), MXU_DTYPE),
                        pltpu.VMEM((D_HEADS // LANES, tm, LANES), jnp.float32)],
        compiler_params=pltpu.CompilerParams(dimension_semantics=("arbitrary",), vmem_limit_bytes=VMEM_LIMIT),
        name="proj_rope",
    )(x2, g, w_pack, cos_t, sin_t)
    return (outs[:n_lay], outs[n_lay:2 * n_lay], outs[2 * n_lay:3 * n_lay]) + tuple(outs[3 * n_lay:])


def _dil_kernel(q_ref, k_ref, v_ref, kp_ref, vp_ref, o_ref, lse_ref):
    n = pl.program_id(2)
    qi = lax.broadcasted_iota(jnp.int32, (BLOCK, 2 * BLOCK), 0)
    kj = lax.broadcasted_iota(jnp.int32, (BLOCK, 2 * BLOCK), 1)
    band = (kj >= qi) & (kj <= qi + BLOCK) & ((kj >= BLOCK) | (n > 0))
    lane = lax.broadcasted_iota(jnp.int32, (1, LANES), 1)
    first_v = lane < HEAD_DIM
    for p in range(N_HEADS // 2):
        cols = slice(p * LANES, (p + 1) * LANES)
        qp = q_ref[0, 0, :, cols]
        k2 = jnp.concatenate([kp_ref[0, 0, :, cols], k_ref[0, 0, :, cols]], axis=0)
        v2 = jnp.concatenate([vp_ref[0, 0, :, cols], v_ref[0, 0, :, cols]], axis=0)
        outs, lses = [], []
        for hh in range(2):
            head_lanes = ((lane // HALF) % 2) == hh
            qm = jnp.where(head_lanes, qp, jnp.zeros_like(qp))
            s = jnp.where(band, _nt_dot(qm, k2), -jnp.inf)
            m = jnp.max(s, axis=-1, keepdims=True)
            e = jnp.exp(s - m)
            den = jnp.sum(e, axis=-1, keepdims=True)
            o = jnp.dot(e.astype(MXU_DTYPE), v2, preferred_element_type=jnp.float32) / den
            outs.append(o)
            lses.append(m + jnp.log(den))
        o_ref[0, 0, :, cols] = jnp.where(first_v, outs[0], outs[1]).astype(o_ref.dtype)
        lse_ref[0, 0, :, cols] = jnp.where(first_v, lses[0], lses[1])


def _dilated(q, k, v, dilation):
    batch, _, m_len, _ = q.shape
    nb = m_len // BLOCK
    cur = lambda b, r, n: (b, r, n, 0)
    prev = lambda b, r, n: (b, r, jnp.maximum(n - 1, 0), 0)
    blk = (1, 1, BLOCK, D_HEADS)
    return pl.pallas_call(
        _dil_kernel,
        grid=(batch, dilation, nb),
        in_specs=[pl.BlockSpec(blk, cur), pl.BlockSpec(blk, cur), pl.BlockSpec(blk, cur),
                  pl.BlockSpec(blk, prev), pl.BlockSpec(blk, prev)],
        out_specs=[pl.BlockSpec(blk, cur), pl.BlockSpec(blk, cur)],
        out_shape=[jax.ShapeDtypeStruct(q.shape, MXU_DTYPE), jax.ShapeDtypeStruct(q.shape, jnp.float32)],
        compiler_params=pltpu.CompilerParams(dimension_semantics=("arbitrary",) * 3),
        name=f"dilated_d{dilation}",
    )(q, k, v, k, v)


def _key_to_f32(key):
    bits = key ^ ((key >> 31) & jnp.int32(0x7FFFFFFF))
    return lax.bitcast_convert_type(bits, jnp.float32)


def _fold_rows(x, op):
    rows = x.shape[0]
    y = op(x.reshape(rows // 64, 64, LANES), axis=0)
    return op(y.reshape(8, 8, LANES), axis=0)


def _sparse_kernel(qi_ref, qb_ref, wt_ref, kid_ref, kbd_ref, vbt_ref, out_ref,
                   sc_ref, mb_ref, qis_ref, qbs_ref, acc_ref, m_ref, sqa_ref, sqb_ref, mxa_ref, mxb_ref,
                   *, topk, index_bits):
    i = pl.program_id(1)
    nch = i // (KEY_CHUNK // BLOCK) + 1
    ncc = i // (COUNT_CHUNK // BLOCK) + 1
    lane = lax.broadcasted_iota(jnp.int32, (1, LANES), 1)
    t_idx = i * BLOCK + lane
    row_iota = lax.broadcasted_iota(jnp.int32, (KEY_CHUNK, LANES), 0)
    crow_iota = lax.broadcasted_iota(jnp.int32, (COUNT_CHUNK, LANES), 0)
    neg_inf = jnp.float32(-jnp.inf)
    f32_lowest = jnp.float32(jnp.finfo(jnp.float32).min)

    eye = jnp.where(lax.broadcasted_iota(jnp.int32, (BLOCK, LANES), 0) == lane, 1.0, 0.0).astype(MXU_DTYPE)
    for h in range(N_HEADS):
        cols = slice((h // 2) * LANES, (h // 2 + 1) * LANES)
        head_lanes = ((lane // HALF) % 2) == (h % 2)
        rows = slice(h * BLOCK, (h + 1) * BLOCK)
        qis_ref[rows, :] = jnp.where(head_lanes, qi_ref[0, :, cols], jnp.zeros((), MXU_DTYPE))
        qbs_ref[rows, :LANES] = jnp.where(head_lanes, qb_ref[0, :, cols], jnp.zeros((), MXU_DTYPE))
        qbs_ref[rows, LANES:] = eye

    def chunk_start(c):
        return pl.multiple_of(c * KEY_CHUNK, KEY_CHUNK)

    def pair_q(ref, p):
        return ref[p * 2 * BLOCK:(p + 1) * 2 * BLOCK, :]

    buf_a, buf_b = (sqa_ref, mxa_ref), (sqb_ref, mxb_ref)

    def pipeline(produce, consume):
        produce(0, buf_a)

        def two_chunks(j, carry):
            c = 2 * j
            produce(c + 1, buf_b)
            consume(c, buf_a)
            produce(c + 2, buf_a)
            consume(c + 1, buf_b)
            return carry

        lax.fori_loop(0, (nch - 1) // 2, two_chunks, 0)
        last = nch - 1

        @pl.when(last % 2 == 1)
        def _():
            produce(last, buf_b)
            consume(last - 1, buf_a)
            consume(last, buf_b)

        @pl.when(last % 2 == 0)
        def _():
            consume(last, buf_a)

    def score_chunk(c, carry):
        r0 = chunk_start(c)
        kc = kid_ref[0, pl.ds(r0, KEY_CHUNK), :]
        acc = jnp.zeros((KEY_CHUNK, LANES), jnp.float32)
        for p in range(N_HEADS // 2):
            d = _nt_dot(kc, pair_q(qis_ref, p))
            for hh in range(2):
                h = 2 * p + hh
                acc = acc + jnp.maximum(d[:, hh * BLOCK:(hh + 1) * BLOCK], 0.0) * wt_ref[0, h:h + 1, :]
        sc_ref[pl.ds(r0, KEY_CHUNK), :] = jnp.where(r0 + row_iota <= t_idx, acc * IDX_SCALE, neg_inf)
        return carry

    lax.fori_loop(0, nch, score_chunk, 0)

    def count(pred):
        def body(c, acc):
            r0 = pl.multiple_of(c * COUNT_CHUNK, COUNT_CHUNK)
            hit = jnp.where(pred(sc_ref[pl.ds(r0, COUNT_CHUNK), :], r0), 1.0, 0.0)
            return acc + jnp.sum(hit.reshape(8, COUNT_CHUNK // 8, LANES), axis=0)

        acc = lax.fori_loop(0, ncc // 2, lambda j, a: body(2 * j + 1, body(2 * j, a)),
                            jnp.zeros((COUNT_CHUNK // 8, LANES), jnp.float32))
        acc = lax.cond(ncc % 2 == 1, lambda a: body(ncc - 1, a), lambda a: a, acc)
        return jnp.sum(acc, axis=0, keepdims=True)

    def bit_step(b, carry):
        res, cnt_res = carry
        cand = res | (jnp.int32(1) << (31 - b))
        thr_c = _key_to_f32(cand ^ jnp.int32(INT_MIN))
        cnt = count(lambda x, r0: x >= thr_c)
        take = cnt >= topk
        return jnp.where(take, cand, res), jnp.where(take, cnt, cnt_res)

    def pending(cnt_res):
        settled = (cnt_res == topk) | (t_idx < topk)
        return jnp.max(jnp.where(settled, 0.0, 1.0))

    res, cnt_ge = lax.fori_loop(0, UNCONDITIONAL_BITS, bit_step, (jnp.zeros((1, LANES), jnp.int32),
                                                                  jnp.zeros((1, LANES), jnp.float32)))

    def more_bits(carry):
        b, res, cnt_res, _ = carry
        res, cnt_res = bit_step(b + 1, bit_step(b, (res, cnt_res)))
        return b + 2, res, cnt_res, pending(cnt_res)

    _, res, cnt_ge, _ = lax.while_loop(lambda c: (c[0] < 32) & (c[3] > 0.0), more_bits,
                                       (jnp.int32(UNCONDITIONAL_BITS), res, cnt_ge, pending(cnt_ge)))
    thr_raw = _key_to_f32(res ^ jnp.int32(INT_MIN))
    enough = thr_raw > f32_lowest
    thr = jnp.where(enough, thr_raw, f32_lowest)

    tie_lane = enough & (cnt_ge > topk)

    @pl.when(jnp.max(jnp.where(tie_lane, 1.0, 0.0)) > 0.0)
    def _():
        need = topk - count(lambda x, r0: x > thr)

        def idx_step(b, lim):
            cand = lim | (jnp.int32(1) << (index_bits - 1 - b))
            below = count(lambda x, r0: (x == thr) & (r0 + crow_iota < cand))
            return jnp.where(below < need, cand, lim)

        lim = lax.fori_loop(0, index_bits, idx_step, jnp.zeros((1, LANES), jnp.int32))

        def demote(c, carry):
            r0 = pl.multiple_of(c * COUNT_CHUNK, COUNT_CHUNK)
            x = sc_ref[pl.ds(r0, COUNT_CHUNK), :]
            drop = (x == thr) & (r0 + crow_iota > lim) & tie_lane
            sc_ref[pl.ds(r0, COUNT_CHUNK), :] = jnp.where(drop, neg_inf, x)
            return carry

        lax.fori_loop(0, ncc, demote, 0)

    mask_off = float(jnp.finfo(MXU_DTYPE).min)

    def build_mask(c, carry):
        r0 = chunk_start(c)
        selected = sc_ref[pl.ds(r0, KEY_CHUNK), :] >= thr
        mb_ref[pl.ds(r0, KEY_CHUNK), :] = jnp.where(selected, 0.0, mask_off).astype(MXU_DTYPE)
        return carry

    lax.fori_loop(0, nch, build_mask, 0)

    def attn_scores(c, buf):
        sq_ref, mx_ref = buf
        r0 = chunk_start(c)
        kaug = jnp.concatenate([kbd_ref[0, pl.ds(r0, KEY_CHUNK), :], mb_ref[pl.ds(r0, KEY_CHUNK), :]], axis=1)
        for p in range(N_HEADS // 2):
            s = _nt_dot(kaug, pair_q(qbs_ref, p))
            sq_ref[:, p * 2 * BLOCK:(p + 1) * 2 * BLOCK] = s
            for hh in range(2):
                cols = slice((2 * p + hh) * BLOCK, (2 * p + hh + 1) * BLOCK)
                mx_ref[:, cols] = _fold_rows(s[:, hh * BLOCK:(hh + 1) * BLOCK], jnp.max)

    def accumulate(c, buf):
        sq_ref, mx_ref = buf
        m_run = m_ref[...]
        m_new = jnp.maximum(m_run, jnp.max(mx_ref[...], axis=0, keepdims=True))
        m_ref[...] = m_new
        acc_ref[...] = acc_ref[...] * jnp.exp2(m_run - m_new)
        vt = vbt_ref[0, c]
        for p in range(N_HEADS // 2):
            pcols = slice(p * 2 * BLOCK, (p + 1) * 2 * BLOCK)
            e = jnp.exp2(sq_ref[:, pcols] - m_new[:, pcols])
            acc_ref[:, pcols] += jnp.dot(vt, e.astype(MXU_DTYPE), preferred_element_type=jnp.float32)

    acc_ref[...] = jnp.zeros(acc_ref.shape, jnp.float32)
    m_ref[...] = jnp.full(m_ref.shape, f32_lowest, jnp.float32)
    pipeline(attn_scores, accumulate)

    o_t = acc_ref[:HEAD_DIM, :] / acc_ref[HEAD_DIM:HEAD_DIM + 1, :]
    for p in range(N_HEADS // 2):
        pair = jnp.concatenate([o_t[:, (2 * p) * BLOCK:(2 * p + 1) * BLOCK],
                                o_t[:, (2 * p + 1) * BLOCK:(2 * p + 2) * BLOCK]], axis=0)
        out_ref[0, :, p * LANES:(p + 1) * LANES] = pair.T.astype(out_ref.dtype)


def _sparse(qi, qb, wt, kid, kbd, vbt, batch, seq):
    topk = min(TOPK_MAX, seq // 4)
    nq = seq // BLOCK
    qblk = pl.BlockSpec((1, BLOCK, D_HEADS), lambda b, i: (b, i, 0))
    full = pl.BlockSpec((1, seq, LANES), lambda b, i: (b, 0, 0))
    kern = functools.partial(_sparse_kernel, topk=topk, index_bits=max(1, (seq - 1).bit_length()))
    return pl.pallas_call(
        kern,
        grid=(batch, nq),
        in_specs=[qblk, qblk,
                  pl.BlockSpec((1, IDX_HEADS, BLOCK), lambda b, i: (b, 0, i)),
                  full, full,
                  pl.BlockSpec((1, seq // KEY_CHUNK, HEAD_DIM + L_ROWS, KEY_CHUNK), lambda b, i: (b, 0, 0, 0))],
        out_specs=qblk,
        out_shape=jax.ShapeDtypeStruct((batch, seq, D_HEADS), MXU_DTYPE),
        scratch_shapes=[pltpu.VMEM((seq, LANES), jnp.float32),
                        pltpu.VMEM((seq, LANES), MXU_DTYPE),
                        pltpu.VMEM((N_HEADS * BLOCK, LANES), MXU_DTYPE),
                        pltpu.VMEM((N_HEADS * BLOCK, 2 * LANES), MXU_DTYPE),
                        pltpu.VMEM((HEAD_DIM + L_ROWS, N_HEADS * BLOCK), jnp.float32),
                        pltpu.VMEM((1, N_HEADS * BLOCK), jnp.float32),
                        pltpu.VMEM((KEY_CHUNK, N_HEADS * BLOCK), jnp.float32),
                        pltpu.VMEM((KEY_CHUNK, N_HEADS * BLOCK), jnp.float32),
                        pltpu.VMEM((8, N_HEADS * BLOCK), jnp.float32),
                        pltpu.VMEM((8, N_HEADS * BLOCK), jnp.float32)],
        compiler_params=pltpu.CompilerParams(dimension_semantics=("arbitrary", "arbitrary"),
                                             vmem_limit_bytes=VMEM_LIMIT),
        name="indexer_sparse_attn",
    )(qi, qb, wt, kid, kbd, vbt)


def _merge_kernel(x_ref, g_ref, *refs):
    n_pat = len(DILATED_PATTERNS)
    o_refs, l_refs = refs[:n_pat], refs[n_pat:2 * n_pat]
    yb_ref, wg_ref, wua_ref, wub_ref, wo_ref, x1_ref = refs[2 * n_pat:2 * n_pat + 6]
    slabs = refs[2 * n_pat + 6:]
    x = x_ref[...]
    h = _rms(x, g_ref[...]).astype(MXU_DTYPE)
    tm = x.shape[0]
    n_grp = D_HEADS // LANES

    o_src, l_src, k = [], [], 0
    for (_, d), o_ref, l_ref in zip(DILATED_PATTERNS, o_refs, l_refs):
        if d == 1:
            o_src.append(lambda g, r=o_ref: r[:, g * LANES:(g + 1) * LANES].astype(jnp.float32))
            l_src.append(lambda g, r=l_ref: r[:, g * LANES:(g + 1) * LANES])
            continue
        o_slab, l_slab = slabs[2 * k], slabs[2 * k + 1]
        k += 1
        for r in range(d):
            for g in range(n_grp):
                cols = slice(g * LANES, (g + 1) * LANES)
                o_slab[g, pl.ds(r, tm // d, stride=d), :] = o_ref[0, r, :, cols].astype(jnp.float32)
                l_slab[g, pl.ds(r, tm // d, stride=d), :] = l_ref[0, r, :, cols]
        o_src.append(lambda g, s=o_slab: s[g])
        l_src.append(lambda g, s=l_slab: s[g])

    parts = []
    for g in range(n_grp):
        ls = [f(g) for f in l_src]
        m = functools.reduce(jnp.maximum, ls)
        es = [jnp.exp(l - m) for l in ls]
        num = functools.reduce(jnp.add, [e * f(g) for e, f in zip(es, o_src)])
        parts.append((num / functools.reduce(jnp.add, es)).astype(MXU_DTYPE))
    ya = jnp.concatenate(parts, axis=1)
    ua = jnp.dot(ya, wua_ref[...], preferred_element_type=jnp.float32)
    ub = jnp.dot(yb_ref[...], wub_ref[...], preferred_element_type=jnp.float32)
    ga = jnp.dot(h, wg_ref[:, :D_MODEL], preferred_element_type=jnp.float32)
    gb = jnp.dot(h, wg_ref[:, D_MODEL:], preferred_element_type=jnp.float32)
    merged = jax.nn.sigmoid(ga) * ua + jax.nn.sigmoid(gb) * ub
    x1_ref[...] = x + jnp.dot(merged.astype(MXU_DTYPE), wo_ref[...], preferred_element_type=jnp.float32)


def _merge(x2, g, os_, lses, yb, wg, wua, wub, wo, seq):
    n = x2.shape[0]
    tm = ROW_TILE
    tiles_per_seq = seq // tm
    row = lambda i: (i, 0)
    const = lambda i: (0, 0)
    half = pl.BlockSpec((tm, D_HEADS), row)
    pat_specs = [half if d == 1 else
                 pl.BlockSpec((1, d, tm // d, D_HEADS), lambda i: (i // tiles_per_seq, 0, i % tiles_per_seq, 0))
                 for _, d in DILATED_PATTERNS]
    return pl.pallas_call(
        _merge_kernel,
        grid=(n // tm,),
        in_specs=[pl.BlockSpec((tm, D_MODEL), row), pl.BlockSpec((1, D_MODEL), const)] + pat_specs * 2 + [half] + [
            pl.BlockSpec((D_MODEL, 2 * D_MODEL), const),
            pl.BlockSpec((D_HEADS, D_MODEL), const),
            pl.BlockSpec((D_HEADS, D_MODEL), const),
            pl.BlockSpec((D_MODEL, D_MODEL), const)],
        out_specs=pl.BlockSpec((tm, D_MODEL), row),
        out_shape=jax.ShapeDtypeStruct((n, D_MODEL), jnp.float32),
        scratch_shapes=[pltpu.VMEM((D_HEADS // LANES, tm, LANES), jnp.float32)] * (2 * len(RESIDUE_DILATIONS)),
        compiler_params=pltpu.CompilerParams(dimension_semantics=("arbitrary",), vmem_limit_bytes=VMEM_LIMIT),
        name="mix_gate_out",
    )(x2, g, *os_, *lses, yb, wg, wua, wub, wo)


def _ffn_kernel(x_ref, g_ref, wgate_ref, wup_ref, wdown_ref, gf_ref, out_ref):
    x = x_ref[...]
    h = _rms(x, g_ref[...]).astype(MXU_DTYPE)
    y = x
    for c0 in range(0, D_FF, FF_CHUNK):
        a = jnp.dot(h, wgate_ref[:, c0:c0 + FF_CHUNK], preferred_element_type=jnp.float32)
        u = jnp.dot(h, wup_ref[:, c0:c0 + FF_CHUNK], preferred_element_type=jnp.float32)
        act = (a * jax.nn.sigmoid(a) * u).astype(MXU_DTYPE)
        y = y + jnp.dot(act, wdown_ref[c0:c0 + FF_CHUNK, :], preferred_element_type=jnp.float32)
    out_ref[...] = _rms(y, gf_ref[...])


def _ffn(x1, g, wgate, wup, wdown, gf):
    n = x1.shape[0]
    tm = ROW_TILE
    row = lambda i: (i, 0)
    const = lambda i: (0, 0)
    return pl.pallas_call(
        _ffn_kernel,
        grid=(n // tm,),
        in_specs=[pl.BlockSpec((tm, D_MODEL), row), pl.BlockSpec((1, D_MODEL), const),
                  pl.BlockSpec((D_MODEL, D_FF), const), pl.BlockSpec((D_MODEL, D_FF), const),
                  pl.BlockSpec((D_FF, D_MODEL), const), pl.BlockSpec((1, D_MODEL), const)],
        out_specs=pl.BlockSpec((tm, D_MODEL), row),
        out_shape=jax.ShapeDtypeStruct((n, D_MODEL), jnp.float32),
        compiler_params=pltpu.CompilerParams(dimension_semantics=("arbitrary",), vmem_limit_bytes=VMEM_LIMIT),
        name="ffn_norm",
    )(x1, g, wgate, wup, wdown, gf)


def _rope_tables(seq):
    inv_freq = ROPE_THETA ** (-jnp.arange(HALF, dtype=jnp.float32) / HALF)
    ang = jnp.arange(seq, dtype=jnp.int32).astype(jnp.float32)[:, None] * inv_freq[None, :]
    cos, sin = jnp.cos(ang), jnp.sin(ang)
    return jnp.tile(cos, (1, 4)), jnp.concatenate([-sin, -sin, sin, sin], axis=1)


def kernel(x, norm_mix, w_in, w_up_a, w_up_b, w_out, norm_ffn, w_gate, w_up, w_down, norm_final):
    batch, seq, _ = x.shape
    assert seq % max(d * BLOCK for _, d in DILATED_PATTERNS) == 0 and seq % KEY_CHUNK == 0
    assert all(w // d == BLOCK for w, d in DILATED_PATTERNS)
    n = batch * seq
    bf = MXU_DTYPE
    xf = x.reshape(n, D_MODEL)
    cos_t, sin_t = _rope_tables(seq)
    for layer in range(w_in.shape[0]):
        w = w_in[layer]
        w_pack = jnp.pad(w[:, _packed_columns()], ((0, 0), (0, _P_TOTAL - _P_MISC - HEAD_DIM - IDX_HEADS))).astype(bf)
        w_gates = w[:, _GA:].astype(bf)
        qas, kas, vas, qb, qi, kbd, kid, misc = _project(xf, norm_mix[layer][None], w_pack, cos_t, sin_t, seq)

        dil = []
        for (_, d), q, k, v in zip(DILATED_PATTERNS, qas, kas, vas):
            if d == 1:
                o, lse = _dilated(*(z.reshape(batch, 1, seq, D_HEADS) for z in (q, k, v)), d)
                dil.append((o.reshape(n, D_HEADS), lse.reshape(n, D_HEADS)))
            else:
                dil.append(_dilated(q, k, v, d))

        ones_col = (jnp.arange(L_ROWS) == 0).astype(bf)[None, :]
        vb = jnp.concatenate([misc[:, :HEAD_DIM].astype(bf), jnp.broadcast_to(ones_col, (n, L_ROWS))], axis=1)
        vbt = jnp.swapaxes(vb.reshape(batch, seq // KEY_CHUNK, KEY_CHUNK, HEAD_DIM + L_ROWS), 2, 3)
        wt = jnp.swapaxes(misc[:, HEAD_DIM:HEAD_DIM + IDX_HEADS].reshape(batch, seq, IDX_HEADS), 1, 2)
        r3 = lambda z: z.reshape(batch, seq, z.shape[-1])
        yb = _sparse(r3(qi), r3(qb), wt, r3(kid), r3(kbd), vbt, batch, seq).reshape(n, D_HEADS)

        x1 = _merge(xf, norm_mix[layer][None], [o for o, _ in dil], [l for _, l in dil], yb,
                    w_gates, w_up_a[layer].astype(bf), w_up_b[layer].astype(bf), w_out[layer].astype(bf), seq)
        last = layer == w_in.shape[0] - 1
        assert last, "the final norm is fused into the FFN kernel of the last layer"
        xf = _ffn(x1, norm_ffn[layer][None], w_gate[layer].astype(bf), w_up[layer].astype(bf),
                  w_down[layer].astype(bf), norm_final[None])
    return xf.reshape(batch, seq, D_MODEL)
```

```python
import functools

import numpy as np
import jax
import jax.numpy as jnp
from jax import lax
from jax.experimental import pallas as pl
from jax.experimental.pallas import tpu as pltpu

D_MODEL = 1024
HEAD_DIM = 64
HALF = HEAD_DIM // 2
N_HEADS = 8
D_HEADS = N_HEADS * HEAD_DIM
IDX_HEADS = 8
DILATED_PATTERNS = ((128, 1), (512, 4), (2048, 16))
RESIDUE_DILATIONS = tuple(d for _, d in DILATED_PATTERNS if d > 1)
TOPK_MAX = 256
D_FF = 2816
ROPE_THETA = 10000.0
RMS_EPS = 1e-6
BLOCK = 128
ATTN_SCALE = HEAD_DIM ** -0.5
IDX_SCALE = (HEAD_DIM ** -0.5) * (IDX_HEADS ** -0.5)
LOG2_E = float(np.log2(np.e))

LANES = 128
INT_MIN = -(2 ** 31)
MIN_NORMAL_KEY = 1 << 23
KEY_CHUNK = 1024
COUNT_CHUNK = 512
VALUE_PASSES = 20
DESCENT_STEPS = 4
SEARCH_PASSES_MIN = 12
L_ROWS = 8
ROW_TILE = 512
FF_CHUNK = 1408
VMEM_LIMIT = 56 * 1024 * 1024
MXU_DTYPE = jnp.bfloat16

_SPLITS = (D_HEADS, D_HEADS, D_HEADS, D_HEADS, HEAD_DIM, HEAD_DIM, IDX_HEADS * HEAD_DIM, HEAD_DIM, IDX_HEADS,
           D_MODEL, D_MODEL)
_OFF = np.concatenate([[0], np.cumsum(_SPLITS)])
(_QA, _KA, _VA, _QB, _KB, _VB, _QI, _KI, _WI, _GA, _GB) = (int(o) for o in _OFF[:-1])

_P_QA, _P_KA, _P_VA, _P_QB, _P_QI = 0, 512, 1024, 1536, 2048
_P_KBD, _P_KID, _P_MISC = 2560, 2688, 2816
_P_TOTAL = 2944


def _pair_perm():
    idx = np.empty(D_HEADS, np.int64)
    for j in range(D_HEADS):
        g, l = divmod(j, LANES)
        quarter, e = divmod(l, HALF)
        head = 2 * g + (quarter % 2)
        idx[j] = head * HEAD_DIM + (quarter // 2) * HALF + e
    return idx


def _dup_perm():
    idx = np.empty(LANES, np.int64)
    for l in range(LANES):
        quarter, e = divmod(l, HALF)
        idx[l] = (quarter // 2) * HALF + e
    return idx


def _packed_columns():
    pp, dp = _pair_perm(), _dup_perm()
    return np.concatenate([
        _QA + pp, _KA + pp, _VA + np.arange(D_HEADS), _QB + pp, _QI + pp,
        _KB + dp, _KI + dp, _VB + np.arange(HEAD_DIM), _WI + np.arange(IDX_HEADS)])


def _rms(x, g):
    ms = jnp.mean(x * x, axis=-1, keepdims=True)
    return x * lax.rsqrt(ms + RMS_EPS) * g


def _nt_dot(a, b):
    return lax.dot_general(a, b, (((1,), (1,)), ((), ())), preferred_element_type=jnp.float32)


def _proj_kernel(x_ref, g_ref, w_ref, cos_ref, sin_ref, *refs):
    n_lay = 1 + len(RESIDUE_DILATIONS)
    qa_refs, ka_refs, va_refs = refs[:n_lay], refs[n_lay:2 * n_lay], refs[2 * n_lay:3 * n_lay]
    qb_ref, qi_ref, kbd_ref, kid_ref, misc_ref, h_ref, slab_ref = refs[3 * n_lay:]
    h_ref[...] = _rms(x_ref[...], g_ref[...]).astype(MXU_DTYPE)
    cos = cos_ref[...]
    sin = sin_ref[...]

    def mm(c0, width):
        return jnp.dot(h_ref[...], w_ref[:, c0:c0 + width], preferred_element_type=jnp.float32)

    def rope(z):
        parts = []
        for g in range(z.shape[1] // LANES):
            zg = z[:, g * LANES:(g + 1) * LANES]
            parts.append(zg * cos + pltpu.roll(zg, 2 * HALF, axis=1) * sin)
        return parts[0] if len(parts) == 1 else jnp.concatenate(parts, axis=1)

    def emit(y, out_refs):
        out_refs[0][...] = y.astype(MXU_DTYPE)
        for g in range(D_HEADS // LANES):
            slab_ref[g] = y[:, g * LANES:(g + 1) * LANES]
        for d, ref in zip(RESIDUE_DILATIONS, out_refs[1:]):
            rows = y.shape[0] // d
            for r in range(d):
                for g in range(D_HEADS // LANES):
                    ref[0, r, :, g * LANES:(g + 1) * LANES] = (
                        slab_ref[g, pl.ds(r, rows, stride=d), :].astype(MXU_DTYPE))

    emit(rope(mm(_P_QA, D_HEADS)) * ATTN_SCALE, qa_refs)
    emit(rope(mm(_P_KA, D_HEADS)), ka_refs)
    emit(mm(_P_VA, D_HEADS), va_refs)
    qb_ref[...] = (rope(mm(_P_QB, D_HEADS)) * (ATTN_SCALE * LOG2_E)).astype(MXU_DTYPE)
    qi_ref[...] = rope(mm(_P_QI, D_HEADS)).astype(MXU_DTYPE)
    kbd_ref[...] = rope(mm(_P_KBD, LANES)).astype(MXU_DTYPE)
    kid_ref[...] = rope(mm(_P_KID, LANES)).astype(MXU_DTYPE)
    misc_ref[...] = mm(_P_MISC, LANES)


def _project(x2, g, w_pack, cos_t, sin_t, seq):
    n = x2.shape[0]
    tm = ROW_TILE
    tiles_per_seq = seq // tm
    row = lambda i: (i, 0)
    const = lambda i: (0, 0)
    pos = lambda i: (i % tiles_per_seq, 0)
    batch = n // seq
    wide = jax.ShapeDtypeStruct((n, D_HEADS), MXU_DTYPE)
    narrow = jax.ShapeDtypeStruct((n, LANES), MXU_DTYPE)
    wide_spec = pl.BlockSpec((tm, D_HEADS), row)
    lay_shapes = [wide] + [jax.ShapeDtypeStruct((batch, d, seq // d, D_HEADS), MXU_DTYPE) for d in RESIDUE_DILATIONS]
    lay_specs = [wide_spec] + [
        pl.BlockSpec((1, d, tm // d, D_HEADS), lambda i: (i // tiles_per_seq, 0, i % tiles_per_seq, 0))
        for d in RESIDUE_DILATIONS]
    n_lay = len(lay_shapes)
    outs = pl.pallas_call(
        _proj_kernel,
        grid=(n // tm,),
        in_specs=[
            pl.BlockSpec((tm, D_MODEL), row),
            pl.BlockSpec((1, D_MODEL), const),
            pl.BlockSpec((D_MODEL, _P_TOTAL), const),
            pl.BlockSpec((tm, LANES), pos),
            pl.BlockSpec((tm, LANES), pos),
        ],
        out_specs=lay_specs * 3 + [wide_spec] * 2 + [pl.BlockSpec((tm, LANES), row)] * 3,
        out_shape=lay_shapes * 3 + [wide] * 2 + [narrow, narrow, jax.ShapeDtypeStruct((n, LANES), jnp.float32)],
        scratch_shapes=[pltpu.VMEM((tm, D_MODEL), MXU_DTYPE),
                        pltpu.VMEM((D_HEADS // LANES, tm, LANES), jnp.float32)],
        compiler_params=pltpu.CompilerParams(dimension_semantics=("arbitrary",), vmem_limit_bytes=VMEM_LIMIT),
        name="proj_rope",
    )(x2, g, w_pack, cos_t, sin_t)
    return (outs[:n_lay], outs[n_lay:2 * n_lay], outs[2 * n_lay:3 * n_lay]) + tuple(outs[3 * n_lay:])


def _dil_kernel(q_ref, k_ref, v_ref, kp_ref, vp_ref, o_ref, lse_ref):
    n = pl.program_id(2)
    qi = lax.broadcasted_iota(jnp.int32, (BLOCK, 2 * BLOCK), 0)
    kj = lax.broadcasted_iota(jnp.int32, (BLOCK, 2 * BLOCK), 1)
    band = (kj >= qi) & (kj <= qi + BLOCK) & ((kj >= BLOCK) | (n > 0))
    lane = lax.broadcasted_iota(jnp.int32, (1, LANES), 1)
    first_v = lane < HEAD_DIM
    for p in range(N_HEADS // 2):
        cols = slice(p * LANES, (p + 1) * LANES)
        qp = q_ref[0, 0, :, cols]
        k2 = jnp.concatenate([kp_ref[0, 0, :, cols], k_ref[0, 0, :, cols]], axis=0)
        v2 = jnp.concatenate([vp_ref[0, 0, :, cols], v_ref[0, 0, :, cols]], axis=0)
        outs, lses = [], []
        for hh in range(2):
            head_lanes = ((lane // HALF) % 2) == hh
            qm = jnp.where(head_lanes, qp, jnp.zeros_like(qp))
            s = jnp.where(band, _nt_dot(qm, k2), -jnp.inf)
            m = jnp.max(s, axis=-1, keepdims=True)
            e = jnp.exp(s - m)
            den = jnp.sum(e, axis=-1, keepdims=True)
            o = jnp.dot(e.astype(MXU_DTYPE), v2, preferred_element_type=jnp.float32) / den
            outs.append(o)
            lses.append(m + jnp.log(den))
        o_ref[0, 0, :, cols] = jnp.where(first_v, outs[0], outs[1]).astype(o_ref.dtype)
        lse_ref[0, 0, :, cols] = jnp.where(first_v, lses[0], lses[1])


def _dilated(q, k, v, dilation):
    batch, _, m_len, _ = q.shape
    nb = m_len // BLOCK
    cur = lambda b, r, n: (b, r, n, 0)
    prev = lambda b, r, n: (b, r, jnp.maximum(n - 1, 0), 0)
    blk = (1, 1, BLOCK, D_HEADS)
    return pl.pallas_call(
        _dil_kernel,
        grid=(batch, dilation, nb),
        in_specs=[pl.BlockSpec(blk, cur), pl.BlockSpec(blk, cur), pl.BlockSpec(blk, cur),
                  pl.BlockSpec(blk, prev), pl.BlockSpec(blk, prev)],
        out_specs=[pl.BlockSpec(blk, cur), pl.BlockSpec(blk, cur)],
        out_shape=[jax.ShapeDtypeStruct(q.shape, MXU_DTYPE), jax.ShapeDtypeStruct(q.shape, jnp.float32)],
        compiler_params=pltpu.CompilerParams(dimension_semantics=("arbitrary",) * 3),
        name=f"dilated_d{dilation}",
    )(q, k, v, k, v)


def _key_to_f32(key):
    bits = key ^ ((key >> 31) & jnp.int32(0x7FFFFFFF))
    return lax.bitcast_convert_type(bits, jnp.float32)


def _f32_to_key(x):
    bits = lax.bitcast_convert_type(x, jnp.int32)
    return bits ^ ((bits >> 31) & jnp.int32(0x7FFFFFFF))


def _fold_rows(x, op):
    rows = x.shape[0]
    y = op(x.reshape(rows // 64, 64, LANES), axis=0)
    return op(y.reshape(8, 8, LANES), axis=0)


def _sparse_kernel(qi_ref, qb_ref, wt_ref, kid_ref, kbd_ref, vbt_ref, out_ref,
                   sc_ref, mb_ref, qis_ref, qbs_ref, acc_ref, m_ref, sqa_ref, sqb_ref, mxa_ref, mxb_ref,
                   *, topk, index_bits):
    i = pl.program_id(1)
    nch = i // (KEY_CHUNK // BLOCK) + 1
    ncc = i // (COUNT_CHUNK // BLOCK) + 1
    lane = lax.broadcasted_iota(jnp.int32, (1, LANES), 1)
    t_idx = i * BLOCK + lane
    row_iota = lax.broadcasted_iota(jnp.int32, (KEY_CHUNK, LANES), 0)
    crow_iota = lax.broadcasted_iota(jnp.int32, (COUNT_CHUNK, LANES), 0)
    neg_inf = jnp.float32(-jnp.inf)
    f32_lowest = jnp.float32(jnp.finfo(jnp.float32).min)

    eye = jnp.where(lax.broadcasted_iota(jnp.int32, (BLOCK, LANES), 0) == lane, 1.0, 0.0).astype(MXU_DTYPE)
    for h in range(N_HEADS):
        cols = slice((h // 2) * LANES, (h // 2 + 1) * LANES)
        head_lanes = ((lane // HALF) % 2) == (h % 2)
        rows = slice(h * BLOCK, (h + 1) * BLOCK)
        qis_ref[rows, :] = jnp.where(head_lanes, qi_ref[0, :, cols], jnp.zeros((), MXU_DTYPE))
        qbs_ref[rows, :LANES] = jnp.where(head_lanes, qb_ref[0, :, cols], jnp.zeros((), MXU_DTYPE))
        qbs_ref[rows, LANES:] = eye

    def chunk_start(c):
        return pl.multiple_of(c * KEY_CHUNK, KEY_CHUNK)

    def pair_q(ref, p):
        return ref[p * 2 * BLOCK:(p + 1) * 2 * BLOCK, :]

    buf_a, buf_b = (sqa_ref, mxa_ref), (sqb_ref, mxb_ref)

    def pipeline(produce, consume):
        produce(0, buf_a)

        def two_chunks(j, carry):
            c = 2 * j
            produce(c + 1, buf_b)
            consume(c, buf_a)
            produce(c + 2, buf_a)
            consume(c + 1, buf_b)
            return carry

        lax.fori_loop(0, (nch - 1) // 2, two_chunks, 0)
        last = nch - 1

        @pl.when(last % 2 == 1)
        def _():
            produce(last, buf_b)
            consume(last - 1, buf_a)
            consume(last, buf_b)

        @pl.when(last % 2 == 0)
        def _():
            consume(last, buf_a)

    def score_chunk(c, carry):
        r0 = chunk_start(c)
        kc = kid_ref[0, pl.ds(r0, KEY_CHUNK), :]
        acc = jnp.zeros((KEY_CHUNK, LANES), jnp.float32)
        for p in range(N_HEADS // 2):
            d = _nt_dot(kc, pair_q(qis_ref, p))
            for hh in range(2):
                h = 2 * p + hh
                acc = acc + jnp.maximum(d[:, hh * BLOCK:(hh + 1) * BLOCK], 0.0) * wt_ref[0, h:h + 1, :]
        admissible = r0 + row_iota <= t_idx
        scores = acc * IDX_SCALE
        masked = jnp.where(admissible, scores, neg_inf)
        sc_ref[pl.ds(r0, KEY_CHUNK), :] = masked
        top8, bot8 = carry
        return (jnp.maximum(top8, _fold_rows(masked, jnp.max)),
                jnp.minimum(bot8, _fold_rows(jnp.where(admissible, scores, -neg_inf), jnp.min)))

    top8, bot8 = lax.fori_loop(0, nch, score_chunk, (jnp.full((8, LANES), neg_inf, jnp.float32),
                                                     jnp.full((8, LANES), -neg_inf, jnp.float32)))
    top = jnp.max(top8, axis=0, keepdims=True)
    bot = jnp.min(bot8, axis=0, keepdims=True)

    def fold_scores(value, op, pairwise, start):
        def body(c, acc):
            r0 = pl.multiple_of(c * COUNT_CHUNK, COUNT_CHUNK)
            val = value(sc_ref[pl.ds(r0, COUNT_CHUNK), :], r0)
            return pairwise(acc, op(val.reshape(8, COUNT_CHUNK // 8, LANES), axis=0))

        acc = lax.fori_loop(0, ncc // 2, lambda j, a: body(2 * j + 1, body(2 * j, a)),
                            jnp.full((COUNT_CHUNK // 8, LANES), start, jnp.float32))
        acc = lax.cond(ncc % 2 == 1, lambda a: body(ncc - 1, a), lambda a: a, acc)
        return op(acc, axis=0, keepdims=True)

    def count(pred):
        return fold_scores(lambda x, r0: jnp.where(pred(x, r0), 1.0, 0.0), jnp.sum, jnp.add, 0.0)

    key_top = _f32_to_key(top)
    lo = _f32_to_key(bot) - jnp.int32(1 << 23)
    hi = key_top + 1
    hi = jnp.where((hi >= -MIN_NORMAL_KEY) & (hi < MIN_NORMAL_KEY), jnp.int32(MIN_NORMAL_KEY), hi)
    unknown = jnp.float32(2 * sc_ref.shape[0])

    def probe(mid, carry):
        lo, hi, cnt_lo, thr_lo = carry
        thr_c = _key_to_f32(mid)
        cnt = count(lambda x, r0: x >= thr_c)
        take = cnt >= topk
        return (jnp.where(take, mid, lo), jnp.where(take, hi, mid), jnp.where(take, cnt, cnt_lo),
                jnp.where(take, thr_c, thr_lo))

    def halve(by_value, carry):
        lo, hi = carry[:2]
        mid = lo + lax.shift_right_logical(hi - lo, 1)
        if by_value:
            mid_val = _f32_to_key(0.5 * _key_to_f32(lo) + 0.5 * _key_to_f32(hi))
            mid = jnp.where((mid_val > lo) & (mid_val < hi), mid_val, mid)
        return probe(mid, carry)

    def descend(carry):
        lo, hi, cnt_lo, thr_lo = carry
        thr_hi = _key_to_f32(hi)
        v = fold_scores(lambda x, r0: jnp.where(x < thr_hi, x, neg_inf), jnp.max, jnp.maximum, neg_inf)
        cnt = count(lambda x, r0: x >= v)
        key_v = _f32_to_key(v)
        take = cnt >= topk
        return (jnp.where(take, key_v, lo), jnp.where(take, key_v + 1, key_v), jnp.where(take, cnt, cnt_lo),
                jnp.where(take, v, thr_lo))

    def pending(carry):
        lo, hi, cnt_lo = carry[:3]
        settled = (cnt_lo == topk) | (hi - lo == 1) | (t_idx < topk)
        return jnp.max(jnp.where(settled, 0.0, 1.0))

    state = lax.fori_loop(0, SEARCH_PASSES_MIN, lambda b, c: halve(True, c),
                          (lo, hi, jnp.full((1, LANES), unknown), _key_to_f32(lo)))

    def two_passes(carry):
        b, state, _ = carry
        state = lax.cond(b < VALUE_PASSES, lambda s: halve(True, halve(True, s)),
                         lambda s: lax.cond(b < VALUE_PASSES + 2 * DESCENT_STEPS, descend,
                                            lambda t: halve(False, halve(False, t)), s), state)
        return b + 2, state, pending(state)

    _, (lo, hi, cnt_ge, thr_raw), _ = lax.while_loop(
        lambda c: (c[0] < VALUE_PASSES + 2 * DESCENT_STEPS + 32) & (c[2] > 0.0), two_passes,
        (jnp.int32(SEARCH_PASSES_MIN), state, pending(state)))
    enough = thr_raw > f32_lowest
    thr = jnp.where(enough, thr_raw, f32_lowest)

    tie_lane = enough & (cnt_ge > topk) & (t_idx >= topk)

    @pl.when(jnp.max(jnp.where(tie_lane, 1.0, 0.0)) > 0.0)
    def _():
        need = topk - count(lambda x, r0: x > thr)

        def idx_step(b, lim):
            cand = lim | (jnp.int32(1) << (index_bits - 1 - b))
            below = count(lambda x, r0: (x == thr) & (r0 + crow_iota < cand))
            return jnp.where(below < need, cand, lim)

        lim = lax.fori_loop(0, index_bits, idx_step, jnp.zeros((1, LANES), jnp.int32))

        def demote(c, carry):
            r0 = pl.multiple_of(c * COUNT_CHUNK, COUNT_CHUNK)
            x = sc_ref[pl.ds(r0, COUNT_CHUNK), :]
            drop = (x == thr) & (r0 + crow_iota > lim) & tie_lane
            sc_ref[pl.ds(r0, COUNT_CHUNK), :] = jnp.where(drop, neg_inf, x)
            return carry

        lax.fori_loop(0, ncc, demote, 0)

    mask_off = float(jnp.finfo(MXU_DTYPE).min)

    def build_mask(c, carry):
        r0 = chunk_start(c)
        selected = sc_ref[pl.ds(r0, KEY_CHUNK), :] >= thr
        mb_ref[pl.ds(r0, KEY_CHUNK), :] = jnp.where(selected, 0.0, mask_off).astype(MXU_DTYPE)
        return carry

    lax.fori_loop(0, nch, build_mask, 0)

    def attn_scores(c, buf):
        sq_ref, mx_ref = buf
        r0 = chunk_start(c)
        kaug = jnp.concatenate([kbd_ref[0, pl.ds(r0, KEY_CHUNK), :], mb_ref[pl.ds(r0, KEY_CHUNK), :]], axis=1)
        for p in range(N_HEADS // 2):
            s = _nt_dot(kaug, pair_q(qbs_ref, p))
            sq_ref[:, p * 2 * BLOCK:(p + 1) * 2 * BLOCK] = s
            for hh in range(2):
                cols = slice((2 * p + hh) * BLOCK, (2 * p + hh + 1) * BLOCK)
                mx_ref[:, cols] = _fold_rows(s[:, hh * BLOCK:(hh + 1) * BLOCK], jnp.max)

    def accumulate(c, buf):
        sq_ref, mx_ref = buf
        m_run = m_ref[...]
        m_new = jnp.maximum(m_run, jnp.max(mx_ref[...], axis=0, keepdims=True))
        m_ref[...] = m_new
        acc_ref[...] = acc_ref[...] * jnp.exp2(m_run - m_new)
        vt = vbt_ref[0, c]
        for p in range(N_HEADS // 2):
            pcols = slice(p * 2 * BLOCK, (p + 1) * 2 * BLOCK)
            e = jnp.exp2(sq_ref[:, pcols] - m_new[:, pcols])
            acc_ref[:, pcols] += jnp.dot(vt, e.astype(MXU_DTYPE), preferred_element_type=jnp.float32)

    acc_ref[...] = jnp.zeros(acc_ref.shape, jnp.float32)
    m_ref[...] = jnp.full(m_ref.shape, f32_lowest, jnp.float32)
    pipeline(attn_scores, accumulate)

    o_t = acc_ref[:HEAD_DIM, :] / acc_ref[HEAD_DIM:HEAD_DIM + 1, :]
    for p in range(N_HEADS // 2):
        pair = jnp.concatenate([o_t[:, (2 * p) * BLOCK:(2 * p + 1) * BLOCK],
                                o_t[:, (2 * p + 1) * BLOCK:(2 * p + 2) * BLOCK]], axis=0)
        out_ref[0, :, p * LANES:(p + 1) * LANES] = pair.T.astype(out_ref.dtype)


def _sparse(qi, qb, wt, kid, kbd, vbt, batch, seq):
    topk = min(TOPK_MAX, seq // 4)
    nq = seq // BLOCK
    qblk = pl.BlockSpec((1, BLOCK, D_HEADS), lambda b, i: (b, i, 0))
    full = pl.BlockSpec((1, seq, LANES), lambda b, i: (b, 0, 0))
    kern = functools.partial(_sparse_kernel, topk=topk, index_bits=max(1, (seq - 1).bit_length()))
    return pl.pallas_call(
        kern,
        grid=(batch, nq),
        in_specs=[qblk, qblk,
                  pl.BlockSpec((1, IDX_HEADS, BLOCK), lambda b, i: (b, 0, i)),
                  full, full,
                  pl.BlockSpec((1, seq // KEY_CHUNK, HEAD_DIM + L_ROWS, KEY_CHUNK), lambda b, i: (b, 0, 0, 0))],
        out_specs=qblk,
        out_shape=jax.ShapeDtypeStruct((batch, seq, D_HEADS), MXU_DTYPE),
        scratch_shapes=[pltpu.VMEM((seq, LANES), jnp.float32),
                        pltpu.VMEM((seq, LANES), MXU_DTYPE),
                        pltpu.VMEM((N_HEADS * BLOCK, LANES), MXU_DTYPE),
                        pltpu.VMEM((N_HEADS * BLOCK, 2 * LANES), MXU_DTYPE),
                        pltpu.VMEM((HEAD_DIM + L_ROWS, N_HEADS * BLOCK), jnp.float32),
                        pltpu.VMEM((1, N_HEADS * BLOCK), jnp.float32),
                        pltpu.VMEM((KEY_CHUNK, N_HEADS * BLOCK), jnp.float32),
                        pltpu.VMEM((KEY_CHUNK, N_HEADS * BLOCK), jnp.float32),
                        pltpu.VMEM((8, N_HEADS * BLOCK), jnp.float32),
                        pltpu.VMEM((8, N_HEADS * BLOCK), jnp.float32)],
        compiler_params=pltpu.CompilerParams(dimension_semantics=("arbitrary", "arbitrary"),
                                             vmem_limit_bytes=VMEM_LIMIT),
        name="indexer_sparse_attn",
    )(qi, qb, wt, kid, kbd, vbt)


def _merge_kernel(x_ref, g_ref, *refs):
    n_pat = len(DILATED_PATTERNS)
    o_refs, l_refs = refs[:n_pat], refs[n_pat:2 * n_pat]
    yb_ref, wg_ref, wua_ref, wub_ref, wo_ref, x1_ref = refs[2 * n_pat:2 * n_pat + 6]
    slabs = refs[2 * n_pat + 6:]
    x = x_ref[...]
    h = _rms(x, g_ref[...]).astype(MXU_DTYPE)
    tm = x.shape[0]
    n_grp = D_HEADS // LANES

    o_src, l_src, k = [], [], 0
    for (_, d), o_ref, l_ref in zip(DILATED_PATTERNS, o_refs, l_refs):
        if d == 1:
            o_src.append(lambda g, r=o_ref: r[:, g * LANES:(g + 1) * LANES].astype(jnp.float32))
            l_src.append(lambda g, r=l_ref: r[:, g * LANES:(g + 1) * LANES])
            continue
        o_slab, l_slab = slabs[2 * k], slabs[2 * k + 1]
        k += 1
        for r in range(d):
            for g in range(n_grp):
                cols = slice(g * LANES, (g + 1) * LANES)
                o_slab[g, pl.ds(r, tm // d, stride=d), :] = o_ref[0, r, :, cols].astype(jnp.float32)
                l_slab[g, pl.ds(r, tm // d, stride=d), :] = l_ref[0, r, :, cols]
        o_src.append(lambda g, s=o_slab: s[g])
        l_src.append(lambda g, s=l_slab: s[g])

    parts = []
    for g in range(n_grp):
        ls = [f(g) for f in l_src]
        m = functools.reduce(jnp.maximum, ls)
        es = [jnp.exp(l - m) for l in ls]
        num = functools.reduce(jnp.add, [e * f(g) for e, f in zip(es, o_src)])
        parts.append((num / functools.reduce(jnp.add, es)).astype(MXU_DTYPE))
    ya = jnp.concatenate(parts, axis=1)
    ua = jnp.dot(ya, wua_ref[...], preferred_element_type=jnp.float32)
    ub = jnp.dot(yb_ref[...], wub_ref[...], preferred_element_type=jnp.float32)
    ga = jnp.dot(h, wg_ref[:, :D_MODEL], preferred_element_type=jnp.float32)
    gb = jnp.dot(h, wg_ref[:, D_MODEL:], preferred_element_type=jnp.float32)
    merged = jax.nn.sigmoid(ga) * ua + jax.nn.sigmoid(gb) * ub
    x1_ref[...] = x + jnp.dot(merged.astype(MXU_DTYPE), wo_ref[...], preferred_element_type=jnp.float32)


def _merge(x2, g, os_, lses, yb, wg, wua, wub, wo, seq):
    n = x2.shape[0]
    tm = ROW_TILE
    tiles_per_seq = seq // tm
    row = lambda i: (i, 0)
    const = lambda i: (0, 0)
    half = pl.BlockSpec((tm, D_HEADS), row)
    pat_specs = [half if d == 1 else
                 pl.BlockSpec((1, d, tm // d, D_HEADS), lambda i: (i // tiles_per_seq, 0, i % tiles_per_seq, 0))
                 for _, d in DILATED_PATTERNS]
    return pl.pallas_call(
        _merge_kernel,
        grid=(n // tm,),
        in_specs=[pl.BlockSpec((tm, D_MODEL), row), pl.BlockSpec((1, D_MODEL), const)] + pat_specs * 2 + [half] + [
            pl.BlockSpec((D_MODEL, 2 * D_MODEL), const),
            pl.BlockSpec((D_HEADS, D_MODEL), const),
            pl.BlockSpec((D_HEADS, D_MODEL), const),
            pl.BlockSpec((D_MODEL, D_MODEL), const)],
        out_specs=pl.BlockSpec((tm, D_MODEL), row),
        out_shape=jax.ShapeDtypeStruct((n, D_MODEL), jnp.float32),
        scratch_shapes=[pltpu.VMEM((D_HEADS // LANES, tm, LANES), jnp.float32)] * (2 * len(RESIDUE_DILATIONS)),
        compiler_params=pltpu.CompilerParams(dimension_semantics=("arbitrary",), vmem_limit_bytes=VMEM_LIMIT),
        name="mix_gate_out",
    )(x2, g, *os_, *lses, yb, wg, wua, wub, wo)


def _ffn_kernel(x_ref, g_ref, wgate_ref, wup_ref, wdown_ref, gf_ref, out_ref):
    x = x_ref[...]
    h = _rms(x, g_ref[...]).astype(MXU_DTYPE)
    y = x
    for c0 in range(0, D_FF, FF_CHUNK):
        a = jnp.dot(h, wgate_ref[:, c0:c0 + FF_CHUNK], preferred_element_type=jnp.float32)
        u = jnp.dot(h, wup_ref[:, c0:c0 + FF_CHUNK], preferred_element_type=jnp.float32)
        act = (a * jax.nn.sigmoid(a) * u).astype(MXU_DTYPE)
        y = y + jnp.dot(act, wdown_ref[c0:c0 + FF_CHUNK, :], preferred_element_type=jnp.float32)
    out_ref[...] = _rms(y, gf_ref[...])


def _ffn(x1, g, wgate, wup, wdown, gf):
    n = x1.shape[0]
    tm = ROW_TILE
    row = lambda i: (i, 0)
    const = lambda i: (0, 0)
    return pl.pallas_call(
        _ffn_kernel,
        grid=(n // tm,),
        in_specs=[pl.BlockSpec((tm, D_MODEL), row), pl.BlockSpec((1, D_MODEL), const),
                  pl.BlockSpec((D_MODEL, D_FF), const), pl.BlockSpec((D_MODEL, D_FF), const),
                  pl.BlockSpec((D_FF, D_MODEL), const), pl.BlockSpec((1, D_MODEL), const)],
        out_specs=pl.BlockSpec((tm, D_MODEL), row),
        out_shape=jax.ShapeDtypeStruct((n, D_MODEL), jnp.float32),
        compiler_params=pltpu.CompilerParams(dimension_semantics=("arbitrary",), vmem_limit_bytes=VMEM_LIMIT),
        name="ffn_norm",
    )(x1, g, wgate, wup, wdown, gf)


def _rope_tables(seq):
    inv_freq = ROPE_THETA ** (-jnp.arange(HALF, dtype=jnp.float32) / HALF)
    ang = jnp.arange(seq, dtype=jnp.int32).astype(jnp.float32)[:, None] * inv_freq[None, :]
    cos, sin = jnp.cos(ang), jnp.sin(ang)
    return jnp.tile(cos, (1, 4)), jnp.concatenate([-sin, -sin, sin, sin], axis=1)


def kernel(x, norm_mix, w_in, w_up_a, w_up_b, w_out, norm_ffn, w_gate, w_up, w_down, norm_final):
    batch, seq, _ = x.shape
    assert seq % max(d * BLOCK for _, d in DILATED_PATTERNS) == 0 and seq % KEY_CHUNK == 0
    assert all(w // d == BLOCK for w, d in DILATED_PATTERNS)
    n = batch * seq
    bf = MXU_DTYPE
    xf = x.reshape(n, D_MODEL)
    cos_t, sin_t = _rope_tables(seq)
    for layer in range(w_in.shape[0]):
        w = w_in[layer]
        w_pack = jnp.pad(w[:, _packed_columns()], ((0, 0), (0, _P_TOTAL - _P_MISC - HEAD_DIM - IDX_HEADS))).astype(bf)
        w_gates = w[:, _GA:].astype(bf)
        qas, kas, vas, qb, qi, kbd, kid, misc = _project(xf, norm_mix[layer][None], w_pack, cos_t, sin_t, seq)

        dil = []
        for (_, d), q, k, v in zip(DILATED_PATTERNS, qas, kas, vas):
            if d == 1:
                o, lse = _dilated(*(z.reshape(batch, 1, seq, D_HEADS) for z in (q, k, v)), d)
                dil.append((o.reshape(n, D_HEADS), lse.reshape(n, D_HEADS)))
            else:
                dil.append(_dilated(q, k, v, d))

        ones_col = (jnp.arange(L_ROWS) == 0).astype(bf)[None, :]
        vb = jnp.concatenate([misc[:, :HEAD_DIM].astype(bf), jnp.broadcast_to(ones_col, (n, L_ROWS))], axis=1)
        vbt = jnp.swapaxes(vb.reshape(batch, seq // KEY_CHUNK, KEY_CHUNK, HEAD_DIM + L_ROWS), 2, 3)
        wt = jnp.swapaxes(misc[:, HEAD_DIM:HEAD_DIM + IDX_HEADS].reshape(batch, seq, IDX_HEADS), 1, 2)
        r3 = lambda z: z.reshape(batch, seq, z.shape[-1])
        yb = _sparse(r3(qi), r3(qb), wt, r3(kid), r3(kbd), vbt, batch, seq).reshape(n, D_HEADS)

        x1 = _merge(xf, norm_mix[layer][None], [o for o, _ in dil], [l for _, l in dil], yb,
                    w_gates, w_up_a[layer].astype(bf), w_up_b[layer].astype(bf), w_out[layer].astype(bf), seq)
        last = layer == w_in.shape[0] - 1
        assert last, "the final norm is fused into the FFN kernel of the last layer"
        xf = _ffn(x1, norm_ffn[layer][None], w_gate[layer].astype(bf), w_up[layer].astype(bf),
                  w_down[layer].astype(bf), norm_final[None])
    return xf.reshape(batch, seq, D_MODEL)
```

```python
import functools

import numpy as np
import jax
import jax.numpy as jnp
from jax import lax
from jax.experimental import pallas as pl
from jax.experimental.pallas import tpu as pltpu

D_MODEL = 1024
HEAD_DIM = 64
HALF = HEAD_DIM // 2
N_HEADS = 8
D_HEADS = N_HEADS * HEAD_DIM
IDX_HEADS = 8
DILATED_PATTERNS = ((128, 1), (512, 4), (2048, 16))
RESIDUE_DILATIONS = tuple(d for _, d in DILATED_PATTERNS if d > 1)
TOPK_MAX = 256
D_FF = 2816
ROPE_THETA = 10000.0
RMS_EPS = 1e-6
BLOCK = 128
ATTN_SCALE = HEAD_DIM ** -0.5
IDX_SCALE = (HEAD_DIM ** -0.5) * (IDX_HEADS ** -0.5)
LOG2_E = float(np.log2(np.e))

LANES = 128
INT_MIN = -(2 ** 31)
MIN_NORMAL_KEY = 1 << 23
KEY_CHUNK = 1024
COUNT_CHUNK = 512
COUNT_FOLD = 4
VALUE_PASSES = 20
DESCENT_STEPS = 4
SEARCH_PASSES_MIN = 12
L_ROWS = 8
ROW_TILE = 512
FF_CHUNK = 1408
VMEM_LIMIT = 56 * 1024 * 1024
MXU_DTYPE = jnp.bfloat16

_SPLITS = (D_HEADS, D_HEADS, D_HEADS, D_HEADS, HEAD_DIM, HEAD_DIM, IDX_HEADS * HEAD_DIM, HEAD_DIM, IDX_HEADS,
           D_MODEL, D_MODEL)
_OFF = np.concatenate([[0], np.cumsum(_SPLITS)])
(_QA, _KA, _VA, _QB, _KB, _VB, _QI, _KI, _WI, _GA, _GB) = (int(o) for o in _OFF[:-1])

_P_QA, _P_KA, _P_VA, _P_QB, _P_QI = 0, 512, 1024, 1536, 2048
_P_KBD, _P_KID, _P_MISC = 2560, 2688, 2816
_P_TOTAL = 2944


def _pair_perm():
    idx = np.empty(D_HEADS, np.int64)
    for j in range(D_HEADS):
        g, l = divmod(j, LANES)
        quarter, e = divmod(l, HALF)
        head = 2 * g + (quarter % 2)
        idx[j] = head * HEAD_DIM + (quarter // 2) * HALF + e
    return idx


def _dup_perm():
    idx = np.empty(LANES, np.int64)
    for l in range(LANES):
        quarter, e = divmod(l, HALF)
        idx[l] = (quarter // 2) * HALF + e
    return idx


def _packed_columns():
    pp, dp = _pair_perm(), _dup_perm()
    return np.concatenate([
        _QA + pp, _KA + pp, _VA + np.arange(D_HEADS), _QB + pp, _QI + pp,
        _KB + dp, _KI + dp, _VB + np.arange(HEAD_DIM), _WI + np.arange(IDX_HEADS)])


def _rms(x, g):
    ms = jnp.mean(x * x, axis=-1, keepdims=True)
    return x * lax.rsqrt(ms + RMS_EPS) * g


def _nt_dot(a, b):
    return lax.dot_general(a, b, (((1,), (1,)), ((), ())), preferred_element_type=jnp.float32)


def _proj_kernel(x_ref, g_ref, w_ref, cos_ref, sin_ref, *refs):
    n_lay = 1 + len(RESIDUE_DILATIONS)
    qa_refs, ka_refs, va_refs = refs[:n_lay], refs[n_lay:2 * n_lay], refs[2 * n_lay:3 * n_lay]
    qb_ref, qi_ref, kbd_ref, kid_ref, misc_ref, h_ref, slab_ref = refs[3 * n_lay:]
    h_ref[...] = _rms(x_ref[...], g_ref[...]).astype(MXU_DTYPE)
    cos = cos_ref[...]
    sin = sin_ref[...]

    def mm(c0, width):
        return jnp.dot(h_ref[...], w_ref[:, c0:c0 + width], preferred_element_type=jnp.float32)

    def rope(z):
        parts = []
        for g in range(z.shape[1] // LANES):
            zg = z[:, g * LANES:(g + 1) * LANES]
            parts.append(zg * cos + pltpu.roll(zg, 2 * HALF, axis=1) * sin)
        return parts[0] if len(parts) == 1 else jnp.concatenate(parts, axis=1)

    def emit(y, out_refs):
        out_refs[0][...] = y.astype(MXU_DTYPE)
        for g in range(D_HEADS // LANES):
            slab_ref[g] = y[:, g * LANES:(g + 1) * LANES]
        for d, ref in zip(RESIDUE_DILATIONS, out_refs[1:]):
            rows = y.shape[0] // d
            for r in range(d):
                for g in range(D_HEADS // LANES):
                    ref[0, r, :, g * LANES:(g + 1) * LANES] = (
                        slab_ref[g, pl.ds(r, rows, stride=d), :].astype(MXU_DTYPE))

    emit(rope(mm(_P_QA, D_HEADS)) * ATTN_SCALE, qa_refs)
    emit(rope(mm(_P_KA, D_HEADS)), ka_refs)
    emit(mm(_P_VA, D_HEADS), va_refs)
    qb_ref[...] = (rope(mm(_P_QB, D_HEADS)) * (ATTN_SCALE * LOG2_E)).astype(MXU_DTYPE)
    qi_ref[...] = rope(mm(_P_QI, D_HEADS)).astype(MXU_DTYPE)
    kbd_ref[...] = rope(mm(_P_KBD, LANES)).astype(MXU_DTYPE)
    kid_ref[...] = rope(mm(_P_KID, LANES)).astype(MXU_DTYPE)
    misc_ref[...] = mm(_P_MISC, LANES)


def _project(x2, g, w_pack, cos_t, sin_t, seq):
    n = x2.shape[0]
    tm = ROW_TILE
    tiles_per_seq = seq // tm
    row = lambda i: (i, 0)
    const = lambda i: (0, 0)
    pos = lambda i: (i % tiles_per_seq, 0)
    batch = n // seq
    wide = jax.ShapeDtypeStruct((n, D_HEADS), MXU_DTYPE)
    narrow = jax.ShapeDtypeStruct((n, LANES), MXU_DTYPE)
    wide_spec = pl.BlockSpec((tm, D_HEADS), row)
    lay_shapes = [wide] + [jax.ShapeDtypeStruct((batch, d, seq // d, D_HEADS), MXU_DTYPE) for d in RESIDUE_DILATIONS]
    lay_specs = [wide_spec] + [
        pl.BlockSpec((1, d, tm // d, D_HEADS), lambda i: (i // tiles_per_seq, 0, i % tiles_per_seq, 0))
        for d in RESIDUE_DILATIONS]
    n_lay = len(lay_shapes)
    outs = pl.pallas_call(
        _proj_kernel,
        grid=(n // tm,),
        in_specs=[
            pl.BlockSpec((tm, D_MODEL), row),
            pl.BlockSpec((1, D_MODEL), const),
            pl.BlockSpec((D_MODEL, _P_TOTAL), const),
            pl.BlockSpec((tm, LANES), pos),
            pl.BlockSpec((tm, LANES), pos),
        ],
        out_specs=lay_specs * 3 + [wide_spec] * 2 + [pl.BlockSpec((tm, LANES), row)] * 3,
        out_shape=lay_shapes * 3 + [wide] * 2 + [narrow, narrow, jax.ShapeDtypeStruct((n, LANES), jnp.float32)],
        scratch_shapes=[pltpu.VMEM((tm, D_MODEL), MXU_DTYPE),
                        pltpu.VMEM((D_HEADS // LANES, tm, LANES), jnp.float32)],
        compiler_params=pltpu.CompilerParams(dimension_semantics=("arbitrary",), vmem_limit_bytes=VMEM_LIMIT),
        name="proj_rope",
    )(x2, g, w_pack, cos_t, sin_t)
    return (outs[:n_lay], outs[n_lay:2 * n_lay], outs[2 * n_lay:3 * n_lay]) + tuple(outs[3 * n_lay:])


def _dil_kernel(q_ref, k_ref, v_ref, kp_ref, vp_ref, o_ref, lse_ref):
    n = pl.program_id(2)
    qi = lax.broadcasted_iota(jnp.int32, (BLOCK, 2 * BLOCK), 0)
    kj = lax.broadcasted_iota(jnp.int32, (BLOCK, 2 * BLOCK), 1)
    band = (kj >= qi) & (kj <= qi + BLOCK) & ((kj >= BLOCK) | (n > 0))
    lane = lax.broadcasted_iota(jnp.int32, (1, LANES), 1)
    first_v = lane < HEAD_DIM
    for p in range(N_HEADS // 2):
        cols = slice(p * LANES, (p + 1) * LANES)
        qp = q_ref[0, 0, :, cols]
        k2 = jnp.concatenate([kp_ref[0, 0, :, cols], k_ref[0, 0, :, cols]], axis=0)
        v2 = jnp.concatenate([vp_ref[0, 0, :, cols], v_ref[0, 0, :, cols]], axis=0)
        outs, lses = [], []
        for hh in range(2):
            head_lanes = ((lane // HALF) % 2) == hh
            qm = jnp.where(head_lanes, qp, jnp.zeros_like(qp))
            s = jnp.where(band, _nt_dot(qm, k2), -jnp.inf)
            m = jnp.max(s, axis=-1, keepdims=True)
            e = jnp.exp(s - m)
            den = jnp.sum(e, axis=-1, keepdims=True)
            o = jnp.dot(e.astype(MXU_DTYPE), v2, preferred_element_type=jnp.float32) / den
            outs.append(o)
            lses.append(m + jnp.log(den))
        o_ref[0, 0, :, cols] = jnp.where(first_v, outs[0], outs[1]).astype(o_ref.dtype)
        lse_ref[0, 0, :, cols] = jnp.where(first_v, lses[0], lses[1])


def _dilated(q, k, v, dilation):
    batch, _, m_len, _ = q.shape
    nb = m_len // BLOCK
    cur = lambda b, r, n: (b, r, n, 0)
    prev = lambda b, r, n: (b, r, jnp.maximum(n - 1, 0), 0)
    blk = (1, 1, BLOCK, D_HEADS)
    return pl.pallas_call(
        _dil_kernel,
        grid=(batch, dilation, nb),
        in_specs=[pl.BlockSpec(blk, cur), pl.BlockSpec(blk, cur), pl.BlockSpec(blk, cur),
                  pl.BlockSpec(blk, prev), pl.BlockSpec(blk, prev)],
        out_specs=[pl.BlockSpec(blk, cur), pl.BlockSpec(blk, cur)],
        out_shape=[jax.ShapeDtypeStruct(q.shape, MXU_DTYPE), jax.ShapeDtypeStruct(q.shape, jnp.float32)],
        compiler_params=pltpu.CompilerParams(dimension_semantics=("arbitrary",) * 3),
        name=f"dilated_d{dilation}",
    )(q, k, v, k, v)


def _key_to_f32(key):
    bits = key ^ ((key >> 31) & jnp.int32(0x7FFFFFFF))
    return lax.bitcast_convert_type(bits, jnp.float32)


def _f32_to_key(x):
    bits = lax.bitcast_convert_type(x, jnp.int32)
    return bits ^ ((bits >> 31) & jnp.int32(0x7FFFFFFF))


def _fold_rows(x, op):
    rows = x.shape[0]
    y = op(x.reshape(rows // 64, 64, LANES), axis=0)
    return op(y.reshape(8, 8, LANES), axis=0)


def _sparse_kernel(qi_ref, qb_ref, wt_ref, kid_ref, kbd_ref, vbt_ref, out_ref,
                   sc_ref, mb_ref, qis_ref, qbs_ref, acc_ref, m_ref, sqa_ref, sqb_ref, mxa_ref, mxb_ref,
                   *, topk, index_bits):
    i = pl.program_id(1)
    nch = i // (KEY_CHUNK // BLOCK) + 1
    ncc = i // (COUNT_CHUNK // BLOCK) + 1
    lane = lax.broadcasted_iota(jnp.int32, (1, LANES), 1)
    t_idx = i * BLOCK + lane
    row_iota = lax.broadcasted_iota(jnp.int32, (KEY_CHUNK, LANES), 0)
    crow_iota = lax.broadcasted_iota(jnp.int32, (COUNT_CHUNK, LANES), 0)
    neg_inf = jnp.float32(-jnp.inf)
    f32_lowest = jnp.float32(jnp.finfo(jnp.float32).min)

    eye = jnp.where(lax.broadcasted_iota(jnp.int32, (BLOCK, LANES), 0) == lane, 1.0, 0.0).astype(MXU_DTYPE)
    for h in range(N_HEADS):
        cols = slice((h // 2) * LANES, (h // 2 + 1) * LANES)
        head_lanes = ((lane // HALF) % 2) == (h % 2)
        rows = slice(h * BLOCK, (h + 1) * BLOCK)
        qis_ref[rows, :] = jnp.where(head_lanes, qi_ref[0, :, cols], jnp.zeros((), MXU_DTYPE))
        qbs_ref[rows, :LANES] = jnp.where(head_lanes, qb_ref[0, :, cols], jnp.zeros((), MXU_DTYPE))
        qbs_ref[rows, LANES:] = eye

    def chunk_start(c):
        return pl.multiple_of(c * KEY_CHUNK, KEY_CHUNK)

    def pair_q(ref, p):
        return ref[p * 2 * BLOCK:(p + 1) * 2 * BLOCK, :]

    buf_a, buf_b = (sqa_ref, mxa_ref), (sqb_ref, mxb_ref)

    def pipeline(produce, consume):
        produce(0, buf_a)

        def two_chunks(j, carry):
            c = 2 * j
            produce(c + 1, buf_b)
            consume(c, buf_a)
            produce(c + 2, buf_a)
            consume(c + 1, buf_b)
            return carry

        lax.fori_loop(0, (nch - 1) // 2, two_chunks, 0)
        last = nch - 1

        @pl.when(last % 2 == 1)
        def _():
            produce(last, buf_b)
            consume(last - 1, buf_a)
            consume(last, buf_b)

        @pl.when(last % 2 == 0)
        def _():
            consume(last, buf_a)

    def score_chunk(c, carry):
        r0 = chunk_start(c)
        kc = kid_ref[0, pl.ds(r0, KEY_CHUNK), :]
        acc = jnp.zeros((KEY_CHUNK, LANES), jnp.float32)
        for p in range(N_HEADS // 2):
            d = _nt_dot(kc, pair_q(qis_ref, p))
            for hh in range(2):
                h = 2 * p + hh
                acc = acc + jnp.maximum(d[:, hh * BLOCK:(hh + 1) * BLOCK], 0.0) * wt_ref[0, h:h + 1, :]
        admissible = r0 + row_iota <= t_idx
        scores = acc * IDX_SCALE
        masked = jnp.where(admissible, scores, neg_inf)
        sc_ref[pl.ds(r0, KEY_CHUNK), :] = masked
        top8, bot8 = carry
        return (jnp.maximum(top8, _fold_rows(masked, jnp.max)),
                jnp.minimum(bot8, _fold_rows(jnp.where(admissible, scores, -neg_inf), jnp.min)))

    top8, bot8 = lax.fori_loop(0, nch, score_chunk, (jnp.full((8, LANES), neg_inf, jnp.float32),
                                                     jnp.full((8, LANES), -neg_inf, jnp.float32)))
    top = jnp.max(top8, axis=0, keepdims=True)
    bot = jnp.min(bot8, axis=0, keepdims=True)

    def fold_scores(value, op, pairwise, start):
        def body(c, acc):
            r0 = pl.multiple_of(c * COUNT_CHUNK, COUNT_CHUNK)
            val = value(sc_ref[pl.ds(r0, COUNT_CHUNK), :], r0)
            return pairwise(acc, op(val.reshape(COUNT_FOLD, COUNT_CHUNK // COUNT_FOLD, LANES), axis=0))

        acc = lax.fori_loop(0, ncc // 2, lambda j, a: body(2 * j + 1, body(2 * j, a)),
                            jnp.full((COUNT_CHUNK // COUNT_FOLD, LANES), start, jnp.float32))
        acc = lax.cond(ncc % 2 == 1, lambda a: body(ncc - 1, a), lambda a: a, acc)
        return op(acc, axis=0, keepdims=True)

    def count(pred):
        return fold_scores(lambda x, r0: jnp.where(pred(x, r0), 1.0, 0.0), jnp.sum, jnp.add, 0.0)

    key_top = _f32_to_key(top)
    lo = _f32_to_key(bot) - jnp.int32(1 << 23)
    hi = key_top + 1
    hi = jnp.where((hi >= -MIN_NORMAL_KEY) & (hi < MIN_NORMAL_KEY), jnp.int32(MIN_NORMAL_KEY), hi)
    unknown = jnp.float32(2 * sc_ref.shape[0])

    def probe(mid, carry):
        lo, hi, cnt_lo, thr_lo = carry
        thr_c = _key_to_f32(mid)
        cnt = count(lambda x, r0: x >= thr_c)
        take = cnt >= topk
        return (jnp.where(take, mid, lo), jnp.where(take, hi, mid), jnp.where(take, cnt, cnt_lo),
                jnp.where(take, thr_c, thr_lo))

    def halve(by_value, carry):
        lo, hi = carry[:2]
        mid = lo + lax.shift_right_logical(hi - lo, 1)
        if by_value:
            mid_val = _f32_to_key(0.5 * _key_to_f32(lo) + 0.5 * _key_to_f32(hi))
            mid = jnp.where((mid_val > lo) & (mid_val < hi), mid_val, mid)
        return probe(mid, carry)

    def descend(carry):
        lo, hi, cnt_lo, thr_lo = carry
        thr_hi = _key_to_f32(hi)
        v = fold_scores(lambda x, r0: jnp.where(x < thr_hi, x, neg_inf), jnp.max, jnp.maximum, neg_inf)
        cnt = count(lambda x, r0: x >= v)
        key_v = _f32_to_key(v)
        take = cnt >= topk
        return (jnp.where(take, key_v, lo), jnp.where(take, key_v + 1, key_v), jnp.where(take, cnt, cnt_lo),
                jnp.where(take, v, thr_lo))

    def pending(carry):
        lo, hi, cnt_lo = carry[:3]
        settled = (cnt_lo == topk) | (hi - lo == 1) | (t_idx < topk)
        return jnp.max(jnp.where(settled, 0.0, 1.0))

    state = lax.fori_loop(0, SEARCH_PASSES_MIN, lambda b, c: halve(True, c),
                          (lo, hi, jnp.full((1, LANES), unknown), _key_to_f32(lo)))

    def two_passes(carry):
        b, state, _ = carry
        state = lax.cond(b < VALUE_PASSES, lambda s: halve(True, halve(True, s)),
                         lambda s: lax.cond(b < VALUE_PASSES + 2 * DESCENT_STEPS, descend,
                                            lambda t: halve(False, halve(False, t)), s), state)
        return b + 2, state, pending(state)

    _, (lo, hi, cnt_ge, thr_raw), _ = lax.while_loop(
        lambda c: (c[0] < VALUE_PASSES + 2 * DESCENT_STEPS + 32) & (c[2] > 0.0), two_passes,
        (jnp.int32(SEARCH_PASSES_MIN), state, pending(state)))
    enough = thr_raw > f32_lowest
    thr = jnp.where(enough, thr_raw, f32_lowest)

    tie_lane = enough & (cnt_ge > topk) & (t_idx >= topk)

    @pl.when(jnp.max(jnp.where(tie_lane, 1.0, 0.0)) > 0.0)
    def _():
        need = topk - count(lambda x, r0: x > thr)

        def idx_step(b, lim):
            cand = lim | (jnp.int32(1) << (index_bits - 1 - b))
            below = count(lambda x, r0: (x == thr) & (r0 + crow_iota < cand))
            return jnp.where(below < need, cand, lim)

        lim = lax.fori_loop(0, index_bits, idx_step, jnp.zeros((1, LANES), jnp.int32))

        def demote(c, carry):
            r0 = pl.multiple_of(c * COUNT_CHUNK, COUNT_CHUNK)
            x = sc_ref[pl.ds(r0, COUNT_CHUNK), :]
            drop = (x == thr) & (r0 + crow_iota > lim) & tie_lane
            sc_ref[pl.ds(r0, COUNT_CHUNK), :] = jnp.where(drop, neg_inf, x)
            return carry

        lax.fori_loop(0, ncc, demote, 0)

    mask_off = float(jnp.finfo(MXU_DTYPE).min)

    def build_mask(c, carry):
        r0 = chunk_start(c)
        selected = sc_ref[pl.ds(r0, KEY_CHUNK), :] >= thr
        mb_ref[pl.ds(r0, KEY_CHUNK), :] = jnp.where(selected, 0.0, mask_off).astype(MXU_DTYPE)
        return carry

    lax.fori_loop(0, nch, build_mask, 0)

    def attn_scores(c, buf):
        sq_ref, mx_ref = buf
        r0 = chunk_start(c)
        kaug = jnp.concatenate([kbd_ref[0, pl.ds(r0, KEY_CHUNK), :], mb_ref[pl.ds(r0, KEY_CHUNK), :]], axis=1)
        for p in range(N_HEADS // 2):
            s = _nt_dot(kaug, pair_q(qbs_ref, p))
            sq_ref[:, p * 2 * BLOCK:(p + 1) * 2 * BLOCK] = s
            for hh in range(2):
                cols = slice((2 * p + hh) * BLOCK, (2 * p + hh + 1) * BLOCK)
                mx_ref[:, cols] = _fold_rows(s[:, hh * BLOCK:(hh + 1) * BLOCK], jnp.max)

    def accumulate(c, buf):
        sq_ref, mx_ref = buf
        m_run = m_ref[...]
        m_new = jnp.maximum(m_run, jnp.max(mx_ref[...], axis=0, keepdims=True))
        m_ref[...] = m_new
        acc_ref[...] = acc_ref[...] * jnp.exp2(m_run - m_new)
        vt = vbt_ref[0, c]
        for p in range(N_HEADS // 2):
            pcols = slice(p * 2 * BLOCK, (p + 1) * 2 * BLOCK)
            e = jnp.exp2(sq_ref[:, pcols] - m_new[:, pcols])
            acc_ref[:, pcols] += jnp.dot(vt, e.astype(MXU_DTYPE), preferred_element_type=jnp.float32)

    acc_ref[...] = jnp.zeros(acc_ref.shape, jnp.float32)
    m_ref[...] = jnp.full(m_ref.shape, f32_lowest, jnp.float32)
    pipeline(attn_scores, accumulate)

    o_t = acc_ref[:HEAD_DIM, :] / acc_ref[HEAD_DIM:HEAD_DIM + 1, :]
    for p in range(N_HEADS // 2):
        pair = jnp.concatenate([o_t[:, (2 * p) * BLOCK:(2 * p + 1) * BLOCK],
                                o_t[:, (2 * p + 1) * BLOCK:(2 * p + 2) * BLOCK]], axis=0)
        out_ref[0, :, p * LANES:(p + 1) * LANES] = pair.T.astype(out_ref.dtype)


def _sparse(qi, qb, wt, kid, kbd, vbt, batch, seq):
    topk = min(TOPK_MAX, seq // 4)
    nq = seq // BLOCK
    qblk = pl.BlockSpec((1, BLOCK, D_HEADS), lambda b, i: (b, i, 0))
    full = pl.BlockSpec((1, seq, LANES), lambda b, i: (b, 0, 0))
    kern = functools.partial(_sparse_kernel, topk=topk, index_bits=max(1, (seq - 1).bit_length()))
    return pl.pallas_call(
        kern,
        grid=(batch, nq),
        in_specs=[qblk, qblk,
                  pl.BlockSpec((1, IDX_HEADS, BLOCK), lambda b, i: (b, 0, i)),
                  full, full,
                  pl.BlockSpec((1, seq // KEY_CHUNK, HEAD_DIM + L_ROWS, KEY_CHUNK), lambda b, i: (b, 0, 0, 0))],
        out_specs=qblk,
        out_shape=jax.ShapeDtypeStruct((batch, seq, D_HEADS), MXU_DTYPE),
        scratch_shapes=[pltpu.VMEM((seq, LANES), jnp.float32),
                        pltpu.VMEM((seq, LANES), MXU_DTYPE),
                        pltpu.VMEM((N_HEADS * BLOCK, LANES), MXU_DTYPE),
                        pltpu.VMEM((N_HEADS * BLOCK, 2 * LANES), MXU_DTYPE),
                        pltpu.VMEM((HEAD_DIM + L_ROWS, N_HEADS * BLOCK), jnp.float32),
                        pltpu.VMEM((1, N_HEADS * BLOCK), jnp.float32),
                        pltpu.VMEM((KEY_CHUNK, N_HEADS * BLOCK), jnp.float32),
                        pltpu.VMEM((KEY_CHUNK, N_HEADS * BLOCK), jnp.float32),
                        pltpu.VMEM((8, N_HEADS * BLOCK), jnp.float32),
                        pltpu.VMEM((8, N_HEADS * BLOCK), jnp.float32)],
        compiler_params=pltpu.CompilerParams(dimension_semantics=("arbitrary", "arbitrary"),
                                             vmem_limit_bytes=VMEM_LIMIT),
        name="indexer_sparse_attn",
    )(qi, qb, wt, kid, kbd, vbt)


def _merge_kernel(x_ref, g_ref, *refs):
    n_pat = len(DILATED_PATTERNS)
    o_refs, l_refs = refs[:n_pat], refs[n_pat:2 * n_pat]
    yb_ref, wg_ref, wua_ref, wub_ref, wo_ref, x1_ref = refs[2 * n_pat:2 * n_pat + 6]
    slabs = refs[2 * n_pat + 6:]
    x = x_ref[...]
    h = _rms(x, g_ref[...]).astype(MXU_DTYPE)
    tm = x.shape[0]
    n_grp = D_HEADS // LANES

    o_src, l_src, k = [], [], 0
    for (_, d), o_ref, l_ref in zip(DILATED_PATTERNS, o_refs, l_refs):
        if d == 1:
            o_src.append(lambda g, r=o_ref: r[:, g * LANES:(g + 1) * LANES].astype(jnp.float32))
            l_src.append(lambda g, r=l_ref: r[:, g * LANES:(g + 1) * LANES])
            continue
        o_slab, l_slab = slabs[2 * k], slabs[2 * k + 1]
        k += 1
        for r in range(d):
            for g in range(n_grp):
                cols = slice(g * LANES, (g + 1) * LANES)
                o_slab[g, pl.ds(r, tm // d, stride=d), :] = o_ref[0, r, :, cols].astype(jnp.float32)
                l_slab[g, pl.ds(r, tm // d, stride=d), :] = l_ref[0, r, :, cols]
        o_src.append(lambda g, s=o_slab: s[g])
        l_src.append(lambda g, s=l_slab: s[g])

    parts = []
    for g in range(n_grp):
        ls = [f(g) for f in l_src]
        m = functools.reduce(jnp.maximum, ls)
        es = [jnp.exp(l - m) for l in ls]
        num = functools.reduce(jnp.add, [e * f(g) for e, f in zip(es, o_src)])
        parts.append((num / functools.reduce(jnp.add, es)).astype(MXU_DTYPE))
    ya = jnp.concatenate(parts, axis=1)
    ua = jnp.dot(ya, wua_ref[...], preferred_element_type=jnp.float32)
    ub = jnp.dot(yb_ref[...], wub_ref[...], preferred_element_type=jnp.float32)
    ga = jnp.dot(h, wg_ref[:, :D_MODEL], preferred_element_type=jnp.float32)
    gb = jnp.dot(h, wg_ref[:, D_MODEL:], preferred_element_type=jnp.float32)
    merged = jax.nn.sigmoid(ga) * ua + jax.nn.sigmoid(gb) * ub
    x1_ref[...] = x + jnp.dot(merged.astype(MXU_DTYPE), wo_ref[...], preferred_element_type=jnp.float32)


def _merge(x2, g, os_, lses, yb, wg, wua, wub, wo, seq):
    n = x2.shape[0]
    tm = ROW_TILE
    tiles_per_seq = seq // tm
    row = lambda i: (i, 0)
    const = lambda i: (0, 0)
    half = pl.BlockSpec((tm, D_HEADS), row)
    pat_specs = [half if d == 1 else
                 pl.BlockSpec((1, d, tm // d, D_HEADS), lambda i: (i // tiles_per_seq, 0, i % tiles_per_seq, 0))
                 for _, d in DILATED_PATTERNS]
    return pl.pallas_call(
        _merge_kernel,
        grid=(n // tm,),
        in_specs=[pl.BlockSpec((tm, D_MODEL), row), pl.BlockSpec((1, D_MODEL), const)] + pat_specs * 2 + [half] + [
            pl.BlockSpec((D_MODEL, 2 * D_MODEL), const),
            pl.BlockSpec((D_HEADS, D_MODEL), const),
            pl.BlockSpec((D_HEADS, D_MODEL), const),
            pl.BlockSpec((D_MODEL, D_MODEL), const)],
        out_specs=pl.BlockSpec((tm, D_MODEL), row),
        out_shape=jax.ShapeDtypeStruct((n, D_MODEL), jnp.float32),
        scratch_shapes=[pltpu.VMEM((D_HEADS // LANES, tm, LANES), jnp.float32)] * (2 * len(RESIDUE_DILATIONS)),
        compiler_params=pltpu.CompilerParams(dimension_semantics=("arbitrary",), vmem_limit_bytes=VMEM_LIMIT),
        name="mix_gate_out",
    )(x2, g, *os_, *lses, yb, wg, wua, wub, wo)


def _ffn_kernel(x_ref, g_ref, wgate_ref, wup_ref, wdown_ref, gf_ref, out_ref):
    x = x_ref[...]
    h = _rms(x, g_ref[...]).astype(MXU_DTYPE)
    y = x
    for c0 in range(0, D_FF, FF_CHUNK):
        a = jnp.dot(h, wgate_ref[:, c0:c0 + FF_CHUNK], preferred_element_type=jnp.float32)
        u = jnp.dot(h, wup_ref[:, c0:c0 + FF_CHUNK], preferred_element_type=jnp.float32)
        act = (a * jax.nn.sigmoid(a) * u).astype(MXU_DTYPE)
        y = y + jnp.dot(act, wdown_ref[c0:c0 + FF_CHUNK, :], preferred_element_type=jnp.float32)
    out_ref[...] = _rms(y, gf_ref[...])


def _ffn(x1, g, wgate, wup, wdown, gf):
    n = x1.shape[0]
    tm = ROW_TILE
    row = lambda i: (i, 0)
    const = lambda i: (0, 0)
    return pl.pallas_call(
        _ffn_kernel,
        grid=(n // tm,),
        in_specs=[pl.BlockSpec((tm, D_MODEL), row), pl.BlockSpec((1, D_MODEL), const),
                  pl.BlockSpec((D_MODEL, D_FF), const), pl.BlockSpec((D_MODEL, D_FF), const),
                  pl.BlockSpec((D_FF, D_MODEL), const), pl.BlockSpec((1, D_MODEL), const)],
        out_specs=pl.BlockSpec((tm, D_MODEL), row),
        out_shape=jax.ShapeDtypeStruct((n, D_MODEL), jnp.float32),
        compiler_params=pltpu.CompilerParams(dimension_semantics=("arbitrary",), vmem_limit_bytes=VMEM_LIMIT),
        name="ffn_norm",
    )(x1, g, wgate, wup, wdown, gf)


def _rope_tables(seq):
    inv_freq = ROPE_THETA ** (-jnp.arange(HALF, dtype=jnp.float32) / HALF)
    ang = jnp.arange(seq, dtype=jnp.int32).astype(jnp.float32)[:, None] * inv_freq[None, :]
    cos, sin = jnp.cos(ang), jnp.sin(ang)
    return jnp.tile(cos, (1, 4)), jnp.concatenate([-sin, -sin, sin, sin], axis=1)


def kernel(x, norm_mix, w_in, w_up_a, w_up_b, w_out, norm_ffn, w_gate, w_up, w_down, norm_final):
    batch, seq, _ = x.shape
    assert seq % max(d * BLOCK for _, d in DILATED_PATTERNS) == 0 and seq % KEY_CHUNK == 0
    assert all(w // d == BLOCK for w, d in DILATED_PATTERNS)
    n = batch * seq
    bf = MXU_DTYPE
    xf = x.reshape(n, D_MODEL)
    cos_t, sin_t = _rope_tables(seq)
    for layer in range(w_in.shape[0]):
        w = w_in[layer]
        w_pack = jnp.pad(w[:, _packed_columns()], ((0, 0), (0, _P_TOTAL - _P_MISC - HEAD_DIM - IDX_HEADS))).astype(bf)
        w_gates = w[:, _GA:].astype(bf)
        qas, kas, vas, qb, qi, kbd, kid, misc = _project(xf, norm_mix[layer][None], w_pack, cos_t, sin_t, seq)

        dil = []
        for (_, d), q, k, v in zip(DILATED_PATTERNS, qas, kas, vas):
            if d == 1:
                o, lse = _dilated(*(z.reshape(batch, 1, seq, D_HEADS) for z in (q, k, v)), d)
                dil.append((o.reshape(n, D_HEADS), lse.reshape(n, D_HEADS)))
            else:
                dil.append(_dilated(q, k, v, d))

        ones_col = (jnp.arange(L_ROWS) == 0).astype(bf)[None, :]
        vb = jnp.concatenate([misc[:, :HEAD_DIM].astype(bf), jnp.broadcast_to(ones_col, (n, L_ROWS))], axis=1)
        vbt = jnp.swapaxes(vb.reshape(batch, seq // KEY_CHUNK, KEY_CHUNK, HEAD_DIM + L_ROWS), 2, 3)
        wt = jnp.swapaxes(misc[:, HEAD_DIM:HEAD_DIM + IDX_HEADS].reshape(batch, seq, IDX_HEADS), 1, 2)
        r3 = lambda z: z.reshape(batch, seq, z.shape[-1])
        yb = _sparse(r3(qi), r3(qb), wt, r3(kid), r3(kbd), vbt, batch, seq).reshape(n, D_HEADS)

        x1 = _merge(xf, norm_mix[layer][None], [o for o, _ in dil], [l for _, l in dil], yb,
                    w_gates, w_up_a[layer].astype(bf), w_up_b[layer].astype(bf), w_out[layer].astype(bf), seq)
        last = layer == w_in.shape[0] - 1
        assert last, "the final norm is fused into the FFN kernel of the last layer"
        xf = _ffn(x1, norm_ffn[layer][None], w_gate[layer].astype(bf), w_up[layer].astype(bf),
                  w_down[layer].astype(bf), norm_final[None])
    return xf.reshape(batch, seq, D_MODEL)
```

```python
import functools

import numpy as np
import jax
import jax.numpy as jnp
from jax import lax
from jax.experimental import pallas as pl
from jax.experimental.pallas import tpu as pltpu

D_MODEL = 1024
HEAD_DIM = 64
HALF = HEAD_DIM // 2
N_HEADS = 8
D_HEADS = N_HEADS * HEAD_DIM
IDX_HEADS = 8
DILATED_PATTERNS = ((128, 1), (512, 4), (2048, 16))
RESIDUE_DILATIONS = tuple(d for _, d in DILATED_PATTERNS if d > 1)
TOPK_MAX = 256
D_FF = 2816
ROPE_THETA = 10000.0
RMS_EPS = 1e-6
BLOCK = 128
ATTN_SCALE = HEAD_DIM ** -0.5
IDX_SCALE = (HEAD_DIM ** -0.5) * (IDX_HEADS ** -0.5)
LOG2_E = float(np.log2(np.e))

LANES = 128
INT_MIN = -(2 ** 31)
MIN_NORMAL_KEY = 1 << 23
KEY_CHUNK = 1024
COUNT_CHUNK = 512
COUNT_FOLD = 8
VALUE_PASSES = 14
DESCENT_STEPS = 6
SEARCH_PASSES_MIN = 12
L_ROWS = 8
ROW_TILE = 512
FF_CHUNK = 1408
VMEM_LIMIT = 56 * 1024 * 1024
MXU_DTYPE = jnp.bfloat16

_SPLITS = (D_HEADS, D_HEADS, D_HEADS, D_HEADS, HEAD_DIM, HEAD_DIM, IDX_HEADS * HEAD_DIM, HEAD_DIM, IDX_HEADS,
           D_MODEL, D_MODEL)
_OFF = np.concatenate([[0], np.cumsum(_SPLITS)])
(_QA, _KA, _VA, _QB, _KB, _VB, _QI, _KI, _WI, _GA, _GB) = (int(o) for o in _OFF[:-1])

_P_QA, _P_KA, _P_VA, _P_QB, _P_QI = 0, 512, 1024, 1536, 2048
_P_KBD, _P_KID, _P_MISC = 2560, 2688, 2816
_P_TOTAL = 2944


def _pair_perm():
    idx = np.empty(D_HEADS, np.int64)
    for j in range(D_HEADS):
        g, l = divmod(j, LANES)
        quarter, e = divmod(l, HALF)
        head = 2 * g + (quarter % 2)
        idx[j] = head * HEAD_DIM + (quarter // 2) * HALF + e
    return idx


def _dup_perm():
    idx = np.empty(LANES, np.int64)
    for l in range(LANES):
        quarter, e = divmod(l, HALF)
        idx[l] = (quarter // 2) * HALF + e
    return idx


def _packed_columns():
    pp, dp = _pair_perm(), _dup_perm()
    return np.concatenate([
        _QA + pp, _KA + pp, _VA + np.arange(D_HEADS), _QB + pp, _QI + pp,
        _KB + dp, _KI + dp, _VB + np.arange(HEAD_DIM), _WI + np.arange(IDX_HEADS)])


def _rms(x, g):
    ms = jnp.mean(x * x, axis=-1, keepdims=True)
    return x * lax.rsqrt(ms + RMS_EPS) * g


def _nt_dot(a, b):
    return lax.dot_general(a, b, (((1,), (1,)), ((), ())), preferred_element_type=jnp.float32)


def _proj_kernel(x_ref, g_ref, w_ref, cos_ref, sin_ref, *refs):
    n_lay = 1 + len(RESIDUE_DILATIONS)
    qa_refs, ka_refs, va_refs = refs[:n_lay], refs[n_lay:2 * n_lay], refs[2 * n_lay:3 * n_lay]
    qb_ref, qi_ref, kbd_ref, kid_ref, misc_ref, h_ref, slab_ref = refs[3 * n_lay:]
    h_ref[...] = _rms(x_ref[...], g_ref[...]).astype(MXU_DTYPE)
    cos = cos_ref[...]
    sin = sin_ref[...]

    def mm(c0, width):
        return jnp.dot(h_ref[...], w_ref[:, c0:c0 + width], preferred_element_type=jnp.float32)

    def rope(z):
        parts = []
        for g in range(z.shape[1] // LANES):
            zg = z[:, g * LANES:(g + 1) * LANES]
            parts.append(zg * cos + pltpu.roll(zg, 2 * HALF, axis=1) * sin)
        return parts[0] if len(parts) == 1 else jnp.concatenate(parts, axis=1)

    def emit(y, out_refs):
        out_refs[0][...] = y.astype(MXU_DTYPE)
        for g in range(D_HEADS // LANES):
            slab_ref[g] = y[:, g * LANES:(g + 1) * LANES]
        for d, ref in zip(RESIDUE_DILATIONS, out_refs[1:]):
            rows = y.shape[0] // d
            for r in range(d):
                for g in range(D_HEADS // LANES):
                    ref[0, r, :, g * LANES:(g + 1) * LANES] = (
                        slab_ref[g, pl.ds(r, rows, stride=d), :].astype(MXU_DTYPE))

    emit(rope(mm(_P_QA, D_HEADS)) * ATTN_SCALE, qa_refs)
    emit(rope(mm(_P_KA, D_HEADS)), ka_refs)
    emit(mm(_P_VA, D_HEADS), va_refs)
    qb_ref[...] = (rope(mm(_P_QB, D_HEADS)) * (ATTN_SCALE * LOG2_E)).astype(MXU_DTYPE)
    qi_ref[...] = rope(mm(_P_QI, D_HEADS)).astype(MXU_DTYPE)
    kbd_ref[...] = rope(mm(_P_KBD, LANES)).astype(MXU_DTYPE)
    kid_ref[...] = rope(mm(_P_KID, LANES)).astype(MXU_DTYPE)
    misc_ref[...] = mm(_P_MISC, LANES)


def _project(x2, g, w_pack, cos_t, sin_t, seq):
    n = x2.shape[0]
    tm = ROW_TILE
    tiles_per_seq = seq // tm
    row = lambda i: (i, 0)
    const = lambda i: (0, 0)
    pos = lambda i: (i % tiles_per_seq, 0)
    batch = n // seq
    wide = jax.ShapeDtypeStruct((n, D_HEADS), MXU_DTYPE)
    narrow = jax.ShapeDtypeStruct((n, LANES), MXU_DTYPE)
    wide_spec = pl.BlockSpec((tm, D_HEADS), row)
    lay_shapes = [wide] + [jax.ShapeDtypeStruct((batch, d, seq // d, D_HEADS), MXU_DTYPE) for d in RESIDUE_DILATIONS]
    lay_specs = [wide_spec] + [
        pl.BlockSpec((1, d, tm // d, D_HEADS), lambda i: (i // tiles_per_seq, 0, i % tiles_per_seq, 0))
        for d in RESIDUE_DILATIONS]
    n_lay = len(lay_shapes)
    outs = pl.pallas_call(
        _proj_kernel,
        grid=(n // tm,),
        in_specs=[
            pl.BlockSpec((tm, D_MODEL), row),
            pl.BlockSpec((1, D_MODEL), const),
            pl.BlockSpec((D_MODEL, _P_TOTAL), const),
            pl.BlockSpec((tm, LANES), pos),
            pl.BlockSpec((tm, LANES), pos),
        ],
        out_specs=lay_specs * 3 + [wide_spec] * 2 + [pl.BlockSpec((tm, LANES), row)] * 3,
        out_shape=lay_shapes * 3 + [wide] * 2 + [narrow, narrow, jax.ShapeDtypeStruct((n, LANES), jnp.float32)],
        scratch_shapes=[pltpu.VMEM((tm, D_MODEL), MXU_DTYPE),
                        pltpu.VMEM((D_HEADS // LANES, tm, LANES), jnp.float32)],
        compiler_params=pltpu.CompilerParams(dimension_semantics=("arbitrary",), vmem_limit_bytes=VMEM_LIMIT),
        name="proj_rope",
    )(x2, g, w_pack, cos_t, sin_t)
    return (outs[:n_lay], outs[n_lay:2 * n_lay], outs[2 * n_lay:3 * n_lay]) + tuple(outs[3 * n_lay:])


def _dil_kernel(q_ref, k_ref, v_ref, kp_ref, vp_ref, o_ref, lse_ref):
    n = pl.program_id(2)
    qi = lax.broadcasted_iota(jnp.int32, (BLOCK, 2 * BLOCK), 0)
    kj = lax.broadcasted_iota(jnp.int32, (BLOCK, 2 * BLOCK), 1)
    band = (kj >= qi) & (kj <= qi + BLOCK) & ((kj >= BLOCK) | (n > 0))
    lane = lax.broadcasted_iota(jnp.int32, (1, LANES), 1)
    first_v = lane < HEAD_DIM
    for p in range(N_HEADS // 2):
        cols = slice(p * LANES, (p + 1) * LANES)
        qp = q_ref[0, 0, :, cols]
        k2 = jnp.concatenate([kp_ref[0, 0, :, cols], k_ref[0, 0, :, cols]], axis=0)
        v2 = jnp.concatenate([vp_ref[0, 0, :, cols], v_ref[0, 0, :, cols]], axis=0)
        outs, lses = [], []
        for hh in range(2):
            head_lanes = ((lane // HALF) % 2) == hh
            qm = jnp.where(head_lanes, qp, jnp.zeros_like(qp))
            s = jnp.where(band, _nt_dot(qm, k2), -jnp.inf)
            m = jnp.max(s, axis=-1, keepdims=True)
            e = jnp.exp(s - m)
            den = jnp.sum(e, axis=-1, keepdims=True)
            o = jnp.dot(e.astype(MXU_DTYPE), v2, preferred_element_type=jnp.float32) / den
            outs.append(o)
            lses.append(m + jnp.log(den))
        o_ref[0, 0, :, cols] = jnp.where(first_v, outs[0], outs[1]).astype(o_ref.dtype)
        lse_ref[0, 0, :, cols] = jnp.where(first_v, lses[0], lses[1])


def _dilated(q, k, v, dilation):
    batch, _, m_len, _ = q.shape
    nb = m_len // BLOCK
    cur = lambda b, r, n: (b, r, n, 0)
    prev = lambda b, r, n: (b, r, jnp.maximum(n - 1, 0), 0)
    blk = (1, 1, BLOCK, D_HEADS)
    return pl.pallas_call(
        _dil_kernel,
        grid=(batch, dilation, nb),
        in_specs=[pl.BlockSpec(blk, cur), pl.BlockSpec(blk, cur), pl.BlockSpec(blk, cur),
                  pl.BlockSpec(blk, prev), pl.BlockSpec(blk, prev)],
        out_specs=[pl.BlockSpec(blk, cur), pl.BlockSpec(blk, cur)],
        out_shape=[jax.ShapeDtypeStruct(q.shape, MXU_DTYPE), jax.ShapeDtypeStruct(q.shape, jnp.float32)],
        compiler_params=pltpu.CompilerParams(dimension_semantics=("arbitrary",) * 3),
        name=f"dilated_d{dilation}",
    )(q, k, v, k, v)


def _key_to_f32(key):
    bits = key ^ ((key >> 31) & jnp.int32(0x7FFFFFFF))
    return lax.bitcast_convert_type(bits, jnp.float32)


def _f32_to_key(x):
    bits = lax.bitcast_convert_type(x, jnp.int32)
    return bits ^ ((bits >> 31) & jnp.int32(0x7FFFFFFF))


def _fold_rows(x, op):
    rows = x.shape[0]
    y = op(x.reshape(rows // 64, 64, LANES), axis=0)
    return op(y.reshape(8, 8, LANES), axis=0)


def _sparse_kernel(qi_ref, qb_ref, wt_ref, kid_ref, kbd_ref, vbt_ref, out_ref,
                   sc_ref, mb_ref, qis_ref, qbs_ref, acc_ref, m_ref, sqa_ref, sqb_ref, mxa_ref, mxb_ref,
                   *, topk, index_bits):
    i = pl.program_id(1)
    nch = i // (KEY_CHUNK // BLOCK) + 1
    ncc = i // (COUNT_CHUNK // BLOCK) + 1
    lane = lax.broadcasted_iota(jnp.int32, (1, LANES), 1)
    t_idx = i * BLOCK + lane
    row_iota = lax.broadcasted_iota(jnp.int32, (KEY_CHUNK, LANES), 0)
    crow_iota = lax.broadcasted_iota(jnp.int32, (COUNT_CHUNK, LANES), 0)
    neg_inf = jnp.float32(-jnp.inf)
    f32_lowest = jnp.float32(jnp.finfo(jnp.float32).min)

    eye = jnp.where(lax.broadcasted_iota(jnp.int32, (BLOCK, LANES), 0) == lane, 1.0, 0.0).astype(MXU_DTYPE)
    for h in range(N_HEADS):
        cols = slice((h // 2) * LANES, (h // 2 + 1) * LANES)
        head_lanes = ((lane // HALF) % 2) == (h % 2)
        rows = slice(h * BLOCK, (h + 1) * BLOCK)
        qis_ref[rows, :] = jnp.where(head_lanes, qi_ref[0, :, cols], jnp.zeros((), MXU_DTYPE))
        qbs_ref[rows, :LANES] = jnp.where(head_lanes, qb_ref[0, :, cols], jnp.zeros((), MXU_DTYPE))
        qbs_ref[rows, LANES:] = eye

    def chunk_start(c):
        return pl.multiple_of(c * KEY_CHUNK, KEY_CHUNK)

    def pair_q(ref, p):
        return ref[p * 2 * BLOCK:(p + 1) * 2 * BLOCK, :]

    buf_a, buf_b = (sqa_ref, mxa_ref), (sqb_ref, mxb_ref)

    def pipeline(produce, consume):
        produce(0, buf_a)

        def two_chunks(j, carry):
            c = 2 * j
            produce(c + 1, buf_b)
            consume(c, buf_a)
            produce(c + 2, buf_a)
            consume(c + 1, buf_b)
            return carry

        lax.fori_loop(0, (nch - 1) // 2, two_chunks, 0)
        last = nch - 1

        @pl.when(last % 2 == 1)
        def _():
            produce(last, buf_b)
            consume(last - 1, buf_a)
            consume(last, buf_b)

        @pl.when(last % 2 == 0)
        def _():
            consume(last, buf_a)

    def score_chunk(c, carry):
        r0 = chunk_start(c)
        kc = kid_ref[0, pl.ds(r0, KEY_CHUNK), :]
        acc = jnp.zeros((KEY_CHUNK, LANES), jnp.float32)
        for p in range(N_HEADS // 2):
            d = _nt_dot(kc, pair_q(qis_ref, p))
            for hh in range(2):
                h = 2 * p + hh
                acc = acc + jnp.maximum(d[:, hh * BLOCK:(hh + 1) * BLOCK], 0.0) * wt_ref[0, h:h + 1, :]
        admissible = r0 + row_iota <= t_idx
        scores = acc * IDX_SCALE
        masked = jnp.where(admissible, scores, neg_inf)
        sc_ref[pl.ds(r0, KEY_CHUNK), :] = masked
        top8, bot8 = carry
        return (jnp.maximum(top8, _fold_rows(masked, jnp.max)),
                jnp.minimum(bot8, _fold_rows(jnp.where(admissible, scores, -neg_inf), jnp.min)))

    top8, bot8 = lax.fori_loop(0, nch, score_chunk, (jnp.full((8, LANES), neg_inf, jnp.float32),
                                                     jnp.full((8, LANES), -neg_inf, jnp.float32)))
    top = jnp.max(top8, axis=0, keepdims=True)
    bot = jnp.min(bot8, axis=0, keepdims=True)

    def fold_scores(value, op, pairwise, start):
        def body(c, acc):
            r0 = pl.multiple_of(c * COUNT_CHUNK, COUNT_CHUNK)
            val = value(sc_ref[pl.ds(r0, COUNT_CHUNK), :], r0)
            return pairwise(acc, op(val.reshape(COUNT_FOLD, COUNT_CHUNK // COUNT_FOLD, LANES), axis=0))

        acc = lax.fori_loop(0, ncc // 2, lambda j, a: body(2 * j + 1, body(2 * j, a)),
                            jnp.full((COUNT_CHUNK // COUNT_FOLD, LANES), start, jnp.float32))
        acc = lax.cond(ncc % 2 == 1, lambda a: body(ncc - 1, a), lambda a: a, acc)
        return op(acc, axis=0, keepdims=True)

    def count(pred):
        return fold_scores(lambda x, r0: jnp.where(pred(x, r0), 1.0, 0.0), jnp.sum, jnp.add, 0.0)

    key_top = _f32_to_key(top)
    lo = _f32_to_key(bot) - jnp.int32(1 << 23)
    hi = key_top + 1
    hi = jnp.where((hi >= -MIN_NORMAL_KEY) & (hi < MIN_NORMAL_KEY), jnp.int32(MIN_NORMAL_KEY), hi)
    unknown = jnp.float32(2 * sc_ref.shape[0])

    def is_settled(carry):
        lo, hi, cnt_lo = carry[:3]
        return (cnt_lo == topk) | (hi - lo == 1) | (t_idx < topk)

    def probe(mid, carry):
        lo, hi, cnt_lo, thr_lo, thr_hi = carry
        thr_c = _key_to_f32(mid)
        cnt = count(lambda x, r0: x >= thr_c)
        take = (cnt >= topk) & ~is_settled(carry)
        drop = (cnt < topk) & ~is_settled(carry)
        return (jnp.where(take, mid, lo), jnp.where(drop, mid, hi), jnp.where(take, cnt, cnt_lo),
                jnp.where(take, thr_c, thr_lo), jnp.where(drop, thr_c, thr_hi))

    def halve(by_value, carry):
        lo, hi = carry[:2]
        mid = lo + lax.shift_right_logical(hi - lo, 1)
        if by_value:
            mid_val = _f32_to_key(0.5 * _key_to_f32(lo) + 0.5 * _key_to_f32(hi))
            mid = jnp.where((mid_val > lo) & (mid_val < hi), mid_val, mid)
        return probe(mid, carry)

    def descend(carry):
        lo, hi, cnt_lo, thr_lo, thr_hi = carry
        v = fold_scores(lambda x, r0: jnp.where(x < thr_hi, x, neg_inf), jnp.max, jnp.maximum, neg_inf)
        cnt = count(lambda x, r0: x >= v)
        key_v = _f32_to_key(v)
        take = (cnt >= topk) & ~is_settled(carry)
        drop = (cnt < topk) & ~is_settled(carry)
        return (jnp.where(take, key_v, lo), jnp.where(take, key_v + 1, jnp.where(drop, key_v, hi)),
                jnp.where(take, cnt, cnt_lo), jnp.where(take, v, thr_lo), jnp.where(drop, v, thr_hi))

    def pending(carry):
        return jnp.max(jnp.where(is_settled(carry), 0.0, 1.0))

    state = lax.fori_loop(0, SEARCH_PASSES_MIN, lambda b, c: halve(True, c),
                          (lo, hi, jnp.full((1, LANES), unknown), _key_to_f32(lo), _key_to_f32(hi)))

    def two_passes(carry):
        b, state, _ = carry
        state = lax.cond(b < VALUE_PASSES, lambda s: halve(True, halve(True, s)),
                         lambda s: lax.cond(b < VALUE_PASSES + 2 * DESCENT_STEPS, descend,
                                            lambda t: halve(False, halve(False, t)), s), state)
        return b + 2, state, pending(state)

    _, (lo, hi, cnt_ge, thr_raw, _), _ = lax.while_loop(
        lambda c: (c[0] < VALUE_PASSES + 2 * DESCENT_STEPS + 32) & (c[2] > 0.0), two_passes,
        (jnp.int32(SEARCH_PASSES_MIN), state, pending(state)))
    enough = thr_raw > f32_lowest
    thr = jnp.where(enough, thr_raw, f32_lowest)

    tie_lane = enough & (cnt_ge > topk) & (t_idx >= topk)

    @pl.when(jnp.max(jnp.where(tie_lane, 1.0, 0.0)) > 0.0)
    def _():
        need = topk - count(lambda x, r0: x > thr)

        def idx_step(b, lim):
            cand = lim | (jnp.int32(1) << (index_bits - 1 - b))
            below = count(lambda x, r0: (x == thr) & (r0 + crow_iota < cand))
            return jnp.where(below < need, cand, lim)

        lim = lax.fori_loop(0, index_bits, idx_step, jnp.zeros((1, LANES), jnp.int32))

        def demote(c, carry):
            r0 = pl.multiple_of(c * COUNT_CHUNK, COUNT_CHUNK)
            x = sc_ref[pl.ds(r0, COUNT_CHUNK), :]
            drop = (x == thr) & (r0 + crow_iota > lim) & tie_lane
            sc_ref[pl.ds(r0, COUNT_CHUNK), :] = jnp.where(drop, neg_inf, x)
            return carry

        lax.fori_loop(0, ncc, demote, 0)

    mask_off = float(jnp.finfo(MXU_DTYPE).min)

    def build_mask(c, carry):
        r0 = chunk_start(c)
        selected = sc_ref[pl.ds(r0, KEY_CHUNK), :] >= thr
        mb_ref[pl.ds(r0, KEY_CHUNK), :] = jnp.where(selected, 0.0, mask_off).astype(MXU_DTYPE)
        return carry

    lax.fori_loop(0, nch, build_mask, 0)

    def attn_scores(c, buf):
        sq_ref, mx_ref = buf
        r0 = chunk_start(c)
        kaug = jnp.concatenate([kbd_ref[0, pl.ds(r0, KEY_CHUNK), :], mb_ref[pl.ds(r0, KEY_CHUNK), :]], axis=1)
        for p in range(N_HEADS // 2):
            s = _nt_dot(kaug, pair_q(qbs_ref, p))
            sq_ref[:, p * 2 * BLOCK:(p + 1) * 2 * BLOCK] = s
            for hh in range(2):
                cols = slice((2 * p + hh) * BLOCK, (2 * p + hh + 1) * BLOCK)
                mx_ref[:, cols] = _fold_rows(s[:, hh * BLOCK:(hh + 1) * BLOCK], jnp.max)

    def accumulate(c, buf):
        sq_ref, mx_ref = buf
        m_run = m_ref[...]
        m_new = jnp.maximum(m_run, jnp.max(mx_ref[...], axis=0, keepdims=True))
        m_ref[...] = m_new
        acc_ref[...] = acc_ref[...] * jnp.exp2(m_run - m_new)
        vt = vbt_ref[0, c]
        for p in range(N_HEADS // 2):
            pcols = slice(p * 2 * BLOCK, (p + 1) * 2 * BLOCK)
            e = jnp.exp2(sq_ref[:, pcols] - m_new[:, pcols])
            acc_ref[:, pcols] += jnp.dot(vt, e.astype(MXU_DTYPE), preferred_element_type=jnp.float32)

    acc_ref[...] = jnp.zeros(acc_ref.shape, jnp.float32)
    m_ref[...] = jnp.full(m_ref.shape, f32_lowest, jnp.float32)
    pipeline(attn_scores, accumulate)

    o_t = acc_ref[:HEAD_DIM, :] / acc_ref[HEAD_DIM:HEAD_DIM + 1, :]
    for p in range(N_HEADS // 2):
        pair = jnp.concatenate([o_t[:, (2 * p) * BLOCK:(2 * p + 1) * BLOCK],
                                o_t[:, (2 * p + 1) * BLOCK:(2 * p + 2) * BLOCK]], axis=0)
        out_ref[0, :, p * LANES:(p + 1) * LANES] = pair.T.astype(out_ref.dtype)


def _sparse(qi, qb, wt, kid, kbd, vbt, batch, seq):
    topk = min(TOPK_MAX, seq // 4)
    nq = seq // BLOCK
    qblk = pl.BlockSpec((1, BLOCK, D_HEADS), lambda b, i: (b, i, 0))
    full = pl.BlockSpec((1, seq, LANES), lambda b, i: (b, 0, 0))
    kern = functools.partial(_sparse_kernel, topk=topk, index_bits=max(1, (seq - 1).bit_length()))
    return pl.pallas_call(
        kern,
        grid=(batch, nq),
        in_specs=[qblk, qblk,
                  pl.BlockSpec((1, IDX_HEADS, BLOCK), lambda b, i: (b, 0, i)),
                  full, full,
                  pl.BlockSpec((1, seq // KEY_CHUNK, HEAD_DIM + L_ROWS, KEY_CHUNK), lambda b, i: (b, 0, 0, 0))],
        out_specs=qblk,
        out_shape=jax.ShapeDtypeStruct((batch, seq, D_HEADS), MXU_DTYPE),
        scratch_shapes=[pltpu.VMEM((seq, LANES), jnp.float32),
                        pltpu.VMEM((seq, LANES), MXU_DTYPE),
                        pltpu.VMEM((N_HEADS * BLOCK, LANES), MXU_DTYPE),
                        pltpu.VMEM((N_HEADS * BLOCK, 2 * LANES), MXU_DTYPE),
                        pltpu.VMEM((HEAD_DIM + L_ROWS, N_HEADS * BLOCK), jnp.float32),
                        pltpu.VMEM((1, N_HEADS * BLOCK), jnp.float32),
                        pltpu.VMEM((KEY_CHUNK, N_HEADS * BLOCK), jnp.float32),
                        pltpu.VMEM((KEY_CHUNK, N_HEADS * BLOCK), jnp.float32),
                        pltpu.VMEM((8, N_HEADS * BLOCK), jnp.float32),
                        pltpu.VMEM((8, N_HEADS * BLOCK), jnp.float32)],
        compiler_params=pltpu.CompilerParams(dimension_semantics=("arbitrary", "arbitrary"),
                                             vmem_limit_bytes=VMEM_LIMIT),
        name="indexer_sparse_attn",
    )(qi, qb, wt, kid, kbd, vbt)


def _merge_kernel(x_ref, g_ref, *refs):
    n_pat = len(DILATED_PATTERNS)
    o_refs, l_refs = refs[:n_pat], refs[n_pat:2 * n_pat]
    yb_ref, wg_ref, wua_ref, wub_ref, wo_ref, x1_ref = refs[2 * n_pat:2 * n_pat + 6]
    slabs = refs[2 * n_pat + 6:]
    x = x_ref[...]
    h = _rms(x, g_ref[...]).astype(MXU_DTYPE)
    tm = x.shape[0]
    n_grp = D_HEADS // LANES

    o_src, l_src, k = [], [], 0
    for (_, d), o_ref, l_ref in zip(DILATED_PATTERNS, o_refs, l_refs):
        if d == 1:
            o_src.append(lambda g, r=o_ref: r[:, g * LANES:(g + 1) * LANES].astype(jnp.float32))
            l_src.append(lambda g, r=l_ref: r[:, g * LANES:(g + 1) * LANES])
            continue
        o_slab, l_slab = slabs[2 * k], slabs[2 * k + 1]
        k += 1
        for r in range(d):
            for g in range(n_grp):
                cols = slice(g * LANES, (g + 1) * LANES)
                o_slab[g, pl.ds(r, tm // d, stride=d), :] = o_ref[0, r, :, cols].astype(jnp.float32)
                l_slab[g, pl.ds(r, tm // d, stride=d), :] = l_ref[0, r, :, cols]
        o_src.append(lambda g, s=o_slab: s[g])
        l_src.append(lambda g, s=l_slab: s[g])

    parts = []
    for g in range(n_grp):
        ls = [f(g) for f in l_src]
        m = functools.reduce(jnp.maximum, ls)
        es = [jnp.exp(l - m) for l in ls]
        num = functools.reduce(jnp.add, [e * f(g) for e, f in zip(es, o_src)])
        parts.append((num / functools.reduce(jnp.add, es)).astype(MXU_DTYPE))
    ya = jnp.concatenate(parts, axis=1)
    ua = jnp.dot(ya, wua_ref[...], preferred_element_type=jnp.float32)
    ub = jnp.dot(yb_ref[...], wub_ref[...], preferred_element_type=jnp.float32)
    ga = jnp.dot(h, wg_ref[:, :D_MODEL], preferred_element_type=jnp.float32)
    gb = jnp.dot(h, wg_ref[:, D_MODEL:], preferred_element_type=jnp.float32)
    merged = jax.nn.sigmoid(ga) * ua + jax.nn.sigmoid(gb) * ub
    x1_ref[...] = x + jnp.dot(merged.astype(MXU_DTYPE), wo_ref[...], preferred_element_type=jnp.float32)


def _merge(x2, g, os_, lses, yb, wg, wua, wub, wo, seq):
    n = x2.shape[0]
    tm = ROW_TILE
    tiles_per_seq = seq // tm
    row = lambda i: (i, 0)
    const = lambda i: (0, 0)
    half = pl.BlockSpec((tm, D_HEADS), row)
    pat_specs = [half if d == 1 else
                 pl.BlockSpec((1, d, tm // d, D_HEADS), lambda i: (i // tiles_per_seq, 0, i % tiles_per_seq, 0))
                 for _, d in DILATED_PATTERNS]
    return pl.pallas_call(
        _merge_kernel,
        grid=(n // tm,),
        in_specs=[pl.BlockSpec((tm, D_MODEL), row), pl.BlockSpec((1, D_MODEL), const)] + pat_specs * 2 + [half] + [
            pl.BlockSpec((D_MODEL, 2 * D_MODEL), const),
            pl.BlockSpec((D_HEADS, D_MODEL), const),
            pl.BlockSpec((D_HEADS, D_MODEL), const),
            pl.BlockSpec((D_MODEL, D_MODEL), const)],
        out_specs=pl.BlockSpec((tm, D_MODEL), row),
        out_shape=jax.ShapeDtypeStruct((n, D_MODEL), jnp.float32),
        scratch_shapes=[pltpu.VMEM((D_HEADS // LANES, tm, LANES), jnp.float32)] * (2 * len(RESIDUE_DILATIONS)),
        compiler_params=pltpu.CompilerParams(dimension_semantics=("arbitrary",), vmem_limit_bytes=VMEM_LIMIT),
        name="mix_gate_out",
    )(x2, g, *os_, *lses, yb, wg, wua, wub, wo)


def _ffn_kernel(x_ref, g_ref, wgate_ref, wup_ref, wdown_ref, gf_ref, out_ref):
    x = x_ref[...]
    h = _rms(x, g_ref[...]).astype(MXU_DTYPE)
    y = x
    for c0 in range(0, D_FF, FF_CHUNK):
        a = jnp.dot(h, wgate_ref[:, c0:c0 + FF_CHUNK], preferred_element_type=jnp.float32)
        u = jnp.dot(h, wup_ref[:, c0:c0 + FF_CHUNK], preferred_element_type=jnp.float32)
        act = (a * jax.nn.sigmoid(a) * u).astype(MXU_DTYPE)
        y = y + jnp.dot(act, wdown_ref[c0:c0 + FF_CHUNK, :], preferred_element_type=jnp.float32)
    out_ref[...] = _rms(y, gf_ref[...])


def _ffn(x1, g, wgate, wup, wdown, gf):
    n = x1.shape[0]
    tm = ROW_TILE
    row = lambda i: (i, 0)
    const = lambda i: (0, 0)
    return pl.pallas_call(
        _ffn_kernel,
        grid=(n // tm,),
        in_specs=[pl.BlockSpec((tm, D_MODEL), row), pl.BlockSpec((1, D_MODEL), const),
                  pl.BlockSpec((D_MODEL, D_FF), const), pl.BlockSpec((D_MODEL, D_FF), const),
                  pl.BlockSpec((D_FF, D_MODEL), const), pl.BlockSpec((1, D_MODEL), const)],
        out_specs=pl.BlockSpec((tm, D_MODEL), row),
        out_shape=jax.ShapeDtypeStruct((n, D_MODEL), jnp.float32),
        compiler_params=pltpu.CompilerParams(dimension_semantics=("arbitrary",), vmem_limit_bytes=VMEM_LIMIT),
        name="ffn_norm",
    )(x1, g, wgate, wup, wdown, gf)


def _rope_tables(seq):
    inv_freq = ROPE_THETA ** (-jnp.arange(HALF, dtype=jnp.float32) / HALF)
    ang = jnp.arange(seq, dtype=jnp.int32).astype(jnp.float32)[:, None] * inv_freq[None, :]
    cos, sin = jnp.cos(ang), jnp.sin(ang)
    return jnp.tile(cos, (1, 4)), jnp.concatenate([-sin, -sin, sin, sin], axis=1)


def kernel(x, norm_mix, w_in, w_up_a, w_up_b, w_out, norm_ffn, w_gate, w_up, w_down, norm_final):
    batch, seq, _ = x.shape
    assert seq % max(d * BLOCK for _, d in DILATED_PATTERNS) == 0 and seq % KEY_CHUNK == 0
    assert all(w // d == BLOCK for w, d in DILATED_PATTERNS)
    n = batch * seq
    bf = MXU_DTYPE
    xf = x.reshape(n, D_MODEL)
    cos_t, sin_t = _rope_tables(seq)
    for layer in range(w_in.shape[0]):
        w = w_in[layer]
        w_pack = jnp.pad(w[:, _packed_columns()], ((0, 0), (0, _P_TOTAL - _P_MISC - HEAD_DIM - IDX_HEADS))).astype(bf)
        w_gates = w[:, _GA:].astype(bf)
        qas, kas, vas, qb, qi, kbd, kid, misc = _project(xf, norm_mix[layer][None], w_pack, cos_t, sin_t, seq)

        dil = []
        for (_, d), q, k, v in zip(DILATED_PATTERNS, qas, kas, vas):
            if d == 1:
                o, lse = _dilated(*(z.reshape(batch, 1, seq, D_HEADS) for z in (q, k, v)), d)
                dil.append((o.reshape(n, D_HEADS), lse.reshape(n, D_HEADS)))
            else:
                dil.append(_dilated(q, k, v, d))

        ones_col = (jnp.arange(L_ROWS) == 0).astype(bf)[None, :]
        vb = jnp.concatenate([misc[:, :HEAD_DIM].astype(bf), jnp.broadcast_to(ones_col, (n, L_ROWS))], axis=1)
        vbt = jnp.swapaxes(vb.reshape(batch, seq // KEY_CHUNK, KEY_CHUNK, HEAD_DIM + L_ROWS), 2, 3)
        wt = jnp.swapaxes(misc[:, HEAD_DIM:HEAD_DIM + IDX_HEADS].reshape(batch, seq, IDX_HEADS), 1, 2)
        r3 = lambda z: z.reshape(batch, seq, z.shape[-1])
        yb = _sparse(r3(qi), r3(qb), wt, r3(kid), r3(kbd), vbt, batch, seq).reshape(n, D_HEADS)

        x1 = _merge(xf, norm_mix[layer][None], [o for o, _ in dil], [l for _, l in dil], yb,
                    w_gates, w_up_a[layer].astype(bf), w_up_b[layer].astype(bf), w_out[layer].astype(bf), seq)
        last = layer == w_in.shape[0] - 1
        assert last, "the final norm is fused into the FFN kernel of the last layer"
        xf = _ffn(x1, norm_ffn[layer][None], w_gate[layer].astype(bf), w_up[layer].astype(bf),
                  w_down[layer].astype(bf), norm_final[None])
    return xf.reshape(batch, seq, D_MODEL)
```

```python
import functools

import numpy as np
import jax
import jax.numpy as jnp
from jax import lax
from jax.experimental import pallas as pl
from jax.experimental.pallas import tpu as pltpu

D_MODEL = 1024
HEAD_DIM = 64
HALF = HEAD_DIM // 2
N_HEADS = 8
D_HEADS = N_HEADS * HEAD_DIM
IDX_HEADS = 8
DILATED_PATTERNS = ((128, 1), (512, 4), (2048, 16))
RESIDUE_DILATIONS = tuple(d for _, d in DILATED_PATTERNS if d > 1)
TOPK_MAX = 256
D_FF = 2816
ROPE_THETA = 10000.0
RMS_EPS = 1e-6
BLOCK = 128
ATTN_SCALE = HEAD_DIM ** -0.5
IDX_SCALE = (HEAD_DIM ** -0.5) * (IDX_HEADS ** -0.5)
LOG2_E = float(np.log2(np.e))

LANES = 128
INT_MIN = -(2 ** 31)
MIN_NORMAL_KEY = 1 << 23
KEY_CHUNK = 1024
COUNT_CHUNK = 512
COUNT_FOLD = 8
VALUE_PASSES = 14
DESCENT_STEPS = 6
SEARCH_PASSES_MIN = 12
L_ROWS = 8
ROW_TILE = 512
FF_CHUNK = 1408
VMEM_LIMIT = 56 * 1024 * 1024
MXU_DTYPE = jnp.bfloat16

_SPLITS = (D_HEADS, D_HEADS, D_HEADS, D_HEADS, HEAD_DIM, HEAD_DIM, IDX_HEADS * HEAD_DIM, HEAD_DIM, IDX_HEADS,
           D_MODEL, D_MODEL)
_OFF = np.concatenate([[0], np.cumsum(_SPLITS)])
(_QA, _KA, _VA, _QB, _KB, _VB, _QI, _KI, _WI, _GA, _GB) = (int(o) for o in _OFF[:-1])

_P_QA, _P_KA, _P_VA, _P_QB, _P_QI = 0, 512, 1024, 1536, 2048
_P_KBD, _P_KID, _P_MISC = 2560, 2688, 2816
_P_TOTAL = 2944


def _pair_perm():
    idx = np.empty(D_HEADS, np.int64)
    for j in range(D_HEADS):
        g, l = divmod(j, LANES)
        quarter, e = divmod(l, HALF)
        head = 2 * g + (quarter % 2)
        idx[j] = head * HEAD_DIM + (quarter // 2) * HALF + e
    return idx


def _dup_perm():
    idx = np.empty(LANES, np.int64)
    for l in range(LANES):
        quarter, e = divmod(l, HALF)
        idx[l] = (quarter // 2) * HALF + e
    return idx


def _packed_columns():
    pp, dp = _pair_perm(), _dup_perm()
    return np.concatenate([
        _QA + pp, _KA + pp, _VA + np.arange(D_HEADS), _QB + pp, _QI + pp,
        _KB + dp, _KI + dp, _VB + np.arange(HEAD_DIM), _WI + np.arange(IDX_HEADS)])


def _rms(x, g):
    ms = jnp.mean(x * x, axis=-1, keepdims=True)
    return x * lax.rsqrt(ms + RMS_EPS) * g


def _nt_dot(a, b):
    return lax.dot_general(a, b, (((1,), (1,)), ((), ())), preferred_element_type=jnp.float32)


def _proj_kernel(x_ref, g_ref, w_ref, cos_ref, sin_ref, *refs):
    n_lay = 1 + len(RESIDUE_DILATIONS)
    qa_refs, ka_refs, va_refs = refs[:n_lay], refs[n_lay:2 * n_lay], refs[2 * n_lay:3 * n_lay]
    qb_ref, qi_ref, kbd_ref, kid_ref, misc_ref, h_ref, slab_ref = refs[3 * n_lay:]
    h_ref[...] = _rms(x_ref[...], g_ref[...]).astype(MXU_DTYPE)
    cos = cos_ref[...]
    sin = sin_ref[...]

    def mm(c0, width):
        return jnp.dot(h_ref[...], w_ref[:, c0:c0 + width], preferred_element_type=jnp.float32)

    def rope(z):
        parts = []
        for g in range(z.shape[1] // LANES):
            zg = z[:, g * LANES:(g + 1) * LANES]
            parts.append(zg * cos + pltpu.roll(zg, 2 * HALF, axis=1) * sin)
        return parts[0] if len(parts) == 1 else jnp.concatenate(parts, axis=1)

    def emit(y, out_refs):
        out_refs[0][...] = y.astype(MXU_DTYPE)
        for g in range(D_HEADS // LANES):
            slab_ref[g] = y[:, g * LANES:(g + 1) * LANES]
        for d, ref in zip(RESIDUE_DILATIONS, out_refs[1:]):
            rows = y.shape[0] // d
            for r in range(d):
                for g in range(D_HEADS // LANES):
                    ref[0, r, :, g * LANES:(g + 1) * LANES] = (
                        slab_ref[g, pl.ds(r, rows, stride=d), :].astype(MXU_DTYPE))

    emit(rope(mm(_P_QA, D_HEADS)) * ATTN_SCALE, qa_refs)
    emit(rope(mm(_P_KA, D_HEADS)), ka_refs)
    emit(mm(_P_VA, D_HEADS), va_refs)
    qb_ref[...] = (rope(mm(_P_QB, D_HEADS)) * (ATTN_SCALE * LOG2_E)).astype(MXU_DTYPE)
    qi_ref[...] = rope(mm(_P_QI, D_HEADS)).astype(MXU_DTYPE)
    kbd_ref[...] = rope(mm(_P_KBD, LANES)).astype(MXU_DTYPE)
    kid_ref[...] = rope(mm(_P_KID, LANES)).astype(MXU_DTYPE)
    misc_ref[...] = mm(_P_MISC, LANES)


def _project(x2, g, w_pack, cos_t, sin_t, seq):
    n = x2.shape[0]
    tm = ROW_TILE
    tiles_per_seq = seq // tm
    row = lambda i: (i, 0)
    const = lambda i: (0, 0)
    pos = lambda i: (i % tiles_per_seq, 0)
    batch = n // seq
    wide = jax.ShapeDtypeStruct((n, D_HEADS), MXU_DTYPE)
    narrow = jax.ShapeDtypeStruct((n, LANES), MXU_DTYPE)
    wide_spec = pl.BlockSpec((tm, D_HEADS), row)
    lay_shapes = [wide] + [jax.ShapeDtypeStruct((batch, d, seq // d, D_HEADS), MXU_DTYPE) for d in RESIDUE_DILATIONS]
    lay_specs = [wide_spec] + [
        pl.BlockSpec((1, d, tm // d, D_HEADS), lambda i: (i // tiles_per_seq, 0, i % tiles_per_seq, 0))
        for d in RESIDUE_DILATIONS]
    n_lay = len(lay_shapes)
    outs = pl.pallas_call(
        _proj_kernel,
        grid=(n // tm,),
        in_specs=[
            pl.BlockSpec((tm, D_MODEL), row),
            pl.BlockSpec((1, D_MODEL), const),
            pl.BlockSpec((D_MODEL, _P_TOTAL), const),
            pl.BlockSpec((tm, LANES), pos),
            pl.BlockSpec((tm, LANES), pos),
        ],
        out_specs=lay_specs * 3 + [wide_spec] * 2 + [pl.BlockSpec((tm, LANES), row)] * 3,
        out_shape=lay_shapes * 3 + [wide] * 2 + [narrow, narrow, jax.ShapeDtypeStruct((n, LANES), jnp.float32)],
        scratch_shapes=[pltpu.VMEM((tm, D_MODEL), MXU_DTYPE),
                        pltpu.VMEM((D_HEADS // LANES, tm, LANES), jnp.float32)],
        compiler_params=pltpu.CompilerParams(dimension_semantics=("arbitrary",), vmem_limit_bytes=VMEM_LIMIT),
        name="proj_rope",
    )(x2, g, w_pack, cos_t, sin_t)
    return (outs[:n_lay], outs[n_lay:2 * n_lay], outs[2 * n_lay:3 * n_lay]) + tuple(outs[3 * n_lay:])


def _dil_kernel(q_ref, k_ref, v_ref, kp_ref, vp_ref, o_ref, lse_ref):
    n = pl.program_id(2)
    kj = lax.broadcasted_iota(jnp.int32, (2 * BLOCK, LANES), 0)
    qi = lax.broadcasted_iota(jnp.int32, (2 * BLOCK, LANES), 1)
    band = (kj >= qi) & (kj <= qi + BLOCK) & ((kj >= BLOCK) | (n > 0))
    bias = jnp.where(band, 0.0, -jnp.inf)
    lane = lax.broadcasted_iota(jnp.int32, (1, LANES), 1)
    pairs = range(N_HEADS // 2)
    col = lambda p: slice(p * LANES, (p + 1) * LANES)

    scores = []
    for p in pairs:
        qp = q_ref[0, 0, :, col(p)]
        qs = jnp.concatenate([jnp.where(((lane // HALF) % 2) == hh, qp, jnp.zeros_like(qp)) for hh in range(2)],
                             axis=0)
        k2 = jnp.concatenate([kp_ref[0, 0, :, col(p)], k_ref[0, 0, :, col(p)]], axis=0)
        scores.append(_nt_dot(k2, qs))

    probs, dens, lses = [], [], []
    for p in pairs:
        for hh in range(2):
            sh = scores[p][:, hh * BLOCK:(hh + 1) * BLOCK] + bias
            m = jnp.max(sh, axis=0, keepdims=True)
            e = jnp.exp(sh - m)
            den = jnp.sum(e, axis=0, keepdims=True)
            probs.append(e.astype(MXU_DTYPE))
            dens.append(den)
            lses.append(m + jnp.log(den))

    for p in pairs:
        v2 = jnp.concatenate([vp_ref[0, 0, :, col(p)], v_ref[0, 0, :, col(p)]], axis=0)
        o_t = lax.dot_general(v2, jnp.concatenate(probs[2 * p:2 * p + 2], axis=1), (((0,), (0,)), ((), ())),
                              preferred_element_type=jnp.float32)
        tile = jnp.concatenate([o_t[:HEAD_DIM, :BLOCK] / dens[2 * p], o_t[HEAD_DIM:, BLOCK:] / dens[2 * p + 1]],
                               axis=0)
        o_ref[0, 0, :, col(p)] = tile.T.astype(o_ref.dtype)
        lse_tile = jnp.concatenate([jnp.broadcast_to(l, (HEAD_DIM, BLOCK)) for l in lses[2 * p:2 * p + 2]], axis=0)
        lse_ref[0, 0, :, col(p)] = lse_tile.T


def _dilated(q, k, v, dilation):
    batch, _, m_len, _ = q.shape
    nb = m_len // BLOCK
    cur = lambda b, r, n: (b, r, n, 0)
    prev = lambda b, r, n: (b, r, jnp.maximum(n - 1, 0), 0)
    blk = (1, 1, BLOCK, D_HEADS)
    return pl.pallas_call(
        _dil_kernel,
        grid=(batch, dilation, nb),
        in_specs=[pl.BlockSpec(blk, cur), pl.BlockSpec(blk, cur), pl.BlockSpec(blk, cur),
                  pl.BlockSpec(blk, prev), pl.BlockSpec(blk, prev)],
        out_specs=[pl.BlockSpec(blk, cur), pl.BlockSpec(blk, cur)],
        out_shape=[jax.ShapeDtypeStruct(q.shape, MXU_DTYPE), jax.ShapeDtypeStruct(q.shape, jnp.float32)],
        compiler_params=pltpu.CompilerParams(dimension_semantics=("arbitrary",) * 3),
        name=f"dilated_d{dilation}",
    )(q, k, v, k, v)


def _key_to_f32(key):
    bits = key ^ ((key >> 31) & jnp.int32(0x7FFFFFFF))
    return lax.bitcast_convert_type(bits, jnp.float32)


def _f32_to_key(x):
    bits = lax.bitcast_convert_type(x, jnp.int32)
    return bits ^ ((bits >> 31) & jnp.int32(0x7FFFFFFF))


def _fold_rows(x, op):
    rows = x.shape[0]
    y = op(x.reshape(rows // 64, 64, LANES), axis=0)
    return op(y.reshape(8, 8, LANES), axis=0)


def _sparse_kernel(qi_ref, qb_ref, wt_ref, kid_ref, kbd_ref, vbt_ref, out_ref,
                   sc_ref, mb_ref, qis_ref, qbs_ref, acc_ref, m_ref, sqa_ref, sqb_ref, mxa_ref, mxb_ref,
                   *, topk, index_bits):
    i = pl.program_id(1)
    nch = i // (KEY_CHUNK // BLOCK) + 1
    ncc = i // (COUNT_CHUNK // BLOCK) + 1
    lane = lax.broadcasted_iota(jnp.int32, (1, LANES), 1)
    t_idx = i * BLOCK + lane
    row_iota = lax.broadcasted_iota(jnp.int32, (KEY_CHUNK, LANES), 0)
    crow_iota = lax.broadcasted_iota(jnp.int32, (COUNT_CHUNK, LANES), 0)
    neg_inf = jnp.float32(-jnp.inf)
    f32_lowest = jnp.float32(jnp.finfo(jnp.float32).min)

    eye = jnp.where(lax.broadcasted_iota(jnp.int32, (BLOCK, LANES), 0) == lane, 1.0, 0.0).astype(MXU_DTYPE)
    for h in range(N_HEADS):
        cols = slice((h // 2) * LANES, (h // 2 + 1) * LANES)
        head_lanes = ((lane // HALF) % 2) == (h % 2)
        rows = slice(h * BLOCK, (h + 1) * BLOCK)
        qis_ref[rows, :] = jnp.where(head_lanes, qi_ref[0, :, cols], jnp.zeros((), MXU_DTYPE))
        qbs_ref[rows, :LANES] = jnp.where(head_lanes, qb_ref[0, :, cols], jnp.zeros((), MXU_DTYPE))
        qbs_ref[rows, LANES:] = eye

    def chunk_start(c):
        return pl.multiple_of(c * KEY_CHUNK, KEY_CHUNK)

    def pair_q(ref, p):
        return ref[p * 2 * BLOCK:(p + 1) * 2 * BLOCK, :]

    buf_a, buf_b = (sqa_ref, mxa_ref), (sqb_ref, mxb_ref)

    def pipeline(produce, consume):
        produce(0, buf_a)

        def two_chunks(j, carry):
            c = 2 * j
            produce(c + 1, buf_b)
            consume(c, buf_a)
            produce(c + 2, buf_a)
            consume(c + 1, buf_b)
            return carry

        lax.fori_loop(0, (nch - 1) // 2, two_chunks, 0)
        last = nch - 1

        @pl.when(last % 2 == 1)
        def _():
            produce(last, buf_b)
            consume(last - 1, buf_a)
            consume(last, buf_b)

        @pl.when(last % 2 == 0)
        def _():
            consume(last, buf_a)

    def score_chunk(c, carry):
        r0 = chunk_start(c)
        kc = kid_ref[0, pl.ds(r0, KEY_CHUNK), :]
        acc = jnp.zeros((KEY_CHUNK, LANES), jnp.float32)
        for p in range(N_HEADS // 2):
            d = _nt_dot(kc, pair_q(qis_ref, p))
            for hh in range(2):
                h = 2 * p + hh
                acc = acc + jnp.maximum(d[:, hh * BLOCK:(hh + 1) * BLOCK], 0.0) * wt_ref[0, h:h + 1, :]
        admissible = r0 + row_iota <= t_idx
        scores = acc * IDX_SCALE
        masked = jnp.where(admissible, scores, neg_inf)
        sc_ref[pl.ds(r0, KEY_CHUNK), :] = masked
        top8, bot8 = carry
        return (jnp.maximum(top8, _fold_rows(masked, jnp.max)),
                jnp.minimum(bot8, _fold_rows(jnp.where(admissible, scores, -neg_inf), jnp.min)))

    top8, bot8 = lax.fori_loop(0, nch, score_chunk, (jnp.full((8, LANES), neg_inf, jnp.float32),
                                                     jnp.full((8, LANES), -neg_inf, jnp.float32)))
    top = jnp.max(top8, axis=0, keepdims=True)
    bot = jnp.min(bot8, axis=0, keepdims=True)

    def fold_scores(value, op, pairwise, start):
        def body(c, acc):
            r0 = pl.multiple_of(c * COUNT_CHUNK, COUNT_CHUNK)
            val = value(sc_ref[pl.ds(r0, COUNT_CHUNK), :], r0)
            return pairwise(acc, op(val.reshape(COUNT_FOLD, COUNT_CHUNK // COUNT_FOLD, LANES), axis=0))

        acc = lax.fori_loop(0, ncc // 2, lambda j, a: body(2 * j + 1, body(2 * j, a)),
                            jnp.full((COUNT_CHUNK // COUNT_FOLD, LANES), start, jnp.float32))
        acc = lax.cond(ncc % 2 == 1, lambda a: body(ncc - 1, a), lambda a: a, acc)
        return op(acc, axis=0, keepdims=True)

    def count(pred):
        return fold_scores(lambda x, r0: jnp.where(pred(x, r0), 1.0, 0.0), jnp.sum, jnp.add, 0.0)

    key_top = _f32_to_key(top)
    lo = _f32_to_key(bot) - jnp.int32(1 << 23)
    hi = key_top + 1
    hi = jnp.where((hi >= -MIN_NORMAL_KEY) & (hi < MIN_NORMAL_KEY), jnp.int32(MIN_NORMAL_KEY), hi)
    unknown = jnp.float32(2 * sc_ref.shape[0])

    def is_settled(carry):
        lo, hi, cnt_lo = carry[:3]
        return (cnt_lo == topk) | (hi - lo == 1) | (t_idx < topk)

    def probe(mid, carry):
        lo, hi, cnt_lo, thr_lo, thr_hi = carry
        thr_c = _key_to_f32(mid)
        cnt = count(lambda x, r0: x >= thr_c)
        take = (cnt >= topk) & ~is_settled(carry)
        drop = (cnt < topk) & ~is_settled(carry)
        return (jnp.where(take, mid, lo), jnp.where(drop, mid, hi), jnp.where(take, cnt, cnt_lo),
                jnp.where(take, thr_c, thr_lo), jnp.where(drop, thr_c, thr_hi))

    def halve(by_value, carry):
        lo, hi = carry[:2]
        mid = lo + lax.shift_right_logical(hi - lo, 1)
        if by_value:
            mid_val = _f32_to_key(0.5 * _key_to_f32(lo) + 0.5 * _key_to_f32(hi))
            mid = jnp.where((mid_val > lo) & (mid_val < hi), mid_val, mid)
        return probe(mid, carry)

    def descend(carry):
        lo, hi, cnt_lo, thr_lo, thr_hi = carry
        v = fold_scores(lambda x, r0: jnp.where(x < thr_hi, x, neg_inf), jnp.max, jnp.maximum, neg_inf)
        cnt = count(lambda x, r0: x >= v)
        key_v = _f32_to_key(v)
        take = (cnt >= topk) & ~is_settled(carry)
        drop = (cnt < topk) & ~is_settled(carry)
        return (jnp.where(take, key_v, lo), jnp.where(take, key_v + 1, jnp.where(drop, key_v, hi)),
                jnp.where(take, cnt, cnt_lo), jnp.where(take, v, thr_lo), jnp.where(drop, v, thr_hi))

    def pending(carry):
        return jnp.max(jnp.where(is_settled(carry), 0.0, 1.0))

    state = lax.fori_loop(0, SEARCH_PASSES_MIN, lambda b, c: halve(True, c),
                          (lo, hi, jnp.full((1, LANES), unknown), _key_to_f32(lo), _key_to_f32(hi)))

    def two_passes(carry):
        b, state, _ = carry
        state = lax.cond(b < VALUE_PASSES, lambda s: halve(True, halve(True, s)),
                         lambda s: lax.cond(b < VALUE_PASSES + 2 * DESCENT_STEPS, descend,
                                            lambda t: halve(False, halve(False, t)), s), state)
        return b + 2, state, pending(state)

    _, (lo, hi, cnt_ge, thr_raw, _), _ = lax.while_loop(
        lambda c: (c[0] < VALUE_PASSES + 2 * DESCENT_STEPS + 32) & (c[2] > 0.0), two_passes,
        (jnp.int32(SEARCH_PASSES_MIN), state, pending(state)))
    enough = thr_raw > f32_lowest
    thr = jnp.where(enough, thr_raw, f32_lowest)

    tie_lane = enough & (cnt_ge > topk) & (t_idx >= topk)

    @pl.when(jnp.max(jnp.where(tie_lane, 1.0, 0.0)) > 0.0)
    def _():
        need = topk - count(lambda x, r0: x > thr)

        def idx_step(b, lim):
            cand = lim | (jnp.int32(1) << (index_bits - 1 - b))
            below = count(lambda x, r0: (x == thr) & (r0 + crow_iota < cand))
            return jnp.where(below < need, cand, lim)

        lim = lax.fori_loop(0, index_bits, idx_step, jnp.zeros((1, LANES), jnp.int32))

        def demote(c, carry):
            r0 = pl.multiple_of(c * COUNT_CHUNK, COUNT_CHUNK)
            x = sc_ref[pl.ds(r0, COUNT_CHUNK), :]
            drop = (x == thr) & (r0 + crow_iota > lim) & tie_lane
            sc_ref[pl.ds(r0, COUNT_CHUNK), :] = jnp.where(drop, neg_inf, x)
            return carry

        lax.fori_loop(0, ncc, demote, 0)

    mask_off = float(jnp.finfo(MXU_DTYPE).min)

    def build_mask(c, carry):
        r0 = chunk_start(c)
        selected = sc_ref[pl.ds(r0, KEY_CHUNK), :] >= thr
        mb_ref[pl.ds(r0, KEY_CHUNK), :] = jnp.where(selected, 0.0, mask_off).astype(MXU_DTYPE)
        return carry

    lax.fori_loop(0, nch, build_mask, 0)

    def attn_scores(c, buf):
        sq_ref, mx_ref = buf
        r0 = chunk_start(c)
        kaug = jnp.concatenate([kbd_ref[0, pl.ds(r0, KEY_CHUNK), :], mb_ref[pl.ds(r0, KEY_CHUNK), :]], axis=1)
        for p in range(N_HEADS // 2):
            s = _nt_dot(kaug, pair_q(qbs_ref, p))
            sq_ref[:, p * 2 * BLOCK:(p + 1) * 2 * BLOCK] = s
            for hh in range(2):
                cols = slice((2 * p + hh) * BLOCK, (2 * p + hh + 1) * BLOCK)
                mx_ref[:, cols] = _fold_rows(s[:, hh * BLOCK:(hh + 1) * BLOCK], jnp.max)

    def accumulate(c, buf):
        sq_ref, mx_ref = buf
        m_run = m_ref[...]
        m_new = jnp.maximum(m_run, jnp.max(mx_ref[...], axis=0, keepdims=True))
        m_ref[...] = m_new
        acc_ref[...] = acc_ref[...] * jnp.exp2(m_run - m_new)
        vt = vbt_ref[0, c]
        for p in range(N_HEADS // 2):
            pcols = slice(p * 2 * BLOCK, (p + 1) * 2 * BLOCK)
            e = jnp.exp2(sq_ref[:, pcols] - m_new[:, pcols])
            acc_ref[:, pcols] += jnp.dot(vt, e.astype(MXU_DTYPE), preferred_element_type=jnp.float32)

    acc_ref[...] = jnp.zeros(acc_ref.shape, jnp.float32)
    m_ref[...] = jnp.full(m_ref.shape, f32_lowest, jnp.float32)
    pipeline(attn_scores, accumulate)

    o_t = acc_ref[:HEAD_DIM, :] / acc_ref[HEAD_DIM:HEAD_DIM + 1, :]
    for p in range(N_HEADS // 2):
        pair = jnp.concatenate([o_t[:, (2 * p) * BLOCK:(2 * p + 1) * BLOCK],
                                o_t[:, (2 * p + 1) * BLOCK:(2 * p + 2) * BLOCK]], axis=0)
        out_ref[0, :, p * LANES:(p + 1) * LANES] = pair.T.astype(out_ref.dtype)


def _sparse(qi, qb, wt, kid, kbd, vbt, batch, seq):
    topk = min(TOPK_MAX, seq // 4)
    nq = seq // BLOCK
    qblk = pl.BlockSpec((1, BLOCK, D_HEADS), lambda b, i: (b, i, 0))
    full = pl.BlockSpec((1, seq, LANES), lambda b, i: (b, 0, 0))
    kern = functools.partial(_sparse_kernel, topk=topk, index_bits=max(1, (seq - 1).bit_length()))
    return pl.pallas_call(
        kern,
        grid=(batch, nq),
        in_specs=[qblk, qblk,
                  pl.BlockSpec((1, IDX_HEADS, BLOCK), lambda b, i: (b, 0, i)),
                  full, full,
                  pl.BlockSpec((1, seq // KEY_CHUNK, HEAD_DIM + L_ROWS, KEY_CHUNK), lambda b, i: (b, 0, 0, 0))],
        out_specs=qblk,
        out_shape=jax.ShapeDtypeStruct((batch, seq, D_HEADS), MXU_DTYPE),
        scratch_shapes=[pltpu.VMEM((seq, LANES), jnp.float32),
                        pltpu.VMEM((seq, LANES), MXU_DTYPE),
                        pltpu.VMEM((N_HEADS * BLOCK, LANES), MXU_DTYPE),
                        pltpu.VMEM((N_HEADS * BLOCK, 2 * LANES), MXU_DTYPE),
                        pltpu.VMEM((HEAD_DIM + L_ROWS, N_HEADS * BLOCK), jnp.float32),
                        pltpu.VMEM((1, N_HEADS * BLOCK), jnp.float32),
                        pltpu.VMEM((KEY_CHUNK, N_HEADS * BLOCK), jnp.float32),
                        pltpu.VMEM((KEY_CHUNK, N_HEADS * BLOCK), jnp.float32),
                        pltpu.VMEM((8, N_HEADS * BLOCK), jnp.float32),
                        pltpu.VMEM((8, N_HEADS * BLOCK), jnp.float32)],
        compiler_params=pltpu.CompilerParams(dimension_semantics=("arbitrary", "arbitrary"),
                                             vmem_limit_bytes=VMEM_LIMIT),
        name="indexer_sparse_attn",
    )(qi, qb, wt, kid, kbd, vbt)


def _merge_kernel(x_ref, g_ref, *refs):
    n_pat = len(DILATED_PATTERNS)
    o_refs, l_refs = refs[:n_pat], refs[n_pat:2 * n_pat]
    yb_ref, wg_ref, wua_ref, wub_ref, wo_ref, x1_ref = refs[2 * n_pat:2 * n_pat + 6]
    slabs = refs[2 * n_pat + 6:]
    x = x_ref[...]
    h = _rms(x, g_ref[...]).astype(MXU_DTYPE)
    tm = x.shape[0]
    n_grp = D_HEADS // LANES

    o_src, l_src, k = [], [], 0
    for (_, d), o_ref, l_ref in zip(DILATED_PATTERNS, o_refs, l_refs):
        if d == 1:
            o_src.append(lambda g, r=o_ref: r[:, g * LANES:(g + 1) * LANES].astype(jnp.float32))
            l_src.append(lambda g, r=l_ref: r[:, g * LANES:(g + 1) * LANES])
            continue
        o_slab, l_slab = slabs[2 * k], slabs[2 * k + 1]
        k += 1
        for r in range(d):
            for g in range(n_grp):
                cols = slice(g * LANES, (g + 1) * LANES)
                o_slab[g, pl.ds(r, tm // d, stride=d), :] = o_ref[0, r, :, cols].astype(jnp.float32)
                l_slab[g, pl.ds(r, tm // d, stride=d), :] = l_ref[0, r, :, cols]
        o_src.append(lambda g, s=o_slab: s[g])
        l_src.append(lambda g, s=l_slab: s[g])

    parts = []
    for g in range(n_grp):
        ls = [f(g) for f in l_src]
        m = functools.reduce(jnp.maximum, ls)
        es = [jnp.exp(l - m) for l in ls]
        num = functools.reduce(jnp.add, [e * f(g) for e, f in zip(es, o_src)])
        parts.append((num / functools.reduce(jnp.add, es)).astype(MXU_DTYPE))
    ya = jnp.concatenate(parts, axis=1)
    ua = jnp.dot(ya, wua_ref[...], preferred_element_type=jnp.float32)
    ub = jnp.dot(yb_ref[...], wub_ref[...], preferred_element_type=jnp.float32)
    ga = jnp.dot(h, wg_ref[:, :D_MODEL], preferred_element_type=jnp.float32)
    gb = jnp.dot(h, wg_ref[:, D_MODEL:], preferred_element_type=jnp.float32)
    merged = jax.nn.sigmoid(ga) * ua + jax.nn.sigmoid(gb) * ub
    x1_ref[...] = x + jnp.dot(merged.astype(MXU_DTYPE), wo_ref[...], preferred_element_type=jnp.float32)


def _merge(x2, g, os_, lses, yb, wg, wua, wub, wo, seq):
    n = x2.shape[0]
    tm = ROW_TILE
    tiles_per_seq = seq // tm
    row = lambda i: (i, 0)
    const = lambda i: (0, 0)
    half = pl.BlockSpec((tm, D_HEADS), row)
    pat_specs = [half if d == 1 else
                 pl.BlockSpec((1, d, tm // d, D_HEADS), lambda i: (i // tiles_per_seq, 0, i % tiles_per_seq, 0))
                 for _, d in DILATED_PATTERNS]
    return pl.pallas_call(
        _merge_kernel,
        grid=(n // tm,),
        in_specs=[pl.BlockSpec((tm, D_MODEL), row), pl.BlockSpec((1, D_MODEL), const)] + pat_specs * 2 + [half] + [
            pl.BlockSpec((D_MODEL, 2 * D_MODEL), const),
            pl.BlockSpec((D_HEADS, D_MODEL), const),
            pl.BlockSpec((D_HEADS, D_MODEL), const),
            pl.BlockSpec((D_MODEL, D_MODEL), const)],
        out_specs=pl.BlockSpec((tm, D_MODEL), row),
        out_shape=jax.ShapeDtypeStruct((n, D_MODEL), jnp.float32),
        scratch_shapes=[pltpu.VMEM((D_HEADS // LANES, tm, LANES), jnp.float32)] * (2 * len(RESIDUE_DILATIONS)),
        compiler_params=pltpu.CompilerParams(dimension_semantics=("arbitrary",), vmem_limit_bytes=VMEM_LIMIT),
        name="mix_gate_out",
    )(x2, g, *os_, *lses, yb, wg, wua, wub, wo)


def _ffn_kernel(x_ref, g_ref, wgate_ref, wup_ref, wdown_ref, gf_ref, out_ref):
    x = x_ref[...]
    h = _rms(x, g_ref[...]).astype(MXU_DTYPE)
    y = x
    for c0 in range(0, D_FF, FF_CHUNK):
        a = jnp.dot(h, wgate_ref[:, c0:c0 + FF_CHUNK], preferred_element_type=jnp.float32)
        u = jnp.dot(h, wup_ref[:, c0:c0 + FF_CHUNK], preferred_element_type=jnp.float32)
        act = (a * jax.nn.sigmoid(a) * u).astype(MXU_DTYPE)
        y = y + jnp.dot(act, wdown_ref[c0:c0 + FF_CHUNK, :], preferred_element_type=jnp.float32)
    out_ref[...] = _rms(y, gf_ref[...])


def _ffn(x1, g, wgate, wup, wdown, gf):
    n = x1.shape[0]
    tm = ROW_TILE
    row = lambda i: (i, 0)
    const = lambda i: (0, 0)
    return pl.pallas_call(
        _ffn_kernel,
        grid=(n // tm,),
        in_specs=[pl.BlockSpec((tm, D_MODEL), row), pl.BlockSpec((1, D_MODEL), const),
                  pl.BlockSpec((D_MODEL, D_FF), const), pl.BlockSpec((D_MODEL, D_FF), const),
                  pl.BlockSpec((D_FF, D_MODEL), const), pl.BlockSpec((1, D_MODEL), const)],
        out_specs=pl.BlockSpec((tm, D_MODEL), row),
        out_shape=jax.ShapeDtypeStruct((n, D_MODEL), jnp.float32),
        compiler_params=pltpu.CompilerParams(dimension_semantics=("arbitrary",), vmem_limit_bytes=VMEM_LIMIT),
        name="ffn_norm",
    )(x1, g, wgate, wup, wdown, gf)


def _rope_tables(seq):
    inv_freq = ROPE_THETA ** (-jnp.arange(HALF, dtype=jnp.float32) / HALF)
    ang = jnp.arange(seq, dtype=jnp.int32).astype(jnp.float32)[:, None] * inv_freq[None, :]
    cos, sin = jnp.cos(ang), jnp.sin(ang)
    return jnp.tile(cos, (1, 4)), jnp.concatenate([-sin, -sin, sin, sin], axis=1)


def kernel(x, norm_mix, w_in, w_up_a, w_up_b, w_out, norm_ffn, w_gate, w_up, w_down, norm_final):
    batch, seq, _ = x.shape
    assert seq % max(d * BLOCK for _, d in DILATED_PATTERNS) == 0 and seq % KEY_CHUNK == 0
    assert all(w // d == BLOCK for w, d in DILATED_PATTERNS)
    n = batch * seq
    bf = MXU_DTYPE
    xf = x.reshape(n, D_MODEL)
    cos_t, sin_t = _rope_tables(seq)
    for layer in range(w_in.shape[0]):
        w = w_in[layer]
        w_pack = jnp.pad(w[:, _packed_columns()], ((0, 0), (0, _P_TOTAL - _P_MISC - HEAD_DIM - IDX_HEADS))).astype(bf)
        w_gates = w[:, _GA:].astype(bf)
        qas, kas, vas, qb, qi, kbd, kid, misc = _project(xf, norm_mix[layer][None], w_pack, cos_t, sin_t, seq)

        dil = []
        for (_, d), q, k, v in zip(DILATED_PATTERNS, qas, kas, vas):
            if d == 1:
                o, lse = _dilated(*(z.reshape(batch, 1, seq, D_HEADS) for z in (q, k, v)), d)
                dil.append((o.reshape(n, D_HEADS), lse.reshape(n, D_HEADS)))
            else:
                dil.append(_dilated(q, k, v, d))

        ones_col = (jnp.arange(L_ROWS) == 0).astype(bf)[None, :]
        vb = jnp.concatenate([misc[:, :HEAD_DIM].astype(bf), jnp.broadcast_to(ones_col, (n, L_ROWS))], axis=1)
        vbt = jnp.swapaxes(vb.reshape(batch, seq // KEY_CHUNK, KEY_CHUNK, HEAD_DIM + L_ROWS), 2, 3)
        wt = jnp.swapaxes(misc[:, HEAD_DIM:HEAD_DIM + IDX_HEADS].reshape(batch, seq, IDX_HEADS), 1, 2)
        r3 = lambda z: z.reshape(batch, seq, z.shape[-1])
        yb = _sparse(r3(qi), r3(qb), wt, r3(kid), r3(kbd), vbt, batch, seq).reshape(n, D_HEADS)

        x1 = _merge(xf, norm_mix[layer][None], [o for o, _ in dil], [l for _, l in dil], yb,
                    w_gates, w_up_a[layer].astype(bf), w_up_b[layer].astype(bf), w_out[layer].astype(bf), seq)
        last = layer == w_in.shape[0] - 1
        assert last, "the final norm is fused into the FFN kernel of the last layer"
        xf = _ffn(x1, norm_ffn[layer][None], w_gate[layer].astype(bf), w_up[layer].astype(bf),
                  w_down[layer].astype(bf), norm_final[None])
    return xf.reshape(batch, seq, D_MODEL)
```

```python
import functools

import numpy as np
import jax
import jax.numpy as jnp
from jax import lax
from jax.experimental import pallas as pl
from jax.experimental.pallas import tpu as pltpu

D_MODEL = 1024
HEAD_DIM = 64
HALF = HEAD_DIM // 2
N_HEADS = 8
D_HEADS = N_HEADS * HEAD_DIM
IDX_HEADS = 8
DILATED_PATTERNS = ((128, 1), (512, 4), (2048, 16))
RESIDUE_DILATIONS = tuple(d for _, d in DILATED_PATTERNS if d > 1)
TOPK_MAX = 256
D_FF = 2816
ROPE_THETA = 10000.0
RMS_EPS = 1e-6
BLOCK = 128
ATTN_SCALE = HEAD_DIM ** -0.5
IDX_SCALE = (HEAD_DIM ** -0.5) * (IDX_HEADS ** -0.5)
LOG2_E = float(np.log2(np.e))

LANES = 128
INT_MIN = -(2 ** 31)
MIN_NORMAL_KEY = 1 << 23
KEY_CHUNK = 1024
COUNT_CHUNK = 512
COUNT_FOLD = 8
VALUE_PASSES = 14
DESCENT_STEPS = 6
SEARCH_PASSES_MIN = 12
L_ROWS = 8
ROW_TILE = 512
DIL_BLOCKS = 4
FF_CHUNK = 1408
VMEM_LIMIT = 56 * 1024 * 1024
MXU_DTYPE = jnp.bfloat16

_SPLITS = (D_HEADS, D_HEADS, D_HEADS, D_HEADS, HEAD_DIM, HEAD_DIM, IDX_HEADS * HEAD_DIM, HEAD_DIM, IDX_HEADS,
           D_MODEL, D_MODEL)
_OFF = np.concatenate([[0], np.cumsum(_SPLITS)])
(_QA, _KA, _VA, _QB, _KB, _VB, _QI, _KI, _WI, _GA, _GB) = (int(o) for o in _OFF[:-1])

_P_QA, _P_KA, _P_VA, _P_QB, _P_QI = 0, 512, 1024, 1536, 2048
_P_KBD, _P_KID, _P_MISC = 2560, 2688, 2816
_P_TOTAL = 2944


def _pair_perm():
    idx = np.empty(D_HEADS, np.int64)
    for j in range(D_HEADS):
        g, l = divmod(j, LANES)
        quarter, e = divmod(l, HALF)
        head = 2 * g + (quarter % 2)
        idx[j] = head * HEAD_DIM + (quarter // 2) * HALF + e
    return idx


def _dup_perm():
    idx = np.empty(LANES, np.int64)
    for l in range(LANES):
        quarter, e = divmod(l, HALF)
        idx[l] = (quarter // 2) * HALF + e
    return idx


def _packed_columns():
    pp, dp = _pair_perm(), _dup_perm()
    return np.concatenate([
        _QA + pp, _KA + pp, _VA + np.arange(D_HEADS), _QB + pp, _QI + pp,
        _KB + dp, _KI + dp, _VB + np.arange(HEAD_DIM), _WI + np.arange(IDX_HEADS)])


def _rms(x, g):
    ms = jnp.mean(x * x, axis=-1, keepdims=True)
    return x * lax.rsqrt(ms + RMS_EPS) * g


def _nt_dot(a, b):
    return lax.dot_general(a, b, (((1,), (1,)), ((), ())), preferred_element_type=jnp.float32)


def _proj_kernel(x_ref, g_ref, w_ref, cos_ref, sin_ref, *refs):
    n_lay = 1 + len(RESIDUE_DILATIONS)
    qa_refs, ka_refs, va_refs = refs[:n_lay], refs[n_lay:2 * n_lay], refs[2 * n_lay:3 * n_lay]
    qb_ref, qi_ref, kbd_ref, kid_ref, misc_ref, h_ref, slab_ref = refs[3 * n_lay:]
    h_ref[...] = _rms(x_ref[...], g_ref[...]).astype(MXU_DTYPE)
    cos = cos_ref[...]
    sin = sin_ref[...]

    def mm(c0, width):
        return jnp.dot(h_ref[...], w_ref[:, c0:c0 + width], preferred_element_type=jnp.float32)

    def rope(z):
        parts = []
        for g in range(z.shape[1] // LANES):
            zg = z[:, g * LANES:(g + 1) * LANES]
            parts.append(zg * cos + pltpu.roll(zg, 2 * HALF, axis=1) * sin)
        return parts[0] if len(parts) == 1 else jnp.concatenate(parts, axis=1)

    def emit(y, out_refs):
        out_refs[0][...] = y.astype(MXU_DTYPE)
        for g in range(D_HEADS // LANES):
            slab_ref[g] = y[:, g * LANES:(g + 1) * LANES]
        for d, ref in zip(RESIDUE_DILATIONS, out_refs[1:]):
            rows = y.shape[0] // d
            for r in range(d):
                for g in range(D_HEADS // LANES):
                    ref[0, r, :, g * LANES:(g + 1) * LANES] = (
                        slab_ref[g, pl.ds(r, rows, stride=d), :].astype(MXU_DTYPE))

    emit(rope(mm(_P_QA, D_HEADS)) * ATTN_SCALE, qa_refs)
    emit(rope(mm(_P_KA, D_HEADS)), ka_refs)
    emit(mm(_P_VA, D_HEADS), va_refs)
    qb_ref[...] = (rope(mm(_P_QB, D_HEADS)) * (ATTN_SCALE * LOG2_E)).astype(MXU_DTYPE)
    qi_ref[...] = rope(mm(_P_QI, D_HEADS)).astype(MXU_DTYPE)
    kbd_ref[...] = rope(mm(_P_KBD, LANES)).astype(MXU_DTYPE)
    kid_ref[...] = rope(mm(_P_KID, LANES)).astype(MXU_DTYPE)
    misc_ref[...] = mm(_P_MISC, LANES)


def _project(x2, g, w_pack, cos_t, sin_t, seq):
    n = x2.shape[0]
    tm = ROW_TILE
    tiles_per_seq = seq // tm
    row = lambda i: (i, 0)
    const = lambda i: (0, 0)
    pos = lambda i: (i % tiles_per_seq, 0)
    batch = n // seq
    wide = jax.ShapeDtypeStruct((n, D_HEADS), MXU_DTYPE)
    narrow = jax.ShapeDtypeStruct((n, LANES), MXU_DTYPE)
    wide_spec = pl.BlockSpec((tm, D_HEADS), row)
    lay_shapes = [wide] + [jax.ShapeDtypeStruct((batch, d, seq // d, D_HEADS), MXU_DTYPE) for d in RESIDUE_DILATIONS]
    lay_specs = [wide_spec] + [
        pl.BlockSpec((1, d, tm // d, D_HEADS), lambda i: (i // tiles_per_seq, 0, i % tiles_per_seq, 0))
        for d in RESIDUE_DILATIONS]
    n_lay = len(lay_shapes)
    outs = pl.pallas_call(
        _proj_kernel,
        grid=(n // tm,),
        in_specs=[
            pl.BlockSpec((tm, D_MODEL), row),
            pl.BlockSpec((1, D_MODEL), const),
            pl.BlockSpec((D_MODEL, _P_TOTAL), const),
            pl.BlockSpec((tm, LANES), pos),
            pl.BlockSpec((tm, LANES), pos),
        ],
        out_specs=lay_specs * 3 + [wide_spec] * 2 + [pl.BlockSpec((tm, LANES), row)] * 3,
        out_shape=lay_shapes * 3 + [wide] * 2 + [narrow, narrow, jax.ShapeDtypeStruct((n, LANES), jnp.float32)],
        scratch_shapes=[pltpu.VMEM((tm, D_MODEL), MXU_DTYPE),
                        pltpu.VMEM((D_HEADS // LANES, tm, LANES), jnp.float32)],
        compiler_params=pltpu.CompilerParams(dimension_semantics=("arbitrary",), vmem_limit_bytes=VMEM_LIMIT),
        name="proj_rope",
    )(x2, g, w_pack, cos_t, sin_t)
    return (outs[:n_lay], outs[n_lay:2 * n_lay], outs[2 * n_lay:3 * n_lay]) + tuple(outs[3 * n_lay:])


def _dil_kernel(q_ref, k_ref, v_ref, kp_ref, vp_ref, o_ref, lse_ref, *, n_blocks):
    n = pl.program_id(2)
    kj = lax.broadcasted_iota(jnp.int32, (2 * BLOCK, LANES), 0)
    qi = lax.broadcasted_iota(jnp.int32, (2 * BLOCK, LANES), 1)
    band = (kj >= qi) & (kj <= qi + BLOCK)
    lane = lax.broadcasted_iota(jnp.int32, (1, LANES), 1)
    pairs = range(N_HEADS // 2)
    col = lambda p: slice(p * LANES, (p + 1) * LANES)

    for j in range(n_blocks):
        rows = slice(j * BLOCK, (j + 1) * BLOCK)
        if j == 0:
            k_prev = lambda p: kp_ref[0, 0, :, col(p)]
            v_prev = lambda p: vp_ref[0, 0, :, col(p)]
            bias = jnp.where(band & ((kj >= BLOCK) | (n > 0)), 0.0, -jnp.inf)
        else:
            prev_rows = slice((j - 1) * BLOCK, j * BLOCK)
            k_prev = lambda p, r=prev_rows: k_ref[0, 0, r, col(p)]
            v_prev = lambda p, r=prev_rows: v_ref[0, 0, r, col(p)]
            bias = jnp.where(band, 0.0, -jnp.inf)

        scores = []
        for p in pairs:
            qp = q_ref[0, 0, rows, col(p)]
            qs = jnp.concatenate([jnp.where(((lane // HALF) % 2) == hh, qp, jnp.zeros_like(qp))
                                  for hh in range(2)], axis=0)
            k2 = jnp.concatenate([k_prev(p), k_ref[0, 0, rows, col(p)]], axis=0)
            scores.append(_nt_dot(k2, qs))

        probs, dens, lses = [], [], []
        for p in pairs:
            for hh in range(2):
                sh = scores[p][:, hh * BLOCK:(hh + 1) * BLOCK] + bias
                m = jnp.max(sh, axis=0, keepdims=True)
                e = jnp.exp(sh - m)
                den = jnp.sum(e, axis=0, keepdims=True)
                probs.append(e.astype(MXU_DTYPE))
                dens.append(den)
                lses.append(m + jnp.log(den))

        for p in pairs:
            v2 = jnp.concatenate([v_prev(p), v_ref[0, 0, rows, col(p)]], axis=0)
            o_t = lax.dot_general(v2, jnp.concatenate(probs[2 * p:2 * p + 2], axis=1), (((0,), (0,)), ((), ())),
                                  preferred_element_type=jnp.float32)
            tile = jnp.concatenate([o_t[:HEAD_DIM, :BLOCK] / dens[2 * p],
                                    o_t[HEAD_DIM:, BLOCK:] / dens[2 * p + 1]], axis=0)
            o_ref[0, 0, rows, col(p)] = tile.T.astype(o_ref.dtype)
            lse_tile = jnp.concatenate([jnp.broadcast_to(l, (HEAD_DIM, BLOCK)) for l in lses[2 * p:2 * p + 2]],
                                       axis=0)
            lse_ref[0, 0, rows, col(p)] = lse_tile.T


def _dilated(q, k, v, dilation):
    batch, _, m_len, _ = q.shape
    n_blocks = min(DIL_BLOCKS, m_len // BLOCK)
    tile_rows = n_blocks * BLOCK
    cur = lambda b, r, n: (b, r, n, 0)
    prev = lambda b, r, n: (b, r, jnp.maximum(n * n_blocks - 1, 0), 0)
    tile = (1, 1, tile_rows, D_HEADS)
    blk = (1, 1, BLOCK, D_HEADS)
    return pl.pallas_call(
        functools.partial(_dil_kernel, n_blocks=n_blocks),
        grid=(batch, dilation, m_len // tile_rows),
        in_specs=[pl.BlockSpec(tile, cur), pl.BlockSpec(tile, cur), pl.BlockSpec(tile, cur),
                  pl.BlockSpec(blk, prev), pl.BlockSpec(blk, prev)],
        out_specs=[pl.BlockSpec(tile, cur), pl.BlockSpec(tile, cur)],
        out_shape=[jax.ShapeDtypeStruct(q.shape, MXU_DTYPE), jax.ShapeDtypeStruct(q.shape, jnp.float32)],
        compiler_params=pltpu.CompilerParams(dimension_semantics=("arbitrary",) * 3),
        name=f"dilated_d{dilation}",
    )(q, k, v, k, v)


def _key_to_f32(key):
    bits = key ^ ((key >> 31) & jnp.int32(0x7FFFFFFF))
    return lax.bitcast_convert_type(bits, jnp.float32)


def _f32_to_key(x):
    bits = lax.bitcast_convert_type(x, jnp.int32)
    return bits ^ ((bits >> 31) & jnp.int32(0x7FFFFFFF))


def _fold_rows(x, op):
    rows = x.shape[0]
    y = op(x.reshape(rows // 64, 64, LANES), axis=0)
    return op(y.reshape(8, 8, LANES), axis=0)


def _sparse_kernel(qi_ref, qb_ref, wt_ref, kid_ref, kbd_ref, vbt_ref, out_ref,
                   sc_ref, mb_ref, qis_ref, qbs_ref, acc_ref, m_ref, sqa_ref, sqb_ref, mxa_ref, mxb_ref,
                   *, topk, index_bits):
    i = pl.program_id(1)
    nch = i // (KEY_CHUNK // BLOCK) + 1
    ncc = i // (COUNT_CHUNK // BLOCK) + 1
    lane = lax.broadcasted_iota(jnp.int32, (1, LANES), 1)
    t_idx = i * BLOCK + lane
    row_iota = lax.broadcasted_iota(jnp.int32, (KEY_CHUNK, LANES), 0)
    crow_iota = lax.broadcasted_iota(jnp.int32, (COUNT_CHUNK, LANES), 0)
    neg_inf = jnp.float32(-jnp.inf)
    f32_lowest = jnp.float32(jnp.finfo(jnp.float32).min)

    eye = jnp.where(lax.broadcasted_iota(jnp.int32, (BLOCK, LANES), 0) == lane, 1.0, 0.0).astype(MXU_DTYPE)
    for h in range(N_HEADS):
        cols = slice((h // 2) * LANES, (h // 2 + 1) * LANES)
        head_lanes = ((lane // HALF) % 2) == (h % 2)
        rows = slice(h * BLOCK, (h + 1) * BLOCK)
        qis_ref[rows, :] = jnp.where(head_lanes, qi_ref[0, :, cols], jnp.zeros((), MXU_DTYPE))
        qbs_ref[rows, :LANES] = jnp.where(head_lanes, qb_ref[0, :, cols], jnp.zeros((), MXU_DTYPE))
        qbs_ref[rows, LANES:] = eye

    def chunk_start(c):
        return pl.multiple_of(c * KEY_CHUNK, KEY_CHUNK)

    def pair_q(ref, p):
        return ref[p * 2 * BLOCK:(p + 1) * 2 * BLOCK, :]

    buf_a, buf_b = (sqa_ref, mxa_ref), (sqb_ref, mxb_ref)

    def pipeline(produce, consume):
        produce(0, buf_a)

        def two_chunks(j, carry):
            c = 2 * j
            produce(c + 1, buf_b)
            consume(c, buf_a)
            produce(c + 2, buf_a)
            consume(c + 1, buf_b)
            return carry

        lax.fori_loop(0, (nch - 1) // 2, two_chunks, 0)
        last = nch - 1

        @pl.when(last % 2 == 1)
        def _():
            produce(last, buf_b)
            consume(last - 1, buf_a)
            consume(last, buf_b)

        @pl.when(last % 2 == 0)
        def _():
            consume(last, buf_a)

    def score_chunk(c, carry):
        r0 = chunk_start(c)
        kc = kid_ref[0, pl.ds(r0, KEY_CHUNK), :]
        acc = jnp.zeros((KEY_CHUNK, LANES), jnp.float32)
        for p in range(N_HEADS // 2):
            d = _nt_dot(kc, pair_q(qis_ref, p))
            for hh in range(2):
                h = 2 * p + hh
                acc = acc + jnp.maximum(d[:, hh * BLOCK:(hh + 1) * BLOCK], 0.0) * wt_ref[0, h:h + 1, :]
        admissible = r0 + row_iota <= t_idx
        scores = acc * IDX_SCALE
        masked = jnp.where(admissible, scores, neg_inf)
        sc_ref[pl.ds(r0, KEY_CHUNK), :] = masked
        top8, bot8 = carry
        return (jnp.maximum(top8, _fold_rows(masked, jnp.max)),
                jnp.minimum(bot8, _fold_rows(jnp.where(admissible, scores, -neg_inf), jnp.min)))

    top8, bot8 = lax.fori_loop(0, nch, score_chunk, (jnp.full((8, LANES), neg_inf, jnp.float32),
                                                     jnp.full((8, LANES), -neg_inf, jnp.float32)))
    top = jnp.max(top8, axis=0, keepdims=True)
    bot = jnp.min(bot8, axis=0, keepdims=True)

    def fold_scores(value, op, pairwise, start):
        def body(c, acc):
            r0 = pl.multiple_of(c * COUNT_CHUNK, COUNT_CHUNK)
            val = value(sc_ref[pl.ds(r0, COUNT_CHUNK), :], r0)
            return pairwise(acc, op(val.reshape(COUNT_FOLD, COUNT_CHUNK // COUNT_FOLD, LANES), axis=0))

        acc = lax.fori_loop(0, ncc // 2, lambda j, a: body(2 * j + 1, body(2 * j, a)),
                            jnp.full((COUNT_CHUNK // COUNT_FOLD, LANES), start, jnp.float32))
        acc = lax.cond(ncc % 2 == 1, lambda a: body(ncc - 1, a), lambda a: a, acc)
        return op(acc, axis=0, keepdims=True)

    def count(pred):
        return fold_scores(lambda x, r0: jnp.where(pred(x, r0), 1.0, 0.0), jnp.sum, jnp.add, 0.0)

    key_top = _f32_to_key(top)
    lo = _f32_to_key(bot) - jnp.int32(1 << 23)
    hi = key_top + 1
    hi = jnp.where((hi >= -MIN_NORMAL_KEY) & (hi < MIN_NORMAL_KEY), jnp.int32(MIN_NORMAL_KEY), hi)
    unknown = jnp.float32(2 * sc_ref.shape[0])

    def is_settled(carry):
        lo, hi, cnt_lo = carry[:3]
        return (cnt_lo == topk) | (hi - lo == 1) | (t_idx < topk)

    def probe(mid, carry):
        lo, hi, cnt_lo, thr_lo, thr_hi = carry
        thr_c = _key_to_f32(mid)
        cnt = count(lambda x, r0: x >= thr_c)
        take = (cnt >= topk) & ~is_settled(carry)
        drop = (cnt < topk) & ~is_settled(carry)
        return (jnp.where(take, mid, lo), jnp.where(drop, mid, hi), jnp.where(take, cnt, cnt_lo),
                jnp.where(take, thr_c, thr_lo), jnp.where(drop, thr_c, thr_hi))

    def halve(by_value, carry):
        lo, hi = carry[:2]
        mid = lo + lax.shift_right_logical(hi - lo, 1)
        if by_value:
            mid_val = _f32_to_key(0.5 * _key_to_f32(lo) + 0.5 * _key_to_f32(hi))
            mid = jnp.where((mid_val > lo) & (mid_val < hi), mid_val, mid)
        return probe(mid, carry)

    def descend(carry):
        lo, hi, cnt_lo, thr_lo, thr_hi = carry
        v = fold_scores(lambda x, r0: jnp.where(x < thr_hi, x, neg_inf), jnp.max, jnp.maximum, neg_inf)
        cnt = count(lambda x, r0: x >= v)
        key_v = _f32_to_key(v)
        take = (cnt >= topk) & ~is_settled(carry)
        drop = (cnt < topk) & ~is_settled(carry)
        return (jnp.where(take, key_v, lo), jnp.where(take, key_v + 1, jnp.where(drop, key_v, hi)),
                jnp.where(take, cnt, cnt_lo), jnp.where(take, v, thr_lo), jnp.where(drop, v, thr_hi))

    def pending(carry):
        return jnp.max(jnp.where(is_settled(carry), 0.0, 1.0))

    state = lax.fori_loop(0, SEARCH_PASSES_MIN, lambda b, c: halve(True, c),
                          (lo, hi, jnp.full((1, LANES), unknown), _key_to_f32(lo), _key_to_f32(hi)))

    def two_passes(carry):
        b, state, _ = carry
        state = lax.cond(b < VALUE_PASSES, lambda s: halve(True, halve(True, s)),
                         lambda s: lax.cond(b < VALUE_PASSES + 2 * DESCENT_STEPS, descend,
                                            lambda t: halve(False, halve(False, t)), s), state)
        return b + 2, state, pending(state)

    _, (lo, hi, cnt_ge, thr_raw, _), _ = lax.while_loop(
        lambda c: (c[0] < VALUE_PASSES + 2 * DESCENT_STEPS + 32) & (c[2] > 0.0), two_passes,
        (jnp.int32(SEARCH_PASSES_MIN), state, pending(state)))
    enough = thr_raw > f32_lowest
    thr = jnp.where(enough, thr_raw, f32_lowest)

    tie_lane = enough & (cnt_ge > topk) & (t_idx >= topk)

    @pl.when(jnp.max(jnp.where(tie_lane, 1.0, 0.0)) > 0.0)
    def _():
        need = topk - count(lambda x, r0: x > thr)

        def idx_step(b, lim):
            cand = lim | (jnp.int32(1) << (index_bits - 1 - b))
            below = count(lambda x, r0: (x == thr) & (r0 + crow_iota < cand))
            return jnp.where(below < need, cand, lim)

        lim = lax.fori_loop(0, index_bits, idx_step, jnp.zeros((1, LANES), jnp.int32))

        def demote(c, carry):
            r0 = pl.multiple_of(c * COUNT_CHUNK, COUNT_CHUNK)
            x = sc_ref[pl.ds(r0, COUNT_CHUNK), :]
            drop = (x == thr) & (r0 + crow_iota > lim) & tie_lane
            sc_ref[pl.ds(r0, COUNT_CHUNK), :] = jnp.where(drop, neg_inf, x)
            return carry

        lax.fori_loop(0, ncc, demote, 0)

    mask_off = float(jnp.finfo(MXU_DTYPE).min)

    def build_mask(c, carry):
        r0 = chunk_start(c)
        selected = sc_ref[pl.ds(r0, KEY_CHUNK), :] >= thr
        mb_ref[pl.ds(r0, KEY_CHUNK), :] = jnp.where(selected, 0.0, mask_off).astype(MXU_DTYPE)
        return carry

    lax.fori_loop(0, nch, build_mask, 0)

    def attn_scores(c, buf):
        sq_ref, mx_ref = buf
        r0 = chunk_start(c)
        kaug = jnp.concatenate([kbd_ref[0, pl.ds(r0, KEY_CHUNK), :], mb_ref[pl.ds(r0, KEY_CHUNK), :]], axis=1)
        for p in range(N_HEADS // 2):
            s = _nt_dot(kaug, pair_q(qbs_ref, p))
            sq_ref[:, p * 2 * BLOCK:(p + 1) * 2 * BLOCK] = s
            for hh in range(2):
                cols = slice((2 * p + hh) * BLOCK, (2 * p + hh + 1) * BLOCK)
                mx_ref[:, cols] = _fold_rows(s[:, hh * BLOCK:(hh + 1) * BLOCK], jnp.max)

    def accumulate(c, buf):
        sq_ref, mx_ref = buf
        m_run = m_ref[...]
        m_new = jnp.maximum(m_run, jnp.max(mx_ref[...], axis=0, keepdims=True))
        m_ref[...] = m_new
        acc_ref[...] = acc_ref[...] * jnp.exp2(m_run - m_new)
        vt = vbt_ref[0, c]
        for p in range(N_HEADS // 2):
            pcols = slice(p * 2 * BLOCK, (p + 1) * 2 * BLOCK)
            e = jnp.exp2(sq_ref[:, pcols] - m_new[:, pcols])
            acc_ref[:, pcols] += jnp.dot(vt, e.astype(MXU_DTYPE), preferred_element_type=jnp.float32)

    acc_ref[...] = jnp.zeros(acc_ref.shape, jnp.float32)
    m_ref[...] = jnp.full(m_ref.shape, f32_lowest, jnp.float32)
    pipeline(attn_scores, accumulate)

    o_t = acc_ref[:HEAD_DIM, :] / acc_ref[HEAD_DIM:HEAD_DIM + 1, :]
    for p in range(N_HEADS // 2):
        pair = jnp.concatenate([o_t[:, (2 * p) * BLOCK:(2 * p + 1) * BLOCK],
                                o_t[:, (2 * p + 1) * BLOCK:(2 * p + 2) * BLOCK]], axis=0)
        out_ref[0, :, p * LANES:(p + 1) * LANES] = pair.T.astype(out_ref.dtype)


def _sparse(qi, qb, wt, kid, kbd, vbt, batch, seq):
    topk = min(TOPK_MAX, seq // 4)
    nq = seq // BLOCK
    qblk = pl.BlockSpec((1, BLOCK, D_HEADS), lambda b, i: (b, i, 0))
    full = pl.BlockSpec((1, seq, LANES), lambda b, i: (b, 0, 0))
    kern = functools.partial(_sparse_kernel, topk=topk, index_bits=max(1, (seq - 1).bit_length()))
    return pl.pallas_call(
        kern,
        grid=(batch, nq),
        in_specs=[qblk, qblk,
                  pl.BlockSpec((1, IDX_HEADS, BLOCK), lambda b, i: (b, 0, i)),
                  full, full,
                  pl.BlockSpec((1, seq // KEY_CHUNK, HEAD_DIM + L_ROWS, KEY_CHUNK), lambda b, i: (b, 0, 0, 0))],
        out_specs=qblk,
        out_shape=jax.ShapeDtypeStruct((batch, seq, D_HEADS), MXU_DTYPE),
        scratch_shapes=[pltpu.VMEM((seq, LANES), jnp.float32),
                        pltpu.VMEM((seq, LANES), MXU_DTYPE),
                        pltpu.VMEM((N_HEADS * BLOCK, LANES), MXU_DTYPE),
                        pltpu.VMEM((N_HEADS * BLOCK, 2 * LANES), MXU_DTYPE),
                        pltpu.VMEM((HEAD_DIM + L_ROWS, N_HEADS * BLOCK), jnp.float32),
                        pltpu.VMEM((1, N_HEADS * BLOCK), jnp.float32),
                        pltpu.VMEM((KEY_CHUNK, N_HEADS * BLOCK), jnp.float32),
                        pltpu.VMEM((KEY_CHUNK, N_HEADS * BLOCK), jnp.float32),
                        pltpu.VMEM((8, N_HEADS * BLOCK), jnp.float32),
                        pltpu.VMEM((8, N_HEADS * BLOCK), jnp.float32)],
        compiler_params=pltpu.CompilerParams(dimension_semantics=("arbitrary", "arbitrary"),
                                             vmem_limit_bytes=VMEM_LIMIT),
        name="indexer_sparse_attn",
    )(qi, qb, wt, kid, kbd, vbt)


def _merge_kernel(x_ref, g_ref, *refs):
    n_pat = len(DILATED_PATTERNS)
    o_refs, l_refs = refs[:n_pat], refs[n_pat:2 * n_pat]
    yb_ref, wg_ref, wua_ref, wub_ref, wo_ref, x1_ref = refs[2 * n_pat:2 * n_pat + 6]
    slabs = refs[2 * n_pat + 6:]
    x = x_ref[...]
    h = _rms(x, g_ref[...]).astype(MXU_DTYPE)
    tm = x.shape[0]
    n_grp = D_HEADS // LANES

    o_src, l_src, k = [], [], 0
    for (_, d), o_ref, l_ref in zip(DILATED_PATTERNS, o_refs, l_refs):
        if d == 1:
            o_src.append(lambda g, r=o_ref: r[:, g * LANES:(g + 1) * LANES].astype(jnp.float32))
            l_src.append(lambda g, r=l_ref: r[:, g * LANES:(g + 1) * LANES])
            continue
        o_slab, l_slab = slabs[2 * k], slabs[2 * k + 1]
        k += 1
        for r in range(d):
            for g in range(n_grp):
                cols = slice(g * LANES, (g + 1) * LANES)
                o_slab[g, pl.ds(r, tm // d, stride=d), :] = o_ref[0, r, :, cols].astype(jnp.float32)
                l_slab[g, pl.ds(r, tm // d, stride=d), :] = l_ref[0, r, :, cols]
        o_src.append(lambda g, s=o_slab: s[g])
        l_src.append(lambda g, s=l_slab: s[g])

    parts = []
    for g in range(n_grp):
        ls = [f(g) for f in l_src]
        m = functools.reduce(jnp.maximum, ls)
        es = [jnp.exp(l - m) for l in ls]
        num = functools.reduce(jnp.add, [e * f(g) for e, f in zip(es, o_src)])
        parts.append((num / functools.reduce(jnp.add, es)).astype(MXU_DTYPE))
    ya = jnp.concatenate(parts, axis=1)
    ua = jnp.dot(ya, wua_ref[...], preferred_element_type=jnp.float32)
    ub = jnp.dot(yb_ref[...], wub_ref[...], preferred_element_type=jnp.float32)
    ga = jnp.dot(h, wg_ref[:, :D_MODEL], preferred_element_type=jnp.float32)
    gb = jnp.dot(h, wg_ref[:, D_MODEL:], preferred_element_type=jnp.float32)
    merged = jax.nn.sigmoid(ga) * ua + jax.nn.sigmoid(gb) * ub
    x1_ref[...] = x + jnp.dot(merged.astype(MXU_DTYPE), wo_ref[...], preferred_element_type=jnp.float32)


def _merge(x2, g, os_, lses, yb, wg, wua, wub, wo, seq):
    n = x2.shape[0]
    tm = ROW_TILE
    tiles_per_seq = seq // tm
    row = lambda i: (i, 0)
    const = lambda i: (0, 0)
    half = pl.BlockSpec((tm, D_HEADS), row)
    pat_specs = [half if d == 1 else
                 pl.BlockSpec((1, d, tm // d, D_HEADS), lambda i: (i // tiles_per_seq, 0, i % tiles_per_seq, 0))
                 for _, d in DILATED_PATTERNS]
    return pl.pallas_call(
        _merge_kernel,
        grid=(n // tm,),
        in_specs=[pl.BlockSpec((tm, D_MODEL), row), pl.BlockSpec((1, D_MODEL), const)] + pat_specs * 2 + [half] + [
            pl.BlockSpec((D_MODEL, 2 * D_MODEL), const),
            pl.BlockSpec((D_HEADS, D_MODEL), const),
            pl.BlockSpec((D_HEADS, D_MODEL), const),
            pl.BlockSpec((D_MODEL, D_MODEL), const)],
        out_specs=pl.BlockSpec((tm, D_MODEL), row),
        out_shape=jax.ShapeDtypeStruct((n, D_MODEL), jnp.float32),
        scratch_shapes=[pltpu.VMEM((D_HEADS // LANES, tm, LANES), jnp.float32)] * (2 * len(RESIDUE_DILATIONS)),
        compiler_params=pltpu.CompilerParams(dimension_semantics=("arbitrary",), vmem_limit_bytes=VMEM_LIMIT),
        name="mix_gate_out",
    )(x2, g, *os_, *lses, yb, wg, wua, wub, wo)


def _ffn_kernel(x_ref, g_ref, wgate_ref, wup_ref, wdown_ref, gf_ref, out_ref):
    x = x_ref[...]
    h = _rms(x, g_ref[...]).astype(MXU_DTYPE)
    y = x
    for c0 in range(0, D_FF, FF_CHUNK):
        a = jnp.dot(h, wgate_ref[:, c0:c0 + FF_CHUNK], preferred_element_type=jnp.float32)
        u = jnp.dot(h, wup_ref[:, c0:c0 + FF_CHUNK], preferred_element_type=jnp.float32)
        act = (a * jax.nn.sigmoid(a) * u).astype(MXU_DTYPE)
        y = y + jnp.dot(act, wdown_ref[c0:c0 + FF_CHUNK, :], preferred_element_type=jnp.float32)
    out_ref[...] = _rms(y, gf_ref[...])


def _ffn(x1, g, wgate, wup, wdown, gf):
    n = x1.shape[0]
    tm = ROW_TILE
    row = lambda i: (i, 0)
    const = lambda i: (0, 0)
    return pl.pallas_call(
        _ffn_kernel,
        grid=(n // tm,),
        in_specs=[pl.BlockSpec((tm, D_MODEL), row), pl.BlockSpec((1, D_MODEL), const),
                  pl.BlockSpec((D_MODEL, D_FF), const), pl.BlockSpec((D_MODEL, D_FF), const),
                  pl.BlockSpec((D_FF, D_MODEL), const), pl.BlockSpec((1, D_MODEL), const)],
        out_specs=pl.BlockSpec((tm, D_MODEL), row),
        out_shape=jax.ShapeDtypeStruct((n, D_MODEL), jnp.float32),
        compiler_params=pltpu.CompilerParams(dimension_semantics=("arbitrary",), vmem_limit_bytes=VMEM_LIMIT),
        name="ffn_norm",
    )(x1, g, wgate, wup, wdown, gf)


def _rope_tables(seq):
    inv_freq = ROPE_THETA ** (-jnp.arange(HALF, dtype=jnp.float32) / HALF)
    ang = jnp.arange(seq, dtype=jnp.int32).astype(jnp.float32)[:, None] * inv_freq[None, :]
    cos, sin = jnp.cos(ang), jnp.sin(ang)
    return jnp.tile(cos, (1, 4)), jnp.concatenate([-sin, -sin, sin, sin], axis=1)


def kernel(x, norm_mix, w_in, w_up_a, w_up_b, w_out, norm_ffn, w_gate, w_up, w_down, norm_final):
    batch, seq, _ = x.shape
    assert seq % max(d * BLOCK for _, d in DILATED_PATTERNS) == 0 and seq % KEY_CHUNK == 0
    assert all(w // d == BLOCK for w, d in DILATED_PATTERNS)
    n = batch * seq
    bf = MXU_DTYPE
    xf = x.reshape(n, D_MODEL)
    cos_t, sin_t = _rope_tables(seq)
    for layer in range(w_in.shape[0]):
        w = w_in[layer]
        w_pack = jnp.pad(w[:, _packed_columns()], ((0, 0), (0, _P_TOTAL - _P_MISC - HEAD_DIM - IDX_HEADS))).astype(bf)
        w_gates = w[:, _GA:].astype(bf)
        qas, kas, vas, qb, qi, kbd, kid, misc = _project(xf, norm_mix[layer][None], w_pack, cos_t, sin_t, seq)

        dil = []
        for (_, d), q, k, v in zip(DILATED_PATTERNS, qas, kas, vas):
            if d == 1:
                o, lse = _dilated(*(z.reshape(batch, 1, seq, D_HEADS) for z in (q, k, v)), d)
                dil.append((o.reshape(n, D_HEADS), lse.reshape(n, D_HEADS)))
            else:
                dil.append(_dilated(q, k, v, d))

        ones_col = (jnp.arange(L_ROWS) == 0).astype(bf)[None, :]
        vb = jnp.concatenate([misc[:, :HEAD_DIM].astype(bf), jnp.broadcast_to(ones_col, (n, L_ROWS))], axis=1)
        vbt = jnp.swapaxes(vb.reshape(batch, seq // KEY_CHUNK, KEY_CHUNK, HEAD_DIM + L_ROWS), 2, 3)
        wt = jnp.swapaxes(misc[:, HEAD_DIM:HEAD_DIM + IDX_HEADS].reshape(batch, seq, IDX_HEADS), 1, 2)
        r3 = lambda z: z.reshape(batch, seq, z.shape[-1])
        yb = _sparse(r3(qi), r3(qb), wt, r3(kid), r3(kbd), vbt, batch, seq).reshape(n, D_HEADS)

        x1 = _merge(xf, norm_mix[layer][None], [o for o, _ in dil], [l for _, l in dil], yb,
                    w_gates, w_up_a[layer].astype(bf), w_up_b[layer].astype(bf), w_out[layer].astype(bf), seq)
        last = layer == w_in.shape[0] - 1
        assert last, "the final norm is fused into the FFN kernel of the last layer"
        xf = _ffn(x1, norm_ffn[layer][None], w_gate[layer].astype(bf), w_up[layer].astype(bf),
                  w_down[layer].astype(bf), norm_final[None])
    return xf.reshape(batch, seq, D_MODEL)
```

```python
import functools

import numpy as np
import jax
import jax.numpy as jnp
from jax import lax
from jax.experimental import pallas as pl
from jax.experimental.pallas import tpu as pltpu

D_MODEL = 1024
HEAD_DIM = 64
HALF = HEAD_DIM // 2
N_HEADS = 8
D_HEADS = N_HEADS * HEAD_DIM
IDX_HEADS = 8
DILATED_PATTERNS = ((128, 1), (512, 4), (2048, 16))
RESIDUE_DILATIONS = tuple(d for _, d in DILATED_PATTERNS if d > 1)
TOPK_MAX = 256
D_FF = 2816
ROPE_THETA = 10000.0
RMS_EPS = 1e-6
BLOCK = 128
ATTN_SCALE = HEAD_DIM ** -0.5
IDX_SCALE = (HEAD_DIM ** -0.5) * (IDX_HEADS ** -0.5)
LOG2_E = float(np.log2(np.e))

LANES = 128
INT_MIN = -(2 ** 31)
MIN_NORMAL_KEY = 1 << 23
KEY_CHUNK = 1024
COUNT_CHUNK = 512
COUNT_FOLD = 8
VALUE_PASSES = 14
DESCENT_STEPS = 6
SEARCH_PASSES_MIN = 12
L_ROWS = 8
ROW_TILE = 512
DIL_BLOCKS = 4
FF_CHUNK = 1408
VMEM_LIMIT = 56 * 1024 * 1024
MXU_DTYPE = jnp.bfloat16

_SPLITS = (D_HEADS, D_HEADS, D_HEADS, D_HEADS, HEAD_DIM, HEAD_DIM, IDX_HEADS * HEAD_DIM, HEAD_DIM, IDX_HEADS,
           D_MODEL, D_MODEL)
_OFF = np.concatenate([[0], np.cumsum(_SPLITS)])
(_QA, _KA, _VA, _QB, _KB, _VB, _QI, _KI, _WI, _GA, _GB) = (int(o) for o in _OFF[:-1])

_P_QA, _P_KA, _P_VA, _P_QB, _P_QI = 0, 512, 1024, 1536, 2048
_P_KBD, _P_KID, _P_MISC = 2560, 2688, 2816
_P_TOTAL = 2944


def _pair_perm():
    idx = np.empty(D_HEADS, np.int64)
    for j in range(D_HEADS):
        g, l = divmod(j, LANES)
        quarter, e = divmod(l, HALF)
        head = 2 * g + (quarter % 2)
        idx[j] = head * HEAD_DIM + (quarter // 2) * HALF + e
    return idx


def _dup_perm():
    idx = np.empty(LANES, np.int64)
    for l in range(LANES):
        quarter, e = divmod(l, HALF)
        idx[l] = (quarter // 2) * HALF + e
    return idx


def _packed_columns():
    pp, dp = _pair_perm(), _dup_perm()
    return np.concatenate([
        _QA + pp, _KA + pp, _VA + np.arange(D_HEADS), _QB + pp, _QI + pp,
        _KB + dp, _KI + dp, _VB + np.arange(HEAD_DIM), _WI + np.arange(IDX_HEADS)])


def _rms(x, g):
    ms = jnp.mean(x * x, axis=-1, keepdims=True)
    return x * lax.rsqrt(ms + RMS_EPS) * g


def _nt_dot(a, b):
    return lax.dot_general(a, b, (((1,), (1,)), ((), ())), preferred_element_type=jnp.float32)


def _proj_kernel(x_ref, g_ref, w_ref, cos_ref, sin_ref, *refs):
    n_lay = 1 + len(RESIDUE_DILATIONS)
    qa_refs, ka_refs, va_refs = refs[:n_lay], refs[n_lay:2 * n_lay], refs[2 * n_lay:3 * n_lay]
    qb_ref, qi_ref, kbd_ref, kid_ref, misc_ref, h_ref, slab_ref = refs[3 * n_lay:]
    h_ref[...] = _rms(x_ref[...], g_ref[...]).astype(MXU_DTYPE)
    cos = cos_ref[...]
    sin = sin_ref[...]

    def mm(c0, width):
        return jnp.dot(h_ref[...], w_ref[:, c0:c0 + width], preferred_element_type=jnp.float32)

    def rope(z):
        parts = []
        for g in range(z.shape[1] // LANES):
            zg = z[:, g * LANES:(g + 1) * LANES]
            parts.append(zg * cos + pltpu.roll(zg, 2 * HALF, axis=1) * sin)
        return parts[0] if len(parts) == 1 else jnp.concatenate(parts, axis=1)

    def emit(y, out_refs):
        out_refs[0][...] = y.astype(MXU_DTYPE)
        for g in range(D_HEADS // LANES):
            slab_ref[g] = y[:, g * LANES:(g + 1) * LANES]
        for d, ref in zip(RESIDUE_DILATIONS, out_refs[1:]):
            rows = y.shape[0] // d
            for r in range(d):
                for g in range(D_HEADS // LANES):
                    ref[0, r, :, g * LANES:(g + 1) * LANES] = (
                        slab_ref[g, pl.ds(r, rows, stride=d), :].astype(MXU_DTYPE))

    emit(rope(mm(_P_QA, D_HEADS)) * ATTN_SCALE, qa_refs)
    emit(rope(mm(_P_KA, D_HEADS)), ka_refs)
    emit(mm(_P_VA, D_HEADS), va_refs)
    qb_ref[...] = (rope(mm(_P_QB, D_HEADS)) * (ATTN_SCALE * LOG2_E)).astype(MXU_DTYPE)
    qi_ref[...] = rope(mm(_P_QI, D_HEADS)).astype(MXU_DTYPE)
    kbd_ref[...] = rope(mm(_P_KBD, LANES)).astype(MXU_DTYPE)
    kid_ref[...] = rope(mm(_P_KID, LANES)).astype(MXU_DTYPE)
    misc_ref[...] = mm(_P_MISC, LANES)


def _project(x2, g, w_pack, cos_t, sin_t, seq):
    n = x2.shape[0]
    tm = ROW_TILE
    tiles_per_seq = seq // tm
    row = lambda i: (i, 0)
    const = lambda i: (0, 0)
    pos = lambda i: (i % tiles_per_seq, 0)
    batch = n // seq
    wide = jax.ShapeDtypeStruct((n, D_HEADS), MXU_DTYPE)
    narrow = jax.ShapeDtypeStruct((n, LANES), MXU_DTYPE)
    wide_spec = pl.BlockSpec((tm, D_HEADS), row)
    lay_shapes = [wide] + [jax.ShapeDtypeStruct((batch, d, seq // d, D_HEADS), MXU_DTYPE) for d in RESIDUE_DILATIONS]
    lay_specs = [wide_spec] + [
        pl.BlockSpec((1, d, tm // d, D_HEADS), lambda i: (i // tiles_per_seq, 0, i % tiles_per_seq, 0))
        for d in RESIDUE_DILATIONS]
    n_lay = len(lay_shapes)
    outs = pl.pallas_call(
        _proj_kernel,
        grid=(n // tm,),
        in_specs=[
            pl.BlockSpec((tm, D_MODEL), row),
            pl.BlockSpec((1, D_MODEL), const),
            pl.BlockSpec((D_MODEL, _P_TOTAL), const),
            pl.BlockSpec((tm, LANES), pos),
            pl.BlockSpec((tm, LANES), pos),
        ],
        out_specs=lay_specs * 3 + [wide_spec] * 2 + [pl.BlockSpec((tm, LANES), row)] * 3,
        out_shape=lay_shapes * 3 + [wide] * 2 + [narrow, narrow, jax.ShapeDtypeStruct((n, LANES), jnp.float32)],
        scratch_shapes=[pltpu.VMEM((tm, D_MODEL), MXU_DTYPE),
                        pltpu.VMEM((D_HEADS // LANES, tm, LANES), jnp.float32)],
        compiler_params=pltpu.CompilerParams(dimension_semantics=("arbitrary",), vmem_limit_bytes=VMEM_LIMIT),
        name="proj_rope",
    )(x2, g, w_pack, cos_t, sin_t)
    return (outs[:n_lay], outs[n_lay:2 * n_lay], outs[2 * n_lay:3 * n_lay]) + tuple(outs[3 * n_lay:])


def _dil_kernel(q_ref, k_ref, v_ref, kp_ref, vp_ref, o_ref, lse_ref, *, n_blocks):
    n = pl.program_id(2)
    kj = lax.broadcasted_iota(jnp.int32, (2 * BLOCK, LANES), 0)
    qi = lax.broadcasted_iota(jnp.int32, (2 * BLOCK, LANES), 1)
    band = (kj >= qi) & (kj <= qi + BLOCK)
    lane = lax.broadcasted_iota(jnp.int32, (1, LANES), 1)
    pairs = range(N_HEADS // 2)
    col = lambda p: slice(p * LANES, (p + 1) * LANES)

    for j in range(n_blocks):
        rows = slice(j * BLOCK, (j + 1) * BLOCK)
        if j == 0:
            k_prev = lambda p: kp_ref[0, 0, :, col(p)]
            v_prev = lambda p: vp_ref[0, 0, :, col(p)]
            bias = jnp.where(band & ((kj >= BLOCK) | (n > 0)), 0.0, -jnp.inf)
        else:
            prev_rows = slice((j - 1) * BLOCK, j * BLOCK)
            k_prev = lambda p, r=prev_rows: k_ref[0, 0, r, col(p)]
            v_prev = lambda p, r=prev_rows: v_ref[0, 0, r, col(p)]
            bias = jnp.where(band, 0.0, -jnp.inf)

        scores = []
        for p in pairs:
            qp = q_ref[0, 0, rows, col(p)]
            qs = jnp.concatenate([jnp.where(((lane // HALF) % 2) == hh, qp, jnp.zeros_like(qp))
                                  for hh in range(2)], axis=0)
            k2 = jnp.concatenate([k_prev(p), k_ref[0, 0, rows, col(p)]], axis=0)
            scores.append(_nt_dot(k2, qs))

        probs, dens, lses = [], [], []
        for p in pairs:
            for hh in range(2):
                sh = scores[p][:, hh * BLOCK:(hh + 1) * BLOCK] + bias
                m = jnp.max(sh, axis=0, keepdims=True)
                e = jnp.exp(sh - m)
                den = jnp.sum(e, axis=0, keepdims=True)
                probs.append(e.astype(MXU_DTYPE))
                dens.append(den)
                lses.append(m + jnp.log(den))

        for p in pairs:
            v2 = jnp.concatenate([v_prev(p), v_ref[0, 0, rows, col(p)]], axis=0)
            o_t = lax.dot_general(v2, jnp.concatenate(probs[2 * p:2 * p + 2], axis=1), (((0,), (0,)), ((), ())),
                                  preferred_element_type=jnp.float32)
            tile = jnp.concatenate([o_t[:HEAD_DIM, :BLOCK] / dens[2 * p],
                                    o_t[HEAD_DIM:, BLOCK:] / dens[2 * p + 1]], axis=0)
            o_ref[0, 0, rows, col(p)] = tile.T.astype(o_ref.dtype)
            lse_tile = jnp.concatenate([jnp.broadcast_to(l, (HEAD_DIM, BLOCK)) for l in lses[2 * p:2 * p + 2]],
                                       axis=0)
            lse_ref[0, 0, rows, col(p)] = lse_tile.T


def _dilated(q, k, v, dilation):
    batch, _, m_len, _ = q.shape
    n_blocks = min(DIL_BLOCKS, m_len // BLOCK)
    tile_rows = n_blocks * BLOCK
    cur = lambda b, r, n: (b, r, n, 0)
    prev = lambda b, r, n: (b, r, jnp.maximum(n * n_blocks - 1, 0), 0)
    tile = (1, 1, tile_rows, D_HEADS)
    blk = (1, 1, BLOCK, D_HEADS)
    return pl.pallas_call(
        functools.partial(_dil_kernel, n_blocks=n_blocks),
        grid=(batch, dilation, m_len // tile_rows),
        in_specs=[pl.BlockSpec(tile, cur), pl.BlockSpec(tile, cur), pl.BlockSpec(tile, cur),
                  pl.BlockSpec(blk, prev), pl.BlockSpec(blk, prev)],
        out_specs=[pl.BlockSpec(tile, cur), pl.BlockSpec(tile, cur)],
        out_shape=[jax.ShapeDtypeStruct(q.shape, MXU_DTYPE), jax.ShapeDtypeStruct(q.shape, jnp.float32)],
        compiler_params=pltpu.CompilerParams(dimension_semantics=("arbitrary",) * 3),
        name=f"dilated_d{dilation}",
    )(q, k, v, k, v)


def _key_to_f32(key):
    bits = key ^ ((key >> 31) & jnp.int32(0x7FFFFFFF))
    return lax.bitcast_convert_type(bits, jnp.float32)


def _f32_to_key(x):
    bits = lax.bitcast_convert_type(x, jnp.int32)
    return bits ^ ((bits >> 31) & jnp.int32(0x7FFFFFFF))


def _fold_rows(x, op):
    rows = x.shape[0]
    y = op(x.reshape(rows // 64, 64, LANES), axis=0)
    return op(y.reshape(8, 8, LANES), axis=0)


def _sparse_kernel(qi_ref, qb_ref, wt_ref, kid_ref, kbd_ref, vbt_ref, out_ref,
                   sc_ref, eq_ref, mb_ref, qis_ref, qbs_ref, acc_ref, m_ref, sqa_ref, sqb_ref, mxa_ref, mxb_ref,
                   *, topk, index_bits):
    i = pl.program_id(1)
    nch = i // (KEY_CHUNK // BLOCK) + 1
    ncc = i // (COUNT_CHUNK // BLOCK) + 1
    lane = lax.broadcasted_iota(jnp.int32, (1, LANES), 1)
    t_idx = i * BLOCK + lane
    row_iota = lax.broadcasted_iota(jnp.int32, (KEY_CHUNK, LANES), 0)
    crow_iota = lax.broadcasted_iota(jnp.int32, (COUNT_CHUNK, LANES), 0)
    neg_inf = jnp.float32(-jnp.inf)
    f32_lowest = jnp.float32(jnp.finfo(jnp.float32).min)

    eye = jnp.where(lax.broadcasted_iota(jnp.int32, (BLOCK, LANES), 0) == lane, 1.0, 0.0).astype(MXU_DTYPE)
    for h in range(N_HEADS):
        cols = slice((h // 2) * LANES, (h // 2 + 1) * LANES)
        head_lanes = ((lane // HALF) % 2) == (h % 2)
        rows = slice(h * BLOCK, (h + 1) * BLOCK)
        qis_ref[rows, :] = jnp.where(head_lanes, qi_ref[0, :, cols], jnp.zeros((), MXU_DTYPE))
        qbs_ref[rows, :LANES] = jnp.where(head_lanes, qb_ref[0, :, cols], jnp.zeros((), MXU_DTYPE))
        qbs_ref[rows, LANES:] = eye

    def chunk_start(c):
        return pl.multiple_of(c * KEY_CHUNK, KEY_CHUNK)

    def pair_q(ref, p):
        return ref[p * 2 * BLOCK:(p + 1) * 2 * BLOCK, :]

    buf_a, buf_b = (sqa_ref, mxa_ref), (sqb_ref, mxb_ref)

    def pipeline(produce, consume):
        produce(0, buf_a)

        def two_chunks(j, carry):
            c = 2 * j
            produce(c + 1, buf_b)
            consume(c, buf_a)
            produce(c + 2, buf_a)
            consume(c + 1, buf_b)
            return carry

        lax.fori_loop(0, (nch - 1) // 2, two_chunks, 0)
        last = nch - 1

        @pl.when(last % 2 == 1)
        def _():
            produce(last, buf_b)
            consume(last - 1, buf_a)
            consume(last, buf_b)

        @pl.when(last % 2 == 0)
        def _():
            consume(last, buf_a)

    def score_chunk(c, carry):
        r0 = chunk_start(c)
        kc = kid_ref[0, pl.ds(r0, KEY_CHUNK), :]
        acc = jnp.zeros((KEY_CHUNK, LANES), jnp.float32)
        for p in range(N_HEADS // 2):
            d = _nt_dot(kc, pair_q(qis_ref, p))
            for hh in range(2):
                h = 2 * p + hh
                acc = acc + jnp.maximum(d[:, hh * BLOCK:(hh + 1) * BLOCK], 0.0) * wt_ref[0, h:h + 1, :]
        admissible = r0 + row_iota <= t_idx
        scores = acc * IDX_SCALE
        masked = jnp.where(admissible, scores, neg_inf)
        sc_ref[pl.ds(r0, KEY_CHUNK), :] = masked
        top8, bot8 = carry
        return (jnp.maximum(top8, _fold_rows(masked, jnp.max)),
                jnp.minimum(bot8, _fold_rows(jnp.where(admissible, scores, -neg_inf), jnp.min)))

    top8, bot8 = lax.fori_loop(0, nch, score_chunk, (jnp.full((8, LANES), neg_inf, jnp.float32),
                                                     jnp.full((8, LANES), -neg_inf, jnp.float32)))
    top = jnp.max(top8, axis=0, keepdims=True)
    bot = jnp.min(bot8, axis=0, keepdims=True)

    def fold_scores(value, op, pairwise, start, src_ref=sc_ref):
        def body(c, acc):
            r0 = pl.multiple_of(c * COUNT_CHUNK, COUNT_CHUNK)
            val = value(src_ref[pl.ds(r0, COUNT_CHUNK), :], r0)
            return pairwise(acc, op(val.reshape(COUNT_FOLD, COUNT_CHUNK // COUNT_FOLD, LANES), axis=0))

        acc = lax.fori_loop(0, ncc // 2, lambda j, a: body(2 * j + 1, body(2 * j, a)),
                            jnp.full((COUNT_CHUNK // COUNT_FOLD, LANES), start, jnp.float32))
        acc = lax.cond(ncc % 2 == 1, lambda a: body(ncc - 1, a), lambda a: a, acc)
        return op(acc, axis=0, keepdims=True)

    def count(pred):
        return fold_scores(lambda x, r0: jnp.where(pred(x, r0), 1.0, 0.0), jnp.sum, jnp.add, 0.0)

    key_top = _f32_to_key(top)
    lo = _f32_to_key(bot) - jnp.int32(1 << 23)
    hi = key_top + 1
    hi = jnp.where((hi >= -MIN_NORMAL_KEY) & (hi < MIN_NORMAL_KEY), jnp.int32(MIN_NORMAL_KEY), hi)
    unknown = jnp.float32(2 * sc_ref.shape[0])

    def is_settled(carry):
        lo, hi, cnt_lo = carry[:3]
        return (cnt_lo == topk) | (hi - lo == 1) | (t_idx < topk)

    def probe(mid, carry):
        lo, hi, cnt_lo, thr_lo, thr_hi = carry
        thr_c = _key_to_f32(mid)
        cnt = count(lambda x, r0: x >= thr_c)
        take = (cnt >= topk) & ~is_settled(carry)
        drop = (cnt < topk) & ~is_settled(carry)
        return (jnp.where(take, mid, lo), jnp.where(drop, mid, hi), jnp.where(take, cnt, cnt_lo),
                jnp.where(take, thr_c, thr_lo), jnp.where(drop, thr_c, thr_hi))

    def halve(by_value, carry):
        lo, hi = carry[:2]
        mid = lo + lax.shift_right_logical(hi - lo, 1)
        if by_value:
            mid_val = _f32_to_key(0.5 * _key_to_f32(lo) + 0.5 * _key_to_f32(hi))
            mid = jnp.where((mid_val > lo) & (mid_val < hi), mid_val, mid)
        return probe(mid, carry)

    def descend(carry):
        lo, hi, cnt_lo, thr_lo, thr_hi = carry
        v = fold_scores(lambda x, r0: jnp.where(x < thr_hi, x, neg_inf), jnp.max, jnp.maximum, neg_inf)
        cnt = count(lambda x, r0: x >= v)
        key_v = _f32_to_key(v)
        take = (cnt >= topk) & ~is_settled(carry)
        drop = (cnt < topk) & ~is_settled(carry)
        return (jnp.where(take, key_v, lo), jnp.where(take, key_v + 1, jnp.where(drop, key_v, hi)),
                jnp.where(take, cnt, cnt_lo), jnp.where(take, v, thr_lo), jnp.where(drop, v, thr_hi))

    def pending(carry):
        return jnp.max(jnp.where(is_settled(carry), 0.0, 1.0))

    state = lax.fori_loop(0, SEARCH_PASSES_MIN, lambda b, c: halve(True, c),
                          (lo, hi, jnp.full((1, LANES), unknown), _key_to_f32(lo), _key_to_f32(hi)))

    def two_passes(carry):
        b, state, _ = carry
        state = lax.cond(b < VALUE_PASSES, lambda s: halve(True, halve(True, s)),
                         lambda s: lax.cond(b < VALUE_PASSES + 2 * DESCENT_STEPS, descend,
                                            lambda t: halve(False, halve(False, t)), s), state)
        return b + 2, state, pending(state)

    _, (lo, hi, cnt_ge, thr_raw, _), _ = lax.while_loop(
        lambda c: (c[0] < VALUE_PASSES + 2 * DESCENT_STEPS + 32) & (c[2] > 0.0), two_passes,
        (jnp.int32(SEARCH_PASSES_MIN), state, pending(state)))
    enough = thr_raw > f32_lowest
    thr = jnp.where(enough, thr_raw, f32_lowest)

    tie_lane = enough & (cnt_ge > topk) & (t_idx >= topk)

    @pl.when(jnp.max(jnp.where(tie_lane, 1.0, 0.0)) > 0.0)
    def _():
        need = topk - count(lambda x, r0: x > thr)

        def mark_equal(c, carry):
            r0 = pl.multiple_of(c * COUNT_CHUNK, COUNT_CHUNK)
            eq_ref[pl.ds(r0, COUNT_CHUNK), :] = jnp.where(sc_ref[pl.ds(r0, COUNT_CHUNK), :] == thr, 1.0, 0.0)
            return carry

        lax.fori_loop(0, ncc, mark_equal, 0)

        def idx_step(b, lim):
            cand = lim | (jnp.int32(1) << (index_bits - 1 - b))
            below = fold_scores(lambda e, r0: jnp.where(r0 + crow_iota < cand, e, 0.0), jnp.sum, jnp.add, 0.0,
                                src_ref=eq_ref)
            return jnp.where(below < need, cand, lim)

        lim = lax.fori_loop(0, index_bits, idx_step, jnp.zeros((1, LANES), jnp.int32))

        def demote(c, carry):
            r0 = pl.multiple_of(c * COUNT_CHUNK, COUNT_CHUNK)
            x = sc_ref[pl.ds(r0, COUNT_CHUNK), :]
            drop = (eq_ref[pl.ds(r0, COUNT_CHUNK), :] > 0.0) & (r0 + crow_iota > lim) & tie_lane
            sc_ref[pl.ds(r0, COUNT_CHUNK), :] = jnp.where(drop, neg_inf, x)
            return carry

        lax.fori_loop(0, ncc, demote, 0)

    mask_off = float(jnp.finfo(MXU_DTYPE).min)

    def build_mask(c, carry):
        r0 = chunk_start(c)
        selected = sc_ref[pl.ds(r0, KEY_CHUNK), :] >= thr
        mb_ref[pl.ds(r0, KEY_CHUNK), :] = jnp.where(selected, 0.0, mask_off).astype(MXU_DTYPE)
        return carry

    lax.fori_loop(0, nch, build_mask, 0)

    def attn_scores(c, buf):
        sq_ref, mx_ref = buf
        r0 = chunk_start(c)
        kaug = jnp.concatenate([kbd_ref[0, pl.ds(r0, KEY_CHUNK), :], mb_ref[pl.ds(r0, KEY_CHUNK), :]], axis=1)
        for p in range(N_HEADS // 2):
            s = _nt_dot(kaug, pair_q(qbs_ref, p))
            sq_ref[:, p * 2 * BLOCK:(p + 1) * 2 * BLOCK] = s
            for hh in range(2):
                cols = slice((2 * p + hh) * BLOCK, (2 * p + hh + 1) * BLOCK)
                mx_ref[:, cols] = _fold_rows(s[:, hh * BLOCK:(hh + 1) * BLOCK], jnp.max)

    def accumulate(c, buf):
        sq_ref, mx_ref = buf
        m_run = m_ref[...]
        m_new = jnp.maximum(m_run, jnp.max(mx_ref[...], axis=0, keepdims=True))
        m_ref[...] = m_new
        acc_ref[...] = acc_ref[...] * jnp.exp2(m_run - m_new)
        vt = vbt_ref[0, c]
        for p in range(N_HEADS // 2):
            pcols = slice(p * 2 * BLOCK, (p + 1) * 2 * BLOCK)
            e = jnp.exp2(sq_ref[:, pcols] - m_new[:, pcols])
            acc_ref[:, pcols] += jnp.dot(vt, e.astype(MXU_DTYPE), preferred_element_type=jnp.float32)

    acc_ref[...] = jnp.zeros(acc_ref.shape, jnp.float32)
    m_ref[...] = jnp.full(m_ref.shape, f32_lowest, jnp.float32)
    pipeline(attn_scores, accumulate)

    o_t = acc_ref[:HEAD_DIM, :] / acc_ref[HEAD_DIM:HEAD_DIM + 1, :]
    for p in range(N_HEADS // 2):
        pair = jnp.concatenate([o_t[:, (2 * p) * BLOCK:(2 * p + 1) * BLOCK],
                                o_t[:, (2 * p + 1) * BLOCK:(2 * p + 2) * BLOCK]], axis=0)
        out_ref[0, :, p * LANES:(p + 1) * LANES] = pair.T.astype(out_ref.dtype)


def _sparse(qi, qb, wt, kid, kbd, vbt, batch, seq):
    topk = min(TOPK_MAX, seq // 4)
    nq = seq // BLOCK
    qblk = pl.BlockSpec((1, BLOCK, D_HEADS), lambda b, i: (b, i, 0))
    full = pl.BlockSpec((1, seq, LANES), lambda b, i: (b, 0, 0))
    kern = functools.partial(_sparse_kernel, topk=topk, index_bits=max(1, (seq - 1).bit_length()))
    return pl.pallas_call(
        kern,
        grid=(batch, nq),
        in_specs=[qblk, qblk,
                  pl.BlockSpec((1, IDX_HEADS, BLOCK), lambda b, i: (b, 0, i)),
                  full, full,
                  pl.BlockSpec((1, seq // KEY_CHUNK, HEAD_DIM + L_ROWS, KEY_CHUNK), lambda b, i: (b, 0, 0, 0))],
        out_specs=qblk,
        out_shape=jax.ShapeDtypeStruct((batch, seq, D_HEADS), MXU_DTYPE),
        scratch_shapes=[pltpu.VMEM((seq, LANES), jnp.float32),
                        pltpu.VMEM((seq, LANES), jnp.float32),
                        pltpu.VMEM((seq, LANES), MXU_DTYPE),
                        pltpu.VMEM((N_HEADS * BLOCK, LANES), MXU_DTYPE),
                        pltpu.VMEM((N_HEADS * BLOCK, 2 * LANES), MXU_DTYPE),
                        pltpu.VMEM((HEAD_DIM + L_ROWS, N_HEADS * BLOCK), jnp.float32),
                        pltpu.VMEM((1, N_HEADS * BLOCK), jnp.float32),
                        pltpu.VMEM((KEY_CHUNK, N_HEADS * BLOCK), jnp.float32),
                        pltpu.VMEM((KEY_CHUNK, N_HEADS * BLOCK), jnp.float32),
                        pltpu.VMEM((8, N_HEADS * BLOCK), jnp.float32),
                        pltpu.VMEM((8, N_HEADS * BLOCK), jnp.float32)],
        compiler_params=pltpu.CompilerParams(dimension_semantics=("arbitrary", "arbitrary"),
                                             vmem_limit_bytes=VMEM_LIMIT),
        name="indexer_sparse_attn",
    )(qi, qb, wt, kid, kbd, vbt)


def _merge_kernel(x_ref, g_ref, *refs):
    n_pat = len(DILATED_PATTERNS)
    o_refs, l_refs = refs[:n_pat], refs[n_pat:2 * n_pat]
    yb_ref, wg_ref, wua_ref, wub_ref, wo_ref, x1_ref = refs[2 * n_pat:2 * n_pat + 6]
    slabs = refs[2 * n_pat + 6:]
    x = x_ref[...]
    h = _rms(x, g_ref[...]).astype(MXU_DTYPE)
    tm = x.shape[0]
    n_grp = D_HEADS // LANES

    o_src, l_src, k = [], [], 0
    for (_, d), o_ref, l_ref in zip(DILATED_PATTERNS, o_refs, l_refs):
        if d == 1:
            o_src.append(lambda g, r=o_ref: r[:, g * LANES:(g + 1) * LANES].astype(jnp.float32))
            l_src.append(lambda g, r=l_ref: r[:, g * LANES:(g + 1) * LANES])
            continue
        o_slab, l_slab = slabs[2 * k], slabs[2 * k + 1]
        k += 1
        for r in range(d):
            for g in range(n_grp):
                cols = slice(g * LANES, (g + 1) * LANES)
                o_slab[g, pl.ds(r, tm // d, stride=d), :] = o_ref[0, r, :, cols].astype(jnp.float32)
                l_slab[g, pl.ds(r, tm // d, stride=d), :] = l_ref[0, r, :, cols]
        o_src.append(lambda g, s=o_slab: s[g])
        l_src.append(lambda g, s=l_slab: s[g])

    parts = []
    for g in range(n_grp):
        ls = [f(g) for f in l_src]
        m = functools.reduce(jnp.maximum, ls)
        es = [jnp.exp(l - m) for l in ls]
        num = functools.reduce(jnp.add, [e * f(g) for e, f in zip(es, o_src)])
        parts.append((num / functools.reduce(jnp.add, es)).astype(MXU_DTYPE))
    ya = jnp.concatenate(parts, axis=1)
    ua = jnp.dot(ya, wua_ref[...], preferred_element_type=jnp.float32)
    ub = jnp.dot(yb_ref[...], wub_ref[...], preferred_element_type=jnp.float32)
    ga = jnp.dot(h, wg_ref[:, :D_MODEL], preferred_element_type=jnp.float32)
    gb = jnp.dot(h, wg_ref[:, D_MODEL:], preferred_element_type=jnp.float32)
    merged = jax.nn.sigmoid(ga) * ua + jax.nn.sigmoid(gb) * ub
    x1_ref[...] = x + jnp.dot(merged.astype(MXU_DTYPE), wo_ref[...], preferred_element_type=jnp.float32)


def _merge(x2, g, os_, lses, yb, wg, wua, wub, wo, seq):
    n = x2.shape[0]
    tm = ROW_TILE
    tiles_per_seq = seq // tm
    row = lambda i: (i, 0)
    const = lambda i: (0, 0)
    half = pl.BlockSpec((tm, D_HEADS), row)
    pat_specs = [half if d == 1 else
                 pl.BlockSpec((1, d, tm // d, D_HEADS), lambda i: (i // tiles_per_seq, 0, i % tiles_per_seq, 0))
                 for _, d in DILATED_PATTERNS]
    return pl.pallas_call(
        _merge_kernel,
        grid=(n // tm,),
        in_specs=[pl.BlockSpec((tm, D_MODEL), row), pl.BlockSpec((1, D_MODEL), const)] + pat_specs * 2 + [half] + [
            pl.BlockSpec((D_MODEL, 2 * D_MODEL), const),
            pl.BlockSpec((D_HEADS, D_MODEL), const),
            pl.BlockSpec((D_HEADS, D_MODEL), const),
            pl.BlockSpec((D_MODEL, D_MODEL), const)],
        out_specs=pl.BlockSpec((tm, D_MODEL), row),
        out_shape=jax.ShapeDtypeStruct((n, D_MODEL), jnp.float32),
        scratch_shapes=[pltpu.VMEM((D_HEADS // LANES, tm, LANES), jnp.float32)] * (2 * len(RESIDUE_DILATIONS)),
        compiler_params=pltpu.CompilerParams(dimension_semantics=("arbitrary",), vmem_limit_bytes=VMEM_LIMIT),
        name="mix_gate_out",
    )(x2, g, *os_, *lses, yb, wg, wua, wub, wo)


def _ffn_kernel(x_ref, g_ref, wgate_ref, wup_ref, wdown_ref, gf_ref, out_ref):
    x = x_ref[...]
    h = _rms(x, g_ref[...]).astype(MXU_DTYPE)
    y = x
    for c0 in range(0, D_FF, FF_CHUNK):
        a = jnp.dot(h, wgate_ref[:, c0:c0 + FF_CHUNK], preferred_element_type=jnp.float32)
        u = jnp.dot(h, wup_ref[:, c0:c0 + FF_CHUNK], preferred_element_type=jnp.float32)
        act = (a * jax.nn.sigmoid(a) * u).astype(MXU_DTYPE)
        y = y + jnp.dot(act, wdown_ref[c0:c0 + FF_CHUNK, :], preferred_element_type=jnp.float32)
    out_ref[...] = _rms(y, gf_ref[...])


def _ffn(x1, g, wgate, wup, wdown, gf):
    n = x1.shape[0]
    tm = ROW_TILE
    row = lambda i: (i, 0)
    const = lambda i: (0, 0)
    return pl.pallas_call(
        _ffn_kernel,
        grid=(n // tm,),
        in_specs=[pl.BlockSpec((tm, D_MODEL), row), pl.BlockSpec((1, D_MODEL), const),
                  pl.BlockSpec((D_MODEL, D_FF), const), pl.BlockSpec((D_MODEL, D_FF), const),
                  pl.BlockSpec((D_FF, D_MODEL), const), pl.BlockSpec((1, D_MODEL), const)],
        out_specs=pl.BlockSpec((tm, D_MODEL), row),
        out_shape=jax.ShapeDtypeStruct((n, D_MODEL), jnp.float32),
        compiler_params=pltpu.CompilerParams(dimension_semantics=("arbitrary",), vmem_limit_bytes=VMEM_LIMIT),
        name="ffn_norm",
    )(x1, g, wgate, wup, wdown, gf)


def _rope_tables(seq):
    inv_freq = ROPE_THETA ** (-jnp.arange(HALF, dtype=jnp.float32) / HALF)
    ang = jnp.arange(seq, dtype=jnp.int32).astype(jnp.float32)[:, None] * inv_freq[None, :]
    cos, sin = jnp.cos(ang), jnp.sin(ang)
    return jnp.tile(cos, (1, 4)), jnp.concatenate([-sin, -sin, sin, sin], axis=1)


def kernel(x, norm_mix, w_in, w_up_a, w_up_b, w_out, norm_ffn, w_gate, w_up, w_down, norm_final):
    batch, seq, _ = x.shape
    assert seq % max(d * BLOCK for _, d in DILATED_PATTERNS) == 0 and seq % KEY_CHUNK == 0
    assert all(w // d == BLOCK for w, d in DILATED_PATTERNS)
    n = batch * seq
    bf = MXU_DTYPE
    xf = x.reshape(n, D_MODEL)
    cos_t, sin_t = _rope_tables(seq)
    for layer in range(w_in.shape[0]):
        w = w_in[layer]
        w_pack = jnp.pad(w[:, _packed_columns()], ((0, 0), (0, _P_TOTAL - _P_MISC - HEAD_DIM - IDX_HEADS))).astype(bf)
        w_gates = w[:, _GA:].astype(bf)
        qas, kas, vas, qb, qi, kbd, kid, misc = _project(xf, norm_mix[layer][None], w_pack, cos_t, sin_t, seq)

        dil = []
        for (_, d), q, k, v in zip(DILATED_PATTERNS, qas, kas, vas):
            if d == 1:
                o, lse = _dilated(*(z.reshape(batch, 1, seq, D_HEADS) for z in (q, k, v)), d)
                dil.append((o.reshape(n, D_HEADS), lse.reshape(n, D_HEADS)))
            else:
                dil.append(_dilated(q, k, v, d))

        ones_col = (jnp.arange(L_ROWS) == 0).astype(bf)[None, :]
        vb = jnp.concatenate([misc[:, :HEAD_DIM].astype(bf), jnp.broadcast_to(ones_col, (n, L_ROWS))], axis=1)
        vbt = jnp.swapaxes(vb.reshape(batch, seq // KEY_CHUNK, KEY_CHUNK, HEAD_DIM + L_ROWS), 2, 3)
        wt = jnp.swapaxes(misc[:, HEAD_DIM:HEAD_DIM + IDX_HEADS].reshape(batch, seq, IDX_HEADS), 1, 2)
        r3 = lambda z: z.reshape(batch, seq, z.shape[-1])
        yb = _sparse(r3(qi), r3(qb), wt, r3(kid), r3(kbd), vbt, batch, seq).reshape(n, D_HEADS)

        x1 = _merge(xf, norm_mix[layer][None], [o for o, _ in dil], [l for _, l in dil], yb,
                    w_gates, w_up_a[layer].astype(bf), w_up_b[layer].astype(bf), w_out[layer].astype(bf), seq)
        last = layer == w_in.shape[0] - 1
        assert last, "the final norm is fused into the FFN kernel of the last layer"
        xf = _ffn(x1, norm_ffn[layer][None], w_gate[layer].astype(bf), w_up[layer].astype(bf),
                  w_down[layer].astype(bf), norm_final[None])
    return xf.reshape(batch, seq, D_MODEL)
```

```python
import functools

import numpy as np
import jax
import jax.numpy as jnp
from jax import lax
from jax.experimental import pallas as pl
from jax.experimental.pallas import tpu as pltpu

D_MODEL = 1024
HEAD_DIM = 64
HALF = HEAD_DIM // 2
N_HEADS = 8
D_HEADS = N_HEADS * HEAD_DIM
IDX_HEADS = 8
DILATED_PATTERNS = ((128, 1), (512, 4), (2048, 16))
RESIDUE_DILATIONS = tuple(d for _, d in DILATED_PATTERNS if d > 1)
TOPK_MAX = 256
D_FF = 2816
ROPE_THETA = 10000.0
RMS_EPS = 1e-6
BLOCK = 128
ATTN_SCALE = HEAD_DIM ** -0.5
IDX_SCALE = (HEAD_DIM ** -0.5) * (IDX_HEADS ** -0.5)
LOG2_E = float(np.log2(np.e))

LANES = 128
INT_MIN = -(2 ** 31)
MIN_NORMAL_KEY = 1 << 23
KEY_CHUNK = 1024
COUNT_CHUNK = 512
COUNT_FOLD = 8
VALUE_PASSES = 14
DESCENT_STEPS = 6
SEARCH_PASSES_MIN = 12
L_ROWS = 8
ROW_TILE = 512
DIL_BLOCKS = 4
FF_CHUNK = 1408
VMEM_LIMIT = 56 * 1024 * 1024
MXU_DTYPE = jnp.bfloat16

_SPLITS = (D_HEADS, D_HEADS, D_HEADS, D_HEADS, HEAD_DIM, HEAD_DIM, IDX_HEADS * HEAD_DIM, HEAD_DIM, IDX_HEADS,
           D_MODEL, D_MODEL)
_OFF = np.concatenate([[0], np.cumsum(_SPLITS)])
(_QA, _KA, _VA, _QB, _KB, _VB, _QI, _KI, _WI, _GA, _GB) = (int(o) for o in _OFF[:-1])

_P_QA, _P_KA, _P_VA, _P_QB, _P_QI = 0, 512, 1024, 1536, 2048
_P_KBD, _P_KID, _P_MISC = 2560, 2688, 2816
_P_TOTAL = 2944


def _pair_perm():
    idx = np.empty(D_HEADS, np.int64)
    for j in range(D_HEADS):
        g, l = divmod(j, LANES)
        quarter, e = divmod(l, HALF)
        head = 2 * g + (quarter % 2)
        idx[j] = head * HEAD_DIM + (quarter // 2) * HALF + e
    return idx


def _dup_perm():
    idx = np.empty(LANES, np.int64)
    for l in range(LANES):
        quarter, e = divmod(l, HALF)
        idx[l] = (quarter // 2) * HALF + e
    return idx


def _packed_columns():
    pp, dp = _pair_perm(), _dup_perm()
    return np.concatenate([
        _QA + pp, _KA + pp, _VA + np.arange(D_HEADS), _QB + pp, _QI + pp,
        _KB + dp, _KI + dp, _VB + np.arange(HEAD_DIM), _WI + np.arange(IDX_HEADS)])


def _rms(x, g):
    ms = jnp.mean(x * x, axis=-1, keepdims=True)
    return x * lax.rsqrt(ms + RMS_EPS) * g


def _nt_dot(a, b):
    return lax.dot_general(a, b, (((1,), (1,)), ((), ())), preferred_element_type=jnp.float32)


def _proj_kernel(x_ref, g_ref, w_ref, cos_ref, sin_ref, *refs):
    n_lay = 1 + len(RESIDUE_DILATIONS)
    qa_refs, ka_refs, va_refs = refs[:n_lay], refs[n_lay:2 * n_lay], refs[2 * n_lay:3 * n_lay]
    qb_ref, qi_ref, kbd_ref, kid_ref, misc_ref, h_ref, slab_ref = refs[3 * n_lay:]
    h_ref[...] = _rms(x_ref[...], g_ref[...]).astype(MXU_DTYPE)
    cos = cos_ref[...]
    sin = sin_ref[...]

    def mm(c0, width):
        return jnp.dot(h_ref[...], w_ref[:, c0:c0 + width], preferred_element_type=jnp.float32)

    def rope(z):
        parts = []
        for g in range(z.shape[1] // LANES):
            zg = z[:, g * LANES:(g + 1) * LANES]
            parts.append(zg * cos + pltpu.roll(zg, 2 * HALF, axis=1) * sin)
        return parts[0] if len(parts) == 1 else jnp.concatenate(parts, axis=1)

    def emit(y, out_refs):
        out_refs[0][...] = y.astype(MXU_DTYPE)
        for g in range(D_HEADS // LANES):
            slab_ref[g] = y[:, g * LANES:(g + 1) * LANES]
        for d, ref in zip(RESIDUE_DILATIONS, out_refs[1:]):
            rows = y.shape[0] // d
            for r in range(d):
                for g in range(D_HEADS // LANES):
                    ref[0, r, :, g * LANES:(g + 1) * LANES] = (
                        slab_ref[g, pl.ds(r, rows, stride=d), :].astype(MXU_DTYPE))

    emit(rope(mm(_P_QA, D_HEADS)) * ATTN_SCALE, qa_refs)
    emit(rope(mm(_P_KA, D_HEADS)), ka_refs)
    emit(mm(_P_VA, D_HEADS), va_refs)
    qb_ref[...] = (rope(mm(_P_QB, D_HEADS)) * (ATTN_SCALE * LOG2_E)).astype(MXU_DTYPE)
    qi_ref[...] = rope(mm(_P_QI, D_HEADS)).astype(MXU_DTYPE)
    kbd_ref[...] = rope(mm(_P_KBD, LANES)).astype(MXU_DTYPE)
    kid_ref[...] = rope(mm(_P_KID, LANES)).astype(MXU_DTYPE)
    misc_ref[...] = mm(_P_MISC, LANES)


def _project(x2, g, w_pack, cos_t, sin_t, seq):
    n = x2.shape[0]
    tm = ROW_TILE
    tiles_per_seq = seq // tm
    row = lambda i: (i, 0)
    const = lambda i: (0, 0)
    pos = lambda i: (i % tiles_per_seq, 0)
    batch = n // seq
    wide = jax.ShapeDtypeStruct((n, D_HEADS), MXU_DTYPE)
    narrow = jax.ShapeDtypeStruct((n, LANES), MXU_DTYPE)
    wide_spec = pl.BlockSpec((tm, D_HEADS), row)
    lay_shapes = [wide] + [jax.ShapeDtypeStruct((batch, d, seq // d, D_HEADS), MXU_DTYPE) for d in RESIDUE_DILATIONS]
    lay_specs = [wide_spec] + [
        pl.BlockSpec((1, d, tm // d, D_HEADS), lambda i: (i // tiles_per_seq, 0, i % tiles_per_seq, 0))
        for d in RESIDUE_DILATIONS]
    n_lay = len(lay_shapes)
    outs = pl.pallas_call(
        _proj_kernel,
        grid=(n // tm,),
        in_specs=[
            pl.BlockSpec((tm, D_MODEL), row),
            pl.BlockSpec((1, D_MODEL), const),
            pl.BlockSpec((D_MODEL, _P_TOTAL), const),
            pl.BlockSpec((tm, LANES), pos),
            pl.BlockSpec((tm, LANES), pos),
        ],
        out_specs=lay_specs * 3 + [wide_spec] * 2 + [pl.BlockSpec((tm, LANES), row)] * 3,
        out_shape=lay_shapes * 3 + [wide] * 2 + [narrow, narrow, jax.ShapeDtypeStruct((n, LANES), jnp.float32)],
        scratch_shapes=[pltpu.VMEM((tm, D_MODEL), MXU_DTYPE),
                        pltpu.VMEM((D_HEADS // LANES, tm, LANES), jnp.float32)],
        compiler_params=pltpu.CompilerParams(dimension_semantics=("arbitrary",), vmem_limit_bytes=VMEM_LIMIT),
        name="proj_rope",
    )(x2, g, w_pack, cos_t, sin_t)
    return (outs[:n_lay], outs[n_lay:2 * n_lay], outs[2 * n_lay:3 * n_lay]) + tuple(outs[3 * n_lay:])


def _dil_kernel(q_ref, k_ref, v_ref, kp_ref, vp_ref, o_ref, lse_ref, *, n_blocks):
    n = pl.program_id(2)
    kj = lax.broadcasted_iota(jnp.int32, (2 * BLOCK, LANES), 0)
    qi = lax.broadcasted_iota(jnp.int32, (2 * BLOCK, LANES), 1)
    band = (kj >= qi) & (kj <= qi + BLOCK)
    lane = lax.broadcasted_iota(jnp.int32, (1, LANES), 1)
    pairs = range(N_HEADS // 2)
    col = lambda p: slice(p * LANES, (p + 1) * LANES)

    for j in range(n_blocks):
        rows = slice(j * BLOCK, (j + 1) * BLOCK)
        if j == 0:
            k_prev = lambda p: kp_ref[0, 0, :, col(p)]
            v_prev = lambda p: vp_ref[0, 0, :, col(p)]
            bias = jnp.where(band & ((kj >= BLOCK) | (n > 0)), 0.0, -jnp.inf)
        else:
            prev_rows = slice((j - 1) * BLOCK, j * BLOCK)
            k_prev = lambda p, r=prev_rows: k_ref[0, 0, r, col(p)]
            v_prev = lambda p, r=prev_rows: v_ref[0, 0, r, col(p)]
            bias = jnp.where(band, 0.0, -jnp.inf)

        scores = []
        for p in pairs:
            qp = q_ref[0, 0, rows, col(p)]
            qs = jnp.concatenate([jnp.where(((lane // HALF) % 2) == hh, qp, jnp.zeros_like(qp))
                                  for hh in range(2)], axis=0)
            k2 = jnp.concatenate([k_prev(p), k_ref[0, 0, rows, col(p)]], axis=0)
            scores.append(_nt_dot(k2, qs))

        probs, dens, lses = [], [], []
        for p in pairs:
            for hh in range(2):
                sh = scores[p][:, hh * BLOCK:(hh + 1) * BLOCK] + bias
                m = jnp.max(sh, axis=0, keepdims=True)
                e = jnp.exp(sh - m)
                den = jnp.sum(e, axis=0, keepdims=True)
                probs.append(e.astype(MXU_DTYPE))
                dens.append(den)
                lses.append(m + jnp.log(den))

        for p in pairs:
            v2 = jnp.concatenate([v_prev(p), v_ref[0, 0, rows, col(p)]], axis=0)
            o_t = lax.dot_general(v2, jnp.concatenate(probs[2 * p:2 * p + 2], axis=1), (((0,), (0,)), ((), ())),
                                  preferred_element_type=jnp.float32)
            tile = jnp.concatenate([o_t[:HEAD_DIM, :BLOCK] / dens[2 * p],
                                    o_t[HEAD_DIM:, BLOCK:] / dens[2 * p + 1]], axis=0)
            o_ref[0, 0, rows, col(p)] = tile.T.astype(o_ref.dtype)
            lse_tile = jnp.concatenate([jnp.broadcast_to(l, (HEAD_DIM, BLOCK)) for l in lses[2 * p:2 * p + 2]],
                                       axis=0)
            lse_ref[0, 0, rows, col(p)] = lse_tile.T


def _dilated(q, k, v, dilation):
    batch, _, m_len, _ = q.shape
    n_blocks = min(DIL_BLOCKS, m_len // BLOCK)
    tile_rows = n_blocks * BLOCK
    cur = lambda b, r, n: (b, r, n, 0)
    prev = lambda b, r, n: (b, r, jnp.maximum(n * n_blocks - 1, 0), 0)
    tile = (1, 1, tile_rows, D_HEADS)
    blk = (1, 1, BLOCK, D_HEADS)
    return pl.pallas_call(
        functools.partial(_dil_kernel, n_blocks=n_blocks),
        grid=(batch, dilation, m_len // tile_rows),
        in_specs=[pl.BlockSpec(tile, cur), pl.BlockSpec(tile, cur), pl.BlockSpec(tile, cur),
                  pl.BlockSpec(blk, prev), pl.BlockSpec(blk, prev)],
        out_specs=[pl.BlockSpec(tile, cur), pl.BlockSpec(tile, cur)],
        out_shape=[jax.ShapeDtypeStruct(q.shape, MXU_DTYPE), jax.ShapeDtypeStruct(q.shape, jnp.float32)],
        compiler_params=pltpu.CompilerParams(dimension_semantics=("arbitrary",) * 3),
        name=f"dilated_d{dilation}",
    )(q, k, v, k, v)


def _key_to_f32(key):
    bits = key ^ ((key >> 31) & jnp.int32(0x7FFFFFFF))
    return lax.bitcast_convert_type(bits, jnp.float32)


def _f32_to_key(x):
    bits = lax.bitcast_convert_type(x, jnp.int32)
    return bits ^ ((bits >> 31) & jnp.int32(0x7FFFFFFF))


def _fold_rows(x, op):
    rows = x.shape[0]
    y = op(x.reshape(rows // 64, 64, LANES), axis=0)
    return op(y.reshape(8, 8, LANES), axis=0)


def _sparse_kernel(qi_ref, qb_ref, wt_ref, kid_ref, kbd_ref, vbt_ref, out_ref,
                   sc_ref, eq_ref, eqc_ref, mb_ref, qis_ref, qbs_ref, acc_ref, m_ref, sqa_ref, sqb_ref, mxa_ref,
                   mxb_ref, *, topk):
    i = pl.program_id(1)
    nch = i // (KEY_CHUNK // BLOCK) + 1
    ncc = i // (COUNT_CHUNK // BLOCK) + 1
    lane = lax.broadcasted_iota(jnp.int32, (1, LANES), 1)
    t_idx = i * BLOCK + lane
    row_iota = lax.broadcasted_iota(jnp.int32, (KEY_CHUNK, LANES), 0)
    crow_iota = lax.broadcasted_iota(jnp.int32, (COUNT_CHUNK, LANES), 0)
    neg_inf = jnp.float32(-jnp.inf)
    f32_lowest = jnp.float32(jnp.finfo(jnp.float32).min)

    eye = jnp.where(lax.broadcasted_iota(jnp.int32, (BLOCK, LANES), 0) == lane, 1.0, 0.0).astype(MXU_DTYPE)
    for h in range(N_HEADS):
        cols = slice((h // 2) * LANES, (h // 2 + 1) * LANES)
        head_lanes = ((lane // HALF) % 2) == (h % 2)
        rows = slice(h * BLOCK, (h + 1) * BLOCK)
        qis_ref[rows, :] = jnp.where(head_lanes, qi_ref[0, :, cols], jnp.zeros((), MXU_DTYPE))
        qbs_ref[rows, :LANES] = jnp.where(head_lanes, qb_ref[0, :, cols], jnp.zeros((), MXU_DTYPE))
        qbs_ref[rows, LANES:] = eye

    def chunk_start(c):
        return pl.multiple_of(c * KEY_CHUNK, KEY_CHUNK)

    def pair_q(ref, p):
        return ref[p * 2 * BLOCK:(p + 1) * 2 * BLOCK, :]

    buf_a, buf_b = (sqa_ref, mxa_ref), (sqb_ref, mxb_ref)

    def pipeline(produce, consume):
        produce(0, buf_a)

        def two_chunks(j, carry):
            c = 2 * j
            produce(c + 1, buf_b)
            consume(c, buf_a)
            produce(c + 2, buf_a)
            consume(c + 1, buf_b)
            return carry

        lax.fori_loop(0, (nch - 1) // 2, two_chunks, 0)
        last = nch - 1

        @pl.when(last % 2 == 1)
        def _():
            produce(last, buf_b)
            consume(last - 1, buf_a)
            consume(last, buf_b)

        @pl.when(last % 2 == 0)
        def _():
            consume(last, buf_a)

    def score_chunk(c, carry):
        r0 = chunk_start(c)
        kc = kid_ref[0, pl.ds(r0, KEY_CHUNK), :]
        acc = jnp.zeros((KEY_CHUNK, LANES), jnp.float32)
        for p in range(N_HEADS // 2):
            d = _nt_dot(kc, pair_q(qis_ref, p))
            for hh in range(2):
                h = 2 * p + hh
                acc = acc + jnp.maximum(d[:, hh * BLOCK:(hh + 1) * BLOCK], 0.0) * wt_ref[0, h:h + 1, :]
        admissible = r0 + row_iota <= t_idx
        scores = acc * IDX_SCALE
        masked = jnp.where(admissible, scores, neg_inf)
        sc_ref[pl.ds(r0, KEY_CHUNK), :] = masked
        top8, bot8 = carry
        return (jnp.maximum(top8, _fold_rows(masked, jnp.max)),
                jnp.minimum(bot8, _fold_rows(jnp.where(admissible, scores, -neg_inf), jnp.min)))

    top8, bot8 = lax.fori_loop(0, nch, score_chunk, (jnp.full((8, LANES), neg_inf, jnp.float32),
                                                     jnp.full((8, LANES), -neg_inf, jnp.float32)))
    top = jnp.max(top8, axis=0, keepdims=True)
    bot = jnp.min(bot8, axis=0, keepdims=True)

    def fold_scores(value, op, pairwise, start, src_ref=sc_ref):
        def body(c, acc):
            r0 = pl.multiple_of(c * COUNT_CHUNK, COUNT_CHUNK)
            val = value(src_ref[pl.ds(r0, COUNT_CHUNK), :], r0)
            return pairwise(acc, op(val.reshape(COUNT_FOLD, COUNT_CHUNK // COUNT_FOLD, LANES), axis=0))

        acc = lax.fori_loop(0, ncc // 2, lambda j, a: body(2 * j + 1, body(2 * j, a)),
                            jnp.full((COUNT_CHUNK // COUNT_FOLD, LANES), start, jnp.float32))
        acc = lax.cond(ncc % 2 == 1, lambda a: body(ncc - 1, a), lambda a: a, acc)
        return op(acc, axis=0, keepdims=True)

    def count(pred):
        return fold_scores(lambda x, r0: jnp.where(pred(x, r0), 1.0, 0.0), jnp.sum, jnp.add, 0.0)

    key_top = _f32_to_key(top)
    lo = _f32_to_key(bot) - jnp.int32(1 << 23)
    hi = key_top + 1
    hi = jnp.where((hi >= -MIN_NORMAL_KEY) & (hi < MIN_NORMAL_KEY), jnp.int32(MIN_NORMAL_KEY), hi)
    unknown = jnp.float32(2 * sc_ref.shape[0])

    def is_settled(carry):
        lo, hi, cnt_lo = carry[:3]
        return (cnt_lo == topk) | (hi - lo == 1) | (t_idx < topk)

    def probe(mid, carry):
        lo, hi, cnt_lo, thr_lo, thr_hi = carry
        thr_c = _key_to_f32(mid)
        cnt = count(lambda x, r0: x >= thr_c)
        take = (cnt >= topk) & ~is_settled(carry)
        drop = (cnt < topk) & ~is_settled(carry)
        return (jnp.where(take, mid, lo), jnp.where(drop, mid, hi), jnp.where(take, cnt, cnt_lo),
                jnp.where(take, thr_c, thr_lo), jnp.where(drop, thr_c, thr_hi))

    def halve(by_value, carry):
        lo, hi = carry[:2]
        mid = lo + lax.shift_right_logical(hi - lo, 1)
        if by_value:
            mid_val = _f32_to_key(0.5 * _key_to_f32(lo) + 0.5 * _key_to_f32(hi))
            mid = jnp.where((mid_val > lo) & (mid_val < hi), mid_val, mid)
        return probe(mid, carry)

    def descend(carry):
        lo, hi, cnt_lo, thr_lo, thr_hi = carry
        v = fold_scores(lambda x, r0: jnp.where(x < thr_hi, x, neg_inf), jnp.max, jnp.maximum, neg_inf)
        cnt = count(lambda x, r0: x >= v)
        key_v = _f32_to_key(v)
        take = (cnt >= topk) & ~is_settled(carry)
        drop = (cnt < topk) & ~is_settled(carry)
        return (jnp.where(take, key_v, lo), jnp.where(take, key_v + 1, jnp.where(drop, key_v, hi)),
                jnp.where(take, cnt, cnt_lo), jnp.where(take, v, thr_lo), jnp.where(drop, v, thr_hi))

    def pending(carry):
        return jnp.max(jnp.where(is_settled(carry), 0.0, 1.0))

    state = lax.fori_loop(0, SEARCH_PASSES_MIN, lambda b, c: halve(True, c),
                          (lo, hi, jnp.full((1, LANES), unknown), _key_to_f32(lo), _key_to_f32(hi)))

    def two_passes(carry):
        b, state, _ = carry
        state = lax.cond(b < VALUE_PASSES, lambda s: halve(True, halve(True, s)),
                         lambda s: lax.cond(b < VALUE_PASSES + 2 * DESCENT_STEPS, descend,
                                            lambda t: halve(False, halve(False, t)), s), state)
        return b + 2, state, pending(state)

    _, (lo, hi, cnt_ge, thr_raw, _), _ = lax.while_loop(
        lambda c: (c[0] < VALUE_PASSES + 2 * DESCENT_STEPS + 32) & (c[2] > 0.0), two_passes,
        (jnp.int32(SEARCH_PASSES_MIN), state, pending(state)))
    enough = thr_raw > f32_lowest
    thr = jnp.where(enough, thr_raw, f32_lowest)

    tie_lane = enough & (cnt_ge > topk) & (t_idx >= topk)

    @pl.when(jnp.max(jnp.where(tie_lane, 1.0, 0.0)) > 0.0)
    def _():
        need = topk - count(lambda x, r0: x > thr)

        def mark_equal(c, carry):
            seen, chunk_of, seen_before = carry
            r0 = pl.multiple_of(c * COUNT_CHUNK, COUNT_CHUNK)
            eq = jnp.where(sc_ref[pl.ds(r0, COUNT_CHUNK), :] == thr, 1.0, 0.0)
            eq_ref[pl.ds(r0, COUNT_CHUNK), :] = eq
            here = jnp.sum(_fold_rows(eq, jnp.sum), axis=0, keepdims=True)
            first_reach = (seen < need) & (seen + here >= need)
            return (seen + here, jnp.where(first_reach, c, chunk_of), jnp.where(first_reach, seen, seen_before))

        zero = jnp.zeros((1, LANES), jnp.float32)
        _, chunk_of, seen_before = lax.fori_loop(0, ncc, mark_equal, (zero, jnp.zeros((1, LANES), jnp.int32), zero))
        need_here = need - seen_before

        eqc_ref[...] = jnp.zeros(eqc_ref.shape, jnp.float32)

        def gather_chunk(c, carry):
            r0 = pl.multiple_of(c * COUNT_CHUNK, COUNT_CHUNK)
            eqc_ref[...] += jnp.where(chunk_of == c, eq_ref[pl.ds(r0, COUNT_CHUNK), :], 0.0)
            return carry

        lax.fori_loop(0, ncc, gather_chunk, 0)

        def idx_step(b, lim):
            cand = lim | (jnp.int32(1) << (COUNT_CHUNK.bit_length() - 2 - b))
            below = jnp.sum(_fold_rows(jnp.where(crow_iota < cand, eqc_ref[...], 0.0), jnp.sum), axis=0, keepdims=True)
            return jnp.where(below < need_here, cand, lim)

        lim = chunk_of * COUNT_CHUNK + lax.fori_loop(0, COUNT_CHUNK.bit_length() - 1, idx_step,
                                                     jnp.zeros((1, LANES), jnp.int32))

        def demote(c, carry):
            r0 = pl.multiple_of(c * COUNT_CHUNK, COUNT_CHUNK)
            x = sc_ref[pl.ds(r0, COUNT_CHUNK), :]
            drop = (eq_ref[pl.ds(r0, COUNT_CHUNK), :] > 0.0) & (r0 + crow_iota > lim) & tie_lane
            sc_ref[pl.ds(r0, COUNT_CHUNK), :] = jnp.where(drop, neg_inf, x)
            return carry

        lax.fori_loop(0, ncc, demote, 0)

    mask_off = float(jnp.finfo(MXU_DTYPE).min)

    def build_mask(c, carry):
        r0 = chunk_start(c)
        selected = sc_ref[pl.ds(r0, KEY_CHUNK), :] >= thr
        mb_ref[pl.ds(r0, KEY_CHUNK), :] = jnp.where(selected, 0.0, mask_off).astype(MXU_DTYPE)
        return carry

    lax.fori_loop(0, nch, build_mask, 0)

    def attn_scores(c, buf):
        sq_ref, mx_ref = buf
        r0 = chunk_start(c)
        kaug = jnp.concatenate([kbd_ref[0, pl.ds(r0, KEY_CHUNK), :], mb_ref[pl.ds(r0, KEY_CHUNK), :]], axis=1)
        for p in range(N_HEADS // 2):
            s = _nt_dot(kaug, pair_q(qbs_ref, p))
            sq_ref[:, p * 2 * BLOCK:(p + 1) * 2 * BLOCK] = s
            for hh in range(2):
                cols = slice((2 * p + hh) * BLOCK, (2 * p + hh + 1) * BLOCK)
                mx_ref[:, cols] = _fold_rows(s[:, hh * BLOCK:(hh + 1) * BLOCK], jnp.max)

    def accumulate(c, buf):
        sq_ref, mx_ref = buf
        m_run = m_ref[...]
        m_new = jnp.maximum(m_run, jnp.max(mx_ref[...], axis=0, keepdims=True))
        m_ref[...] = m_new
        acc_ref[...] = acc_ref[...] * jnp.exp2(m_run - m_new)
        vt = vbt_ref[0, c]
        for p in range(N_HEADS // 2):
            pcols = slice(p * 2 * BLOCK, (p + 1) * 2 * BLOCK)
            e = jnp.exp2(sq_ref[:, pcols] - m_new[:, pcols])
            acc_ref[:, pcols] += jnp.dot(vt, e.astype(MXU_DTYPE), preferred_element_type=jnp.float32)

    acc_ref[...] = jnp.zeros(acc_ref.shape, jnp.float32)
    m_ref[...] = jnp.full(m_ref.shape, f32_lowest, jnp.float32)
    pipeline(attn_scores, accumulate)

    o_t = acc_ref[:HEAD_DIM, :] / acc_ref[HEAD_DIM:HEAD_DIM + 1, :]
    for p in range(N_HEADS // 2):
        pair = jnp.concatenate([o_t[:, (2 * p) * BLOCK:(2 * p + 1) * BLOCK],
                                o_t[:, (2 * p + 1) * BLOCK:(2 * p + 2) * BLOCK]], axis=0)
        out_ref[0, :, p * LANES:(p + 1) * LANES] = pair.T.astype(out_ref.dtype)


def _sparse(qi, qb, wt, kid, kbd, vbt, batch, seq):
    topk = min(TOPK_MAX, seq // 4)
    nq = seq // BLOCK
    qblk = pl.BlockSpec((1, BLOCK, D_HEADS), lambda b, i: (b, i, 0))
    full = pl.BlockSpec((1, seq, LANES), lambda b, i: (b, 0, 0))
    kern = functools.partial(_sparse_kernel, topk=topk)
    return pl.pallas_call(
        kern,
        grid=(batch, nq),
        in_specs=[qblk, qblk,
                  pl.BlockSpec((1, IDX_HEADS, BLOCK), lambda b, i: (b, 0, i)),
                  full, full,
                  pl.BlockSpec((1, seq // KEY_CHUNK, HEAD_DIM + L_ROWS, KEY_CHUNK), lambda b, i: (b, 0, 0, 0))],
        out_specs=qblk,
        out_shape=jax.ShapeDtypeStruct((batch, seq, D_HEADS), MXU_DTYPE),
        scratch_shapes=[pltpu.VMEM((seq, LANES), jnp.float32),
                        pltpu.VMEM((seq, LANES), jnp.float32),
                        pltpu.VMEM((COUNT_CHUNK, LANES), jnp.float32),
                        pltpu.VMEM((seq, LANES), MXU_DTYPE),
                        pltpu.VMEM((N_HEADS * BLOCK, LANES), MXU_DTYPE),
                        pltpu.VMEM((N_HEADS * BLOCK, 2 * LANES), MXU_DTYPE),
                        pltpu.VMEM((HEAD_DIM + L_ROWS, N_HEADS * BLOCK), jnp.float32),
                        pltpu.VMEM((1, N_HEADS * BLOCK), jnp.float32),
                        pltpu.VMEM((KEY_CHUNK, N_HEADS * BLOCK), jnp.float32),
                        pltpu.VMEM((KEY_CHUNK, N_HEADS * BLOCK), jnp.float32),
                        pltpu.VMEM((8, N_HEADS * BLOCK), jnp.float32),
                        pltpu.VMEM((8, N_HEADS * BLOCK), jnp.float32)],
        compiler_params=pltpu.CompilerParams(dimension_semantics=("arbitrary", "arbitrary"),
                                             vmem_limit_bytes=VMEM_LIMIT),
        name="indexer_sparse_attn",
    )(qi, qb, wt, kid, kbd, vbt)


def _merge_kernel(x_ref, g_ref, *refs):
    n_pat = len(DILATED_PATTERNS)
    o_refs, l_refs = refs[:n_pat], refs[n_pat:2 * n_pat]
    yb_ref, wg_ref, wua_ref, wub_ref, wo_ref, x1_ref = refs[2 * n_pat:2 * n_pat + 6]
    slabs = refs[2 * n_pat + 6:]
    x = x_ref[...]
    h = _rms(x, g_ref[...]).astype(MXU_DTYPE)
    tm = x.shape[0]
    n_grp = D_HEADS // LANES

    o_src, l_src, k = [], [], 0
    for (_, d), o_ref, l_ref in zip(DILATED_PATTERNS, o_refs, l_refs):
        if d == 1:
            o_src.append(lambda g, r=o_ref: r[:, g * LANES:(g + 1) * LANES].astype(jnp.float32))
            l_src.append(lambda g, r=l_ref: r[:, g * LANES:(g + 1) * LANES])
            continue
        o_slab, l_slab = slabs[2 * k], slabs[2 * k + 1]
        k += 1
        for r in range(d):
            for g in range(n_grp):
                cols = slice(g * LANES, (g + 1) * LANES)
                o_slab[g, pl.ds(r, tm // d, stride=d), :] = o_ref[0, r, :, cols].astype(jnp.float32)
                l_slab[g, pl.ds(r, tm // d, stride=d), :] = l_ref[0, r, :, cols]
        o_src.append(lambda g, s=o_slab: s[g])
        l_src.append(lambda g, s=l_slab: s[g])

    parts = []
    for g in range(n_grp):
        ls = [f(g) for f in l_src]
        m = functools.reduce(jnp.maximum, ls)
        es = [jnp.exp(l - m) for l in ls]
        num = functools.reduce(jnp.add, [e * f(g) for e, f in zip(es, o_src)])
        parts.append((num / functools.reduce(jnp.add, es)).astype(MXU_DTYPE))
    ya = jnp.concatenate(parts, axis=1)
    ua = jnp.dot(ya, wua_ref[...], preferred_element_type=jnp.float32)
    ub = jnp.dot(yb_ref[...], wub_ref[...], preferred_element_type=jnp.float32)
    ga = jnp.dot(h, wg_ref[:, :D_MODEL], preferred_element_type=jnp.float32)
    gb = jnp.dot(h, wg_ref[:, D_MODEL:], preferred_element_type=jnp.float32)
    merged = jax.nn.sigmoid(ga) * ua + jax.nn.sigmoid(gb) * ub
    x1_ref[...] = x + jnp.dot(merged.astype(MXU_DTYPE), wo_ref[...], preferred_element_type=jnp.float32)


def _merge(x2, g, os_, lses, yb, wg, wua, wub, wo, seq):
    n = x2.shape[0]
    tm = ROW_TILE
    tiles_per_seq = seq // tm
    row = lambda i: (i, 0)
    const = lambda i: (0, 0)
    half = pl.BlockSpec((tm, D_HEADS), row)
    pat_specs = [half if d == 1 else
                 pl.BlockSpec((1, d, tm // d, D_HEADS), lambda i: (i // tiles_per_seq, 0, i % tiles_per_seq, 0))
                 for _, d in DILATED_PATTERNS]
    return pl.pallas_call(
        _merge_kernel,
        grid=(n // tm,),
        in_specs=[pl.BlockSpec((tm, D_MODEL), row), pl.BlockSpec((1, D_MODEL), const)] + pat_specs * 2 + [half] + [
            pl.BlockSpec((D_MODEL, 2 * D_MODEL), const),
            pl.BlockSpec((D_HEADS, D_MODEL), const),
            pl.BlockSpec((D_HEADS, D_MODEL), const),
            pl.BlockSpec((D_MODEL, D_MODEL), const)],
        out_specs=pl.BlockSpec((tm, D_MODEL), row),
        out_shape=jax.ShapeDtypeStruct((n, D_MODEL), jnp.float32),
        scratch_shapes=[pltpu.VMEM((D_HEADS // LANES, tm, LANES), jnp.float32)] * (2 * len(RESIDUE_DILATIONS)),
        compiler_params=pltpu.CompilerParams(dimension_semantics=("arbitrary",), vmem_limit_bytes=VMEM_LIMIT),
        name="mix_gate_out",
    )(x2, g, *os_, *lses, yb, wg, wua, wub, wo)


def _ffn_kernel(x_ref, g_ref, wgate_ref, wup_ref, wdown_ref, gf_ref, out_ref):
    x = x_ref[...]
    h = _rms(x, g_ref[...]).astype(MXU_DTYPE)
    y = x
    for c0 in range(0, D_FF, FF_CHUNK):
        a = jnp.dot(h, wgate_ref[:, c0:c0 + FF_CHUNK], preferred_element_type=jnp.float32)
        u = jnp.dot(h, wup_ref[:, c0:c0 + FF_CHUNK], preferred_element_type=jnp.float32)
        act = (a * jax.nn.sigmoid(a) * u).astype(MXU_DTYPE)
        y = y + jnp.dot(act, wdown_ref[c0:c0 + FF_CHUNK, :], preferred_element_type=jnp.float32)
    out_ref[...] = _rms(y, gf_ref[...])


def _ffn(x1, g, wgate, wup, wdown, gf):
    n = x1.shape[0]
    tm = ROW_TILE
    row = lambda i: (i, 0)
    const = lambda i: (0, 0)
    return pl.pallas_call(
        _ffn_kernel,
        grid=(n // tm,),
        in_specs=[pl.BlockSpec((tm, D_MODEL), row), pl.BlockSpec((1, D_MODEL), const),
                  pl.BlockSpec((D_MODEL, D_FF), const), pl.BlockSpec((D_MODEL, D_FF), const),
                  pl.BlockSpec((D_FF, D_MODEL), const), pl.BlockSpec((1, D_MODEL), const)],
        out_specs=pl.BlockSpec((tm, D_MODEL), row),
        out_shape=jax.ShapeDtypeStruct((n, D_MODEL), jnp.float32),
        compiler_params=pltpu.CompilerParams(dimension_semantics=("arbitrary",), vmem_limit_bytes=VMEM_LIMIT),
        name="ffn_norm",
    )(x1, g, wgate, wup, wdown, gf)


def _rope_tables(seq):
    inv_freq = ROPE_THETA ** (-jnp.arange(HALF, dtype=jnp.float32) / HALF)
    ang = jnp.arange(seq, dtype=jnp.int32).astype(jnp.float32)[:, None] * inv_freq[None, :]
    cos, sin = jnp.cos(ang), jnp.sin(ang)
    return jnp.tile(cos, (1, 4)), jnp.concatenate([-sin, -sin, sin, sin], axis=1)


def kernel(x, norm_mix, w_in, w_up_a, w_up_b, w_out, norm_ffn, w_gate, w_up, w_down, norm_final):
    batch, seq, _ = x.shape
    assert seq % max(d * BLOCK for _, d in DILATED_PATTERNS) == 0 and seq % KEY_CHUNK == 0
    assert all(w // d == BLOCK for w, d in DILATED_PATTERNS)
    n = batch * seq
    bf = MXU_DTYPE
    xf = x.reshape(n, D_MODEL)
    cos_t, sin_t = _rope_tables(seq)
    for layer in range(w_in.shape[0]):
        w = w_in[layer]
        w_pack = jnp.pad(w[:, _packed_columns()], ((0, 0), (0, _P_TOTAL - _P_MISC - HEAD_DIM - IDX_HEADS))).astype(bf)
        w_gates = w[:, _GA:].astype(bf)
        qas, kas, vas, qb, qi, kbd, kid, misc = _project(xf, norm_mix[layer][None], w_pack, cos_t, sin_t, seq)

        dil = []
        for (_, d), q, k, v in zip(DILATED_PATTERNS, qas, kas, vas):
            if d == 1:
                o, lse = _dilated(*(z.reshape(batch, 1, seq, D_HEADS) for z in (q, k, v)), d)
                dil.append((o.reshape(n, D_HEADS), lse.reshape(n, D_HEADS)))
            else:
                dil.append(_dilated(q, k, v, d))

        ones_col = (jnp.arange(L_ROWS) == 0).astype(bf)[None, :]
        vb = jnp.concatenate([misc[:, :HEAD_DIM].astype(bf), jnp.broadcast_to(ones_col, (n, L_ROWS))], axis=1)
        vbt = jnp.swapaxes(vb.reshape(batch, seq // KEY_CHUNK, KEY_CHUNK, HEAD_DIM + L_ROWS), 2, 3)
        wt = jnp.swapaxes(misc[:, HEAD_DIM:HEAD_DIM + IDX_HEADS].reshape(batch, seq, IDX_HEADS), 1, 2)
        r3 = lambda z: z.reshape(batch, seq, z.shape[-1])
        yb = _sparse(r3(qi), r3(qb), wt, r3(kid), r3(kbd), vbt, batch, seq).reshape(n, D_HEADS)

        x1 = _merge(xf, norm_mix[layer][None], [o for o, _ in dil], [l for _, l in dil], yb,
                    w_gates, w_up_a[layer].astype(bf), w_up_b[layer].astype(bf), w_out[layer].astype(bf), seq)
        last = layer == w_in.shape[0] - 1
        assert last, "the final norm is fused into the FFN kernel of the last layer"
        xf = _ffn(x1, norm_ffn[layer][None], w_gate[layer].astype(bf), w_up[layer].astype(bf),
                  w_down[layer].astype(bf), norm_final[None])
    return xf.reshape(batch, seq, D_MODEL)
```

```python
import functools

import numpy as np
import jax
import jax.numpy as jnp
from jax import lax
from jax.experimental import pallas as pl
from jax.experimental.pallas import tpu as pltpu

D_MODEL = 1024
HEAD_DIM = 64
HALF = HEAD_DIM // 2
N_HEADS = 8
D_HEADS = N_HEADS * HEAD_DIM
IDX_HEADS = 8
DILATED_PATTERNS = ((128, 1), (512, 4), (2048, 16))
RESIDUE_DILATIONS = tuple(d for _, d in DILATED_PATTERNS if d > 1)
TOPK_MAX = 256
D_FF = 2816
ROPE_THETA = 10000.0
RMS_EPS = 1e-6
BLOCK = 128
ATTN_SCALE = HEAD_DIM ** -0.5
IDX_SCALE = (HEAD_DIM ** -0.5) * (IDX_HEADS ** -0.5)
LOG2_E = float(np.log2(np.e))

LANES = 128
INT_MIN = -(2 ** 31)
MIN_NORMAL_KEY = 1 << 23
KEY_CHUNK = 1024
COUNT_CHUNK = 512
COUNT_FOLD = 8
VALUE_PASSES = 14
DESCENT_STEPS = 6
SEARCH_PASSES_MIN = 12
L_ROWS = 8
ROW_TILE = 512
FFN_ROW_TILE = 1024
DIL_BLOCKS = 4
FF_CHUNK = 1408
VMEM_LIMIT = 56 * 1024 * 1024
MXU_DTYPE = jnp.bfloat16

_SPLITS = (D_HEADS, D_HEADS, D_HEADS, D_HEADS, HEAD_DIM, HEAD_DIM, IDX_HEADS * HEAD_DIM, HEAD_DIM, IDX_HEADS,
           D_MODEL, D_MODEL)
_OFF = np.concatenate([[0], np.cumsum(_SPLITS)])
(_QA, _KA, _VA, _QB, _KB, _VB, _QI, _KI, _WI, _GA, _GB) = (int(o) for o in _OFF[:-1])

_P_QA, _P_KA, _P_VA, _P_QB, _P_QI = 0, 512, 1024, 1536, 2048
_P_KBD, _P_KID, _P_MISC = 2560, 2688, 2816
_P_TOTAL = 2944


def _pair_perm():
    idx = np.empty(D_HEADS, np.int64)
    for j in range(D_HEADS):
        g, l = divmod(j, LANES)
        quarter, e = divmod(l, HALF)
        head = 2 * g + (quarter % 2)
        idx[j] = head * HEAD_DIM + (quarter // 2) * HALF + e
    return idx


def _dup_perm():
    idx = np.empty(LANES, np.int64)
    for l in range(LANES):
        quarter, e = divmod(l, HALF)
        idx[l] = (quarter // 2) * HALF + e
    return idx


def _packed_columns():
    pp, dp = _pair_perm(), _dup_perm()
    return np.concatenate([
        _QA + pp, _KA + pp, _VA + np.arange(D_HEADS), _QB + pp, _QI + pp,
        _KB + dp, _KI + dp, _VB + np.arange(HEAD_DIM), _WI + np.arange(IDX_HEADS)])


def _rms(x, g):
    ms = jnp.mean(x * x, axis=-1, keepdims=True)
    return x * lax.rsqrt(ms + RMS_EPS) * g


def _nt_dot(a, b):
    return lax.dot_general(a, b, (((1,), (1,)), ((), ())), preferred_element_type=jnp.float32)


def _proj_kernel(x_ref, g_ref, w_ref, cos_ref, sin_ref, *refs):
    n_lay = 1 + len(RESIDUE_DILATIONS)
    qa_refs, ka_refs, va_refs = refs[:n_lay], refs[n_lay:2 * n_lay], refs[2 * n_lay:3 * n_lay]
    qb_ref, qi_ref, kbd_ref, kid_ref, misc_ref, h_ref, slab_ref = refs[3 * n_lay:]
    h_ref[...] = _rms(x_ref[...], g_ref[...]).astype(MXU_DTYPE)
    cos = cos_ref[...]
    sin = sin_ref[...]

    def mm(c0, width):
        return jnp.dot(h_ref[...], w_ref[:, c0:c0 + width], preferred_element_type=jnp.float32)

    def rope(z):
        parts = []
        for g in range(z.shape[1] // LANES):
            zg = z[:, g * LANES:(g + 1) * LANES]
            parts.append(zg * cos + pltpu.roll(zg, 2 * HALF, axis=1) * sin)
        return parts[0] if len(parts) == 1 else jnp.concatenate(parts, axis=1)

    def emit(y, out_refs):
        out_refs[0][...] = y.astype(MXU_DTYPE)
        for g in range(D_HEADS // LANES):
            slab_ref[g] = y[:, g * LANES:(g + 1) * LANES]
        for d, ref in zip(RESIDUE_DILATIONS, out_refs[1:]):
            rows = y.shape[0] // d
            for r in range(d):
                for g in range(D_HEADS // LANES):
                    ref[0, r, :, g * LANES:(g + 1) * LANES] = (
                        slab_ref[g, pl.ds(r, rows, stride=d), :].astype(MXU_DTYPE))

    emit(rope(mm(_P_QA, D_HEADS)) * ATTN_SCALE, qa_refs)
    emit(rope(mm(_P_KA, D_HEADS)), ka_refs)
    emit(mm(_P_VA, D_HEADS), va_refs)
    qb_ref[...] = (rope(mm(_P_QB, D_HEADS)) * (ATTN_SCALE * LOG2_E)).astype(MXU_DTYPE)
    qi_ref[...] = rope(mm(_P_QI, D_HEADS)).astype(MXU_DTYPE)
    kbd_ref[...] = rope(mm(_P_KBD, LANES)).astype(MXU_DTYPE)
    kid_ref[...] = rope(mm(_P_KID, LANES)).astype(MXU_DTYPE)
    misc_ref[...] = mm(_P_MISC, LANES)


def _project(x2, g, w_pack, cos_t, sin_t, seq):
    n = x2.shape[0]
    tm = ROW_TILE
    tiles_per_seq = seq // tm
    row = lambda i: (i, 0)
    const = lambda i: (0, 0)
    pos = lambda i: (i % tiles_per_seq, 0)
    batch = n // seq
    wide = jax.ShapeDtypeStruct((n, D_HEADS), MXU_DTYPE)
    narrow = jax.ShapeDtypeStruct((n, LANES), MXU_DTYPE)
    wide_spec = pl.BlockSpec((tm, D_HEADS), row)
    lay_shapes = [wide] + [jax.ShapeDtypeStruct((batch, d, seq // d, D_HEADS), MXU_DTYPE) for d in RESIDUE_DILATIONS]
    lay_specs = [wide_spec] + [
        pl.BlockSpec((1, d, tm // d, D_HEADS), lambda i: (i // tiles_per_seq, 0, i % tiles_per_seq, 0))
        for d in RESIDUE_DILATIONS]
    n_lay = len(lay_shapes)
    outs = pl.pallas_call(
        _proj_kernel,
        grid=(n // tm,),
        in_specs=[
            pl.BlockSpec((tm, D_MODEL), row),
            pl.BlockSpec((1, D_MODEL), const),
            pl.BlockSpec((D_MODEL, _P_TOTAL), const),
            pl.BlockSpec((tm, LANES), pos),
            pl.BlockSpec((tm, LANES), pos),
        ],
        out_specs=lay_specs * 3 + [wide_spec] * 2 + [pl.BlockSpec((tm, LANES), row)] * 3,
        out_shape=lay_shapes * 3 + [wide] * 2 + [narrow, narrow, jax.ShapeDtypeStruct((n, LANES), jnp.float32)],
        scratch_shapes=[pltpu.VMEM((tm, D_MODEL), MXU_DTYPE),
                        pltpu.VMEM((D_HEADS // LANES, tm, LANES), jnp.float32)],
        compiler_params=pltpu.CompilerParams(dimension_semantics=("arbitrary",), vmem_limit_bytes=VMEM_LIMIT),
        name="proj_rope",
    )(x2, g, w_pack, cos_t, sin_t)
    return (outs[:n_lay], outs[n_lay:2 * n_lay], outs[2 * n_lay:3 * n_lay]) + tuple(outs[3 * n_lay:])


def _dil_kernel(q_ref, k_ref, v_ref, kp_ref, vp_ref, o_ref, lse_ref, *, n_blocks):
    n = pl.program_id(2)
    kj = lax.broadcasted_iota(jnp.int32, (2 * BLOCK, LANES), 0)
    qi = lax.broadcasted_iota(jnp.int32, (2 * BLOCK, LANES), 1)
    band = (kj >= qi) & (kj <= qi + BLOCK)
    lane = lax.broadcasted_iota(jnp.int32, (1, LANES), 1)
    pairs = range(N_HEADS // 2)
    col = lambda p: slice(p * LANES, (p + 1) * LANES)

    for j in range(n_blocks):
        rows = slice(j * BLOCK, (j + 1) * BLOCK)
        if j == 0:
            k_prev = lambda p: kp_ref[0, 0, :, col(p)]
            v_prev = lambda p: vp_ref[0, 0, :, col(p)]
            bias = jnp.where(band & ((kj >= BLOCK) | (n > 0)), 0.0, -jnp.inf)
        else:
            prev_rows = slice((j - 1) * BLOCK, j * BLOCK)
            k_prev = lambda p, r=prev_rows: k_ref[0, 0, r, col(p)]
            v_prev = lambda p, r=prev_rows: v_ref[0, 0, r, col(p)]
            bias = jnp.where(band, 0.0, -jnp.inf)

        scores = []
        for p in pairs:
            qp = q_ref[0, 0, rows, col(p)]
            qs = jnp.concatenate([jnp.where(((lane // HALF) % 2) == hh, qp, jnp.zeros_like(qp))
                                  for hh in range(2)], axis=0)
            k2 = jnp.concatenate([k_prev(p), k_ref[0, 0, rows, col(p)]], axis=0)
            scores.append(_nt_dot(k2, qs))

        probs, dens, lses = [], [], []
        for p in pairs:
            for hh in range(2):
                sh = scores[p][:, hh * BLOCK:(hh + 1) * BLOCK] + bias
                m = jnp.max(sh, axis=0, keepdims=True)
                e = jnp.exp(sh - m)
                den = jnp.sum(e, axis=0, keepdims=True)
                probs.append(e.astype(MXU_DTYPE))
                dens.append(den)
                lses.append(m + jnp.log(den))

        for p in pairs:
            v2 = jnp.concatenate([v_prev(p), v_ref[0, 0, rows, col(p)]], axis=0)
            o_t = lax.dot_general(v2, jnp.concatenate(probs[2 * p:2 * p + 2], axis=1), (((0,), (0,)), ((), ())),
                                  preferred_element_type=jnp.float32)
            tile = jnp.concatenate([o_t[:HEAD_DIM, :BLOCK] / dens[2 * p],
                                    o_t[HEAD_DIM:, BLOCK:] / dens[2 * p + 1]], axis=0)
            o_ref[0, 0, rows, col(p)] = tile.T.astype(o_ref.dtype)
            lse_tile = jnp.concatenate([jnp.broadcast_to(l, (HEAD_DIM, BLOCK)) for l in lses[2 * p:2 * p + 2]],
                                       axis=0)
            lse_ref[0, 0, rows, col(p)] = lse_tile.T


def _dilated(q, k, v, dilation):
    batch, _, m_len, _ = q.shape
    n_blocks = min(DIL_BLOCKS, m_len // BLOCK)
    tile_rows = n_blocks * BLOCK
    cur = lambda b, r, n: (b, r, n, 0)
    prev = lambda b, r, n: (b, r, jnp.maximum(n * n_blocks - 1, 0), 0)
    tile = (1, 1, tile_rows, D_HEADS)
    blk = (1, 1, BLOCK, D_HEADS)
    return pl.pallas_call(
        functools.partial(_dil_kernel, n_blocks=n_blocks),
        grid=(batch, dilation, m_len // tile_rows),
        in_specs=[pl.BlockSpec(tile, cur), pl.BlockSpec(tile, cur), pl.BlockSpec(tile, cur),
                  pl.BlockSpec(blk, prev), pl.BlockSpec(blk, prev)],
        out_specs=[pl.BlockSpec(tile, cur), pl.BlockSpec(tile, cur)],
        out_shape=[jax.ShapeDtypeStruct(q.shape, MXU_DTYPE), jax.ShapeDtypeStruct(q.shape, jnp.float32)],
        compiler_params=pltpu.CompilerParams(dimension_semantics=("arbitrary",) * 3),
        name=f"dilated_d{dilation}",
    )(q, k, v, k, v)


def _key_to_f32(key):
    bits = key ^ ((key >> 31) & jnp.int32(0x7FFFFFFF))
    return lax.bitcast_convert_type(bits, jnp.float32)


def _f32_to_key(x):
    bits = lax.bitcast_convert_type(x, jnp.int32)
    return bits ^ ((bits >> 31) & jnp.int32(0x7FFFFFFF))


def _fold_rows(x, op):
    rows = x.shape[0]
    y = op(x.reshape(rows // 64, 64, LANES), axis=0)
    return op(y.reshape(8, 8, LANES), axis=0)


def _sparse_kernel(qi_ref, qb_ref, wt_ref, kid_ref, kbd_ref, vbt_ref, out_ref,
                   sc_ref, eq_ref, eqc_ref, mb_ref, qis_ref, qbs_ref, acc_ref, m_ref, sqa_ref, sqb_ref, mxa_ref,
                   mxb_ref, *, topk):
    i = pl.program_id(1)
    nch = i // (KEY_CHUNK // BLOCK) + 1
    ncc = i // (COUNT_CHUNK // BLOCK) + 1
    lane = lax.broadcasted_iota(jnp.int32, (1, LANES), 1)
    t_idx = i * BLOCK + lane
    row_iota = lax.broadcasted_iota(jnp.int32, (KEY_CHUNK, LANES), 0)
    crow_iota = lax.broadcasted_iota(jnp.int32, (COUNT_CHUNK, LANES), 0)
    neg_inf = jnp.float32(-jnp.inf)
    f32_lowest = jnp.float32(jnp.finfo(jnp.float32).min)

    eye = jnp.where(lax.broadcasted_iota(jnp.int32, (BLOCK, LANES), 0) == lane, 1.0, 0.0).astype(MXU_DTYPE)
    for h in range(N_HEADS):
        cols = slice((h // 2) * LANES, (h // 2 + 1) * LANES)
        head_lanes = ((lane // HALF) % 2) == (h % 2)
        rows = slice(h * BLOCK, (h + 1) * BLOCK)
        qis_ref[rows, :] = jnp.where(head_lanes, qi_ref[0, :, cols], jnp.zeros((), MXU_DTYPE))
        qbs_ref[rows, :LANES] = jnp.where(head_lanes, qb_ref[0, :, cols], jnp.zeros((), MXU_DTYPE))
        qbs_ref[rows, LANES:] = eye

    def chunk_start(c):
        return pl.multiple_of(c * KEY_CHUNK, KEY_CHUNK)

    def pair_q(ref, p):
        return ref[p * 2 * BLOCK:(p + 1) * 2 * BLOCK, :]

    buf_a, buf_b = (sqa_ref, mxa_ref), (sqb_ref, mxb_ref)

    def pipeline(produce, consume):
        produce(0, buf_a)

        def two_chunks(j, carry):
            c = 2 * j
            produce(c + 1, buf_b)
            consume(c, buf_a)
            produce(c + 2, buf_a)
            consume(c + 1, buf_b)
            return carry

        lax.fori_loop(0, (nch - 1) // 2, two_chunks, 0)
        last = nch - 1

        @pl.when(last % 2 == 1)
        def _():
            produce(last, buf_b)
            consume(last - 1, buf_a)
            consume(last, buf_b)

        @pl.when(last % 2 == 0)
        def _():
            consume(last, buf_a)

    def score_chunk(c, carry):
        r0 = chunk_start(c)
        kc = kid_ref[0, pl.ds(r0, KEY_CHUNK), :]
        acc = jnp.zeros((KEY_CHUNK, LANES), jnp.float32)
        for p in range(N_HEADS // 2):
            d = _nt_dot(kc, pair_q(qis_ref, p))
            for hh in range(2):
                h = 2 * p + hh
                acc = acc + jnp.maximum(d[:, hh * BLOCK:(hh + 1) * BLOCK], 0.0) * wt_ref[0, h:h + 1, :]
        admissible = r0 + row_iota <= t_idx
        scores = acc * IDX_SCALE
        masked = jnp.where(admissible, scores, neg_inf)
        sc_ref[pl.ds(r0, KEY_CHUNK), :] = masked
        top8, bot8 = carry
        return (jnp.maximum(top8, _fold_rows(masked, jnp.max)),
                jnp.minimum(bot8, _fold_rows(jnp.where(admissible, scores, -neg_inf), jnp.min)))

    top8, bot8 = lax.fori_loop(0, nch, score_chunk, (jnp.full((8, LANES), neg_inf, jnp.float32),
                                                     jnp.full((8, LANES), -neg_inf, jnp.float32)))
    top = jnp.max(top8, axis=0, keepdims=True)
    bot = jnp.min(bot8, axis=0, keepdims=True)

    def fold_scores(value, op, pairwise, start, src_ref=sc_ref):
        def body(c, acc):
            r0 = pl.multiple_of(c * COUNT_CHUNK, COUNT_CHUNK)
            val = value(src_ref[pl.ds(r0, COUNT_CHUNK), :], r0)
            return pairwise(acc, op(val.reshape(COUNT_FOLD, COUNT_CHUNK // COUNT_FOLD, LANES), axis=0))

        acc = lax.fori_loop(0, ncc // 2, lambda j, a: body(2 * j + 1, body(2 * j, a)),
                            jnp.full((COUNT_CHUNK // COUNT_FOLD, LANES), start, jnp.float32))
        acc = lax.cond(ncc % 2 == 1, lambda a: body(ncc - 1, a), lambda a: a, acc)
        return op(acc, axis=0, keepdims=True)

    def count(pred):
        return fold_scores(lambda x, r0: jnp.where(pred(x, r0), 1.0, 0.0), jnp.sum, jnp.add, 0.0)

    key_top = _f32_to_key(top)
    lo = _f32_to_key(bot) - jnp.int32(1 << 23)
    hi = key_top + 1
    hi = jnp.where((hi >= -MIN_NORMAL_KEY) & (hi < MIN_NORMAL_KEY), jnp.int32(MIN_NORMAL_KEY), hi)
    unknown = jnp.float32(2 * sc_ref.shape[0])

    def is_settled(carry):
        lo, hi, cnt_lo = carry[:3]
        return (cnt_lo == topk) | (hi - lo == 1) | (t_idx < topk)

    def probe(mid, carry):
        lo, hi, cnt_lo, thr_lo, thr_hi = carry
        thr_c = _key_to_f32(mid)
        cnt = count(lambda x, r0: x >= thr_c)
        take = (cnt >= topk) & ~is_settled(carry)
        drop = (cnt < topk) & ~is_settled(carry)
        return (jnp.where(take, mid, lo), jnp.where(drop, mid, hi), jnp.where(take, cnt, cnt_lo),
                jnp.where(take, thr_c, thr_lo), jnp.where(drop, thr_c, thr_hi))

    def halve(by_value, carry):
        lo, hi = carry[:2]
        mid = lo + lax.shift_right_logical(hi - lo, 1)
        if by_value:
            mid_val = _f32_to_key(0.5 * _key_to_f32(lo) + 0.5 * _key_to_f32(hi))
            mid = jnp.where((mid_val > lo) & (mid_val < hi), mid_val, mid)
        return probe(mid, carry)

    def descend(carry):
        lo, hi, cnt_lo, thr_lo, thr_hi = carry
        v = fold_scores(lambda x, r0: jnp.where(x < thr_hi, x, neg_inf), jnp.max, jnp.maximum, neg_inf)
        cnt = count(lambda x, r0: x >= v)
        key_v = _f32_to_key(v)
        take = (cnt >= topk) & ~is_settled(carry)
        drop = (cnt < topk) & ~is_settled(carry)
        return (jnp.where(take, key_v, lo), jnp.where(take, key_v + 1, jnp.where(drop, key_v, hi)),
                jnp.where(take, cnt, cnt_lo), jnp.where(take, v, thr_lo), jnp.where(drop, v, thr_hi))

    def pending(carry):
        return jnp.max(jnp.where(is_settled(carry), 0.0, 1.0))

    state = lax.fori_loop(0, SEARCH_PASSES_MIN, lambda b, c: halve(True, c),
                          (lo, hi, jnp.full((1, LANES), unknown), _key_to_f32(lo), _key_to_f32(hi)))

    def two_passes(carry):
        b, state, _ = carry
        state = lax.cond(b < VALUE_PASSES, lambda s: halve(True, halve(True, s)),
                         lambda s: lax.cond(b < VALUE_PASSES + 2 * DESCENT_STEPS, descend,
                                            lambda t: halve(False, halve(False, t)), s), state)
        return b + 2, state, pending(state)

    _, (lo, hi, cnt_ge, thr_raw, _), _ = lax.while_loop(
        lambda c: (c[0] < VALUE_PASSES + 2 * DESCENT_STEPS + 32) & (c[2] > 0.0), two_passes,
        (jnp.int32(SEARCH_PASSES_MIN), state, pending(state)))
    enough = thr_raw > f32_lowest
    thr = jnp.where(enough, thr_raw, f32_lowest)

    tie_lane = enough & (cnt_ge > topk) & (t_idx >= topk)

    @pl.when(jnp.max(jnp.where(tie_lane, 1.0, 0.0)) > 0.0)
    def _():
        need = topk - count(lambda x, r0: x > thr)

        def mark_equal(c, carry):
            seen, chunk_of, seen_before = carry
            r0 = pl.multiple_of(c * COUNT_CHUNK, COUNT_CHUNK)
            eq = jnp.where(sc_ref[pl.ds(r0, COUNT_CHUNK), :] == thr, 1.0, 0.0)
            eq_ref[pl.ds(r0, COUNT_CHUNK), :] = eq
            here = jnp.sum(_fold_rows(eq, jnp.sum), axis=0, keepdims=True)
            first_reach = (seen < need) & (seen + here >= need)
            return (seen + here, jnp.where(first_reach, c, chunk_of), jnp.where(first_reach, seen, seen_before))

        zero = jnp.zeros((1, LANES), jnp.float32)
        _, chunk_of, seen_before = lax.fori_loop(0, ncc, mark_equal, (zero, jnp.zeros((1, LANES), jnp.int32), zero))
        need_here = need - seen_before

        eqc_ref[...] = jnp.zeros(eqc_ref.shape, jnp.float32)

        def gather_chunk(c, carry):
            r0 = pl.multiple_of(c * COUNT_CHUNK, COUNT_CHUNK)
            eqc_ref[...] += jnp.where(chunk_of == c, eq_ref[pl.ds(r0, COUNT_CHUNK), :], 0.0)
            return carry

        lax.fori_loop(0, ncc, gather_chunk, 0)

        def idx_step(b, lim):
            cand = lim | (jnp.int32(1) << (COUNT_CHUNK.bit_length() - 2 - b))
            below = jnp.sum(_fold_rows(jnp.where(crow_iota < cand, eqc_ref[...], 0.0), jnp.sum), axis=0, keepdims=True)
            return jnp.where(below < need_here, cand, lim)

        lim = chunk_of * COUNT_CHUNK + lax.fori_loop(0, COUNT_CHUNK.bit_length() - 1, idx_step,
                                                     jnp.zeros((1, LANES), jnp.int32))

        def demote(c, carry):
            r0 = pl.multiple_of(c * COUNT_CHUNK, COUNT_CHUNK)
            x = sc_ref[pl.ds(r0, COUNT_CHUNK), :]
            drop = (eq_ref[pl.ds(r0, COUNT_CHUNK), :] > 0.0) & (r0 + crow_iota > lim) & tie_lane
            sc_ref[pl.ds(r0, COUNT_CHUNK), :] = jnp.where(drop, neg_inf, x)
            return carry

        lax.fori_loop(0, ncc, demote, 0)

    mask_off = float(jnp.finfo(MXU_DTYPE).min)

    def build_mask(c, carry):
        r0 = chunk_start(c)
        selected = sc_ref[pl.ds(r0, KEY_CHUNK), :] >= thr
        mb_ref[pl.ds(r0, KEY_CHUNK), :] = jnp.where(selected, 0.0, mask_off).astype(MXU_DTYPE)
        return carry

    lax.fori_loop(0, nch, build_mask, 0)

    def attn_scores(c, buf):
        sq_ref, mx_ref = buf
        r0 = chunk_start(c)
        kaug = jnp.concatenate([kbd_ref[0, pl.ds(r0, KEY_CHUNK), :], mb_ref[pl.ds(r0, KEY_CHUNK), :]], axis=1)
        for p in range(N_HEADS // 2):
            s = _nt_dot(kaug, pair_q(qbs_ref, p))
            sq_ref[:, p * 2 * BLOCK:(p + 1) * 2 * BLOCK] = s
            for hh in range(2):
                cols = slice((2 * p + hh) * BLOCK, (2 * p + hh + 1) * BLOCK)
                mx_ref[:, cols] = _fold_rows(s[:, hh * BLOCK:(hh + 1) * BLOCK], jnp.max)

    def accumulate(c, buf):
        sq_ref, mx_ref = buf
        m_run = m_ref[...]
        m_new = jnp.maximum(m_run, jnp.max(mx_ref[...], axis=0, keepdims=True))
        m_ref[...] = m_new
        acc_ref[...] = acc_ref[...] * jnp.exp2(m_run - m_new)
        vt = vbt_ref[0, c]
        for p in range(N_HEADS // 2):
            pcols = slice(p * 2 * BLOCK, (p + 1) * 2 * BLOCK)
            e = jnp.exp2(sq_ref[:, pcols] - m_new[:, pcols])
            acc_ref[:, pcols] += jnp.dot(vt, e.astype(MXU_DTYPE), preferred_element_type=jnp.float32)

    acc_ref[...] = jnp.zeros(acc_ref.shape, jnp.float32)
    m_ref[...] = jnp.full(m_ref.shape, f32_lowest, jnp.float32)
    pipeline(attn_scores, accumulate)

    o_t = acc_ref[:HEAD_DIM, :] / acc_ref[HEAD_DIM:HEAD_DIM + 1, :]
    for p in range(N_HEADS // 2):
        pair = jnp.concatenate([o_t[:, (2 * p) * BLOCK:(2 * p + 1) * BLOCK],
                                o_t[:, (2 * p + 1) * BLOCK:(2 * p + 2) * BLOCK]], axis=0)
        out_ref[0, :, p * LANES:(p + 1) * LANES] = pair.T.astype(out_ref.dtype)


def _sparse(qi, qb, wt, kid, kbd, vbt, batch, seq):
    topk = min(TOPK_MAX, seq // 4)
    nq = seq // BLOCK
    qblk = pl.BlockSpec((1, BLOCK, D_HEADS), lambda b, i: (b, i, 0))
    full = pl.BlockSpec((1, seq, LANES), lambda b, i: (b, 0, 0))
    kern = functools.partial(_sparse_kernel, topk=topk)
    return pl.pallas_call(
        kern,
        grid=(batch, nq),
        in_specs=[qblk, qblk,
                  pl.BlockSpec((1, IDX_HEADS, BLOCK), lambda b, i: (b, 0, i)),
                  full, full,
                  pl.BlockSpec((1, seq // KEY_CHUNK, HEAD_DIM + L_ROWS, KEY_CHUNK), lambda b, i: (b, 0, 0, 0))],
        out_specs=qblk,
        out_shape=jax.ShapeDtypeStruct((batch, seq, D_HEADS), MXU_DTYPE),
        scratch_shapes=[pltpu.VMEM((seq, LANES), jnp.float32),
                        pltpu.VMEM((seq, LANES), jnp.float32),
                        pltpu.VMEM((COUNT_CHUNK, LANES), jnp.float32),
                        pltpu.VMEM((seq, LANES), MXU_DTYPE),
                        pltpu.VMEM((N_HEADS * BLOCK, LANES), MXU_DTYPE),
                        pltpu.VMEM((N_HEADS * BLOCK, 2 * LANES), MXU_DTYPE),
                        pltpu.VMEM((HEAD_DIM + L_ROWS, N_HEADS * BLOCK), jnp.float32),
                        pltpu.VMEM((1, N_HEADS * BLOCK), jnp.float32),
                        pltpu.VMEM((KEY_CHUNK, N_HEADS * BLOCK), jnp.float32),
                        pltpu.VMEM((KEY_CHUNK, N_HEADS * BLOCK), jnp.float32),
                        pltpu.VMEM((8, N_HEADS * BLOCK), jnp.float32),
                        pltpu.VMEM((8, N_HEADS * BLOCK), jnp.float32)],
        compiler_params=pltpu.CompilerParams(dimension_semantics=("arbitrary", "arbitrary"),
                                             vmem_limit_bytes=VMEM_LIMIT),
        name="indexer_sparse_attn",
    )(qi, qb, wt, kid, kbd, vbt)


def _merge_kernel(x_ref, g_ref, *refs):
    n_pat = len(DILATED_PATTERNS)
    o_refs, l_refs = refs[:n_pat], refs[n_pat:2 * n_pat]
    yb_ref, wg_ref, wua_ref, wub_ref, wo_ref, x1_ref = refs[2 * n_pat:2 * n_pat + 6]
    slabs = refs[2 * n_pat + 6:]
    x = x_ref[...]
    h = _rms(x, g_ref[...]).astype(MXU_DTYPE)
    tm = x.shape[0]
    n_grp = D_HEADS // LANES

    o_src, l_src, k = [], [], 0
    for (_, d), o_ref, l_ref in zip(DILATED_PATTERNS, o_refs, l_refs):
        if d == 1:
            o_src.append(lambda g, r=o_ref: r[:, g * LANES:(g + 1) * LANES].astype(jnp.float32))
            l_src.append(lambda g, r=l_ref: r[:, g * LANES:(g + 1) * LANES])
            continue
        o_slab, l_slab = slabs[2 * k], slabs[2 * k + 1]
        k += 1
        for r in range(d):
            for g in range(n_grp):
                cols = slice(g * LANES, (g + 1) * LANES)
                o_slab[g, pl.ds(r, tm // d, stride=d), :] = o_ref[0, r, :, cols].astype(jnp.float32)
                l_slab[g, pl.ds(r, tm // d, stride=d), :] = l_ref[0, r, :, cols]
        o_src.append(lambda g, s=o_slab: s[g])
        l_src.append(lambda g, s=l_slab: s[g])

    parts = []
    for g in range(n_grp):
        ls = [f(g) for f in l_src]
        m = functools.reduce(jnp.maximum, ls)
        es = [jnp.exp(l - m) for l in ls]
        num = functools.reduce(jnp.add, [e * f(g) for e, f in zip(es, o_src)])
        parts.append((num / functools.reduce(jnp.add, es)).astype(MXU_DTYPE))
    ya = jnp.concatenate(parts, axis=1)
    ua = jnp.dot(ya, wua_ref[...], preferred_element_type=jnp.float32)
    ub = jnp.dot(yb_ref[...], wub_ref[...], preferred_element_type=jnp.float32)
    ga = jnp.dot(h, wg_ref[:, :D_MODEL], preferred_element_type=jnp.float32)
    gb = jnp.dot(h, wg_ref[:, D_MODEL:], preferred_element_type=jnp.float32)
    merged = jax.nn.sigmoid(ga) * ua + jax.nn.sigmoid(gb) * ub
    x1_ref[...] = x + jnp.dot(merged.astype(MXU_DTYPE), wo_ref[...], preferred_element_type=jnp.float32)


def _merge(x2, g, os_, lses, yb, wg, wua, wub, wo, seq):
    n = x2.shape[0]
    tm = ROW_TILE
    tiles_per_seq = seq // tm
    row = lambda i: (i, 0)
    const = lambda i: (0, 0)
    half = pl.BlockSpec((tm, D_HEADS), row)
    pat_specs = [half if d == 1 else
                 pl.BlockSpec((1, d, tm // d, D_HEADS), lambda i: (i // tiles_per_seq, 0, i % tiles_per_seq, 0))
                 for _, d in DILATED_PATTERNS]
    return pl.pallas_call(
        _merge_kernel,
        grid=(n // tm,),
        in_specs=[pl.BlockSpec((tm, D_MODEL), row), pl.BlockSpec((1, D_MODEL), const)] + pat_specs * 2 + [half] + [
            pl.BlockSpec((D_MODEL, 2 * D_MODEL), const),
            pl.BlockSpec((D_HEADS, D_MODEL), const),
            pl.BlockSpec((D_HEADS, D_MODEL), const),
            pl.BlockSpec((D_MODEL, D_MODEL), const)],
        out_specs=pl.BlockSpec((tm, D_MODEL), row),
        out_shape=jax.ShapeDtypeStruct((n, D_MODEL), jnp.float32),
        scratch_shapes=[pltpu.VMEM((D_HEADS // LANES, tm, LANES), jnp.float32)] * (2 * len(RESIDUE_DILATIONS)),
        compiler_params=pltpu.CompilerParams(dimension_semantics=("arbitrary",), vmem_limit_bytes=VMEM_LIMIT),
        name="mix_gate_out",
    )(x2, g, *os_, *lses, yb, wg, wua, wub, wo)


def _ffn_kernel(x_ref, g_ref, wgate_ref, wup_ref, wdown_ref, gf_ref, out_ref):
    x = x_ref[...]
    h = _rms(x, g_ref[...]).astype(MXU_DTYPE)
    y = x
    for c0 in range(0, D_FF, FF_CHUNK):
        a = jnp.dot(h, wgate_ref[:, c0:c0 + FF_CHUNK], preferred_element_type=jnp.float32)
        u = jnp.dot(h, wup_ref[:, c0:c0 + FF_CHUNK], preferred_element_type=jnp.float32)
        act = (a * jax.nn.sigmoid(a) * u).astype(MXU_DTYPE)
        y = y + jnp.dot(act, wdown_ref[c0:c0 + FF_CHUNK, :], preferred_element_type=jnp.float32)
    out_ref[...] = _rms(y, gf_ref[...])


def _ffn(x1, g, wgate, wup, wdown, gf):
    n = x1.shape[0]
    tm = FFN_ROW_TILE
    row = lambda i: (i, 0)
    const = lambda i: (0, 0)
    once = pl.Buffered(1)
    return pl.pallas_call(
        _ffn_kernel,
        grid=(n // tm,),
        in_specs=[pl.BlockSpec((tm, D_MODEL), row), pl.BlockSpec((1, D_MODEL), const),
                  pl.BlockSpec((D_MODEL, D_FF), const, pipeline_mode=once),
                  pl.BlockSpec((D_MODEL, D_FF), const, pipeline_mode=once),
                  pl.BlockSpec((D_FF, D_MODEL), const, pipeline_mode=once), pl.BlockSpec((1, D_MODEL), const)],
        out_specs=pl.BlockSpec((tm, D_MODEL), row),
        out_shape=jax.ShapeDtypeStruct((n, D_MODEL), jnp.float32),
        compiler_params=pltpu.CompilerParams(dimension_semantics=("arbitrary",), vmem_limit_bytes=VMEM_LIMIT),
        name="ffn_norm",
    )(x1, g, wgate, wup, wdown, gf)


def _rope_tables(seq):
    inv_freq = ROPE_THETA ** (-jnp.arange(HALF, dtype=jnp.float32) / HALF)
    ang = jnp.arange(seq, dtype=jnp.int32).astype(jnp.float32)[:, None] * inv_freq[None, :]
    cos, sin = jnp.cos(ang), jnp.sin(ang)
    return jnp.tile(cos, (1, 4)), jnp.concatenate([-sin, -sin, sin, sin], axis=1)


def kernel(x, norm_mix, w_in, w_up_a, w_up_b, w_out, norm_ffn, w_gate, w_up, w_down, norm_final):
    batch, seq, _ = x.shape
    assert seq % max(d * BLOCK for _, d in DILATED_PATTERNS) == 0 and seq % KEY_CHUNK == 0
    assert all(w // d == BLOCK for w, d in DILATED_PATTERNS)
    n = batch * seq
    bf = MXU_DTYPE
    xf = x.reshape(n, D_MODEL)
    cos_t, sin_t = _rope_tables(seq)
    for layer in range(w_in.shape[0]):
        w = w_in[layer]
        w_pack = jnp.pad(w[:, _packed_columns()], ((0, 0), (0, _P_TOTAL - _P_MISC - HEAD_DIM - IDX_HEADS))).astype(bf)
        w_gates = w[:, _GA:].astype(bf)
        qas, kas, vas, qb, qi, kbd, kid, misc = _project(xf, norm_mix[layer][None], w_pack, cos_t, sin_t, seq)

        dil = []
        for (_, d), q, k, v in zip(DILATED_PATTERNS, qas, kas, vas):
            if d == 1:
                o, lse = _dilated(*(z.reshape(batch, 1, seq, D_HEADS) for z in (q, k, v)), d)
                dil.append((o.reshape(n, D_HEADS), lse.reshape(n, D_HEADS)))
            else:
                dil.append(_dilated(q, k, v, d))

        ones_col = (jnp.arange(L_ROWS) == 0).astype(bf)[None, :]
        vb = jnp.concatenate([misc[:, :HEAD_DIM].astype(bf), jnp.broadcast_to(ones_col, (n, L_ROWS))], axis=1)
        vbt = jnp.swapaxes(vb.reshape(batch, seq // KEY_CHUNK, KEY_CHUNK, HEAD_DIM + L_ROWS), 2, 3)
        wt = jnp.swapaxes(misc[:, HEAD_DIM:HEAD_DIM + IDX_HEADS].reshape(batch, seq, IDX_HEADS), 1, 2)
        r3 = lambda z: z.reshape(batch, seq, z.shape[-1])
        yb = _sparse(r3(qi), r3(qb), wt, r3(kid), r3(kbd), vbt, batch, seq).reshape(n, D_HEADS)

        x1 = _merge(xf, norm_mix[layer][None], [o for o, _ in dil], [l for _, l in dil], yb,
                    w_gates, w_up_a[layer].astype(bf), w_up_b[layer].astype(bf), w_out[layer].astype(bf), seq)
        last = layer == w_in.shape[0] - 1
        assert last, "the final norm is fused into the FFN kernel of the last layer"
        xf = _ffn(x1, norm_ffn[layer][None], w_gate[layer].astype(bf), w_up[layer].astype(bf),
                  w_down[layer].astype(bf), norm_final[None])
    return xf.reshape(batch, seq, D_MODEL)
```

```python
import functools

import numpy as np
import jax
import jax.numpy as jnp
from jax import lax
from jax.experimental import pallas as pl
from jax.experimental.pallas import tpu as pltpu

D_MODEL = 1024
HEAD_DIM = 64
HALF = HEAD_DIM // 2
N_HEADS = 8
D_HEADS = N_HEADS * HEAD_DIM
IDX_HEADS = 8
DILATED_PATTERNS = ((128, 1), (512, 4), (2048, 16))
RESIDUE_DILATIONS = tuple(d for _, d in DILATED_PATTERNS if d > 1)
TOPK_MAX = 256
D_FF = 2816
ROPE_THETA = 10000.0
RMS_EPS = 1e-6
BLOCK = 128
ATTN_SCALE = HEAD_DIM ** -0.5
IDX_SCALE = (HEAD_DIM ** -0.5) * (IDX_HEADS ** -0.5)
LOG2_E = float(np.log2(np.e))

LANES = 128
MIN_NORMAL_KEY = 1 << 23
KEY_CHUNK = 1024
COUNT_CHUNK = 512
COUNT_FOLD = 8
VALUE_PASSES = 14
DESCENT_STEPS = 6
SEARCH_PASSES_MIN = 12
L_ROWS = 8
ROW_TILE = 512
DIL_BLOCKS = 8
FF_CHUNK = 1408
VMEM_LIMIT = 56 * 1024 * 1024
MXU_DTYPE = jnp.bfloat16

_SPLITS = (D_HEADS, D_HEADS, D_HEADS, D_HEADS, HEAD_DIM, HEAD_DIM, IDX_HEADS * HEAD_DIM, HEAD_DIM, IDX_HEADS,
           D_MODEL, D_MODEL)
_OFF = np.concatenate([[0], np.cumsum(_SPLITS)])
(_QA, _KA, _VA, _QB, _KB, _VB, _QI, _KI, _WI, _GA, _GB) = (int(o) for o in _OFF[:-1])

_P_QA, _P_KA, _P_VA, _P_QB, _P_QI = 0, 512, 1024, 1536, 2048
_P_KBD, _P_KID, _P_MISC = 2560, 2688, 2816
_P_TOTAL = 2944


def _pair_perm():
    idx = np.empty(D_HEADS, np.int64)
    for j in range(D_HEADS):
        g, l = divmod(j, LANES)
        quarter, e = divmod(l, HALF)
        head = 2 * g + (quarter % 2)
        idx[j] = head * HEAD_DIM + (quarter // 2) * HALF + e
    return idx


def _dup_perm():
    idx = np.empty(LANES, np.int64)
    for l in range(LANES):
        quarter, e = divmod(l, HALF)
        idx[l] = (quarter // 2) * HALF + e
    return idx


def _packed_columns():
    pp, dp = _pair_perm(), _dup_perm()
    return np.concatenate([
        _QA + pp, _KA + pp, _VA + np.arange(D_HEADS), _QB + pp, _QI + pp,
        _KB + dp, _KI + dp, _VB + np.arange(HEAD_DIM), _WI + np.arange(IDX_HEADS)])


def _rms(x, g):
    ms = jnp.mean(x * x, axis=-1, keepdims=True)
    return x * lax.rsqrt(ms + RMS_EPS) * g


def _nt_dot(a, b):
    return lax.dot_general(a, b, (((1,), (1,)), ((), ())), preferred_element_type=jnp.float32)


def _proj_kernel(x_ref, g_ref, w_ref, cos_ref, sin_ref, *refs):
    n_lay = 1 + len(RESIDUE_DILATIONS)
    qa_refs, ka_refs, va_refs = refs[:n_lay], refs[n_lay:2 * n_lay], refs[2 * n_lay:3 * n_lay]
    qb_ref, qi_ref, kbd_ref, kid_ref, misc_ref, h_ref, slab_ref = refs[3 * n_lay:]
    h_ref[...] = _rms(x_ref[...], g_ref[...]).astype(MXU_DTYPE)
    cos = cos_ref[...]
    sin = sin_ref[...]

    def mm(c0, width):
        return jnp.dot(h_ref[...], w_ref[:, c0:c0 + width], preferred_element_type=jnp.float32)

    def rope(z):
        parts = []
        for g in range(z.shape[1] // LANES):
            zg = z[:, g * LANES:(g + 1) * LANES]
            parts.append(zg * cos + pltpu.roll(zg, 2 * HALF, axis=1) * sin)
        return parts[0] if len(parts) == 1 else jnp.concatenate(parts, axis=1)

    def emit(y, out_refs):
        out_refs[0][...] = y.astype(MXU_DTYPE)
        for g in range(D_HEADS // LANES):
            slab_ref[g] = y[:, g * LANES:(g + 1) * LANES]
        for d, ref in zip(RESIDUE_DILATIONS, out_refs[1:]):
            rows = y.shape[0] // d
            for r in range(d):
                for g in range(D_HEADS // LANES):
                    ref[0, r, :, g * LANES:(g + 1) * LANES] = (
                        slab_ref[g, pl.ds(r, rows, stride=d), :].astype(MXU_DTYPE))

    emit(rope(mm(_P_QA, D_HEADS)) * ATTN_SCALE, qa_refs)
    emit(rope(mm(_P_KA, D_HEADS)), ka_refs)
    emit(mm(_P_VA, D_HEADS), va_refs)
    qb_ref[...] = (rope(mm(_P_QB, D_HEADS)) * (ATTN_SCALE * LOG2_E)).astype(MXU_DTYPE)
    qi_ref[...] = rope(mm(_P_QI, D_HEADS)).astype(MXU_DTYPE)
    kbd_ref[...] = rope(mm(_P_KBD, LANES)).astype(MXU_DTYPE)
    kid_ref[...] = rope(mm(_P_KID, LANES)).astype(MXU_DTYPE)
    misc_ref[...] = mm(_P_MISC, LANES)


def _project(x2, g, w_pack, cos_t, sin_t, seq):
    n = x2.shape[0]
    tm = ROW_TILE
    tiles_per_seq = seq // tm
    row = lambda i: (i, 0)
    const = lambda i: (0, 0)
    pos = lambda i: (i % tiles_per_seq, 0)
    batch = n // seq
    wide = jax.ShapeDtypeStruct((n, D_HEADS), MXU_DTYPE)
    narrow = jax.ShapeDtypeStruct((n, LANES), MXU_DTYPE)
    wide_spec = pl.BlockSpec((tm, D_HEADS), row)
    lay_shapes = [wide] + [jax.ShapeDtypeStruct((batch, d, seq // d, D_HEADS), MXU_DTYPE) for d in RESIDUE_DILATIONS]
    lay_specs = [wide_spec] + [
        pl.BlockSpec((1, d, tm // d, D_HEADS), lambda i: (i // tiles_per_seq, 0, i % tiles_per_seq, 0))
        for d in RESIDUE_DILATIONS]
    n_lay = len(lay_shapes)
    outs = pl.pallas_call(
        _proj_kernel,
        grid=(n // tm,),
        in_specs=[
            pl.BlockSpec((tm, D_MODEL), row),
            pl.BlockSpec((1, D_MODEL), const),
            pl.BlockSpec((D_MODEL, _P_TOTAL), const),
            pl.BlockSpec((tm, LANES), pos),
            pl.BlockSpec((tm, LANES), pos),
        ],
        out_specs=lay_specs * 3 + [wide_spec] * 2 + [pl.BlockSpec((tm, LANES), row)] * 3,
        out_shape=lay_shapes * 3 + [wide] * 2 + [narrow, narrow, jax.ShapeDtypeStruct((n, LANES), jnp.float32)],
        scratch_shapes=[pltpu.VMEM((tm, D_MODEL), MXU_DTYPE),
                        pltpu.VMEM((D_HEADS // LANES, tm, LANES), jnp.float32)],
        compiler_params=pltpu.CompilerParams(dimension_semantics=("arbitrary",), vmem_limit_bytes=VMEM_LIMIT),
        name="proj_rope",
    )(x2, g, w_pack, cos_t, sin_t)
    return (outs[:n_lay], outs[n_lay:2 * n_lay], outs[2 * n_lay:3 * n_lay]) + tuple(outs[3 * n_lay:])


def _dil_kernel(q_ref, k_ref, v_ref, kp_ref, vp_ref, o_ref, lse_ref, *, n_blocks):
    n = pl.program_id(2)
    kj = lax.broadcasted_iota(jnp.int32, (2 * BLOCK, LANES), 0)
    qi = lax.broadcasted_iota(jnp.int32, (2 * BLOCK, LANES), 1)
    band = (kj >= qi) & (kj <= qi + BLOCK)
    lane = lax.broadcasted_iota(jnp.int32, (1, LANES), 1)
    pairs = range(N_HEADS // 2)
    col = lambda p: slice(p * LANES, (p + 1) * LANES)

    for j in range(n_blocks):
        rows = slice(j * BLOCK, (j + 1) * BLOCK)
        if j == 0:
            k_prev = lambda p: kp_ref[0, 0, :, col(p)]
            v_prev = lambda p: vp_ref[0, 0, :, col(p)]
            bias = jnp.where(band & ((kj >= BLOCK) | (n > 0)), 0.0, -jnp.inf)
        else:
            prev_rows = slice((j - 1) * BLOCK, j * BLOCK)
            k_prev = lambda p, r=prev_rows: k_ref[0, 0, r, col(p)]
            v_prev = lambda p, r=prev_rows: v_ref[0, 0, r, col(p)]
            bias = jnp.where(band, 0.0, -jnp.inf)

        scores = []
        for p in pairs:
            qp = q_ref[0, 0, rows, col(p)]
            qs = jnp.concatenate([jnp.where(((lane // HALF) % 2) == hh, qp, jnp.zeros_like(qp))
                                  for hh in range(2)], axis=0)
            k2 = jnp.concatenate([k_prev(p), k_ref[0, 0, rows, col(p)]], axis=0)
            scores.append(_nt_dot(k2, qs))

        probs, dens, lses = [], [], []
        for p in pairs:
            for hh in range(2):
                sh = scores[p][:, hh * BLOCK:(hh + 1) * BLOCK] + bias
                m = jnp.max(sh, axis=0, keepdims=True)
                e = jnp.exp(sh - m)
                den = jnp.sum(e, axis=0, keepdims=True)
                probs.append(e.astype(MXU_DTYPE))
                dens.append(den)
                lses.append(m + jnp.log(den))

        for p in pairs:
            v2 = jnp.concatenate([v_prev(p), v_ref[0, 0, rows, col(p)]], axis=0)
            o_t = lax.dot_general(v2, jnp.concatenate(probs[2 * p:2 * p + 2], axis=1), (((0,), (0,)), ((), ())),
                                  preferred_element_type=jnp.float32)
            tile = jnp.concatenate([o_t[:HEAD_DIM, :BLOCK] / dens[2 * p],
                                    o_t[HEAD_DIM:, BLOCK:] / dens[2 * p + 1]], axis=0)
            o_ref[0, 0, rows, col(p)] = tile.T.astype(o_ref.dtype)
            lse_tile = jnp.concatenate([jnp.broadcast_to(l, (HEAD_DIM, BLOCK)) for l in lses[2 * p:2 * p + 2]],
                                       axis=0)
            lse_ref[0, 0, rows, col(p)] = lse_tile.T


def _dilated(q, k, v, dilation):
    batch, _, m_len, _ = q.shape
    n_blocks = min(DIL_BLOCKS, m_len // BLOCK)
    tile_rows = n_blocks * BLOCK
    cur = lambda b, r, n: (b, r, n, 0)
    prev = lambda b, r, n: (b, r, jnp.maximum(n * n_blocks - 1, 0), 0)
    tile = (1, 1, tile_rows, D_HEADS)
    blk = (1, 1, BLOCK, D_HEADS)
    return pl.pallas_call(
        functools.partial(_dil_kernel, n_blocks=n_blocks),
        grid=(batch, dilation, m_len // tile_rows),
        in_specs=[pl.BlockSpec(tile, cur), pl.BlockSpec(tile, cur), pl.BlockSpec(tile, cur),
                  pl.BlockSpec(blk, prev), pl.BlockSpec(blk, prev)],
        out_specs=[pl.BlockSpec(tile, cur), pl.BlockSpec(tile, cur)],
        out_shape=[jax.ShapeDtypeStruct(q.shape, MXU_DTYPE), jax.ShapeDtypeStruct(q.shape, jnp.float32)],
        compiler_params=pltpu.CompilerParams(dimension_semantics=("arbitrary",) * 3),
        name=f"dilated_d{dilation}",
    )(q, k, v, k, v)


def _key_to_f32(key):
    bits = key ^ ((key >> 31) & jnp.int32(0x7FFFFFFF))
    return lax.bitcast_convert_type(bits, jnp.float32)


def _f32_to_key(x):
    bits = lax.bitcast_convert_type(x, jnp.int32)
    return bits ^ ((bits >> 31) & jnp.int32(0x7FFFFFFF))


def _fold_rows(x, op):
    rows = x.shape[0]
    y = op(x.reshape(rows // 64, 64, LANES), axis=0)
    return op(y.reshape(8, 8, LANES), axis=0)


def _sparse_kernel(qi_ref, qb_ref, wt_ref, kid_ref, kbd_ref, vbt_ref, out_ref,
                   sc_ref, eq_ref, eqc_ref, mb_ref, qis_ref, qbs_ref, acc_ref, m_ref, sqa_ref, sqb_ref, mxa_ref,
                   mxb_ref, *, topk):
    i = pl.program_id(1)
    nch = i // (KEY_CHUNK // BLOCK) + 1
    ncc = i // (COUNT_CHUNK // BLOCK) + 1
    lane = lax.broadcasted_iota(jnp.int32, (1, LANES), 1)
    t_idx = i * BLOCK + lane
    row_iota = lax.broadcasted_iota(jnp.int32, (KEY_CHUNK, LANES), 0)
    crow_iota = lax.broadcasted_iota(jnp.int32, (COUNT_CHUNK, LANES), 0)
    neg_inf = jnp.float32(-jnp.inf)
    f32_lowest = jnp.float32(jnp.finfo(jnp.float32).min)

    eye = jnp.where(lax.broadcasted_iota(jnp.int32, (BLOCK, LANES), 0) == lane, 1.0, 0.0).astype(MXU_DTYPE)
    for h in range(N_HEADS):
        cols = slice((h // 2) * LANES, (h // 2 + 1) * LANES)
        head_lanes = ((lane // HALF) % 2) == (h % 2)
        rows = slice(h * BLOCK, (h + 1) * BLOCK)
        qis_ref[rows, :] = jnp.where(head_lanes, qi_ref[0, :, cols], jnp.zeros((), MXU_DTYPE))
        qbs_ref[rows, :LANES] = jnp.where(head_lanes, qb_ref[0, :, cols], jnp.zeros((), MXU_DTYPE))
        qbs_ref[rows, LANES:] = eye

    def chunk_start(c):
        return pl.multiple_of(c * KEY_CHUNK, KEY_CHUNK)

    def pair_q(ref, p):
        return ref[p * 2 * BLOCK:(p + 1) * 2 * BLOCK, :]

    buf_a, buf_b = (sqa_ref, mxa_ref), (sqb_ref, mxb_ref)

    def pipeline(produce, consume):
        produce(0, buf_a)

        def two_chunks(j, carry):
            c = 2 * j
            produce(c + 1, buf_b)
            consume(c, buf_a)
            produce(c + 2, buf_a)
            consume(c + 1, buf_b)
            return carry

        lax.fori_loop(0, (nch - 1) // 2, two_chunks, 0)
        last = nch - 1

        @pl.when(last % 2 == 1)
        def _():
            produce(last, buf_b)
            consume(last - 1, buf_a)
            consume(last, buf_b)

        @pl.when(last % 2 == 0)
        def _():
            consume(last, buf_a)

    def score_chunk(c, carry):
        r0 = chunk_start(c)
        kc = kid_ref[0, pl.ds(r0, KEY_CHUNK), :]
        acc = jnp.zeros((KEY_CHUNK, LANES), jnp.float32)
        for p in range(N_HEADS // 2):
            d = _nt_dot(kc, pair_q(qis_ref, p))
            for hh in range(2):
                h = 2 * p + hh
                acc = acc + jnp.maximum(d[:, hh * BLOCK:(hh + 1) * BLOCK], 0.0) * wt_ref[0, h:h + 1, :]
        admissible = r0 + row_iota <= t_idx
        scores = acc * IDX_SCALE
        masked = jnp.where(admissible, scores, neg_inf)
        sc_ref[pl.ds(r0, KEY_CHUNK), :] = masked
        top8, bot8 = carry
        return (jnp.maximum(top8, _fold_rows(masked, jnp.max)),
                jnp.minimum(bot8, _fold_rows(jnp.where(admissible, scores, -neg_inf), jnp.min)))

    top8, bot8 = lax.fori_loop(0, nch, score_chunk, (jnp.full((8, LANES), neg_inf, jnp.float32),
                                                     jnp.full((8, LANES), -neg_inf, jnp.float32)))
    top = jnp.max(top8, axis=0, keepdims=True)
    bot = jnp.min(bot8, axis=0, keepdims=True)

    def fold_scores(value, op, pairwise, start, src_ref=sc_ref):
        def body(c, acc):
            r0 = pl.multiple_of(c * COUNT_CHUNK, COUNT_CHUNK)
            val = value(src_ref[pl.ds(r0, COUNT_CHUNK), :], r0)
            return pairwise(acc, op(val.reshape(COUNT_FOLD, COUNT_CHUNK // COUNT_FOLD, LANES), axis=0))

        acc = lax.fori_loop(0, ncc // 2, lambda j, a: body(2 * j + 1, body(2 * j, a)),
                            jnp.full((COUNT_CHUNK // COUNT_FOLD, LANES), start, jnp.float32))
        acc = lax.cond(ncc % 2 == 1, lambda a: body(ncc - 1, a), lambda a: a, acc)
        return op(acc, axis=0, keepdims=True)

    def count(pred):
        return fold_scores(lambda x, r0: jnp.where(pred(x, r0), 1.0, 0.0), jnp.sum, jnp.add, 0.0)

    key_top = _f32_to_key(top)
    lo = _f32_to_key(bot) - jnp.int32(1 << 23)
    hi = key_top + 1
    hi = jnp.where((hi >= -MIN_NORMAL_KEY) & (hi < MIN_NORMAL_KEY), jnp.int32(MIN_NORMAL_KEY), hi)
    unknown = jnp.float32(2 * sc_ref.shape[0])

    def is_settled(carry):
        lo, hi, cnt_lo = carry[:3]
        return (cnt_lo == topk) | (hi - lo == 1) | (t_idx < topk)

    def probe(mid, carry):
        lo, hi, cnt_lo, thr_lo, thr_hi = carry
        thr_c = _key_to_f32(mid)
        cnt = count(lambda x, r0: x >= thr_c)
        take = (cnt >= topk) & ~is_settled(carry)
        drop = (cnt < topk) & ~is_settled(carry)
        return (jnp.where(take, mid, lo), jnp.where(drop, mid, hi), jnp.where(take, cnt, cnt_lo),
                jnp.where(take, thr_c, thr_lo), jnp.where(drop, thr_c, thr_hi))

    def halve(by_value, carry):
        lo, hi = carry[:2]
        mid = lo + lax.shift_right_logical(hi - lo, 1)
        if by_value:
            mid_val = _f32_to_key(0.5 * _key_to_f32(lo) + 0.5 * _key_to_f32(hi))
            mid = jnp.where((mid_val > lo) & (mid_val < hi), mid_val, mid)
        return probe(mid, carry)

    def descend(carry):
        lo, hi, cnt_lo, thr_lo, thr_hi = carry
        v = fold_scores(lambda x, r0: jnp.where(x < thr_hi, x, neg_inf), jnp.max, jnp.maximum, neg_inf)
        cnt = count(lambda x, r0: x >= v)
        key_v = _f32_to_key(v)
        take = (cnt >= topk) & ~is_settled(carry)
        drop = (cnt < topk) & ~is_settled(carry)
        return (jnp.where(take, key_v, lo), jnp.where(take, key_v + 1, jnp.where(drop, key_v, hi)),
                jnp.where(take, cnt, cnt_lo), jnp.where(take, v, thr_lo), jnp.where(drop, v, thr_hi))

    def pending(carry):
        return jnp.max(jnp.where(is_settled(carry), 0.0, 1.0))

    state = lax.fori_loop(0, SEARCH_PASSES_MIN, lambda b, c: halve(True, c),
                          (lo, hi, jnp.full((1, LANES), unknown), _key_to_f32(lo), _key_to_f32(hi)))

    def two_passes(carry):
        b, state, _ = carry
        state = lax.cond(b < VALUE_PASSES, lambda s: halve(True, halve(True, s)),
                         lambda s: lax.cond(b < VALUE_PASSES + 2 * DESCENT_STEPS, descend,
                                            lambda t: halve(False, halve(False, t)), s), state)
        return b + 2, state, pending(state)

    _, (lo, hi, cnt_ge, thr_raw, _), _ = lax.while_loop(
        lambda c: (c[0] < VALUE_PASSES + 2 * DESCENT_STEPS + 32) & (c[2] > 0.0), two_passes,
        (jnp.int32(SEARCH_PASSES_MIN), state, pending(state)))
    enough = thr_raw > f32_lowest
    thr = jnp.where(enough, thr_raw, f32_lowest)

    tie_lane = enough & (cnt_ge > topk) & (t_idx >= topk)

    @pl.when(jnp.max(jnp.where(tie_lane, 1.0, 0.0)) > 0.0)
    def _():
        need = topk - count(lambda x, r0: x > thr)

        def mark_equal(c, carry):
            seen, chunk_of, seen_before = carry
            r0 = pl.multiple_of(c * COUNT_CHUNK, COUNT_CHUNK)
            eq = jnp.where(sc_ref[pl.ds(r0, COUNT_CHUNK), :] == thr, 1.0, 0.0)
            eq_ref[pl.ds(r0, COUNT_CHUNK), :] = eq
            here = jnp.sum(_fold_rows(eq, jnp.sum), axis=0, keepdims=True)
            first_reach = (seen < need) & (seen + here >= need)
            return (seen + here, jnp.where(first_reach, c, chunk_of), jnp.where(first_reach, seen, seen_before))

        zero = jnp.zeros((1, LANES), jnp.float32)
        _, chunk_of, seen_before = lax.fori_loop(0, ncc, mark_equal, (zero, jnp.zeros((1, LANES), jnp.int32), zero))
        need_here = need - seen_before

        eqc_ref[...] = jnp.zeros(eqc_ref.shape, jnp.float32)

        def gather_chunk(c, carry):
            r0 = pl.multiple_of(c * COUNT_CHUNK, COUNT_CHUNK)
            eqc_ref[...] += jnp.where(chunk_of == c, eq_ref[pl.ds(r0, COUNT_CHUNK), :], 0.0)
            return carry

        lax.fori_loop(0, ncc, gather_chunk, 0)

        def idx_step(b, lim):
            cand = lim | (jnp.int32(1) << (COUNT_CHUNK.bit_length() - 2 - b))
            below = jnp.sum(_fold_rows(jnp.where(crow_iota < cand, eqc_ref[...], 0.0), jnp.sum), axis=0, keepdims=True)
            return jnp.where(below < need_here, cand, lim)

        lim = chunk_of * COUNT_CHUNK + lax.fori_loop(0, COUNT_CHUNK.bit_length() - 1, idx_step,
                                                     jnp.zeros((1, LANES), jnp.int32))

        def demote(c, carry):
            r0 = pl.multiple_of(c * COUNT_CHUNK, COUNT_CHUNK)
            x = sc_ref[pl.ds(r0, COUNT_CHUNK), :]
            drop = (eq_ref[pl.ds(r0, COUNT_CHUNK), :] > 0.0) & (r0 + crow_iota > lim) & tie_lane
            sc_ref[pl.ds(r0, COUNT_CHUNK), :] = jnp.where(drop, neg_inf, x)
            return carry

        lax.fori_loop(0, ncc, demote, 0)

    mask_off = float(jnp.finfo(MXU_DTYPE).min)

    def build_mask(c, carry):
        r0 = chunk_start(c)
        selected = sc_ref[pl.ds(r0, KEY_CHUNK), :] >= thr
        mb_ref[pl.ds(r0, KEY_CHUNK), :] = jnp.where(selected, 0.0, mask_off).astype(MXU_DTYPE)
        return carry

    lax.fori_loop(0, nch, build_mask, 0)

    def attn_scores(c, buf):
        sq_ref, mx_ref = buf
        r0 = chunk_start(c)
        kaug = jnp.concatenate([kbd_ref[0, pl.ds(r0, KEY_CHUNK), :], mb_ref[pl.ds(r0, KEY_CHUNK), :]], axis=1)
        for p in range(N_HEADS // 2):
            s = _nt_dot(kaug, pair_q(qbs_ref, p))
            sq_ref[:, p * 2 * BLOCK:(p + 1) * 2 * BLOCK] = s
            for hh in range(2):
                cols = slice((2 * p + hh) * BLOCK, (2 * p + hh + 1) * BLOCK)
                mx_ref[:, cols] = _fold_rows(s[:, hh * BLOCK:(hh + 1) * BLOCK], jnp.max)

    def accumulate(c, buf):
        sq_ref, mx_ref = buf
        m_run = m_ref[...]
        m_new = jnp.maximum(m_run, jnp.max(mx_ref[...], axis=0, keepdims=True))
        m_ref[...] = m_new
        acc_ref[...] = acc_ref[...] * jnp.exp2(m_run - m_new)
        vt = vbt_ref[0, c]
        for p in range(N_HEADS // 2):
            pcols = slice(p * 2 * BLOCK, (p + 1) * 2 * BLOCK)
            e = jnp.exp2(sq_ref[:, pcols] - m_new[:, pcols])
            acc_ref[:, pcols] += jnp.dot(vt, e.astype(MXU_DTYPE), preferred_element_type=jnp.float32)

    acc_ref[...] = jnp.zeros(acc_ref.shape, jnp.float32)
    m_ref[...] = jnp.full(m_ref.shape, f32_lowest, jnp.float32)
    pipeline(attn_scores, accumulate)

    o_t = acc_ref[:HEAD_DIM, :] / acc_ref[HEAD_DIM:HEAD_DIM + 1, :]
    for p in range(N_HEADS // 2):
        pair = jnp.concatenate([o_t[:, (2 * p) * BLOCK:(2 * p + 1) * BLOCK],
                                o_t[:, (2 * p + 1) * BLOCK:(2 * p + 2) * BLOCK]], axis=0)
        out_ref[0, :, p * LANES:(p + 1) * LANES] = pair.T.astype(out_ref.dtype)


def _sparse(qi, qb, wt, kid, kbd, vbt, batch, seq):
    topk = min(TOPK_MAX, seq // 4)
    nq = seq // BLOCK
    qblk = pl.BlockSpec((1, BLOCK, D_HEADS), lambda b, i: (b, i, 0))
    full = pl.BlockSpec((1, seq, LANES), lambda b, i: (b, 0, 0))
    kern = functools.partial(_sparse_kernel, topk=topk)
    return pl.pallas_call(
        kern,
        grid=(batch, nq),
        in_specs=[qblk, qblk,
                  pl.BlockSpec((1, IDX_HEADS, BLOCK), lambda b, i: (b, 0, i)),
                  full, full,
                  pl.BlockSpec((1, seq // KEY_CHUNK, HEAD_DIM + L_ROWS, KEY_CHUNK), lambda b, i: (b, 0, 0, 0))],
        out_specs=qblk,
        out_shape=jax.ShapeDtypeStruct((batch, seq, D_HEADS), MXU_DTYPE),
        scratch_shapes=[pltpu.VMEM((seq, LANES), jnp.float32),
                        pltpu.VMEM((seq, LANES), jnp.float32),
                        pltpu.VMEM((COUNT_CHUNK, LANES), jnp.float32),
                        pltpu.VMEM((seq, LANES), MXU_DTYPE),
                        pltpu.VMEM((N_HEADS * BLOCK, LANES), MXU_DTYPE),
                        pltpu.VMEM((N_HEADS * BLOCK, 2 * LANES), MXU_DTYPE),
                        pltpu.VMEM((HEAD_DIM + L_ROWS, N_HEADS * BLOCK), jnp.float32),
                        pltpu.VMEM((1, N_HEADS * BLOCK), jnp.float32),
                        pltpu.VMEM((KEY_CHUNK, N_HEADS * BLOCK), jnp.float32),
                        pltpu.VMEM((KEY_CHUNK, N_HEADS * BLOCK), jnp.float32),
                        pltpu.VMEM((8, N_HEADS * BLOCK), jnp.float32),
                        pltpu.VMEM((8, N_HEADS * BLOCK), jnp.float32)],
        compiler_params=pltpu.CompilerParams(dimension_semantics=("arbitrary", "arbitrary"),
                                             vmem_limit_bytes=VMEM_LIMIT),
        name="indexer_sparse_attn",
    )(qi, qb, wt, kid, kbd, vbt)


def _merge_kernel(x_ref, g_ref, *refs):
    n_pat = len(DILATED_PATTERNS)
    o_refs, l_refs = refs[:n_pat], refs[n_pat:2 * n_pat]
    yb_ref, wg_ref, wua_ref, wub_ref, wo_ref, x1_ref = refs[2 * n_pat:2 * n_pat + 6]
    slabs = refs[2 * n_pat + 6:]
    x = x_ref[...]
    h = _rms(x, g_ref[...]).astype(MXU_DTYPE)
    tm = x.shape[0]
    n_grp = D_HEADS // LANES

    o_src, l_src, k = [], [], 0
    for (_, d), o_ref, l_ref in zip(DILATED_PATTERNS, o_refs, l_refs):
        if d == 1:
            o_src.append(lambda g, r=o_ref: r[:, g * LANES:(g + 1) * LANES].astype(jnp.float32))
            l_src.append(lambda g, r=l_ref: r[:, g * LANES:(g + 1) * LANES])
            continue
        o_slab, l_slab = slabs[2 * k], slabs[2 * k + 1]
        k += 1
        for r in range(d):
            for g in range(n_grp):
                cols = slice(g * LANES, (g + 1) * LANES)
                o_slab[g, pl.ds(r, tm // d, stride=d), :] = o_ref[0, r, :, cols].astype(jnp.float32)
                l_slab[g, pl.ds(r, tm // d, stride=d), :] = l_ref[0, r, :, cols]
        o_src.append(lambda g, s=o_slab: s[g])
        l_src.append(lambda g, s=l_slab: s[g])

    parts = []
    for g in range(n_grp):
        ls = [f(g) for f in l_src]
        m = functools.reduce(jnp.maximum, ls)
        es = [jnp.exp(l - m) for l in ls]
        num = functools.reduce(jnp.add, [e * f(g) for e, f in zip(es, o_src)])
        parts.append((num / functools.reduce(jnp.add, es)).astype(MXU_DTYPE))
    ya = jnp.concatenate(parts, axis=1)
    ua = jnp.dot(ya, wua_ref[...], preferred_element_type=jnp.float32)
    ub = jnp.dot(yb_ref[...], wub_ref[...], preferred_element_type=jnp.float32)
    ga = jnp.dot(h, wg_ref[:, :D_MODEL], preferred_element_type=jnp.float32)
    gb = jnp.dot(h, wg_ref[:, D_MODEL:], preferred_element_type=jnp.float32)
    merged = jax.nn.sigmoid(ga) * ua + jax.nn.sigmoid(gb) * ub
    x1_ref[...] = x + jnp.dot(merged.astype(MXU_DTYPE), wo_ref[...], preferred_element_type=jnp.float32)


def _merge(x2, g, os_, lses, yb, wg, wua, wub, wo, seq):
    n = x2.shape[0]
    tm = ROW_TILE
    tiles_per_seq = seq // tm
    row = lambda i: (i, 0)
    const = lambda i: (0, 0)
    half = pl.BlockSpec((tm, D_HEADS), row)
    pat_specs = [half if d == 1 else
                 pl.BlockSpec((1, d, tm // d, D_HEADS), lambda i: (i // tiles_per_seq, 0, i % tiles_per_seq, 0))
                 for _, d in DILATED_PATTERNS]
    return pl.pallas_call(
        _merge_kernel,
        grid=(n // tm,),
        in_specs=[pl.BlockSpec((tm, D_MODEL), row), pl.BlockSpec((1, D_MODEL), const)] + pat_specs * 2 + [half] + [
            pl.BlockSpec((D_MODEL, 2 * D_MODEL), const),
            pl.BlockSpec((D_HEADS, D_MODEL), const),
            pl.BlockSpec((D_HEADS, D_MODEL), const),
            pl.BlockSpec((D_MODEL, D_MODEL), const)],
        out_specs=pl.BlockSpec((tm, D_MODEL), row),
        out_shape=jax.ShapeDtypeStruct((n, D_MODEL), jnp.float32),
        scratch_shapes=[pltpu.VMEM((D_HEADS // LANES, tm, LANES), jnp.float32)] * (2 * len(RESIDUE_DILATIONS)),
        compiler_params=pltpu.CompilerParams(dimension_semantics=("arbitrary",), vmem_limit_bytes=VMEM_LIMIT),
        name="mix_gate_out",
    )(x2, g, *os_, *lses, yb, wg, wua, wub, wo)


def _ffn_kernel(x_ref, g_ref, wgate_ref, wup_ref, wdown_ref, gf_ref, out_ref):
    x = x_ref[...]
    h = _rms(x, g_ref[...]).astype(MXU_DTYPE)
    y = x
    for c0 in range(0, D_FF, FF_CHUNK):
        a = jnp.dot(h, wgate_ref[:, c0:c0 + FF_CHUNK], preferred_element_type=jnp.float32)
        u = jnp.dot(h, wup_ref[:, c0:c0 + FF_CHUNK], preferred_element_type=jnp.float32)
        act = (a * jax.nn.sigmoid(a) * u).astype(MXU_DTYPE)
        y = y + jnp.dot(act, wdown_ref[c0:c0 + FF_CHUNK, :], preferred_element_type=jnp.float32)
    out_ref[...] = _rms(y, gf_ref[...])


def _ffn(x1, g, wgate, wup, wdown, gf):
    n = x1.shape[0]
    tm = ROW_TILE
    row = lambda i: (i, 0)
    const = lambda i: (0, 0)
    return pl.pallas_call(
        _ffn_kernel,
        grid=(n // tm,),
        in_specs=[pl.BlockSpec((tm, D_MODEL), row), pl.BlockSpec((1, D_MODEL), const),
                  pl.BlockSpec((D_MODEL, D_FF), const), pl.BlockSpec((D_MODEL, D_FF), const),
                  pl.BlockSpec((D_FF, D_MODEL), const), pl.BlockSpec((1, D_MODEL), const)],
        out_specs=pl.BlockSpec((tm, D_MODEL), row),
        out_shape=jax.ShapeDtypeStruct((n, D_MODEL), jnp.float32),
        compiler_params=pltpu.CompilerParams(dimension_semantics=("arbitrary",), vmem_limit_bytes=VMEM_LIMIT),
        name="ffn_norm",
    )(x1, g, wgate, wup, wdown, gf)


def _rope_tables(seq):
    inv_freq = ROPE_THETA ** (-jnp.arange(HALF, dtype=jnp.float32) / HALF)
    ang = jnp.arange(seq, dtype=jnp.int32).astype(jnp.float32)[:, None] * inv_freq[None, :]
    cos, sin = jnp.cos(ang), jnp.sin(ang)
    return jnp.tile(cos, (1, 4)), jnp.concatenate([-sin, -sin, sin, sin], axis=1)


def kernel(x, norm_mix, w_in, w_up_a, w_up_b, w_out, norm_ffn, w_gate, w_up, w_down, norm_final):
    batch, seq, _ = x.shape
    assert seq % max(d * BLOCK for _, d in DILATED_PATTERNS) == 0 and seq % KEY_CHUNK == 0
    assert all(w // d == BLOCK for w, d in DILATED_PATTERNS)
    n = batch * seq
    bf = MXU_DTYPE
    xf = x.reshape(n, D_MODEL)
    cos_t, sin_t = _rope_tables(seq)
    for layer in range(w_in.shape[0]):
        w = w_in[layer]
        w_pack = jnp.pad(w[:, _packed_columns()], ((0, 0), (0, _P_TOTAL - _P_MISC - HEAD_DIM - IDX_HEADS))).astype(bf)
        w_gates = w[:, _GA:].astype(bf)
        qas, kas, vas, qb, qi, kbd, kid, misc = _project(xf, norm_mix[layer][None], w_pack, cos_t, sin_t, seq)

        dil = []
        for (_, d), q, k, v in zip(DILATED_PATTERNS, qas, kas, vas):
            if d == 1:
                o, lse = _dilated(*(z.reshape(batch, 1, seq, D_HEADS) for z in (q, k, v)), d)
                dil.append((o.reshape(n, D_HEADS), lse.reshape(n, D_HEADS)))
            else:
                dil.append(_dilated(q, k, v, d))

        ones_col = (jnp.arange(L_ROWS) == 0).astype(bf)[None, :]
        vb = jnp.concatenate([misc[:, :HEAD_DIM].astype(bf), jnp.broadcast_to(ones_col, (n, L_ROWS))], axis=1)
        vbt = jnp.swapaxes(vb.reshape(batch, seq // KEY_CHUNK, KEY_CHUNK, HEAD_DIM + L_ROWS), 2, 3)
        wt = jnp.swapaxes(misc[:, HEAD_DIM:HEAD_DIM + IDX_HEADS].reshape(batch, seq, IDX_HEADS), 1, 2)
        r3 = lambda z: z.reshape(batch, seq, z.shape[-1])
        yb = _sparse(r3(qi), r3(qb), wt, r3(kid), r3(kbd), vbt, batch, seq).reshape(n, D_HEADS)

        x1 = _merge(xf, norm_mix[layer][None], [o for o, _ in dil], [l for _, l in dil], yb,
                    w_gates, w_up_a[layer].astype(bf), w_up_b[layer].astype(bf), w_out[layer].astype(bf), seq)
        last = layer == w_in.shape[0] - 1
        assert last, "the final norm is fused into the FFN kernel of the last layer"
        xf = _ffn(x1, norm_ffn[layer][None], w_gate[layer].astype(bf), w_up[layer].astype(bf),
                  w_down[layer].astype(bf), norm_final[None])
    return xf.reshape(batch, seq, D_MODEL)
```

```python
import functools

import numpy as np
import jax
import jax.numpy as jnp
from jax import lax
from jax.experimental import pallas as pl
from jax.experimental.pallas import tpu as pltpu

D_MODEL = 1024
HEAD_DIM = 64
HALF = HEAD_DIM // 2
N_HEADS = 8
D_HEADS = N_HEADS * HEAD_DIM
IDX_HEADS = 8
DILATED_PATTERNS = ((128, 1), (512, 4), (2048, 16))
RESIDUE_DILATIONS = tuple(d for _, d in DILATED_PATTERNS if d > 1)
TOPK_MAX = 256
D_FF = 2816
ROPE_THETA = 10000.0
RMS_EPS = 1e-6
BLOCK = 128
ATTN_SCALE = HEAD_DIM ** -0.5
IDX_SCALE = (HEAD_DIM ** -0.5) * (IDX_HEADS ** -0.5)
LOG2_E = float(np.log2(np.e))

LANES = 128
MIN_NORMAL_KEY = 1 << 23
KEY_CHUNK = 1024
COUNT_CHUNK = 512
COUNT_FOLD = 8
VALUE_PASSES = 14
DESCENT_STEPS = 6
SEARCH_PASSES_MIN = 12
L_ROWS = 8
ROW_TILE = 512
DIL_BLOCKS = 8
FF_CHUNK = 1408
VMEM_LIMIT = 56 * 1024 * 1024
MXU_DTYPE = jnp.bfloat16

_SPLITS = (D_HEADS, D_HEADS, D_HEADS, D_HEADS, HEAD_DIM, HEAD_DIM, IDX_HEADS * HEAD_DIM, HEAD_DIM, IDX_HEADS,
           D_MODEL, D_MODEL)
_OFF = np.concatenate([[0], np.cumsum(_SPLITS)])
(_QA, _KA, _VA, _QB, _KB, _VB, _QI, _KI, _WI, _GA, _GB) = (int(o) for o in _OFF[:-1])

_P_QA, _P_KA, _P_VA, _P_QB, _P_QI = 0, 512, 1024, 1536, 2048
_P_KBD, _P_KID, _P_MISC = 2560, 2688, 2816
_P_TOTAL = 2944


def _pair_perm():
    idx = np.empty(D_HEADS, np.int64)
    for j in range(D_HEADS):
        g, l = divmod(j, LANES)
        quarter, e = divmod(l, HALF)
        head = 2 * g + (quarter % 2)
        idx[j] = head * HEAD_DIM + (quarter // 2) * HALF + e
    return idx


def _dup_perm():
    idx = np.empty(LANES, np.int64)
    for l in range(LANES):
        quarter, e = divmod(l, HALF)
        idx[l] = (quarter // 2) * HALF + e
    return idx


def _packed_columns():
    pp, dp = _pair_perm(), _dup_perm()
    return np.concatenate([
        _QA + pp, _KA + pp, _VA + np.arange(D_HEADS), _QB + pp, _QI + pp,
        _KB + dp, _KI + dp, _VB + np.arange(HEAD_DIM), _WI + np.arange(IDX_HEADS)])


def _rms(x, g):
    ms = jnp.mean(x * x, axis=-1, keepdims=True)
    return x * lax.rsqrt(ms + RMS_EPS) * g


def _nt_dot(a, b):
    return lax.dot_general(a, b, (((1,), (1,)), ((), ())), preferred_element_type=jnp.float32)


def _proj_kernel(x_ref, g_ref, w_ref, cos_ref, sin_ref, *refs):
    n_lay = 1 + len(RESIDUE_DILATIONS)
    qa_refs, ka_refs, va_refs = refs[:n_lay], refs[n_lay:2 * n_lay], refs[2 * n_lay:3 * n_lay]
    qb_ref, qi_ref, kbd_ref, kid_ref, misc_ref, h_ref, slab_ref = refs[3 * n_lay:]
    h_ref[...] = _rms(x_ref[...], g_ref[...]).astype(MXU_DTYPE)
    cos = cos_ref[...]
    sin = sin_ref[...]

    def mm(c0, width):
        return jnp.dot(h_ref[...], w_ref[:, c0:c0 + width], preferred_element_type=jnp.float32)

    def rope(z):
        parts = []
        for g in range(z.shape[1] // LANES):
            zg = z[:, g * LANES:(g + 1) * LANES]
            parts.append(zg * cos + pltpu.roll(zg, 2 * HALF, axis=1) * sin)
        return parts[0] if len(parts) == 1 else jnp.concatenate(parts, axis=1)

    def emit(y, out_refs):
        out_refs[0][...] = y.astype(MXU_DTYPE)
        for g in range(D_HEADS // LANES):
            slab_ref[g] = y[:, g * LANES:(g + 1) * LANES]
        for d, ref in zip(RESIDUE_DILATIONS, out_refs[1:]):
            rows = y.shape[0] // d
            for r in range(d):
                for g in range(D_HEADS // LANES):
                    ref[0, r, :, g * LANES:(g + 1) * LANES] = (
                        slab_ref[g, pl.ds(r, rows, stride=d), :].astype(MXU_DTYPE))

    emit(rope(mm(_P_QA, D_HEADS)) * ATTN_SCALE, qa_refs)
    emit(rope(mm(_P_KA, D_HEADS)), ka_refs)
    emit(mm(_P_VA, D_HEADS), va_refs)
    qb_ref[...] = (rope(mm(_P_QB, D_HEADS)) * (ATTN_SCALE * LOG2_E)).astype(MXU_DTYPE)
    qi_ref[...] = rope(mm(_P_QI, D_HEADS)).astype(MXU_DTYPE)
    kbd_ref[...] = rope(mm(_P_KBD, LANES)).astype(MXU_DTYPE)
    kid_ref[...] = rope(mm(_P_KID, LANES)).astype(MXU_DTYPE)
    misc_ref[...] = mm(_P_MISC, LANES)


def _project(x2, g, w_pack, cos_t, sin_t, seq):
    n = x2.shape[0]
    tm = ROW_TILE
    tiles_per_seq = seq // tm
    row = lambda i: (i, 0)
    const = lambda i: (0, 0)
    pos = lambda i: (i % tiles_per_seq, 0)
    batch = n // seq
    wide = jax.ShapeDtypeStruct((n, D_HEADS), MXU_DTYPE)
    narrow = jax.ShapeDtypeStruct((n, LANES), MXU_DTYPE)
    wide_spec = pl.BlockSpec((tm, D_HEADS), row)
    lay_shapes = [wide] + [jax.ShapeDtypeStruct((batch, d, seq // d, D_HEADS), MXU_DTYPE) for d in RESIDUE_DILATIONS]
    lay_specs = [wide_spec] + [
        pl.BlockSpec((1, d, tm // d, D_HEADS), lambda i: (i // tiles_per_seq, 0, i % tiles_per_seq, 0))
        for d in RESIDUE_DILATIONS]
    n_lay = len(lay_shapes)
    outs = pl.pallas_call(
        _proj_kernel,
        grid=(n // tm,),
        in_specs=[
            pl.BlockSpec((tm, D_MODEL), row),
            pl.BlockSpec((1, D_MODEL), const),
            pl.BlockSpec((D_MODEL, _P_TOTAL), const),
            pl.BlockSpec((tm, LANES), pos),
            pl.BlockSpec((tm, LANES), pos),
        ],
        out_specs=lay_specs * 3 + [wide_spec] * 2 + [pl.BlockSpec((tm, LANES), row)] * 3,
        out_shape=lay_shapes * 3 + [wide] * 2 + [narrow, narrow, jax.ShapeDtypeStruct((n, LANES), jnp.float32)],
        scratch_shapes=[pltpu.VMEM((tm, D_MODEL), MXU_DTYPE),
                        pltpu.VMEM((D_HEADS // LANES, tm, LANES), jnp.float32)],
        compiler_params=pltpu.CompilerParams(dimension_semantics=("arbitrary",), vmem_limit_bytes=VMEM_LIMIT),
        name="proj_rope",
    )(x2, g, w_pack, cos_t, sin_t)
    return (outs[:n_lay], outs[n_lay:2 * n_lay], outs[2 * n_lay:3 * n_lay]) + tuple(outs[3 * n_lay:])


def _dil_kernel(q_ref, k_ref, v_ref, kp_ref, vp_ref, o_ref, lse_ref, *, n_blocks):
    n = pl.program_id(2)
    kj = lax.broadcasted_iota(jnp.int32, (2 * BLOCK, LANES), 0)
    qi = lax.broadcasted_iota(jnp.int32, (2 * BLOCK, LANES), 1)
    band = (kj >= qi) & (kj <= qi + BLOCK)
    lane = lax.broadcasted_iota(jnp.int32, (1, LANES), 1)
    pairs = range(N_HEADS // 2)
    col = lambda p: slice(p * LANES, (p + 1) * LANES)

    for j in range(n_blocks):
        rows = slice(j * BLOCK, (j + 1) * BLOCK)
        if j == 0:
            k_prev = lambda p: kp_ref[0, 0, :, col(p)]
            v_prev = lambda p: vp_ref[0, 0, :, col(p)]
            bias = jnp.where(band & ((kj >= BLOCK) | (n > 0)), 0.0, -jnp.inf)
        else:
            prev_rows = slice((j - 1) * BLOCK, j * BLOCK)
            k_prev = lambda p, r=prev_rows: k_ref[0, 0, r, col(p)]
            v_prev = lambda p, r=prev_rows: v_ref[0, 0, r, col(p)]
            bias = jnp.where(band, 0.0, -jnp.inf)

        scores = []
        for p in pairs:
            qp = q_ref[0, 0, rows, col(p)]
            qs = jnp.concatenate([jnp.where(((lane // HALF) % 2) == hh, qp, jnp.zeros_like(qp))
                                  for hh in range(2)], axis=0)
            k2 = jnp.concatenate([k_prev(p), k_ref[0, 0, rows, col(p)]], axis=0)
            scores.append(_nt_dot(k2, qs))

        probs, dens, lses = [], [], []
        for p in pairs:
            for hh in range(2):
                sh = scores[p][:, hh * BLOCK:(hh + 1) * BLOCK] + bias
                m = jnp.max(sh, axis=0, keepdims=True)
                e = jnp.exp(sh - m)
                den = jnp.sum(e, axis=0, keepdims=True)
                probs.append(e.astype(MXU_DTYPE))
                dens.append(den)
                lses.append(m + jnp.log(den))

        for p in pairs:
            v2 = jnp.concatenate([v_prev(p), v_ref[0, 0, rows, col(p)]], axis=0)
            o_t = lax.dot_general(v2, jnp.concatenate(probs[2 * p:2 * p + 2], axis=1), (((0,), (0,)), ((), ())),
                                  preferred_element_type=jnp.float32)
            tile = jnp.concatenate([o_t[:HEAD_DIM, :BLOCK] / dens[2 * p],
                                    o_t[HEAD_DIM:, BLOCK:] / dens[2 * p + 1]], axis=0)
            o_ref[0, 0, rows, col(p)] = tile.T.astype(o_ref.dtype)
            lse_tile = jnp.concatenate([jnp.broadcast_to(l, (HEAD_DIM, BLOCK)) for l in lses[2 * p:2 * p + 2]],
                                       axis=0)
            lse_ref[0, 0, rows, col(p)] = lse_tile.T


def _dilated(q, k, v, dilation):
    batch, _, m_len, _ = q.shape
    n_blocks = min(DIL_BLOCKS, m_len // BLOCK)
    tile_rows = n_blocks * BLOCK
    cur = lambda b, r, n: (b, r, n, 0)
    prev = lambda b, r, n: (b, r, jnp.maximum(n * n_blocks - 1, 0), 0)
    tile = (1, 1, tile_rows, D_HEADS)
    blk = (1, 1, BLOCK, D_HEADS)
    return pl.pallas_call(
        functools.partial(_dil_kernel, n_blocks=n_blocks),
        grid=(batch, dilation, m_len // tile_rows),
        in_specs=[pl.BlockSpec(tile, cur), pl.BlockSpec(tile, cur), pl.BlockSpec(tile, cur),
                  pl.BlockSpec(blk, prev), pl.BlockSpec(blk, prev)],
        out_specs=[pl.BlockSpec(tile, cur), pl.BlockSpec(tile, cur)],
        out_shape=[jax.ShapeDtypeStruct(q.shape, MXU_DTYPE), jax.ShapeDtypeStruct(q.shape, jnp.float32)],
        compiler_params=pltpu.CompilerParams(dimension_semantics=("arbitrary",) * 3),
        name=f"dilated_d{dilation}",
    )(q, k, v, k, v)


def _key_to_f32(key):
    bits = key ^ ((key >> 31) & jnp.int32(0x7FFFFFFF))
    return lax.bitcast_convert_type(bits, jnp.float32)


def _f32_to_key(x):
    bits = lax.bitcast_convert_type(x, jnp.int32)
    return bits ^ ((bits >> 31) & jnp.int32(0x7FFFFFFF))


def _fold_rows(x, op):
    rows = x.shape[0]
    y = op(x.reshape(rows // 64, 64, LANES), axis=0)
    return op(y.reshape(8, 8, LANES), axis=0)


def _sparse_kernel(qi_ref, qb_ref, wt_ref, kid_ref, kbd_ref, vbt_ref, out_ref,
                   sc_ref, eq_ref, eqc_ref, mb_ref, qis_ref, qbs_ref, acc_ref, m_ref, sqa_ref, sqb_ref, mxa_ref,
                   mxb_ref, *, topk):
    i = pl.program_id(1)
    nch = i // (KEY_CHUNK // BLOCK) + 1
    ncc = i // (COUNT_CHUNK // BLOCK) + 1
    lane = lax.broadcasted_iota(jnp.int32, (1, LANES), 1)
    t_idx = i * BLOCK + lane
    row_iota = lax.broadcasted_iota(jnp.int32, (KEY_CHUNK, LANES), 0)
    crow_iota = lax.broadcasted_iota(jnp.int32, (COUNT_CHUNK, LANES), 0)
    neg_inf = jnp.float32(-jnp.inf)
    f32_lowest = jnp.float32(jnp.finfo(jnp.float32).min)

    eye = jnp.where(lax.broadcasted_iota(jnp.int32, (BLOCK, LANES), 0) == lane, 1.0, 0.0).astype(MXU_DTYPE)
    for h in range(N_HEADS):
        cols = slice((h // 2) * LANES, (h // 2 + 1) * LANES)
        head_lanes = ((lane // HALF) % 2) == (h % 2)
        rows = slice(h * BLOCK, (h + 1) * BLOCK)
        qis_ref[rows, :] = jnp.where(head_lanes, qi_ref[0, :, cols], jnp.zeros((), MXU_DTYPE))
        qbs_ref[rows, :LANES] = jnp.where(head_lanes, qb_ref[0, :, cols], jnp.zeros((), MXU_DTYPE))
        qbs_ref[rows, LANES:] = eye

    def chunk_start(c):
        return pl.multiple_of(c * KEY_CHUNK, KEY_CHUNK)

    def pair_q(ref, p):
        return ref[p * 2 * BLOCK:(p + 1) * 2 * BLOCK, :]

    buf_a, buf_b = (sqa_ref, mxa_ref), (sqb_ref, mxb_ref)

    def pipeline(produce, consume):
        produce(0, buf_a)

        def two_chunks(j, carry):
            c = 2 * j
            produce(c + 1, buf_b)
            consume(c, buf_a)
            produce(c + 2, buf_a)
            consume(c + 1, buf_b)
            return carry

        lax.fori_loop(0, (nch - 1) // 2, two_chunks, 0)
        last = nch - 1

        @pl.when(last % 2 == 1)
        def _():
            produce(last, buf_b)
            consume(last - 1, buf_a)
            consume(last, buf_b)

        @pl.when(last % 2 == 0)
        def _():
            consume(last, buf_a)

    def score_chunk(c, carry):
        r0 = chunk_start(c)
        kc = kid_ref[0, pl.ds(r0, KEY_CHUNK), :]
        acc = jnp.zeros((KEY_CHUNK, LANES), jnp.float32)
        for p in range(N_HEADS // 2):
            d = _nt_dot(kc, pair_q(qis_ref, p))
            for hh in range(2):
                h = 2 * p + hh
                acc = acc + jnp.maximum(d[:, hh * BLOCK:(hh + 1) * BLOCK], 0.0) * wt_ref[0, h:h + 1, :]
        admissible = r0 + row_iota <= t_idx
        scores = acc * IDX_SCALE
        masked = jnp.where(admissible, scores, neg_inf)
        sc_ref[pl.ds(r0, KEY_CHUNK), :] = masked
        top8, bot8 = carry
        return (jnp.maximum(top8, _fold_rows(masked, jnp.max)),
                jnp.minimum(bot8, _fold_rows(jnp.where(admissible, scores, -neg_inf), jnp.min)))

    ends = lax.fori_loop(0, nch // 2, lambda j, e: score_chunk(2 * j + 1, score_chunk(2 * j, e)),
                         (jnp.full((8, LANES), neg_inf, jnp.float32), jnp.full((8, LANES), -neg_inf, jnp.float32)))
    top8, bot8 = lax.cond(nch % 2 == 1, lambda e: score_chunk(nch - 1, e), lambda e: e, ends)
    top = jnp.max(top8, axis=0, keepdims=True)
    bot = jnp.min(bot8, axis=0, keepdims=True)

    def fold_scores(value, op, pairwise, start, src_ref=sc_ref):
        def body(c, acc):
            r0 = pl.multiple_of(c * COUNT_CHUNK, COUNT_CHUNK)
            val = value(src_ref[pl.ds(r0, COUNT_CHUNK), :], r0)
            return pairwise(acc, op(val.reshape(COUNT_FOLD, COUNT_CHUNK // COUNT_FOLD, LANES), axis=0))

        acc = lax.fori_loop(0, ncc // 2, lambda j, a: body(2 * j + 1, body(2 * j, a)),
                            jnp.full((COUNT_CHUNK // COUNT_FOLD, LANES), start, jnp.float32))
        acc = lax.cond(ncc % 2 == 1, lambda a: body(ncc - 1, a), lambda a: a, acc)
        return op(acc, axis=0, keepdims=True)

    def count(pred):
        return fold_scores(lambda x, r0: jnp.where(pred(x, r0), 1.0, 0.0), jnp.sum, jnp.add, 0.0)

    key_top = _f32_to_key(top)
    lo = _f32_to_key(bot) - jnp.int32(1 << 23)
    hi = key_top + 1
    hi = jnp.where((hi >= -MIN_NORMAL_KEY) & (hi < MIN_NORMAL_KEY), jnp.int32(MIN_NORMAL_KEY), hi)
    unknown = jnp.float32(2 * sc_ref.shape[0])

    def is_settled(carry):
        lo, hi, cnt_lo = carry[:3]
        return (cnt_lo == topk) | (hi - lo == 1) | (t_idx < topk)

    def probe(mid, carry):
        lo, hi, cnt_lo, thr_lo, thr_hi = carry
        thr_c = _key_to_f32(mid)
        cnt = count(lambda x, r0: x >= thr_c)
        take = (cnt >= topk) & ~is_settled(carry)
        drop = (cnt < topk) & ~is_settled(carry)
        return (jnp.where(take, mid, lo), jnp.where(drop, mid, hi), jnp.where(take, cnt, cnt_lo),
                jnp.where(take, thr_c, thr_lo), jnp.where(drop, thr_c, thr_hi))

    def halve(by_value, carry):
        lo, hi = carry[:2]
        mid = lo + lax.shift_right_logical(hi - lo, 1)
        if by_value:
            mid_val = _f32_to_key(0.5 * _key_to_f32(lo) + 0.5 * _key_to_f32(hi))
            mid = jnp.where((mid_val > lo) & (mid_val < hi), mid_val, mid)
        return probe(mid, carry)

    def descend(carry):
        lo, hi, cnt_lo, thr_lo, thr_hi = carry
        v = fold_scores(lambda x, r0: jnp.where(x < thr_hi, x, neg_inf), jnp.max, jnp.maximum, neg_inf)
        cnt = count(lambda x, r0: x >= v)
        key_v = _f32_to_key(v)
        take = (cnt >= topk) & ~is_settled(carry)
        drop = (cnt < topk) & ~is_settled(carry)
        return (jnp.where(take, key_v, lo), jnp.where(take, key_v + 1, jnp.where(drop, key_v, hi)),
                jnp.where(take, cnt, cnt_lo), jnp.where(take, v, thr_lo), jnp.where(drop, v, thr_hi))

    def pending(carry):
        return jnp.max(jnp.where(is_settled(carry), 0.0, 1.0))

    state = lax.fori_loop(0, SEARCH_PASSES_MIN, lambda b, c: halve(True, c),
                          (lo, hi, jnp.full((1, LANES), unknown), _key_to_f32(lo), _key_to_f32(hi)))

    def two_passes(carry):
        b, state, _ = carry
        state = lax.cond(b < VALUE_PASSES, lambda s: halve(True, halve(True, s)),
                         lambda s: lax.cond(b < VALUE_PASSES + 2 * DESCENT_STEPS, descend,
                                            lambda t: halve(False, halve(False, t)), s), state)
        return b + 2, state, pending(state)

    _, (lo, hi, cnt_ge, thr_raw, _), _ = lax.while_loop(
        lambda c: (c[0] < VALUE_PASSES + 2 * DESCENT_STEPS + 32) & (c[2] > 0.0), two_passes,
        (jnp.int32(SEARCH_PASSES_MIN), state, pending(state)))
    enough = thr_raw > f32_lowest
    thr = jnp.where(enough, thr_raw, f32_lowest)

    tie_lane = enough & (cnt_ge > topk) & (t_idx >= topk)

    @pl.when(jnp.max(jnp.where(tie_lane, 1.0, 0.0)) > 0.0)
    def _():
        need = topk - count(lambda x, r0: x > thr)

        def mark_equal(c, carry):
            seen, chunk_of, seen_before = carry
            r0 = pl.multiple_of(c * COUNT_CHUNK, COUNT_CHUNK)
            eq = jnp.where(sc_ref[pl.ds(r0, COUNT_CHUNK), :] == thr, 1.0, 0.0)
            eq_ref[pl.ds(r0, COUNT_CHUNK), :] = eq
            here = jnp.sum(_fold_rows(eq, jnp.sum), axis=0, keepdims=True)
            first_reach = (seen < need) & (seen + here >= need)
            return (seen + here, jnp.where(first_reach, c, chunk_of), jnp.where(first_reach, seen, seen_before))

        zero = jnp.zeros((1, LANES), jnp.float32)
        _, chunk_of, seen_before = lax.fori_loop(0, ncc, mark_equal, (zero, jnp.zeros((1, LANES), jnp.int32), zero))
        need_here = need - seen_before

        eqc_ref[...] = jnp.zeros(eqc_ref.shape, jnp.float32)

        def gather_chunk(c, carry):
            r0 = pl.multiple_of(c * COUNT_CHUNK, COUNT_CHUNK)
            eqc_ref[...] += jnp.where(chunk_of == c, eq_ref[pl.ds(r0, COUNT_CHUNK), :], 0.0)
            return carry

        lax.fori_loop(0, ncc, gather_chunk, 0)

        def idx_step(b, lim):
            cand = lim | (jnp.int32(1) << (COUNT_CHUNK.bit_length() - 2 - b))
            below = jnp.sum(_fold_rows(jnp.where(crow_iota < cand, eqc_ref[...], 0.0), jnp.sum), axis=0, keepdims=True)
            return jnp.where(below < need_here, cand, lim)

        lim = chunk_of * COUNT_CHUNK + lax.fori_loop(0, COUNT_CHUNK.bit_length() - 1, idx_step,
                                                     jnp.zeros((1, LANES), jnp.int32))

        def demote(c, carry):
            r0 = pl.multiple_of(c * COUNT_CHUNK, COUNT_CHUNK)
            x = sc_ref[pl.ds(r0, COUNT_CHUNK), :]
            drop = (eq_ref[pl.ds(r0, COUNT_CHUNK), :] > 0.0) & (r0 + crow_iota > lim) & tie_lane
            sc_ref[pl.ds(r0, COUNT_CHUNK), :] = jnp.where(drop, neg_inf, x)
            return carry

        lax.fori_loop(0, ncc, demote, 0)

    mask_off = float(jnp.finfo(MXU_DTYPE).min)

    def build_mask(c, carry):
        r0 = chunk_start(c)
        selected = sc_ref[pl.ds(r0, KEY_CHUNK), :] >= thr
        mb_ref[pl.ds(r0, KEY_CHUNK), :] = jnp.where(selected, 0.0, mask_off).astype(MXU_DTYPE)
        return carry

    lax.fori_loop(0, nch, build_mask, 0)

    def attn_scores(c, buf):
        sq_ref, mx_ref = buf
        r0 = chunk_start(c)
        kaug = jnp.concatenate([kbd_ref[0, pl.ds(r0, KEY_CHUNK), :], mb_ref[pl.ds(r0, KEY_CHUNK), :]], axis=1)
        for p in range(N_HEADS // 2):
            s = _nt_dot(kaug, pair_q(qbs_ref, p))
            sq_ref[:, p * 2 * BLOCK:(p + 1) * 2 * BLOCK] = s
            for hh in range(2):
                cols = slice((2 * p + hh) * BLOCK, (2 * p + hh + 1) * BLOCK)
                mx_ref[:, cols] = _fold_rows(s[:, hh * BLOCK:(hh + 1) * BLOCK], jnp.max)

    def accumulate(c, buf):
        sq_ref, mx_ref = buf
        m_run = m_ref[...]
        m_new = jnp.maximum(m_run, jnp.max(mx_ref[...], axis=0, keepdims=True))
        m_ref[...] = m_new
        acc_ref[...] = acc_ref[...] * jnp.exp2(m_run - m_new)
        vt = vbt_ref[0, c]
        for p in range(N_HEADS // 2):
            pcols = slice(p * 2 * BLOCK, (p + 1) * 2 * BLOCK)
            e = jnp.exp2(sq_ref[:, pcols] - m_new[:, pcols])
            acc_ref[:, pcols] += jnp.dot(vt, e.astype(MXU_DTYPE), preferred_element_type=jnp.float32)

    acc_ref[...] = jnp.zeros(acc_ref.shape, jnp.float32)
    m_ref[...] = jnp.full(m_ref.shape, f32_lowest, jnp.float32)
    pipeline(attn_scores, accumulate)

    o_t = acc_ref[:HEAD_DIM, :] / acc_ref[HEAD_DIM:HEAD_DIM + 1, :]
    for p in range(N_HEADS // 2):
        pair = jnp.concatenate([o_t[:, (2 * p) * BLOCK:(2 * p + 1) * BLOCK],
                                o_t[:, (2 * p + 1) * BLOCK:(2 * p + 2) * BLOCK]], axis=0)
        out_ref[0, :, p * LANES:(p + 1) * LANES] = pair.T.astype(out_ref.dtype)


def _sparse(qi, qb, wt, kid, kbd, vbt, batch, seq):
    topk = min(TOPK_MAX, seq // 4)
    nq = seq // BLOCK
    qblk = pl.BlockSpec((1, BLOCK, D_HEADS), lambda b, i: (b, i, 0))
    full = pl.BlockSpec((1, seq, LANES), lambda b, i: (b, 0, 0))
    kern = functools.partial(_sparse_kernel, topk=topk)
    return pl.pallas_call(
        kern,
        grid=(batch, nq),
        in_specs=[qblk, qblk,
                  pl.BlockSpec((1, IDX_HEADS, BLOCK), lambda b, i: (b, 0, i)),
                  full, full,
                  pl.BlockSpec((1, seq // KEY_CHUNK, HEAD_DIM + L_ROWS, KEY_CHUNK), lambda b, i: (b, 0, 0, 0))],
        out_specs=qblk,
        out_shape=jax.ShapeDtypeStruct((batch, seq, D_HEADS), MXU_DTYPE),
        scratch_shapes=[pltpu.VMEM((seq, LANES), jnp.float32),
                        pltpu.VMEM((seq, LANES), jnp.float32),
                        pltpu.VMEM((COUNT_CHUNK, LANES), jnp.float32),
                        pltpu.VMEM((seq, LANES), MXU_DTYPE),
                        pltpu.VMEM((N_HEADS * BLOCK, LANES), MXU_DTYPE),
                        pltpu.VMEM((N_HEADS * BLOCK, 2 * LANES), MXU_DTYPE),
                        pltpu.VMEM((HEAD_DIM + L_ROWS, N_HEADS * BLOCK), jnp.float32),
                        pltpu.VMEM((1, N_HEADS * BLOCK), jnp.float32),
                        pltpu.VMEM((KEY_CHUNK, N_HEADS * BLOCK), jnp.float32),
                        pltpu.VMEM((KEY_CHUNK, N_HEADS * BLOCK), jnp.float32),
                        pltpu.VMEM((8, N_HEADS * BLOCK), jnp.float32),
                        pltpu.VMEM((8, N_HEADS * BLOCK), jnp.float32)],
        compiler_params=pltpu.CompilerParams(dimension_semantics=("arbitrary", "arbitrary"),
                                             vmem_limit_bytes=VMEM_LIMIT),
        name="indexer_sparse_attn",
    )(qi, qb, wt, kid, kbd, vbt)


def _merge_kernel(x_ref, g_ref, *refs):
    n_pat = len(DILATED_PATTERNS)
    o_refs, l_refs = refs[:n_pat], refs[n_pat:2 * n_pat]
    yb_ref, wg_ref, wua_ref, wub_ref, wo_ref, x1_ref = refs[2 * n_pat:2 * n_pat + 6]
    slabs = refs[2 * n_pat + 6:]
    x = x_ref[...]
    h = _rms(x, g_ref[...]).astype(MXU_DTYPE)
    tm = x.shape[0]
    n_grp = D_HEADS // LANES

    o_src, l_src, k = [], [], 0
    for (_, d), o_ref, l_ref in zip(DILATED_PATTERNS, o_refs, l_refs):
        if d == 1:
            o_src.append(lambda g, r=o_ref: r[:, g * LANES:(g + 1) * LANES].astype(jnp.float32))
            l_src.append(lambda g, r=l_ref: r[:, g * LANES:(g + 1) * LANES])
            continue
        o_slab, l_slab = slabs[2 * k], slabs[2 * k + 1]
        k += 1
        for r in range(d):
            for g in range(n_grp):
                cols = slice(g * LANES, (g + 1) * LANES)
                o_slab[g, pl.ds(r, tm // d, stride=d), :] = o_ref[0, r, :, cols].astype(jnp.float32)
                l_slab[g, pl.ds(r, tm // d, stride=d), :] = l_ref[0, r, :, cols]
        o_src.append(lambda g, s=o_slab: s[g])
        l_src.append(lambda g, s=l_slab: s[g])

    parts = []
    for g in range(n_grp):
        ls = [f(g) for f in l_src]
        m = functools.reduce(jnp.maximum, ls)
        es = [jnp.exp(l - m) for l in ls]
        num = functools.reduce(jnp.add, [e * f(g) for e, f in zip(es, o_src)])
        parts.append((num / functools.reduce(jnp.add, es)).astype(MXU_DTYPE))
    ya = jnp.concatenate(parts, axis=1)
    ua = jnp.dot(ya, wua_ref[...], preferred_element_type=jnp.float32)
    ub = jnp.dot(yb_ref[...], wub_ref[...], preferred_element_type=jnp.float32)
    ga = jnp.dot(h, wg_ref[:, :D_MODEL], preferred_element_type=jnp.float32)
    gb = jnp.dot(h, wg_ref[:, D_MODEL:], preferred_element_type=jnp.float32)
    merged = jax.nn.sigmoid(ga) * ua + jax.nn.sigmoid(gb) * ub
    x1_ref[...] = x + jnp.dot(merged.astype(MXU_DTYPE), wo_ref[...], preferred_element_type=jnp.float32)


def _merge(x2, g, os_, lses, yb, wg, wua, wub, wo, seq):
    n = x2.shape[0]
    tm = ROW_TILE
    tiles_per_seq = seq // tm
    row = lambda i: (i, 0)
    const = lambda i: (0, 0)
    half = pl.BlockSpec((tm, D_HEADS), row)
    pat_specs = [half if d == 1 else
                 pl.BlockSpec((1, d, tm // d, D_HEADS), lambda i: (i // tiles_per_seq, 0, i % tiles_per_seq, 0))
                 for _, d in DILATED_PATTERNS]
    return pl.pallas_call(
        _merge_kernel,
        grid=(n // tm,),
        in_specs=[pl.BlockSpec((tm, D_MODEL), row), pl.BlockSpec((1, D_MODEL), const)] + pat_specs * 2 + [half] + [
            pl.BlockSpec((D_MODEL, 2 * D_MODEL), const),
            pl.BlockSpec((D_HEADS, D_MODEL), const),
            pl.BlockSpec((D_HEADS, D_MODEL), const),
            pl.BlockSpec((D_MODEL, D_MODEL), const)],
        out_specs=pl.BlockSpec((tm, D_MODEL), row),
        out_shape=jax.ShapeDtypeStruct((n, D_MODEL), jnp.float32),
        scratch_shapes=[pltpu.VMEM((D_HEADS // LANES, tm, LANES), jnp.float32)] * (2 * len(RESIDUE_DILATIONS)),
        compiler_params=pltpu.CompilerParams(dimension_semantics=("arbitrary",), vmem_limit_bytes=VMEM_LIMIT),
        name="mix_gate_out",
    )(x2, g, *os_, *lses, yb, wg, wua, wub, wo)


def _ffn_kernel(x_ref, g_ref, wgate_ref, wup_ref, wdown_ref, gf_ref, out_ref):
    x = x_ref[...]
    h = _rms(x, g_ref[...]).astype(MXU_DTYPE)
    y = x
    for c0 in range(0, D_FF, FF_CHUNK):
        a = jnp.dot(h, wgate_ref[:, c0:c0 + FF_CHUNK], preferred_element_type=jnp.float32)
        u = jnp.dot(h, wup_ref[:, c0:c0 + FF_CHUNK], preferred_element_type=jnp.float32)
        act = (a * jax.nn.sigmoid(a) * u).astype(MXU_DTYPE)
        y = y + jnp.dot(act, wdown_ref[c0:c0 + FF_CHUNK, :], preferred_element_type=jnp.float32)
    out_ref[...] = _rms(y, gf_ref[...])


def _ffn(x1, g, wgate, wup, wdown, gf):
    n = x1.shape[0]
    tm = ROW_TILE
    row = lambda i: (i, 0)
    const = lambda i: (0, 0)
    return pl.pallas_call(
        _ffn_kernel,
        grid=(n // tm,),
        in_specs=[pl.BlockSpec((tm, D_MODEL), row), pl.BlockSpec((1, D_MODEL), const),
                  pl.BlockSpec((D_MODEL, D_FF), const), pl.BlockSpec((D_MODEL, D_FF), const),
                  pl.BlockSpec((D_FF, D_MODEL), const), pl.BlockSpec((1, D_MODEL), const)],
        out_specs=pl.BlockSpec((tm, D_MODEL), row),
        out_shape=jax.ShapeDtypeStruct((n, D_MODEL), jnp.float32),
        compiler_params=pltpu.CompilerParams(dimension_semantics=("arbitrary",), vmem_limit_bytes=VMEM_LIMIT),
        name="ffn_norm",
    )(x1, g, wgate, wup, wdown, gf)


def _rope_tables(seq):
    inv_freq = ROPE_THETA ** (-jnp.arange(HALF, dtype=jnp.float32) / HALF)
    ang = jnp.arange(seq, dtype=jnp.int32).astype(jnp.float32)[:, None] * inv_freq[None, :]
    cos, sin = jnp.cos(ang), jnp.sin(ang)
    return jnp.tile(cos, (1, 4)), jnp.concatenate([-sin, -sin, sin, sin], axis=1)


def kernel(x, norm_mix, w_in, w_up_a, w_up_b, w_out, norm_ffn, w_gate, w_up, w_down, norm_final):
    batch, seq, _ = x.shape
    assert seq % max(d * BLOCK for _, d in DILATED_PATTERNS) == 0 and seq % KEY_CHUNK == 0
    assert all(w // d == BLOCK for w, d in DILATED_PATTERNS)
    n = batch * seq
    bf = MXU_DTYPE
    xf = x.reshape(n, D_MODEL)
    cos_t, sin_t = _rope_tables(seq)
    for layer in range(w_in.shape[0]):
        w = w_in[layer]
        w_pack = jnp.pad(w[:, _packed_columns()], ((0, 0), (0, _P_TOTAL - _P_MISC - HEAD_DIM - IDX_HEADS))).astype(bf)
        w_gates = w[:, _GA:].astype(bf)
        qas, kas, vas, qb, qi, kbd, kid, misc = _project(xf, norm_mix[layer][None], w_pack, cos_t, sin_t, seq)

        dil = []
        for (_, d), q, k, v in zip(DILATED_PATTERNS, qas, kas, vas):
            if d == 1:
                o, lse = _dilated(*(z.reshape(batch, 1, seq, D_HEADS) for z in (q, k, v)), d)
                dil.append((o.reshape(n, D_HEADS), lse.reshape(n, D_HEADS)))
            else:
                dil.append(_dilated(q, k, v, d))

        ones_col = (jnp.arange(L_ROWS) == 0).astype(bf)[None, :]
        vb = jnp.concatenate([misc[:, :HEAD_DIM].astype(bf), jnp.broadcast_to(ones_col, (n, L_ROWS))], axis=1)
        vbt = jnp.swapaxes(vb.reshape(batch, seq // KEY_CHUNK, KEY_CHUNK, HEAD_DIM + L_ROWS), 2, 3)
        wt = jnp.swapaxes(misc[:, HEAD_DIM:HEAD_DIM + IDX_HEADS].reshape(batch, seq, IDX_HEADS), 1, 2)
        r3 = lambda z: z.reshape(batch, seq, z.shape[-1])
        yb = _sparse(r3(qi), r3(qb), wt, r3(kid), r3(kbd), vbt, batch, seq).reshape(n, D_HEADS)

        x1 = _merge(xf, norm_mix[layer][None], [o for o, _ in dil], [l for _, l in dil], yb,
                    w_gates, w_up_a[layer].astype(bf), w_up_b[layer].astype(bf), w_out[layer].astype(bf), seq)
        last = layer == w_in.shape[0] - 1
        assert last, "the final norm is fused into the FFN kernel of the last layer"
        xf = _ffn(x1, norm_ffn[layer][None], w_gate[layer].astype(bf), w_up[layer].astype(bf),
                  w_down[layer].astype(bf), norm_final[None])
    return xf.reshape(batch, seq, D_MODEL)
```

```python
import functools

import numpy as np
import jax
import jax.numpy as jnp
from jax import lax
from jax.experimental import pallas as pl
from jax.experimental.pallas import tpu as pltpu

D_MODEL = 1024
HEAD_DIM = 64
HALF = HEAD_DIM // 2
N_HEADS = 8
D_HEADS = N_HEADS * HEAD_DIM
IDX_HEADS = 8
DILATED_PATTERNS = ((128, 1), (512, 4), (2048, 16))
RESIDUE_DILATIONS = tuple(d for _, d in DILATED_PATTERNS if d > 1)
TOPK_MAX = 256
D_FF = 2816
ROPE_THETA = 10000.0
RMS_EPS = 1e-6
BLOCK = 128
ATTN_SCALE = HEAD_DIM ** -0.5
IDX_SCALE = (HEAD_DIM ** -0.5) * (IDX_HEADS ** -0.5)
LOG2_E = float(np.log2(np.e))

LANES = 128
MIN_NORMAL_KEY = 1 << 23
KEY_CHUNK = 1024
COUNT_CHUNK = 512
COUNT_FOLD = 8
VALUE_PASSES = 14
DESCENT_STEPS = 6
SEARCH_PASSES_MIN = 12
L_ROWS = 8
ROW_TILE = 512
DIL_BLOCKS = 8
FF_CHUNK = 1408
VMEM_LIMIT = 56 * 1024 * 1024
MXU_DTYPE = jnp.bfloat16

_SPLITS = (D_HEADS, D_HEADS, D_HEADS, D_HEADS, HEAD_DIM, HEAD_DIM, IDX_HEADS * HEAD_DIM, HEAD_DIM, IDX_HEADS,
           D_MODEL, D_MODEL)
_OFF = np.concatenate([[0], np.cumsum(_SPLITS)])
(_QA, _KA, _VA, _QB, _KB, _VB, _QI, _KI, _WI, _GA, _GB) = (int(o) for o in _OFF[:-1])

_P_QA, _P_KA, _P_VA, _P_QB, _P_QI = 0, 512, 1024, 1536, 2048
_P_KBD, _P_KID, _P_MISC = 2560, 2688, 2816
_P_TOTAL = 2944


def _pair_perm():
    idx = np.empty(D_HEADS, np.int64)
    for j in range(D_HEADS):
        g, l = divmod(j, LANES)
        quarter, e = divmod(l, HALF)
        head = 2 * g + (quarter % 2)
        idx[j] = head * HEAD_DIM + (quarter // 2) * HALF + e
    return idx


def _dup_perm():
    idx = np.empty(LANES, np.int64)
    for l in range(LANES):
        quarter, e = divmod(l, HALF)
        idx[l] = (quarter // 2) * HALF + e
    return idx


def _packed_columns():
    pp, dp = _pair_perm(), _dup_perm()
    return np.concatenate([
        _QA + pp, _KA + pp, _VA + np.arange(D_HEADS), _QB + pp, _QI + pp,
        _KB + dp, _KI + dp, _VB + np.arange(HEAD_DIM), _WI + np.arange(IDX_HEADS)])


def _rms(x, g):
    ms = jnp.mean(x * x, axis=-1, keepdims=True)
    return x * lax.rsqrt(ms + RMS_EPS) * g


def _nt_dot(a, b):
    return lax.dot_general(a, b, (((1,), (1,)), ((), ())), preferred_element_type=jnp.float32)


def _proj_kernel(x_ref, g_ref, w_ref, cos_ref, sin_ref, *refs):
    n_lay = 1 + len(RESIDUE_DILATIONS)
    qa_refs, ka_refs, va_refs = refs[:n_lay], refs[n_lay:2 * n_lay], refs[2 * n_lay:3 * n_lay]
    qb_ref, qi_ref, kbd_ref, kid_ref, misc_ref, h_ref, slab_ref = refs[3 * n_lay:]
    h_ref[...] = _rms(x_ref[...], g_ref[...]).astype(MXU_DTYPE)
    cos = cos_ref[...]
    sin = sin_ref[...]

    def mm(c0, width):
        return jnp.dot(h_ref[...], w_ref[:, c0:c0 + width], preferred_element_type=jnp.float32)

    def rope(z):
        parts = []
        for g in range(z.shape[1] // LANES):
            zg = z[:, g * LANES:(g + 1) * LANES]
            parts.append(zg * cos + pltpu.roll(zg, 2 * HALF, axis=1) * sin)
        return parts[0] if len(parts) == 1 else jnp.concatenate(parts, axis=1)

    def emit(y, out_refs):
        out_refs[0][...] = y.astype(MXU_DTYPE)
        for g in range(D_HEADS // LANES):
            slab_ref[g] = y[:, g * LANES:(g + 1) * LANES]
        for d, ref in zip(RESIDUE_DILATIONS, out_refs[1:]):
            rows = y.shape[0] // d
            for r in range(d):
                for g in range(D_HEADS // LANES):
                    ref[0, r, :, g * LANES:(g + 1) * LANES] = (
                        slab_ref[g, pl.ds(r, rows, stride=d), :].astype(MXU_DTYPE))

    emit(rope(mm(_P_QA, D_HEADS)) * ATTN_SCALE, qa_refs)
    emit(rope(mm(_P_KA, D_HEADS)), ka_refs)
    emit(mm(_P_VA, D_HEADS), va_refs)
    qb_ref[...] = (rope(mm(_P_QB, D_HEADS)) * (ATTN_SCALE * LOG2_E)).astype(MXU_DTYPE)
    qi_ref[...] = rope(mm(_P_QI, D_HEADS)).astype(MXU_DTYPE)
    kbd_ref[...] = rope(mm(_P_KBD, LANES)).astype(MXU_DTYPE)
    kid_ref[...] = rope(mm(_P_KID, LANES)).astype(MXU_DTYPE)
    misc_ref[...] = mm(_P_MISC, LANES)


def _project(x2, g, w_pack, cos_t, sin_t, seq):
    n = x2.shape[0]
    tm = ROW_TILE
    tiles_per_seq = seq // tm
    row = lambda i: (i, 0)
    const = lambda i: (0, 0)
    pos = lambda i: (i % tiles_per_seq, 0)
    batch = n // seq
    wide = jax.ShapeDtypeStruct((n, D_HEADS), MXU_DTYPE)
    narrow = jax.ShapeDtypeStruct((n, LANES), MXU_DTYPE)
    wide_spec = pl.BlockSpec((tm, D_HEADS), row)
    lay_shapes = [wide] + [jax.ShapeDtypeStruct((batch, d, seq // d, D_HEADS), MXU_DTYPE) for d in RESIDUE_DILATIONS]
    lay_specs = [wide_spec] + [
        pl.BlockSpec((1, d, tm // d, D_HEADS), lambda i: (i // tiles_per_seq, 0, i % tiles_per_seq, 0))
        for d in RESIDUE_DILATIONS]
    n_lay = len(lay_shapes)
    outs = pl.pallas_call(
        _proj_kernel,
        grid=(n // tm,),
        in_specs=[
            pl.BlockSpec((tm, D_MODEL), row),
            pl.BlockSpec((1, D_MODEL), const),
            pl.BlockSpec((D_MODEL, _P_TOTAL), const),
            pl.BlockSpec((tm, LANES), pos),
            pl.BlockSpec((tm, LANES), pos),
        ],
        out_specs=lay_specs * 3 + [wide_spec] * 2 + [pl.BlockSpec((tm, LANES), row)] * 3,
        out_shape=lay_shapes * 3 + [wide] * 2 + [narrow, narrow, jax.ShapeDtypeStruct((n, LANES), jnp.float32)],
        scratch_shapes=[pltpu.VMEM((tm, D_MODEL), MXU_DTYPE),
                        pltpu.VMEM((D_HEADS // LANES, tm, LANES), jnp.float32)],
        compiler_params=pltpu.CompilerParams(dimension_semantics=("arbitrary",), vmem_limit_bytes=VMEM_LIMIT),
        name="proj_rope",
    )(x2, g, w_pack, cos_t, sin_t)
    return (outs[:n_lay], outs[n_lay:2 * n_lay], outs[2 * n_lay:3 * n_lay]) + tuple(outs[3 * n_lay:])


def _dil_kernel(q_ref, k_ref, v_ref, kp_ref, vp_ref, o_ref, lse_ref, *, n_blocks):
    n = pl.program_id(2)
    kj = lax.broadcasted_iota(jnp.int32, (2 * BLOCK, LANES), 0)
    qi = lax.broadcasted_iota(jnp.int32, (2 * BLOCK, LANES), 1)
    band = (kj >= qi) & (kj <= qi + BLOCK)
    lane = lax.broadcasted_iota(jnp.int32, (1, LANES), 1)
    pairs = range(N_HEADS // 2)
    col = lambda p: slice(p * LANES, (p + 1) * LANES)

    for j in range(n_blocks):
        rows = slice(j * BLOCK, (j + 1) * BLOCK)
        if j == 0:
            k_prev = lambda p: kp_ref[0, 0, :, col(p)]
            v_prev = lambda p: vp_ref[0, 0, :, col(p)]
            bias = jnp.where(band & ((kj >= BLOCK) | (n > 0)), 0.0, -jnp.inf)
        else:
            prev_rows = slice((j - 1) * BLOCK, j * BLOCK)
            k_prev = lambda p, r=prev_rows: k_ref[0, 0, r, col(p)]
            v_prev = lambda p, r=prev_rows: v_ref[0, 0, r, col(p)]
            bias = jnp.where(band, 0.0, -jnp.inf)

        scores = []
        for p in pairs:
            qp = q_ref[0, 0, rows, col(p)]
            qs = jnp.concatenate([jnp.where(((lane // HALF) % 2) == hh, qp, jnp.zeros_like(qp))
                                  for hh in range(2)], axis=0)
            k2 = jnp.concatenate([k_prev(p), k_ref[0, 0, rows, col(p)]], axis=0)
            scores.append(_nt_dot(k2, qs))

        probs, dens, lses = [], [], []
        for p in pairs:
            for hh in range(2):
                sh = scores[p][:, hh * BLOCK:(hh + 1) * BLOCK] + bias
                m = jnp.max(sh, axis=0, keepdims=True)
                e = jnp.exp(sh - m)
                den = jnp.sum(e, axis=0, keepdims=True)
                probs.append(e.astype(MXU_DTYPE))
                dens.append(den)
                lses.append(m + jnp.log(den))

        for p in pairs:
            v2 = jnp.concatenate([v_prev(p), v_ref[0, 0, rows, col(p)]], axis=0)
            o_t = lax.dot_general(v2, jnp.concatenate(probs[2 * p:2 * p + 2], axis=1), (((0,), (0,)), ((), ())),
                                  preferred_element_type=jnp.float32)
            tile = jnp.concatenate([o_t[:HEAD_DIM, :BLOCK] / dens[2 * p],
                                    o_t[HEAD_DIM:, BLOCK:] / dens[2 * p + 1]], axis=0)
            o_ref[0, 0, rows, col(p)] = tile.T.astype(o_ref.dtype)
            lse_tile = jnp.concatenate([jnp.broadcast_to(l, (HEAD_DIM, BLOCK)) for l in lses[2 * p:2 * p + 2]],
                                       axis=0)
            lse_ref[0, 0, rows, col(p)] = lse_tile.T


def _dilated(q, k, v, dilation):
    batch, _, m_len, _ = q.shape
    n_blocks = min(DIL_BLOCKS, m_len // BLOCK)
    tile_rows = n_blocks * BLOCK
    cur = lambda b, r, n: (b, r, n, 0)
    prev = lambda b, r, n: (b, r, jnp.maximum(n * n_blocks - 1, 0), 0)
    tile = (1, 1, tile_rows, D_HEADS)
    blk = (1, 1, BLOCK, D_HEADS)
    return pl.pallas_call(
        functools.partial(_dil_kernel, n_blocks=n_blocks),
        grid=(batch, dilation, m_len // tile_rows),
        in_specs=[pl.BlockSpec(tile, cur), pl.BlockSpec(tile, cur), pl.BlockSpec(tile, cur),
                  pl.BlockSpec(blk, prev), pl.BlockSpec(blk, prev)],
        out_specs=[pl.BlockSpec(tile, cur), pl.BlockSpec(tile, cur)],
        out_shape=[jax.ShapeDtypeStruct(q.shape, MXU_DTYPE), jax.ShapeDtypeStruct(q.shape, jnp.float32)],
        compiler_params=pltpu.CompilerParams(dimension_semantics=("arbitrary",) * 3),
        name=f"dilated_d{dilation}",
    )(q, k, v, k, v)


def _key_to_f32(key):
    bits = key ^ ((key >> 31) & jnp.int32(0x7FFFFFFF))
    return lax.bitcast_convert_type(bits, jnp.float32)


def _f32_to_key(x):
    bits = lax.bitcast_convert_type(x, jnp.int32)
    return bits ^ ((bits >> 31) & jnp.int32(0x7FFFFFFF))


def _fold_rows(x, op):
    rows = x.shape[0]
    y = op(x.reshape(rows // 64, 64, LANES), axis=0)
    return op(y.reshape(8, 8, LANES), axis=0)


def _sparse_kernel(qi_ref, qb_ref, wt_ref, kid_ref, kbd_ref, vbt_ref, out_ref,
                   sc_ref, eq_ref, eqc_ref, mb_ref, qis_ref, qbs_ref, acc_ref, m_ref, sqa_ref, sqb_ref, mxa_ref,
                   mxb_ref, *, topk):
    i = pl.program_id(1)
    nch = i // (KEY_CHUNK // BLOCK) + 1
    ncc = i // (COUNT_CHUNK // BLOCK) + 1
    lane = lax.broadcasted_iota(jnp.int32, (1, LANES), 1)
    t_idx = i * BLOCK + lane
    row_iota = lax.broadcasted_iota(jnp.int32, (KEY_CHUNK, LANES), 0)
    crow_iota = lax.broadcasted_iota(jnp.int32, (COUNT_CHUNK, LANES), 0)
    neg_inf = jnp.float32(-jnp.inf)
    f32_lowest = jnp.float32(jnp.finfo(jnp.float32).min)

    eye = jnp.where(lax.broadcasted_iota(jnp.int32, (BLOCK, LANES), 0) == lane, 1.0, 0.0).astype(MXU_DTYPE)
    for h in range(N_HEADS):
        cols = slice((h // 2) * LANES, (h // 2 + 1) * LANES)
        head_lanes = ((lane // HALF) % 2) == (h % 2)
        rows = slice(h * BLOCK, (h + 1) * BLOCK)
        qis_ref[rows, :] = jnp.where(head_lanes, qi_ref[0, :, cols], jnp.zeros((), MXU_DTYPE))
        qbs_ref[rows, :LANES] = jnp.where(head_lanes, qb_ref[0, :, cols], jnp.zeros((), MXU_DTYPE))
        qbs_ref[rows, LANES:] = eye

    def chunk_start(c):
        return pl.multiple_of(c * KEY_CHUNK, KEY_CHUNK)

    def pair_q(ref, p):
        return ref[p * 2 * BLOCK:(p + 1) * 2 * BLOCK, :]

    buf_a, buf_b = (sqa_ref, mxa_ref), (sqb_ref, mxb_ref)

    def pipeline(produce, consume):
        produce(0, buf_a)

        def two_chunks(j, carry):
            c = 2 * j
            produce(c + 1, buf_b)
            consume(c, buf_a)
            produce(c + 2, buf_a)
            consume(c + 1, buf_b)
            return carry

        lax.fori_loop(0, (nch - 1) // 2, two_chunks, 0)
        last = nch - 1

        @pl.when(last % 2 == 1)
        def _():
            produce(last, buf_b)
            consume(last - 1, buf_a)
            consume(last, buf_b)

        @pl.when(last % 2 == 0)
        def _():
            consume(last, buf_a)

    def score_chunk(c, carry):
        r0 = chunk_start(c)
        kc = kid_ref[0, pl.ds(r0, KEY_CHUNK), :]
        acc = jnp.zeros((KEY_CHUNK, LANES), jnp.float32)
        for p in range(N_HEADS // 2):
            d = _nt_dot(kc, pair_q(qis_ref, p))
            for hh in range(2):
                h = 2 * p + hh
                acc = acc + jnp.maximum(d[:, hh * BLOCK:(hh + 1) * BLOCK], 0.0) * wt_ref[0, h:h + 1, :]
        admissible = r0 + row_iota <= t_idx
        scores = acc * IDX_SCALE
        masked = jnp.where(admissible, scores, neg_inf)
        sc_ref[pl.ds(r0, KEY_CHUNK), :] = masked
        top8, bot8 = carry
        return (jnp.maximum(top8, _fold_rows(masked, jnp.max)),
                jnp.minimum(bot8, _fold_rows(jnp.where(admissible, scores, -neg_inf), jnp.min)))

    def for_chunks(n_chunks, body, init):
        carry = lax.fori_loop(0, n_chunks // 4,
                              lambda j, a: body(4 * j + 3, body(4 * j + 2, body(4 * j + 1, body(4 * j, a)))), init)
        done = (n_chunks // 4) * 4
        carry = lax.cond((n_chunks & 2) != 0, lambda a: body(done + 1, body(done, a)), lambda a: a, carry)
        done = done + (n_chunks & 2)
        return lax.cond((n_chunks & 1) != 0, lambda a: body(done, a), lambda a: a, carry)

    top8, bot8 = for_chunks(nch, score_chunk, (jnp.full((8, LANES), neg_inf, jnp.float32),
                                               jnp.full((8, LANES), -neg_inf, jnp.float32)))
    top = jnp.max(top8, axis=0, keepdims=True)
    bot = jnp.min(bot8, axis=0, keepdims=True)

    def fold_scores(value, op, pairwise, start, src_ref=sc_ref):
        def body(c, acc):
            r0 = pl.multiple_of(c * COUNT_CHUNK, COUNT_CHUNK)
            val = value(src_ref[pl.ds(r0, COUNT_CHUNK), :], r0)
            return pairwise(acc, op(val.reshape(COUNT_FOLD, COUNT_CHUNK // COUNT_FOLD, LANES), axis=0))

        acc = for_chunks(ncc, body, jnp.full((COUNT_CHUNK // COUNT_FOLD, LANES), start, jnp.float32))
        return op(acc, axis=0, keepdims=True)

    def count(pred):
        return fold_scores(lambda x, r0: jnp.where(pred(x, r0), 1.0, 0.0), jnp.sum, jnp.add, 0.0)

    key_top = _f32_to_key(top)
    lo = _f32_to_key(bot) - jnp.int32(1 << 23)
    hi = key_top + 1
    hi = jnp.where((hi >= -MIN_NORMAL_KEY) & (hi < MIN_NORMAL_KEY), jnp.int32(MIN_NORMAL_KEY), hi)
    unknown = jnp.float32(2 * sc_ref.shape[0])

    def is_settled(carry):
        lo, hi, cnt_lo = carry[:3]
        return (cnt_lo == topk) | (hi - lo == 1) | (t_idx < topk)

    def probe(mid, carry):
        lo, hi, cnt_lo, thr_lo, thr_hi = carry
        thr_c = _key_to_f32(mid)
        cnt = count(lambda x, r0: x >= thr_c)
        take = (cnt >= topk) & ~is_settled(carry)
        drop = (cnt < topk) & ~is_settled(carry)
        return (jnp.where(take, mid, lo), jnp.where(drop, mid, hi), jnp.where(take, cnt, cnt_lo),
                jnp.where(take, thr_c, thr_lo), jnp.where(drop, thr_c, thr_hi))

    def halve(by_value, carry):
        lo, hi = carry[:2]
        mid = lo + lax.shift_right_logical(hi - lo, 1)
        if by_value:
            mid_val = _f32_to_key(0.5 * _key_to_f32(lo) + 0.5 * _key_to_f32(hi))
            mid = jnp.where((mid_val > lo) & (mid_val < hi), mid_val, mid)
        return probe(mid, carry)

    def descend(carry):
        lo, hi, cnt_lo, thr_lo, thr_hi = carry
        v = fold_scores(lambda x, r0: jnp.where(x < thr_hi, x, neg_inf), jnp.max, jnp.maximum, neg_inf)
        cnt = count(lambda x, r0: x >= v)
        key_v = _f32_to_key(v)
        take = (cnt >= topk) & ~is_settled(carry)
        drop = (cnt < topk) & ~is_settled(carry)
        return (jnp.where(take, key_v, lo), jnp.where(take, key_v + 1, jnp.where(drop, key_v, hi)),
                jnp.where(take, cnt, cnt_lo), jnp.where(take, v, thr_lo), jnp.where(drop, v, thr_hi))

    def pending(carry):
        return jnp.max(jnp.where(is_settled(carry), 0.0, 1.0))

    state = lax.fori_loop(0, SEARCH_PASSES_MIN, lambda b, c: halve(True, c),
                          (lo, hi, jnp.full((1, LANES), unknown), _key_to_f32(lo), _key_to_f32(hi)))

    def two_passes(carry):
        b, state, _ = carry
        state = lax.cond(b < VALUE_PASSES, lambda s: halve(True, halve(True, s)),
                         lambda s: lax.cond(b < VALUE_PASSES + 2 * DESCENT_STEPS, descend,
                                            lambda t: halve(False, halve(False, t)), s), state)
        return b + 2, state, pending(state)

    _, (lo, hi, cnt_ge, thr_raw, _), _ = lax.while_loop(
        lambda c: (c[0] < VALUE_PASSES + 2 * DESCENT_STEPS + 32) & (c[2] > 0.0), two_passes,
        (jnp.int32(SEARCH_PASSES_MIN), state, pending(state)))
    enough = thr_raw > f32_lowest
    thr = jnp.where(enough, thr_raw, f32_lowest)

    tie_lane = enough & (cnt_ge > topk) & (t_idx >= topk)

    @pl.when(jnp.max(jnp.where(tie_lane, 1.0, 0.0)) > 0.0)
    def _():
        need = topk - count(lambda x, r0: x > thr)

        def mark_equal(c, carry):
            seen, chunk_of, seen_before = carry
            r0 = pl.multiple_of(c * COUNT_CHUNK, COUNT_CHUNK)
            eq = jnp.where(sc_ref[pl.ds(r0, COUNT_CHUNK), :] == thr, 1.0, 0.0)
            eq_ref[pl.ds(r0, COUNT_CHUNK), :] = eq
            here = jnp.sum(_fold_rows(eq, jnp.sum), axis=0, keepdims=True)
            first_reach = (seen < need) & (seen + here >= need)
            return (seen + here, jnp.where(first_reach, c, chunk_of), jnp.where(first_reach, seen, seen_before))

        zero = jnp.zeros((1, LANES), jnp.float32)
        _, chunk_of, seen_before = lax.fori_loop(0, ncc, mark_equal, (zero, jnp.zeros((1, LANES), jnp.int32), zero))
        need_here = need - seen_before

        eqc_ref[...] = jnp.zeros(eqc_ref.shape, jnp.float32)

        def gather_chunk(c, carry):
            r0 = pl.multiple_of(c * COUNT_CHUNK, COUNT_CHUNK)
            eqc_ref[...] += jnp.where(chunk_of == c, eq_ref[pl.ds(r0, COUNT_CHUNK), :], 0.0)
            return carry

        lax.fori_loop(0, ncc, gather_chunk, 0)

        def idx_step(b, lim):
            cand = lim | (jnp.int32(1) << (COUNT_CHUNK.bit_length() - 2 - b))
            below = jnp.sum(_fold_rows(jnp.where(crow_iota < cand, eqc_ref[...], 0.0), jnp.sum), axis=0, keepdims=True)
            return jnp.where(below < need_here, cand, lim)

        lim = chunk_of * COUNT_CHUNK + lax.fori_loop(0, COUNT_CHUNK.bit_length() - 1, idx_step,
                                                     jnp.zeros((1, LANES), jnp.int32))

        def demote(c, carry):
            r0 = pl.multiple_of(c * COUNT_CHUNK, COUNT_CHUNK)
            x = sc_ref[pl.ds(r0, COUNT_CHUNK), :]
            drop = (eq_ref[pl.ds(r0, COUNT_CHUNK), :] > 0.0) & (r0 + crow_iota > lim) & tie_lane
            sc_ref[pl.ds(r0, COUNT_CHUNK), :] = jnp.where(drop, neg_inf, x)
            return carry

        lax.fori_loop(0, ncc, demote, 0)

    mask_off = float(jnp.finfo(MXU_DTYPE).min)

    def build_mask(c, carry):
        r0 = chunk_start(c)
        selected = sc_ref[pl.ds(r0, KEY_CHUNK), :] >= thr
        mb_ref[pl.ds(r0, KEY_CHUNK), :] = jnp.where(selected, 0.0, mask_off).astype(MXU_DTYPE)
        return carry

    lax.fori_loop(0, nch, build_mask, 0)

    def attn_scores(c, buf):
        sq_ref, mx_ref = buf
        r0 = chunk_start(c)
        kaug = jnp.concatenate([kbd_ref[0, pl.ds(r0, KEY_CHUNK), :], mb_ref[pl.ds(r0, KEY_CHUNK), :]], axis=1)
        for p in range(N_HEADS // 2):
            s = _nt_dot(kaug, pair_q(qbs_ref, p))
            sq_ref[:, p * 2 * BLOCK:(p + 1) * 2 * BLOCK] = s
            for hh in range(2):
                cols = slice((2 * p + hh) * BLOCK, (2 * p + hh + 1) * BLOCK)
                mx_ref[:, cols] = _fold_rows(s[:, hh * BLOCK:(hh + 1) * BLOCK], jnp.max)

    def accumulate(c, buf):
        sq_ref, mx_ref = buf
        m_run = m_ref[...]
        m_new = jnp.maximum(m_run, jnp.max(mx_ref[...], axis=0, keepdims=True))
        m_ref[...] = m_new
        acc_ref[...] = acc_ref[...] * jnp.exp2(m_run - m_new)
        vt = vbt_ref[0, c]
        for p in range(N_HEADS // 2):
            pcols = slice(p * 2 * BLOCK, (p + 1) * 2 * BLOCK)
            e = jnp.exp2(sq_ref[:, pcols] - m_new[:, pcols])
            acc_ref[:, pcols] += jnp.dot(vt, e.astype(MXU_DTYPE), preferred_element_type=jnp.float32)

    acc_ref[...] = jnp.zeros(acc_ref.shape, jnp.float32)
    m_ref[...] = jnp.full(m_ref.shape, f32_lowest, jnp.float32)
    pipeline(attn_scores, accumulate)

    o_t = acc_ref[:HEAD_DIM, :] / acc_ref[HEAD_DIM:HEAD_DIM + 1, :]
    for p in range(N_HEADS // 2):
        pair = jnp.concatenate([o_t[:, (2 * p) * BLOCK:(2 * p + 1) * BLOCK],
                                o_t[:, (2 * p + 1) * BLOCK:(2 * p + 2) * BLOCK]], axis=0)
        out_ref[0, :, p * LANES:(p + 1) * LANES] = pair.T.astype(out_ref.dtype)


def _sparse(qi, qb, wt, kid, kbd, vbt, batch, seq):
    topk = min(TOPK_MAX, seq // 4)
    nq = seq // BLOCK
    qblk = pl.BlockSpec((1, BLOCK, D_HEADS), lambda b, i: (b, i, 0))
    full = pl.BlockSpec((1, seq, LANES), lambda b, i: (b, 0, 0))
    kern = functools.partial(_sparse_kernel, topk=topk)
    return pl.pallas_call(
        kern,
        grid=(batch, nq),
        in_specs=[qblk, qblk,
                  pl.BlockSpec((1, IDX_HEADS, BLOCK), lambda b, i: (b, 0, i)),
                  full, full,
                  pl.BlockSpec((1, seq // KEY_CHUNK, HEAD_DIM + L_ROWS, KEY_CHUNK), lambda b, i: (b, 0, 0, 0))],
        out_specs=qblk,
        out_shape=jax.ShapeDtypeStruct((batch, seq, D_HEADS), MXU_DTYPE),
        scratch_shapes=[pltpu.VMEM((seq, LANES), jnp.float32),
                        pltpu.VMEM((seq, LANES), jnp.float32),
                        pltpu.VMEM((COUNT_CHUNK, LANES), jnp.float32),
                        pltpu.VMEM((seq, LANES), MXU_DTYPE),
                        pltpu.VMEM((N_HEADS * BLOCK, LANES), MXU_DTYPE),
                        pltpu.VMEM((N_HEADS * BLOCK, 2 * LANES), MXU_DTYPE),
                        pltpu.VMEM((HEAD_DIM + L_ROWS, N_HEADS * BLOCK), jnp.float32),
                        pltpu.VMEM((1, N_HEADS * BLOCK), jnp.float32),
                        pltpu.VMEM((KEY_CHUNK, N_HEADS * BLOCK), jnp.float32),
                        pltpu.VMEM((KEY_CHUNK, N_HEADS * BLOCK), jnp.float32),
                        pltpu.VMEM((8, N_HEADS * BLOCK), jnp.float32),
                        pltpu.VMEM((8, N_HEADS * BLOCK), jnp.float32)],
        compiler_params=pltpu.CompilerParams(dimension_semantics=("arbitrary", "arbitrary"),
                                             vmem_limit_bytes=VMEM_LIMIT),
        name="indexer_sparse_attn",
    )(qi, qb, wt, kid, kbd, vbt)


def _merge_kernel(x_ref, g_ref, *refs):
    n_pat = len(DILATED_PATTERNS)
    o_refs, l_refs = refs[:n_pat], refs[n_pat:2 * n_pat]
    yb_ref, wg_ref, wua_ref, wub_ref, wo_ref, x1_ref = refs[2 * n_pat:2 * n_pat + 6]
    slabs = refs[2 * n_pat + 6:]
    x = x_ref[...]
    h = _rms(x, g_ref[...]).astype(MXU_DTYPE)
    tm = x.shape[0]
    n_grp = D_HEADS // LANES

    o_src, l_src, k = [], [], 0
    for (_, d), o_ref, l_ref in zip(DILATED_PATTERNS, o_refs, l_refs):
        if d == 1:
            o_src.append(lambda g, r=o_ref: r[:, g * LANES:(g + 1) * LANES].astype(jnp.float32))
            l_src.append(lambda g, r=l_ref: r[:, g * LANES:(g + 1) * LANES])
            continue
        o_slab, l_slab = slabs[2 * k], slabs[2 * k + 1]
        k += 1
        for r in range(d):
            for g in range(n_grp):
                cols = slice(g * LANES, (g + 1) * LANES)
                o_slab[g, pl.ds(r, tm // d, stride=d), :] = o_ref[0, r, :, cols].astype(jnp.float32)
                l_slab[g, pl.ds(r, tm // d, stride=d), :] = l_ref[0, r, :, cols]
        o_src.append(lambda g, s=o_slab: s[g])
        l_src.append(lambda g, s=l_slab: s[g])

    parts = []
    for g in range(n_grp):
        ls = [f(g) for f in l_src]
        m = functools.reduce(jnp.maximum, ls)
        es = [jnp.exp(l - m) for l in ls]
        num = functools.reduce(jnp.add, [e * f(g) for e, f in zip(es, o_src)])
        parts.append((num / functools.reduce(jnp.add, es)).astype(MXU_DTYPE))
    ya = jnp.concatenate(parts, axis=1)
    ua = jnp.dot(ya, wua_ref[...], preferred_element_type=jnp.float32)
    ub = jnp.dot(yb_ref[...], wub_ref[...], preferred_element_type=jnp.float32)
    ga = jnp.dot(h, wg_ref[:, :D_MODEL], preferred_element_type=jnp.float32)
    gb = jnp.dot(h, wg_ref[:, D_MODEL:], preferred_element_type=jnp.float32)
    merged = jax.nn.sigmoid(ga) * ua + jax.nn.sigmoid(gb) * ub
    x1_ref[...] = x + jnp.dot(merged.astype(MXU_DTYPE), wo_ref[...], preferred_element_type=jnp.float32)


def _merge(x2, g, os_, lses, yb, wg, wua, wub, wo, seq):
    n = x2.shape[0]
    tm = ROW_TILE
    tiles_per_seq = seq // tm
    row = lambda i: (i, 0)
    const = lambda i: (0, 0)
    half = pl.BlockSpec((tm, D_HEADS), row)
    pat_specs = [half if d == 1 else
                 pl.BlockSpec((1, d, tm // d, D_HEADS), lambda i: (i // tiles_per_seq, 0, i % tiles_per_seq, 0))
                 for _, d in DILATED_PATTERNS]
    return pl.pallas_call(
        _merge_kernel,
        grid=(n // tm,),
        in_specs=[pl.BlockSpec((tm, D_MODEL), row), pl.BlockSpec((1, D_MODEL), const)] + pat_specs * 2 + [half] + [
            pl.BlockSpec((D_MODEL, 2 * D_MODEL), const),
            pl.BlockSpec((D_HEADS, D_MODEL), const),
            pl.BlockSpec((D_HEADS, D_MODEL), const),
            pl.BlockSpec((D_MODEL, D_MODEL), const)],
        out_specs=pl.BlockSpec((tm, D_MODEL), row),
        out_shape=jax.ShapeDtypeStruct((n, D_MODEL), jnp.float32),
        scratch_shapes=[pltpu.VMEM((D_HEADS // LANES, tm, LANES), jnp.float32)] * (2 * len(RESIDUE_DILATIONS)),
        compiler_params=pltpu.CompilerParams(dimension_semantics=("arbitrary",), vmem_limit_bytes=VMEM_LIMIT),
        name="mix_gate_out",
    )(x2, g, *os_, *lses, yb, wg, wua, wub, wo)


def _ffn_kernel(x_ref, g_ref, wgate_ref, wup_ref, wdown_ref, gf_ref, out_ref):
    x = x_ref[...]
    h = _rms(x, g_ref[...]).astype(MXU_DTYPE)
    y = x
    for c0 in range(0, D_FF, FF_CHUNK):
        a = jnp.dot(h, wgate_ref[:, c0:c0 + FF_CHUNK], preferred_element_type=jnp.float32)
        u = jnp.dot(h, wup_ref[:, c0:c0 + FF_CHUNK], preferred_element_type=jnp.float32)
        act = (a * jax.nn.sigmoid(a) * u).astype(MXU_DTYPE)
        y = y + jnp.dot(act, wdown_ref[c0:c0 + FF_CHUNK, :], preferred_element_type=jnp.float32)
    out_ref[...] = _rms(y, gf_ref[...])


def _ffn(x1, g, wgate, wup, wdown, gf):
    n = x1.shape[0]
    tm = ROW_TILE
    row = lambda i: (i, 0)
    const = lambda i: (0, 0)
    return pl.pallas_call(
        _ffn_kernel,
        grid=(n // tm,),
        in_specs=[pl.BlockSpec((tm, D_MODEL), row), pl.BlockSpec((1, D_MODEL), const),
                  pl.BlockSpec((D_MODEL, D_FF), const), pl.BlockSpec((D_MODEL, D_FF), const),
                  pl.BlockSpec((D_FF, D_MODEL), const), pl.BlockSpec((1, D_MODEL), const)],
        out_specs=pl.BlockSpec((tm, D_MODEL), row),
        out_shape=jax.ShapeDtypeStruct((n, D_MODEL), jnp.float32),
        compiler_params=pltpu.CompilerParams(dimension_semantics=("arbitrary",), vmem_limit_bytes=VMEM_LIMIT),
        name="ffn_norm",
    )(x1, g, wgate, wup, wdown, gf)


def _rope_tables(seq):
    inv_freq = ROPE_THETA ** (-jnp.arange(HALF, dtype=jnp.float32) / HALF)
    ang = jnp.arange(seq, dtype=jnp.int32).astype(jnp.float32)[:, None] * inv_freq[None, :]
    cos, sin = jnp.cos(ang), jnp.sin(ang)
    return jnp.tile(cos, (1, 4)), jnp.concatenate([-sin, -sin, sin, sin], axis=1)


def kernel(x, norm_mix, w_in, w_up_a, w_up_b, w_out, norm_ffn, w_gate, w_up, w_down, norm_final):
    batch, seq, _ = x.shape
    assert seq % max(d * BLOCK for _, d in DILATED_PATTERNS) == 0 and seq % KEY_CHUNK == 0
    assert all(w // d == BLOCK for w, d in DILATED_PATTERNS)
    n = batch * seq
    bf = MXU_DTYPE
    xf = x.reshape(n, D_MODEL)
    cos_t, sin_t = _rope_tables(seq)
    for layer in range(w_in.shape[0]):
        w = w_in[layer]
        w_pack = jnp.pad(w[:, _packed_columns()], ((0, 0), (0, _P_TOTAL - _P_MISC - HEAD_DIM - IDX_HEADS))).astype(bf)
        w_gates = w[:, _GA:].astype(bf)
        qas, kas, vas, qb, qi, kbd, kid, misc = _project(xf, norm_mix[layer][None], w_pack, cos_t, sin_t, seq)

        dil = []
        for (_, d), q, k, v in zip(DILATED_PATTERNS, qas, kas, vas):
            if d == 1:
                o, lse = _dilated(*(z.reshape(batch, 1, seq, D_HEADS) for z in (q, k, v)), d)
                dil.append((o.reshape(n, D_HEADS), lse.reshape(n, D_HEADS)))
            else:
                dil.append(_dilated(q, k, v, d))

        ones_col = (jnp.arange(L_ROWS) == 0).astype(bf)[None, :]
        vb = jnp.concatenate([misc[:, :HEAD_DIM].astype(bf), jnp.broadcast_to(ones_col, (n, L_ROWS))], axis=1)
        vbt = jnp.swapaxes(vb.reshape(batch, seq // KEY_CHUNK, KEY_CHUNK, HEAD_DIM + L_ROWS), 2, 3)
        wt = jnp.swapaxes(misc[:, HEAD_DIM:HEAD_DIM + IDX_HEADS].reshape(batch, seq, IDX_HEADS), 1, 2)
        r3 = lambda z: z.reshape(batch, seq, z.shape[-1])
        yb = _sparse(r3(qi), r3(qb), wt, r3(kid), r3(kbd), vbt, batch, seq).reshape(n, D_HEADS)

        x1 = _merge(xf, norm_mix[layer][None], [o for o, _ in dil], [l for _, l in dil], yb,
                    w_gates, w_up_a[layer].astype(bf), w_up_b[layer].astype(bf), w_out[layer].astype(bf), seq)
        last = layer == w_in.shape[0] - 1
        assert last, "the final norm is fused into the FFN kernel of the last layer"
        xf = _ffn(x1, norm_ffn[layer][None], w_gate[layer].astype(bf), w_up[layer].astype(bf),
                  w_down[layer].astype(bf), norm_final[None])
    return xf.reshape(batch, seq, D_MODEL)
```

```python
import functools

import numpy as np
import jax
import jax.numpy as jnp
from jax import lax
from jax.experimental import pallas as pl
from jax.experimental.pallas import tpu as pltpu

D_MODEL = 1024
HEAD_DIM = 64
HALF = HEAD_DIM // 2
N_HEADS = 8
D_HEADS = N_HEADS * HEAD_DIM
IDX_HEADS = 8
DILATED_PATTERNS = ((128, 1), (512, 4), (2048, 16))
RESIDUE_DILATIONS = tuple(d for _, d in DILATED_PATTERNS if d > 1)
TOPK_MAX = 256
D_FF = 2816
ROPE_THETA = 10000.0
RMS_EPS = 1e-6
BLOCK = 128
ATTN_SCALE = HEAD_DIM ** -0.5
IDX_SCALE = (HEAD_DIM ** -0.5) * (IDX_HEADS ** -0.5)
LOG2_E = float(np.log2(np.e))

LANES = 128
MIN_NORMAL_KEY = 1 << 23
KEY_CHUNK = 1024
COUNT_CHUNK = 512
COUNT_FOLD = 8
VALUE_PASSES = 14
DESCENT_STEPS = 6
SEARCH_PASSES_MIN = 12
L_ROWS = 8
ROW_TILE = 512
DIL_BLOCKS = 8
FF_CHUNK = 1408
VMEM_LIMIT = 56 * 1024 * 1024
MXU_DTYPE = jnp.bfloat16

_SPLITS = (D_HEADS, D_HEADS, D_HEADS, D_HEADS, HEAD_DIM, HEAD_DIM, IDX_HEADS * HEAD_DIM, HEAD_DIM, IDX_HEADS,
           D_MODEL, D_MODEL)
_OFF = np.concatenate([[0], np.cumsum(_SPLITS)])
(_QA, _KA, _VA, _QB, _KB, _VB, _QI, _KI, _WI, _GA, _GB) = (int(o) for o in _OFF[:-1])

_P_QA, _P_KA, _P_VA, _P_QB, _P_QI = 0, 512, 1024, 1536, 2048
_P_KBD, _P_KID, _P_MISC = 2560, 2688, 2816
_P_TOTAL = 2944


def _pair_perm():
    idx = np.empty(D_HEADS, np.int64)
    for j in range(D_HEADS):
        g, l = divmod(j, LANES)
        quarter, e = divmod(l, HALF)
        head = 2 * g + (quarter % 2)
        idx[j] = head * HEAD_DIM + (quarter // 2) * HALF + e
    return idx


def _dup_perm():
    idx = np.empty(LANES, np.int64)
    for l in range(LANES):
        quarter, e = divmod(l, HALF)
        idx[l] = (quarter // 2) * HALF + e
    return idx


def _packed_columns():
    pp, dp = _pair_perm(), _dup_perm()
    return np.concatenate([
        _QA + pp, _KA + pp, _VA + np.arange(D_HEADS), _QB + pp, _QI + pp,
        _KB + dp, _KI + dp, _VB + np.arange(HEAD_DIM), _WI + np.arange(IDX_HEADS)])


def _rms(x, g):
    ms = jnp.mean(x * x, axis=-1, keepdims=True)
    return x * lax.rsqrt(ms + RMS_EPS) * g


def _nt_dot(a, b):
    return lax.dot_general(a, b, (((1,), (1,)), ((), ())), preferred_element_type=jnp.float32)


def _proj_kernel(x_ref, g_ref, w_ref, cos_ref, sin_ref, *refs):
    n_lay = 1 + len(RESIDUE_DILATIONS)
    qa_refs, ka_refs, va_refs = refs[:n_lay], refs[n_lay:2 * n_lay], refs[2 * n_lay:3 * n_lay]
    qb_ref, qi_ref, kbd_ref, kid_ref, misc_ref, h_ref, slab_ref = refs[3 * n_lay:]
    h_ref[...] = _rms(x_ref[...], g_ref[...]).astype(MXU_DTYPE)
    cos = cos_ref[...]
    sin = sin_ref[...]

    def mm(c0, width):
        return jnp.dot(h_ref[...], w_ref[:, c0:c0 + width], preferred_element_type=jnp.float32)

    def rope(z):
        parts = []
        for g in range(z.shape[1] // LANES):
            zg = z[:, g * LANES:(g + 1) * LANES]
            parts.append(zg * cos + pltpu.roll(zg, 2 * HALF, axis=1) * sin)
        return parts[0] if len(parts) == 1 else jnp.concatenate(parts, axis=1)

    def emit(y, out_refs):
        out_refs[0][...] = y.astype(MXU_DTYPE)
        for g in range(D_HEADS // LANES):
            slab_ref[g] = y[:, g * LANES:(g + 1) * LANES]
        for d, ref in zip(RESIDUE_DILATIONS, out_refs[1:]):
            rows = y.shape[0] // d
            for r in range(d):
                for g in range(D_HEADS // LANES):
                    ref[0, r, :, g * LANES:(g + 1) * LANES] = (
                        slab_ref[g, pl.ds(r, rows, stride=d), :].astype(MXU_DTYPE))

    emit(rope(mm(_P_QA, D_HEADS)) * ATTN_SCALE, qa_refs)
    emit(rope(mm(_P_KA, D_HEADS)), ka_refs)
    emit(mm(_P_VA, D_HEADS), va_refs)
    qb_ref[...] = (rope(mm(_P_QB, D_HEADS)) * (ATTN_SCALE * LOG2_E)).astype(MXU_DTYPE)
    qi_ref[...] = rope(mm(_P_QI, D_HEADS)).astype(MXU_DTYPE)
    kbd_ref[...] = rope(mm(_P_KBD, LANES)).astype(MXU_DTYPE)
    kid_ref[...] = rope(mm(_P_KID, LANES)).astype(MXU_DTYPE)
    misc_ref[...] = mm(_P_MISC, LANES)


def _project(x2, g, w_pack, cos_t, sin_t, seq):
    n = x2.shape[0]
    tm = ROW_TILE
    tiles_per_seq = seq // tm
    row = lambda i: (i, 0)
    const = lambda i: (0, 0)
    pos = lambda i: (i % tiles_per_seq, 0)
    batch = n // seq
    wide = jax.ShapeDtypeStruct((n, D_HEADS), MXU_DTYPE)
    narrow = jax.ShapeDtypeStruct((n, LANES), MXU_DTYPE)
    wide_spec = pl.BlockSpec((tm, D_HEADS), row)
    lay_shapes = [wide] + [jax.ShapeDtypeStruct((batch, d, seq // d, D_HEADS), MXU_DTYPE) for d in RESIDUE_DILATIONS]
    lay_specs = [wide_spec] + [
        pl.BlockSpec((1, d, tm // d, D_HEADS), lambda i: (i // tiles_per_seq, 0, i % tiles_per_seq, 0))
        for d in RESIDUE_DILATIONS]
    n_lay = len(lay_shapes)
    outs = pl.pallas_call(
        _proj_kernel,
        grid=(n // tm,),
        in_specs=[
            pl.BlockSpec((tm, D_MODEL), row),
            pl.BlockSpec((1, D_MODEL), const),
            pl.BlockSpec((D_MODEL, _P_TOTAL), const),
            pl.BlockSpec((tm, LANES), pos),
            pl.BlockSpec((tm, LANES), pos),
        ],
        out_specs=lay_specs * 3 + [wide_spec] * 2 + [pl.BlockSpec((tm, LANES), row)] * 3,
        out_shape=lay_shapes * 3 + [wide] * 2 + [narrow, narrow, jax.ShapeDtypeStruct((n, LANES), jnp.float32)],
        scratch_shapes=[pltpu.VMEM((tm, D_MODEL), MXU_DTYPE),
                        pltpu.VMEM((D_HEADS // LANES, tm, LANES), jnp.float32)],
        compiler_params=pltpu.CompilerParams(dimension_semantics=("arbitrary",), vmem_limit_bytes=VMEM_LIMIT),
        name="proj_rope",
    )(x2, g, w_pack, cos_t, sin_t)
    return (outs[:n_lay], outs[n_lay:2 * n_lay], outs[2 * n_lay:3 * n_lay]) + tuple(outs[3 * n_lay:])


def _dil_kernel(q_ref, k_ref, v_ref, kp_ref, vp_ref, o_ref, lse_ref, *, n_blocks):
    n = pl.program_id(2)
    kj = lax.broadcasted_iota(jnp.int32, (2 * BLOCK, LANES), 0)
    qi = lax.broadcasted_iota(jnp.int32, (2 * BLOCK, LANES), 1)
    band = (kj >= qi) & (kj <= qi + BLOCK)
    lane = lax.broadcasted_iota(jnp.int32, (1, LANES), 1)
    pairs = range(N_HEADS // 2)
    col = lambda p: slice(p * LANES, (p + 1) * LANES)

    for j in range(n_blocks):
        rows = slice(j * BLOCK, (j + 1) * BLOCK)
        if j == 0:
            k_prev = lambda p: kp_ref[0, 0, :, col(p)]
            v_prev = lambda p: vp_ref[0, 0, :, col(p)]
            bias = jnp.where(band & ((kj >= BLOCK) | (n > 0)), 0.0, -jnp.inf)
        else:
            prev_rows = slice((j - 1) * BLOCK, j * BLOCK)
            k_prev = lambda p, r=prev_rows: k_ref[0, 0, r, col(p)]
            v_prev = lambda p, r=prev_rows: v_ref[0, 0, r, col(p)]
            bias = jnp.where(band, 0.0, -jnp.inf)

        scores = []
        for p in pairs:
            qp = q_ref[0, 0, rows, col(p)]
            qs = jnp.concatenate([jnp.where(((lane // HALF) % 2) == hh, qp, jnp.zeros_like(qp))
                                  for hh in range(2)], axis=0)
            k2 = jnp.concatenate([k_prev(p), k_ref[0, 0, rows, col(p)]], axis=0)
            scores.append(_nt_dot(k2, qs))

        probs, dens, lses = [], [], []
        for p in pairs:
            for hh in range(2):
                sh = scores[p][:, hh * BLOCK:(hh + 1) * BLOCK] + bias
                m = jnp.max(sh, axis=0, keepdims=True)
                e = jnp.exp(sh - m)
                den = jnp.sum(e, axis=0, keepdims=True)
                probs.append(e.astype(MXU_DTYPE))
                dens.append(den)
                lses.append(m + jnp.log(den))

        for p in pairs:
            v2 = jnp.concatenate([v_prev(p), v_ref[0, 0, rows, col(p)]], axis=0)
            o_t = lax.dot_general(v2, jnp.concatenate(probs[2 * p:2 * p + 2], axis=1), (((0,), (0,)), ((), ())),
                                  preferred_element_type=jnp.float32)
            tile = jnp.concatenate([o_t[:HEAD_DIM, :BLOCK] / dens[2 * p],
                                    o_t[HEAD_DIM:, BLOCK:] / dens[2 * p + 1]], axis=0)
            o_ref[0, 0, rows, col(p)] = tile.T.astype(o_ref.dtype)
            lse_tile = jnp.concatenate([jnp.broadcast_to(l, (HEAD_DIM, BLOCK)) for l in lses[2 * p:2 * p + 2]],
                                       axis=0)
            lse_ref[0, 0, rows, col(p)] = lse_tile.T


def _dilated(q, k, v, dilation):
    batch, _, m_len, _ = q.shape
    n_blocks = min(DIL_BLOCKS, m_len // BLOCK)
    tile_rows = n_blocks * BLOCK
    cur = lambda b, r, n: (b, r, n, 0)
    prev = lambda b, r, n: (b, r, jnp.maximum(n * n_blocks - 1, 0), 0)
    tile = (1, 1, tile_rows, D_HEADS)
    blk = (1, 1, BLOCK, D_HEADS)
    return pl.pallas_call(
        functools.partial(_dil_kernel, n_blocks=n_blocks),
        grid=(batch, dilation, m_len // tile_rows),
        in_specs=[pl.BlockSpec(tile, cur), pl.BlockSpec(tile, cur), pl.BlockSpec(tile, cur),
                  pl.BlockSpec(blk, prev), pl.BlockSpec(blk, prev)],
        out_specs=[pl.BlockSpec(tile, cur), pl.BlockSpec(tile, cur)],
        out_shape=[jax.ShapeDtypeStruct(q.shape, MXU_DTYPE), jax.ShapeDtypeStruct(q.shape, jnp.float32)],
        compiler_params=pltpu.CompilerParams(dimension_semantics=("arbitrary",) * 3),
        name=f"dilated_d{dilation}",
    )(q, k, v, k, v)


def _key_to_f32(key):
    bits = key ^ ((key >> 31) & jnp.int32(0x7FFFFFFF))
    return lax.bitcast_convert_type(bits, jnp.float32)


def _f32_to_key(x):
    bits = lax.bitcast_convert_type(x, jnp.int32)
    return bits ^ ((bits >> 31) & jnp.int32(0x7FFFFFFF))


def _fold_rows(x, op):
    rows = x.shape[0]
    y = op(x.reshape(rows // 64, 64, LANES), axis=0)
    return op(y.reshape(8, 8, LANES), axis=0)


def _sparse_kernel(qi_ref, qb_ref, wt_ref, kid_ref, kbd_ref, vbt_ref, out_ref,
                   sc_ref, eq_ref, eqc_ref, mb_ref, qis_ref, qbs_ref, acc_ref, m_ref, sqa_ref, sqb_ref, mxa_ref,
                   mxb_ref, *, topk):
    i = pl.program_id(1)
    nch = i // (KEY_CHUNK // BLOCK) + 1
    ncc = i // (COUNT_CHUNK // BLOCK) + 1
    lane = lax.broadcasted_iota(jnp.int32, (1, LANES), 1)
    t_idx = i * BLOCK + lane
    row_iota = lax.broadcasted_iota(jnp.int32, (KEY_CHUNK, LANES), 0)
    crow_iota = lax.broadcasted_iota(jnp.int32, (COUNT_CHUNK, LANES), 0)
    neg_inf = jnp.float32(-jnp.inf)
    f32_lowest = jnp.float32(jnp.finfo(jnp.float32).min)

    eye = jnp.where(lax.broadcasted_iota(jnp.int32, (BLOCK, LANES), 0) == lane, 1.0, 0.0).astype(MXU_DTYPE)
    for h in range(N_HEADS):
        cols = slice((h // 2) * LANES, (h // 2 + 1) * LANES)
        head_lanes = ((lane // HALF) % 2) == (h % 2)
        rows = slice(h * BLOCK, (h + 1) * BLOCK)
        qis_ref[rows, :] = jnp.where(head_lanes, qi_ref[0, :, cols], jnp.zeros((), MXU_DTYPE))
        qbs_ref[rows, :LANES] = jnp.where(head_lanes, qb_ref[0, :, cols], jnp.zeros((), MXU_DTYPE))
        qbs_ref[rows, LANES:] = eye

    def chunk_start(c):
        return pl.multiple_of(c * KEY_CHUNK, KEY_CHUNK)

    def pair_q(ref, p):
        return ref[p * 2 * BLOCK:(p + 1) * 2 * BLOCK, :]

    buf_a, buf_b = (sqa_ref, mxa_ref), (sqb_ref, mxb_ref)

    def pipeline(produce, consume):
        produce(0, buf_a)

        def two_chunks(j, carry):
            c = 2 * j
            produce(c + 1, buf_b)
            consume(c, buf_a)
            produce(c + 2, buf_a)
            consume(c + 1, buf_b)
            return carry

        lax.fori_loop(0, (nch - 1) // 2, two_chunks, 0)
        last = nch - 1

        @pl.when(last % 2 == 1)
        def _():
            produce(last, buf_b)
            consume(last - 1, buf_a)
            consume(last, buf_b)

        @pl.when(last % 2 == 0)
        def _():
            consume(last, buf_a)

    def score_chunk(c, carry):
        r0 = chunk_start(c)
        kc = kid_ref[0, pl.ds(r0, KEY_CHUNK), :]
        acc = jnp.zeros((KEY_CHUNK, LANES), jnp.float32)
        for p in range(N_HEADS // 2):
            d = _nt_dot(kc, pair_q(qis_ref, p))
            for hh in range(2):
                h = 2 * p + hh
                acc = acc + jnp.maximum(d[:, hh * BLOCK:(hh + 1) * BLOCK], 0.0) * wt_ref[0, h:h + 1, :]
        admissible = r0 + row_iota <= t_idx
        scores = acc * IDX_SCALE
        masked = jnp.where(admissible, scores, neg_inf)
        sc_ref[pl.ds(r0, KEY_CHUNK), :] = masked
        top8, bot8 = carry
        return (jnp.maximum(top8, _fold_rows(masked, jnp.max)),
                jnp.minimum(bot8, _fold_rows(jnp.where(admissible, scores, -neg_inf), jnp.min)))

    def for_chunks(n_chunks, body, init):
        carry = lax.fori_loop(0, n_chunks // 4,
                              lambda j, a: body(4 * j + 3, body(4 * j + 2, body(4 * j + 1, body(4 * j, a)))), init)
        done = (n_chunks // 4) * 4
        carry = lax.cond((n_chunks & 2) != 0, lambda a: body(done + 1, body(done, a)), lambda a: a, carry)
        done = done + (n_chunks & 2)
        return lax.cond((n_chunks & 1) != 0, lambda a: body(done, a), lambda a: a, carry)

    top8, bot8 = for_chunks(nch, score_chunk, (jnp.full((8, LANES), neg_inf, jnp.float32),
                                               jnp.full((8, LANES), -neg_inf, jnp.float32)))
    top = jnp.max(top8, axis=0, keepdims=True)
    bot = jnp.min(bot8, axis=0, keepdims=True)

    def fold_scores(value, op, pairwise, start, src_ref=sc_ref):
        def body(c, acc):
            r0 = pl.multiple_of(c * COUNT_CHUNK, COUNT_CHUNK)
            val = value(src_ref[pl.ds(r0, COUNT_CHUNK), :], r0)
            return pairwise(acc, op(val.reshape(COUNT_FOLD, COUNT_CHUNK // COUNT_FOLD, LANES), axis=0))

        acc = for_chunks(ncc, body, jnp.full((COUNT_CHUNK // COUNT_FOLD, LANES), start, jnp.float32))
        return op(acc, axis=0, keepdims=True)

    def count(pred):
        return fold_scores(lambda x, r0: jnp.where(pred(x, r0), 1.0, 0.0), jnp.sum, jnp.add, 0.0)

    key_top = _f32_to_key(top)
    lo = _f32_to_key(bot) - jnp.int32(1 << 23)
    hi = key_top + 1
    hi = jnp.where((hi >= -MIN_NORMAL_KEY) & (hi < MIN_NORMAL_KEY), jnp.int32(MIN_NORMAL_KEY), hi)
    unknown = jnp.float32(2 * sc_ref.shape[0])

    def is_settled(carry):
        lo, hi, cnt_lo = carry[:3]
        return (cnt_lo == topk) | (hi - lo == 1) | (t_idx < topk)

    def probe(mid, carry):
        lo, hi, cnt_lo, thr_lo, thr_hi = carry
        thr_c = _key_to_f32(mid)
        cnt = count(lambda x, r0: x >= thr_c)
        take = (cnt >= topk) & ~is_settled(carry)
        drop = (cnt < topk) & ~is_settled(carry)
        return (jnp.where(take, mid, lo), jnp.where(drop, mid, hi), jnp.where(take, cnt, cnt_lo),
                jnp.where(take, thr_c, thr_lo), jnp.where(drop, thr_c, thr_hi))

    def halve(by_value, carry):
        lo, hi = carry[:2]
        mid = lo + lax.shift_right_logical(hi - lo, 1)
        if by_value:
            mid_val = _f32_to_key(0.5 * _key_to_f32(lo) + 0.5 * _key_to_f32(hi))
            mid = jnp.where((mid_val > lo) & (mid_val < hi), mid_val, mid)
        return probe(mid, carry)

    def descend(carry):
        lo, hi, cnt_lo, thr_lo, thr_hi = carry
        v = fold_scores(lambda x, r0: jnp.where(x < thr_hi, x, neg_inf), jnp.max, jnp.maximum, neg_inf)
        cnt = count(lambda x, r0: x >= v)
        key_v = _f32_to_key(v)
        take = (cnt >= topk) & ~is_settled(carry)
        drop = (cnt < topk) & ~is_settled(carry)
        return (jnp.where(take, key_v, lo), jnp.where(take, key_v + 1, jnp.where(drop, key_v, hi)),
                jnp.where(take, cnt, cnt_lo), jnp.where(take, v, thr_lo), jnp.where(drop, v, thr_hi))

    def pending(carry):
        return jnp.max(jnp.where(is_settled(carry), 0.0, 1.0))

    state = lax.fori_loop(0, SEARCH_PASSES_MIN, lambda b, c: halve(True, c),
                          (lo, hi, jnp.full((1, LANES), unknown), _key_to_f32(lo), _key_to_f32(hi)))

    def two_passes(carry):
        b, state, _ = carry
        state = lax.cond(b < VALUE_PASSES, lambda s: halve(True, halve(True, s)),
                         lambda s: lax.cond(b < VALUE_PASSES + 2 * DESCENT_STEPS, descend,
                                            lambda t: halve(False, halve(False, t)), s), state)
        return b + 2, state, pending(state)

    _, (lo, hi, cnt_ge, thr_raw, _), _ = lax.while_loop(
        lambda c: (c[0] < VALUE_PASSES + 2 * DESCENT_STEPS + 32) & (c[2] > 0.0), two_passes,
        (jnp.int32(SEARCH_PASSES_MIN), state, pending(state)))
    enough = thr_raw > f32_lowest
    thr = jnp.where(enough, thr_raw, f32_lowest)

    tie_lane = enough & (cnt_ge > topk) & (t_idx >= topk)

    @pl.when(jnp.max(jnp.where(tie_lane, 1.0, 0.0)) > 0.0)
    def _():
        need = topk - count(lambda x, r0: x > thr)

        def mark_equal(c, carry):
            seen, chunk_of, seen_before = carry
            r0 = pl.multiple_of(c * COUNT_CHUNK, COUNT_CHUNK)
            eq = jnp.where(sc_ref[pl.ds(r0, COUNT_CHUNK), :] == thr, 1.0, 0.0)
            eq_ref[pl.ds(r0, COUNT_CHUNK), :] = eq
            here = jnp.sum(_fold_rows(eq, jnp.sum), axis=0, keepdims=True)
            first_reach = (seen < need) & (seen + here >= need)
            return (seen + here, jnp.where(first_reach, c, chunk_of), jnp.where(first_reach, seen, seen_before))

        zero = jnp.zeros((1, LANES), jnp.float32)
        _, chunk_of, seen_before = for_chunks(ncc, mark_equal, (zero, jnp.zeros((1, LANES), jnp.int32), zero))
        need_here = need - seen_before

        eqc_ref[...] = jnp.zeros(eqc_ref.shape, jnp.float32)

        def gather_chunk(c, carry):
            r0 = pl.multiple_of(c * COUNT_CHUNK, COUNT_CHUNK)
            eqc_ref[...] += jnp.where(chunk_of == c, eq_ref[pl.ds(r0, COUNT_CHUNK), :], 0.0)
            return carry

        for_chunks(ncc, gather_chunk, 0)

        def idx_step(b, lim):
            cand = lim | (jnp.int32(1) << (COUNT_CHUNK.bit_length() - 2 - b))
            below = jnp.sum(_fold_rows(jnp.where(crow_iota < cand, eqc_ref[...], 0.0), jnp.sum), axis=0, keepdims=True)
            return jnp.where(below < need_here, cand, lim)

        lim = chunk_of * COUNT_CHUNK + lax.fori_loop(0, COUNT_CHUNK.bit_length() - 1, idx_step,
                                                     jnp.zeros((1, LANES), jnp.int32))

        def demote(c, carry):
            r0 = pl.multiple_of(c * COUNT_CHUNK, COUNT_CHUNK)
            x = sc_ref[pl.ds(r0, COUNT_CHUNK), :]
            drop = (eq_ref[pl.ds(r0, COUNT_CHUNK), :] > 0.0) & (r0 + crow_iota > lim) & tie_lane
            sc_ref[pl.ds(r0, COUNT_CHUNK), :] = jnp.where(drop, neg_inf, x)
            return carry

        for_chunks(ncc, demote, 0)

    mask_off = float(jnp.finfo(MXU_DTYPE).min)

    def build_mask(c, carry):
        r0 = chunk_start(c)
        selected = sc_ref[pl.ds(r0, KEY_CHUNK), :] >= thr
        mb_ref[pl.ds(r0, KEY_CHUNK), :] = jnp.where(selected, 0.0, mask_off).astype(MXU_DTYPE)
        return carry

    for_chunks(nch, build_mask, 0)

    def attn_scores(c, buf):
        sq_ref, mx_ref = buf
        r0 = chunk_start(c)
        kaug = jnp.concatenate([kbd_ref[0, pl.ds(r0, KEY_CHUNK), :], mb_ref[pl.ds(r0, KEY_CHUNK), :]], axis=1)
        for p in range(N_HEADS // 2):
            s = _nt_dot(kaug, pair_q(qbs_ref, p))
            sq_ref[:, p * 2 * BLOCK:(p + 1) * 2 * BLOCK] = s
            for hh in range(2):
                cols = slice((2 * p + hh) * BLOCK, (2 * p + hh + 1) * BLOCK)
                mx_ref[:, cols] = _fold_rows(s[:, hh * BLOCK:(hh + 1) * BLOCK], jnp.max)

    def accumulate(c, buf):
        sq_ref, mx_ref = buf
        m_run = m_ref[...]
        m_new = jnp.maximum(m_run, jnp.max(mx_ref[...], axis=0, keepdims=True))
        m_ref[...] = m_new
        acc_ref[...] = acc_ref[...] * jnp.exp2(m_run - m_new)
        vt = vbt_ref[0, c]
        for p in range(N_HEADS // 2):
            pcols = slice(p * 2 * BLOCK, (p + 1) * 2 * BLOCK)
            e = jnp.exp2(sq_ref[:, pcols] - m_new[:, pcols])
            acc_ref[:, pcols] += jnp.dot(vt, e.astype(MXU_DTYPE), preferred_element_type=jnp.float32)

    acc_ref[...] = jnp.zeros(acc_ref.shape, jnp.float32)
    m_ref[...] = jnp.full(m_ref.shape, f32_lowest, jnp.float32)
    pipeline(attn_scores, accumulate)

    o_t = acc_ref[:HEAD_DIM, :] / acc_ref[HEAD_DIM:HEAD_DIM + 1, :]
    for p in range(N_HEADS // 2):
        pair = jnp.concatenate([o_t[:, (2 * p) * BLOCK:(2 * p + 1) * BLOCK],
                                o_t[:, (2 * p + 1) * BLOCK:(2 * p + 2) * BLOCK]], axis=0)
        out_ref[0, :, p * LANES:(p + 1) * LANES] = pair.T.astype(out_ref.dtype)


def _sparse(qi, qb, wt, kid, kbd, vbt, batch, seq):
    topk = min(TOPK_MAX, seq // 4)
    nq = seq // BLOCK
    qblk = pl.BlockSpec((1, BLOCK, D_HEADS), lambda b, i: (b, i, 0))
    full = pl.BlockSpec((1, seq, LANES), lambda b, i: (b, 0, 0))
    kern = functools.partial(_sparse_kernel, topk=topk)
    return pl.pallas_call(
        kern,
        grid=(batch, nq),
        in_specs=[qblk, qblk,
                  pl.BlockSpec((1, IDX_HEADS, BLOCK), lambda b, i: (b, 0, i)),
                  full, full,
                  pl.BlockSpec((1, seq // KEY_CHUNK, HEAD_DIM + L_ROWS, KEY_CHUNK), lambda b, i: (b, 0, 0, 0))],
        out_specs=qblk,
        out_shape=jax.ShapeDtypeStruct((batch, seq, D_HEADS), MXU_DTYPE),
        scratch_shapes=[pltpu.VMEM((seq, LANES), jnp.float32),
                        pltpu.VMEM((seq, LANES), jnp.float32),
                        pltpu.VMEM((COUNT_CHUNK, LANES), jnp.float32),
                        pltpu.VMEM((seq, LANES), MXU_DTYPE),
                        pltpu.VMEM((N_HEADS * BLOCK, LANES), MXU_DTYPE),
                        pltpu.VMEM((N_HEADS * BLOCK, 2 * LANES), MXU_DTYPE),
                        pltpu.VMEM((HEAD_DIM + L_ROWS, N_HEADS * BLOCK), jnp.float32),
                        pltpu.VMEM((1, N_HEADS * BLOCK), jnp.float32),
                        pltpu.VMEM((KEY_CHUNK, N_HEADS * BLOCK), jnp.float32),
                        pltpu.VMEM((KEY_CHUNK, N_HEADS * BLOCK), jnp.float32),
                        pltpu.VMEM((8, N_HEADS * BLOCK), jnp.float32),
                        pltpu.VMEM((8, N_HEADS * BLOCK), jnp.float32)],
        compiler_params=pltpu.CompilerParams(dimension_semantics=("arbitrary", "arbitrary"),
                                             vmem_limit_bytes=VMEM_LIMIT),
        name="indexer_sparse_attn",
    )(qi, qb, wt, kid, kbd, vbt)


def _merge_kernel(x_ref, g_ref, *refs):
    n_pat = len(DILATED_PATTERNS)
    o_refs, l_refs = refs[:n_pat], refs[n_pat:2 * n_pat]
    yb_ref, wg_ref, wua_ref, wub_ref, wo_ref, x1_ref = refs[2 * n_pat:2 * n_pat + 6]
    slabs = refs[2 * n_pat + 6:]
    x = x_ref[...]
    h = _rms(x, g_ref[...]).astype(MXU_DTYPE)
    tm = x.shape[0]
    n_grp = D_HEADS // LANES

    o_src, l_src, k = [], [], 0
    for (_, d), o_ref, l_ref in zip(DILATED_PATTERNS, o_refs, l_refs):
        if d == 1:
            o_src.append(lambda g, r=o_ref: r[:, g * LANES:(g + 1) * LANES].astype(jnp.float32))
            l_src.append(lambda g, r=l_ref: r[:, g * LANES:(g + 1) * LANES])
            continue
        o_slab, l_slab = slabs[2 * k], slabs[2 * k + 1]
        k += 1
        for r in range(d):
            for g in range(n_grp):
                cols = slice(g * LANES, (g + 1) * LANES)
                o_slab[g, pl.ds(r, tm // d, stride=d), :] = o_ref[0, r, :, cols].astype(jnp.float32)
                l_slab[g, pl.ds(r, tm // d, stride=d), :] = l_ref[0, r, :, cols]
        o_src.append(lambda g, s=o_slab: s[g])
        l_src.append(lambda g, s=l_slab: s[g])

    parts = []
    for g in range(n_grp):
        ls = [f(g) for f in l_src]
        m = functools.reduce(jnp.maximum, ls)
        es = [jnp.exp(l - m) for l in ls]
        num = functools.reduce(jnp.add, [e * f(g) for e, f in zip(es, o_src)])
        parts.append((num / functools.reduce(jnp.add, es)).astype(MXU_DTYPE))
    ya = jnp.concatenate(parts, axis=1)
    ua = jnp.dot(ya, wua_ref[...], preferred_element_type=jnp.float32)
    ub = jnp.dot(yb_ref[...], wub_ref[...], preferred_element_type=jnp.float32)
    ga = jnp.dot(h, wg_ref[:, :D_MODEL], preferred_element_type=jnp.float32)
    gb = jnp.dot(h, wg_ref[:, D_MODEL:], preferred_element_type=jnp.float32)
    merged = jax.nn.sigmoid(ga) * ua + jax.nn.sigmoid(gb) * ub
    x1_ref[...] = x + jnp.dot(merged.astype(MXU_DTYPE), wo_ref[...], preferred_element_type=jnp.float32)


def _merge(x2, g, os_, lses, yb, wg, wua, wub, wo, seq):
    n = x2.shape[0]
    tm = ROW_TILE
    tiles_per_seq = seq // tm
    row = lambda i: (i, 0)
    const = lambda i: (0, 0)
    half = pl.BlockSpec((tm, D_HEADS), row)
    pat_specs = [half if d == 1 else
                 pl.BlockSpec((1, d, tm // d, D_HEADS), lambda i: (i // tiles_per_seq, 0, i % tiles_per_seq, 0))
                 for _, d in DILATED_PATTERNS]
    return pl.pallas_call(
        _merge_kernel,
        grid=(n // tm,),
        in_specs=[pl.BlockSpec((tm, D_MODEL), row), pl.BlockSpec((1, D_MODEL), const)] + pat_specs * 2 + [half] + [
            pl.BlockSpec((D_MODEL, 2 * D_MODEL), const),
            pl.BlockSpec((D_HEADS, D_MODEL), const),
            pl.BlockSpec((D_HEADS, D_MODEL), const),
            pl.BlockSpec((D_MODEL, D_MODEL), const)],
        out_specs=pl.BlockSpec((tm, D_MODEL), row),
        out_shape=jax.ShapeDtypeStruct((n, D_MODEL), jnp.float32),
        scratch_shapes=[pltpu.VMEM((D_HEADS // LANES, tm, LANES), jnp.float32)] * (2 * len(RESIDUE_DILATIONS)),
        compiler_params=pltpu.CompilerParams(dimension_semantics=("arbitrary",), vmem_limit_bytes=VMEM_LIMIT),
        name="mix_gate_out",
    )(x2, g, *os_, *lses, yb, wg, wua, wub, wo)


def _ffn_kernel(x_ref, g_ref, wgate_ref, wup_ref, wdown_ref, gf_ref, out_ref):
    x = x_ref[...]
    h = _rms(x, g_ref[...]).astype(MXU_DTYPE)
    y = x
    for c0 in range(0, D_FF, FF_CHUNK):
        a = jnp.dot(h, wgate_ref[:, c0:c0 + FF_CHUNK], preferred_element_type=jnp.float32)
        u = jnp.dot(h, wup_ref[:, c0:c0 + FF_CHUNK], preferred_element_type=jnp.float32)
        act = (a * jax.nn.sigmoid(a) * u).astype(MXU_DTYPE)
        y = y + jnp.dot(act, wdown_ref[c0:c0 + FF_CHUNK, :], preferred_element_type=jnp.float32)
    out_ref[...] = _rms(y, gf_ref[...])


def _ffn(x1, g, wgate, wup, wdown, gf):
    n = x1.shape[0]
    tm = ROW_TILE
    row = lambda i: (i, 0)
    const = lambda i: (0, 0)
    return pl.pallas_call(
        _ffn_kernel,
        grid=(n // tm,),
        in_specs=[pl.BlockSpec((tm, D_MODEL), row), pl.BlockSpec((1, D_MODEL), const),
                  pl.BlockSpec((D_MODEL, D_FF), const), pl.BlockSpec((D_MODEL, D_FF), const),
                  pl.BlockSpec((D_FF, D_MODEL), const), pl.BlockSpec((1, D_MODEL), const)],
        out_specs=pl.BlockSpec((tm, D_MODEL), row),
        out_shape=jax.ShapeDtypeStruct((n, D_MODEL), jnp.float32),
        compiler_params=pltpu.CompilerParams(dimension_semantics=("arbitrary",), vmem_limit_bytes=VMEM_LIMIT),
        name="ffn_norm",
    )(x1, g, wgate, wup, wdown, gf)


def _rope_tables(seq):
    inv_freq = ROPE_THETA ** (-jnp.arange(HALF, dtype=jnp.float32) / HALF)
    ang = jnp.arange(seq, dtype=jnp.int32).astype(jnp.float32)[:, None] * inv_freq[None, :]
    cos, sin = jnp.cos(ang), jnp.sin(ang)
    return jnp.tile(cos, (1, 4)), jnp.concatenate([-sin, -sin, sin, sin], axis=1)


def kernel(x, norm_mix, w_in, w_up_a, w_up_b, w_out, norm_ffn, w_gate, w_up, w_down, norm_final):
    batch, seq, _ = x.shape
    assert seq % max(d * BLOCK for _, d in DILATED_PATTERNS) == 0 and seq % KEY_CHUNK == 0
    assert all(w // d == BLOCK for w, d in DILATED_PATTERNS)
    n = batch * seq
    bf = MXU_DTYPE
    xf = x.reshape(n, D_MODEL)
    cos_t, sin_t = _rope_tables(seq)
    for layer in range(w_in.shape[0]):
        w = w_in[layer]
        w_pack = jnp.pad(w[:, _packed_columns()], ((0, 0), (0, _P_TOTAL - _P_MISC - HEAD_DIM - IDX_HEADS))).astype(bf)
        w_gates = w[:, _GA:].astype(bf)
        qas, kas, vas, qb, qi, kbd, kid, misc = _project(xf, norm_mix[layer][None], w_pack, cos_t, sin_t, seq)

        dil = []
        for (_, d), q, k, v in zip(DILATED_PATTERNS, qas, kas, vas):
            if d == 1:
                o, lse = _dilated(*(z.reshape(batch, 1, seq, D_HEADS) for z in (q, k, v)), d)
                dil.append((o.reshape(n, D_HEADS), lse.reshape(n, D_HEADS)))
            else:
                dil.append(_dilated(q, k, v, d))

        ones_col = (jnp.arange(L_ROWS) == 0).astype(bf)[None, :]
        vb = jnp.concatenate([misc[:, :HEAD_DIM].astype(bf), jnp.broadcast_to(ones_col, (n, L_ROWS))], axis=1)
        vbt = jnp.swapaxes(vb.reshape(batch, seq // KEY_CHUNK, KEY_CHUNK, HEAD_DIM + L_ROWS), 2, 3)
        wt = jnp.swapaxes(misc[:, HEAD_DIM:HEAD_DIM + IDX_HEADS].reshape(batch, seq, IDX_HEADS), 1, 2)
        r3 = lambda z: z.reshape(batch, seq, z.shape[-1])
        yb = _sparse(r3(qi), r3(qb), wt, r3(kid), r3(kbd), vbt, batch, seq).reshape(n, D_HEADS)

        x1 = _merge(xf, norm_mix[layer][None], [o for o, _ in dil], [l for _, l in dil], yb,
                    w_gates, w_up_a[layer].astype(bf), w_up_b[layer].astype(bf), w_out[layer].astype(bf), seq)
        last = layer == w_in.shape[0] - 1
        assert last, "the final norm is fused into the FFN kernel of the last layer"
        xf = _ffn(x1, norm_ffn[layer][None], w_gate[layer].astype(bf), w_up[layer].astype(bf),
                  w_down[layer].astype(bf), norm_final[None])
    return xf.reshape(batch, seq, D_MODEL)
```

```python
import functools

import numpy as np
import jax
import jax.numpy as jnp
from jax import lax
from jax.experimental import pallas as pl
from jax.experimental.pallas import tpu as pltpu

D_MODEL = 1024
HEAD_DIM = 64
HALF = HEAD_DIM // 2
N_HEADS = 8
D_HEADS = N_HEADS * HEAD_DIM
IDX_HEADS = 8
DILATED_PATTERNS = ((128, 1), (512, 4), (2048, 16))
RESIDUE_DILATIONS = tuple(d for _, d in DILATED_PATTERNS if d > 1)
TOPK_MAX = 256
D_FF = 2816
ROPE_THETA = 10000.0
RMS_EPS = 1e-6
BLOCK = 128
ATTN_SCALE = HEAD_DIM ** -0.5
IDX_SCALE = (HEAD_DIM ** -0.5) * (IDX_HEADS ** -0.5)
LOG2_E = float(np.log2(np.e))

LANES = 128
MIN_NORMAL_KEY = 1 << 23
KEY_CHUNK = 1024
COUNT_CHUNK = 512
COUNT_FOLD = 8
VALUE_PASSES = 14
DESCENT_STEPS = 6
SEARCH_PASSES_MIN = 12
L_ROWS = 8
ROW_TILE = 512
DIL_BLOCKS = 8
FF_CHUNK = 1408
VMEM_LIMIT = 56 * 1024 * 1024
MXU_DTYPE = jnp.bfloat16

_SPLITS = (D_HEADS, D_HEADS, D_HEADS, D_HEADS, HEAD_DIM, HEAD_DIM, IDX_HEADS * HEAD_DIM, HEAD_DIM, IDX_HEADS,
           D_MODEL, D_MODEL)
_OFF = np.concatenate([[0], np.cumsum(_SPLITS)])
(_QA, _KA, _VA, _QB, _KB, _VB, _QI, _KI, _WI, _GA, _GB) = (int(o) for o in _OFF[:-1])

_P_QA, _P_KA, _P_VA, _P_QB, _P_QI = 0, 512, 1024, 1536, 2048
_P_KBD, _P_KID, _P_MISC = 2560, 2688, 2816
_P_TOTAL = 2944


def _pair_perm():
    idx = np.empty(D_HEADS, np.int64)
    for j in range(D_HEADS):
        g, l = divmod(j, LANES)
        quarter, e = divmod(l, HALF)
        head = 2 * g + (quarter % 2)
        idx[j] = head * HEAD_DIM + (quarter // 2) * HALF + e
    return idx


def _dup_perm():
    idx = np.empty(LANES, np.int64)
    for l in range(LANES):
        quarter, e = divmod(l, HALF)
        idx[l] = (quarter // 2) * HALF + e
    return idx


def _packed_columns():
    pp, dp = _pair_perm(), _dup_perm()
    return np.concatenate([
        _QA + pp, _KA + pp, _VA + np.arange(D_HEADS), _QB + pp, _QI + pp,
        _KB + dp, _KI + dp, _VB + np.arange(HEAD_DIM), _WI + np.arange(IDX_HEADS)])


def _rms(x, g):
    ms = jnp.mean(x * x, axis=-1, keepdims=True)
    return x * lax.rsqrt(ms + RMS_EPS) * g


def _nt_dot(a, b):
    return lax.dot_general(a, b, (((1,), (1,)), ((), ())), preferred_element_type=jnp.float32)


def _proj_kernel(x_ref, g_ref, w_ref, cos_ref, sin_ref, *refs):
    n_lay = 1 + len(RESIDUE_DILATIONS)
    qa_refs, ka_refs, va_refs = refs[:n_lay], refs[n_lay:2 * n_lay], refs[2 * n_lay:3 * n_lay]
    qb_ref, qi_ref, kbd_ref, kid_ref, misc_ref, h_ref, slab_ref = refs[3 * n_lay:]
    h_ref[...] = _rms(x_ref[...], g_ref[...]).astype(MXU_DTYPE)
    cos = cos_ref[...]
    sin = sin_ref[...]

    def mm(c0, width):
        return jnp.dot(h_ref[...], w_ref[:, c0:c0 + width], preferred_element_type=jnp.float32)

    def rope(z):
        parts = []
        for g in range(z.shape[1] // LANES):
            zg = z[:, g * LANES:(g + 1) * LANES]
            parts.append(zg * cos + pltpu.roll(zg, 2 * HALF, axis=1) * sin)
        return parts[0] if len(parts) == 1 else jnp.concatenate(parts, axis=1)

    def emit(y, out_refs):
        out_refs[0][...] = y.astype(MXU_DTYPE)
        for g in range(D_HEADS // LANES):
            slab_ref[g] = y[:, g * LANES:(g + 1) * LANES]
        for d, ref in zip(RESIDUE_DILATIONS, out_refs[1:]):
            rows = y.shape[0] // d
            for r in range(d):
                for g in range(D_HEADS // LANES):
                    ref[0, r, :, g * LANES:(g + 1) * LANES] = (
                        slab_ref[g, pl.ds(r, rows, stride=d), :].astype(MXU_DTYPE))

    emit(rope(mm(_P_QA, D_HEADS)) * ATTN_SCALE, qa_refs)
    emit(rope(mm(_P_KA, D_HEADS)), ka_refs)
    emit(mm(_P_VA, D_HEADS), va_refs)
    qb_ref[...] = (rope(mm(_P_QB, D_HEADS)) * (ATTN_SCALE * LOG2_E)).astype(MXU_DTYPE)
    qi_ref[...] = rope(mm(_P_QI, D_HEADS)).astype(MXU_DTYPE)
    kbd_ref[...] = rope(mm(_P_KBD, LANES)).astype(MXU_DTYPE)
    kid_ref[...] = rope(mm(_P_KID, LANES)).astype(MXU_DTYPE)
    misc_ref[...] = mm(_P_MISC, LANES)


def _project(x2, g, w_pack, cos_t, sin_t, seq):
    n = x2.shape[0]
    tm = ROW_TILE
    tiles_per_seq = seq // tm
    row = lambda i: (i, 0)
    const = lambda i: (0, 0)
    pos = lambda i: (i % tiles_per_seq, 0)
    batch = n // seq
    wide = jax.ShapeDtypeStruct((n, D_HEADS), MXU_DTYPE)
    narrow = jax.ShapeDtypeStruct((n, LANES), MXU_DTYPE)
    wide_spec = pl.BlockSpec((tm, D_HEADS), row)
    lay_shapes = [wide] + [jax.ShapeDtypeStruct((batch, d, seq // d, D_HEADS), MXU_DTYPE) for d in RESIDUE_DILATIONS]
    lay_specs = [wide_spec] + [
        pl.BlockSpec((1, d, tm // d, D_HEADS), lambda i: (i // tiles_per_seq, 0, i % tiles_per_seq, 0))
        for d in RESIDUE_DILATIONS]
    n_lay = len(lay_shapes)
    outs = pl.pallas_call(
        _proj_kernel,
        grid=(n // tm,),
        in_specs=[
            pl.BlockSpec((tm, D_MODEL), row),
            pl.BlockSpec((1, D_MODEL), const),
            pl.BlockSpec((D_MODEL, _P_TOTAL), const),
            pl.BlockSpec((tm, LANES), pos),
            pl.BlockSpec((tm, LANES), pos),
        ],
        out_specs=lay_specs * 3 + [wide_spec] * 2 + [pl.BlockSpec((tm, LANES), row)] * 3,
        out_shape=lay_shapes * 3 + [wide] * 2 + [narrow, narrow, jax.ShapeDtypeStruct((n, LANES), jnp.float32)],
        scratch_shapes=[pltpu.VMEM((tm, D_MODEL), MXU_DTYPE),
                        pltpu.VMEM((D_HEADS // LANES, tm, LANES), jnp.float32)],
        compiler_params=pltpu.CompilerParams(dimension_semantics=("arbitrary",), vmem_limit_bytes=VMEM_LIMIT),
        name="proj_rope",
    )(x2, g, w_pack, cos_t, sin_t)
    return (outs[:n_lay], outs[n_lay:2 * n_lay], outs[2 * n_lay:3 * n_lay]) + tuple(outs[3 * n_lay:])


def _dil_kernel(q_ref, k_ref, v_ref, kp_ref, vp_ref, o_ref, lse_ref, *, n_blocks):
    n = pl.program_id(2)
    kj = lax.broadcasted_iota(jnp.int32, (2 * BLOCK, LANES), 0)
    qi = lax.broadcasted_iota(jnp.int32, (2 * BLOCK, LANES), 1)
    band = (kj >= qi) & (kj <= qi + BLOCK)
    lane = lax.broadcasted_iota(jnp.int32, (1, LANES), 1)
    pairs = range(N_HEADS // 2)
    col = lambda p: slice(p * LANES, (p + 1) * LANES)

    for j in range(n_blocks):
        rows = slice(j * BLOCK, (j + 1) * BLOCK)
        if j == 0:
            k_prev = lambda p: kp_ref[0, 0, :, col(p)]
            v_prev = lambda p: vp_ref[0, 0, :, col(p)]
            bias = jnp.where(band & ((kj >= BLOCK) | (n > 0)), 0.0, -jnp.inf)
        else:
            prev_rows = slice((j - 1) * BLOCK, j * BLOCK)
            k_prev = lambda p, r=prev_rows: k_ref[0, 0, r, col(p)]
            v_prev = lambda p, r=prev_rows: v_ref[0, 0, r, col(p)]
            bias = jnp.where(band, 0.0, -jnp.inf)

        scores = []
        for p in pairs:
            qp = q_ref[0, 0, rows, col(p)]
            qs = jnp.concatenate([jnp.where(((lane // HALF) % 2) == hh, qp, jnp.zeros_like(qp))
                                  for hh in range(2)], axis=0)
            k2 = jnp.concatenate([k_prev(p), k_ref[0, 0, rows, col(p)]], axis=0)
            scores.append(_nt_dot(k2, qs))

        probs, dens, lses = [], [], []
        for p in pairs:
            for hh in range(2):
                sh = scores[p][:, hh * BLOCK:(hh + 1) * BLOCK] + bias
                m = jnp.max(sh, axis=0, keepdims=True)
                e = jnp.exp(sh - m)
                den = jnp.sum(e, axis=0, keepdims=True)
                probs.append(e.astype(MXU_DTYPE))
                dens.append(den)
                lses.append(m + jnp.log(den))

        for p in pairs:
            v2 = jnp.concatenate([v_prev(p), v_ref[0, 0, rows, col(p)]], axis=0)
            o_t = lax.dot_general(v2, jnp.concatenate(probs[2 * p:2 * p + 2], axis=1), (((0,), (0,)), ((), ())),
                                  preferred_element_type=jnp.float32)
            tile = jnp.concatenate([o_t[:HEAD_DIM, :BLOCK] / dens[2 * p],
                                    o_t[HEAD_DIM:, BLOCK:] / dens[2 * p + 1]], axis=0)
            o_ref[0, 0, rows, col(p)] = tile.T.astype(o_ref.dtype)
            lse_tile = jnp.concatenate([jnp.broadcast_to(l, (HEAD_DIM, BLOCK)) for l in lses[2 * p:2 * p + 2]],
                                       axis=0)
            lse_ref[0, 0, rows, col(p)] = lse_tile.T


def _dilated(q, k, v, dilation):
    batch, _, m_len, _ = q.shape
    n_blocks = min(DIL_BLOCKS, m_len // BLOCK)
    tile_rows = n_blocks * BLOCK
    cur = lambda b, r, n: (b, r, n, 0)
    prev = lambda b, r, n: (b, r, jnp.maximum(n * n_blocks - 1, 0), 0)
    tile = (1, 1, tile_rows, D_HEADS)
    blk = (1, 1, BLOCK, D_HEADS)
    return pl.pallas_call(
        functools.partial(_dil_kernel, n_blocks=n_blocks),
        grid=(batch, dilation, m_len // tile_rows),
        in_specs=[pl.BlockSpec(tile, cur), pl.BlockSpec(tile, cur), pl.BlockSpec(tile, cur),
                  pl.BlockSpec(blk, prev), pl.BlockSpec(blk, prev)],
        out_specs=[pl.BlockSpec(tile, cur), pl.BlockSpec(tile, cur)],
        out_shape=[jax.ShapeDtypeStruct(q.shape, MXU_DTYPE), jax.ShapeDtypeStruct(q.shape, jnp.float32)],
        compiler_params=pltpu.CompilerParams(dimension_semantics=("arbitrary",) * 3),
        name=f"dilated_d{dilation}",
    )(q, k, v, k, v)


def _key_to_f32(key):
    bits = key ^ ((key >> 31) & jnp.int32(0x7FFFFFFF))
    return lax.bitcast_convert_type(bits, jnp.float32)


def _f32_to_key(x):
    bits = lax.bitcast_convert_type(x, jnp.int32)
    return bits ^ ((bits >> 31) & jnp.int32(0x7FFFFFFF))


def _fold_rows(x, op):
    rows = x.shape[0]
    y = op(x.reshape(rows // 64, 64, LANES), axis=0)
    return op(y.reshape(8, 8, LANES), axis=0)


def _sparse_kernel(qi_ref, qb_ref, wt_ref, kid_ref, kbd_ref, vbt_ref, out_ref,
                   sc_ref, eq_ref, eqc_ref, qis_ref, qbs_ref, acc_ref, m_ref, sqa_ref, sqb_ref, mxa_ref,
                   mxb_ref, *, topk):
    i = pl.program_id(1)
    nch = i // (KEY_CHUNK // BLOCK) + 1
    ncc = i // (COUNT_CHUNK // BLOCK) + 1
    lane = lax.broadcasted_iota(jnp.int32, (1, LANES), 1)
    t_idx = i * BLOCK + lane
    row_iota = lax.broadcasted_iota(jnp.int32, (KEY_CHUNK, LANES), 0)
    crow_iota = lax.broadcasted_iota(jnp.int32, (COUNT_CHUNK, LANES), 0)
    neg_inf = jnp.float32(-jnp.inf)
    f32_lowest = jnp.float32(jnp.finfo(jnp.float32).min)

    eye = jnp.where(lax.broadcasted_iota(jnp.int32, (BLOCK, LANES), 0) == lane, 1.0, 0.0).astype(MXU_DTYPE)
    for h in range(N_HEADS):
        cols = slice((h // 2) * LANES, (h // 2 + 1) * LANES)
        head_lanes = ((lane // HALF) % 2) == (h % 2)
        rows = slice(h * BLOCK, (h + 1) * BLOCK)
        qis_ref[rows, :] = jnp.where(head_lanes, qi_ref[0, :, cols], jnp.zeros((), MXU_DTYPE))
        qbs_ref[rows, :LANES] = jnp.where(head_lanes, qb_ref[0, :, cols], jnp.zeros((), MXU_DTYPE))
        qbs_ref[rows, LANES:] = eye

    def chunk_start(c):
        return pl.multiple_of(c * KEY_CHUNK, KEY_CHUNK)

    def pair_q(ref, p):
        return ref[p * 2 * BLOCK:(p + 1) * 2 * BLOCK, :]

    buf_a, buf_b = (sqa_ref, mxa_ref), (sqb_ref, mxb_ref)

    def pipeline(produce, consume):
        produce(0, buf_a)

        def two_chunks(j, carry):
            c = 2 * j
            produce(c + 1, buf_b)
            consume(c, buf_a)
            produce(c + 2, buf_a)
            consume(c + 1, buf_b)
            return carry

        lax.fori_loop(0, (nch - 1) // 2, two_chunks, 0)
        last = nch - 1

        @pl.when(last % 2 == 1)
        def _():
            produce(last, buf_b)
            consume(last - 1, buf_a)
            consume(last, buf_b)

        @pl.when(last % 2 == 0)
        def _():
            consume(last, buf_a)

    def score_chunk(c, carry):
        r0 = chunk_start(c)
        kc = kid_ref[0, pl.ds(r0, KEY_CHUNK), :]
        acc = jnp.zeros((KEY_CHUNK, LANES), jnp.float32)
        for p in range(N_HEADS // 2):
            d = _nt_dot(kc, pair_q(qis_ref, p))
            for hh in range(2):
                h = 2 * p + hh
                acc = acc + jnp.maximum(d[:, hh * BLOCK:(hh + 1) * BLOCK], 0.0) * wt_ref[0, h:h + 1, :]
        admissible = r0 + row_iota <= t_idx
        scores = acc * IDX_SCALE
        masked = jnp.where(admissible, scores, neg_inf)
        sc_ref[pl.ds(r0, KEY_CHUNK), :] = masked
        top8, bot8 = carry
        return (jnp.maximum(top8, _fold_rows(masked, jnp.max)),
                jnp.minimum(bot8, _fold_rows(jnp.where(admissible, scores, -neg_inf), jnp.min)))

    def for_chunks(n_chunks, body, init):
        carry = lax.fori_loop(0, n_chunks // 4,
                              lambda j, a: body(4 * j + 3, body(4 * j + 2, body(4 * j + 1, body(4 * j, a)))), init)
        done = (n_chunks // 4) * 4
        carry = lax.cond((n_chunks & 2) != 0, lambda a: body(done + 1, body(done, a)), lambda a: a, carry)
        done = done + (n_chunks & 2)
        return lax.cond((n_chunks & 1) != 0, lambda a: body(done, a), lambda a: a, carry)

    top8, bot8 = for_chunks(nch, score_chunk, (jnp.full((8, LANES), neg_inf, jnp.float32),
                                               jnp.full((8, LANES), -neg_inf, jnp.float32)))
    top = jnp.max(top8, axis=0, keepdims=True)
    bot = jnp.min(bot8, axis=0, keepdims=True)

    def fold_scores(value, op, pairwise, start, src_ref=sc_ref):
        def body(c, acc):
            r0 = pl.multiple_of(c * COUNT_CHUNK, COUNT_CHUNK)
            val = value(src_ref[pl.ds(r0, COUNT_CHUNK), :], r0)
            return pairwise(acc, op(val.reshape(COUNT_FOLD, COUNT_CHUNK // COUNT_FOLD, LANES), axis=0))

        acc = for_chunks(ncc, body, jnp.full((COUNT_CHUNK // COUNT_FOLD, LANES), start, jnp.float32))
        return op(acc, axis=0, keepdims=True)

    def count(pred):
        return fold_scores(lambda x, r0: jnp.where(pred(x, r0), 1.0, 0.0), jnp.sum, jnp.add, 0.0)

    key_top = _f32_to_key(top)
    lo = _f32_to_key(bot) - jnp.int32(1 << 23)
    hi = key_top + 1
    hi = jnp.where((hi >= -MIN_NORMAL_KEY) & (hi < MIN_NORMAL_KEY), jnp.int32(MIN_NORMAL_KEY), hi)
    unknown = jnp.float32(2 * sc_ref.shape[0])

    def is_settled(carry):
        lo, hi, cnt_lo = carry[:3]
        return (cnt_lo == topk) | (hi - lo == 1) | (t_idx < topk)

    def probe(mid, carry):
        lo, hi, cnt_lo, thr_lo, thr_hi = carry
        thr_c = _key_to_f32(mid)
        cnt = count(lambda x, r0: x >= thr_c)
        take = (cnt >= topk) & ~is_settled(carry)
        drop = (cnt < topk) & ~is_settled(carry)
        return (jnp.where(take, mid, lo), jnp.where(drop, mid, hi), jnp.where(take, cnt, cnt_lo),
                jnp.where(take, thr_c, thr_lo), jnp.where(drop, thr_c, thr_hi))

    def halve(by_value, carry):
        lo, hi = carry[:2]
        mid = lo + lax.shift_right_logical(hi - lo, 1)
        if by_value:
            mid_val = _f32_to_key(0.5 * _key_to_f32(lo) + 0.5 * _key_to_f32(hi))
            mid = jnp.where((mid_val > lo) & (mid_val < hi), mid_val, mid)
        return probe(mid, carry)

    def descend(carry):
        lo, hi, cnt_lo, thr_lo, thr_hi = carry
        v = fold_scores(lambda x, r0: jnp.where(x < thr_hi, x, neg_inf), jnp.max, jnp.maximum, neg_inf)
        cnt = count(lambda x, r0: x >= v)
        key_v = _f32_to_key(v)
        take = (cnt >= topk) & ~is_settled(carry)
        drop = (cnt < topk) & ~is_settled(carry)
        return (jnp.where(take, key_v, lo), jnp.where(take, key_v + 1, jnp.where(drop, key_v, hi)),
                jnp.where(take, cnt, cnt_lo), jnp.where(take, v, thr_lo), jnp.where(drop, v, thr_hi))

    def pending(carry):
        return jnp.max(jnp.where(is_settled(carry), 0.0, 1.0))

    state = lax.fori_loop(0, SEARCH_PASSES_MIN, lambda b, c: halve(True, c),
                          (lo, hi, jnp.full((1, LANES), unknown), _key_to_f32(lo), _key_to_f32(hi)))

    def two_passes(carry):
        b, state, _ = carry
        state = lax.cond(b < VALUE_PASSES, lambda s: halve(True, halve(True, s)),
                         lambda s: lax.cond(b < VALUE_PASSES + 2 * DESCENT_STEPS, descend,
                                            lambda t: halve(False, halve(False, t)), s), state)
        return b + 2, state, pending(state)

    _, (lo, hi, cnt_ge, thr_raw, _), _ = lax.while_loop(
        lambda c: (c[0] < VALUE_PASSES + 2 * DESCENT_STEPS + 32) & (c[2] > 0.0), two_passes,
        (jnp.int32(SEARCH_PASSES_MIN), state, pending(state)))
    enough = thr_raw > f32_lowest
    thr = jnp.where(enough, thr_raw, f32_lowest)

    tie_lane = enough & (cnt_ge > topk) & (t_idx >= topk)

    @pl.when(jnp.max(jnp.where(tie_lane, 1.0, 0.0)) > 0.0)
    def _():
        need = topk - count(lambda x, r0: x > thr)

        def mark_equal(c, carry):
            seen, chunk_of, seen_before = carry
            r0 = pl.multiple_of(c * COUNT_CHUNK, COUNT_CHUNK)
            eq = jnp.where(sc_ref[pl.ds(r0, COUNT_CHUNK), :] == thr, 1.0, 0.0)
            eq_ref[pl.ds(r0, COUNT_CHUNK), :] = eq
            here = jnp.sum(_fold_rows(eq, jnp.sum), axis=0, keepdims=True)
            first_reach = (seen < need) & (seen + here >= need)
            return (seen + here, jnp.where(first_reach, c, chunk_of), jnp.where(first_reach, seen, seen_before))

        zero = jnp.zeros((1, LANES), jnp.float32)
        _, chunk_of, seen_before = lax.fori_loop(0, ncc, mark_equal, (zero, jnp.zeros((1, LANES), jnp.int32), zero))
        need_here = need - seen_before

        eqc_ref[...] = jnp.zeros(eqc_ref.shape, jnp.float32)

        def gather_chunk(c, carry):
            r0 = pl.multiple_of(c * COUNT_CHUNK, COUNT_CHUNK)
            eqc_ref[...] += jnp.where(chunk_of == c, eq_ref[pl.ds(r0, COUNT_CHUNK), :], 0.0)
            return carry

        lax.fori_loop(0, ncc, gather_chunk, 0)

        def idx_step(b, lim):
            cand = lim | (jnp.int32(1) << (COUNT_CHUNK.bit_length() - 2 - b))
            below = jnp.sum(_fold_rows(jnp.where(crow_iota < cand, eqc_ref[...], 0.0), jnp.sum), axis=0, keepdims=True)
            return jnp.where(below < need_here, cand, lim)

        lim = chunk_of * COUNT_CHUNK + lax.fori_loop(0, COUNT_CHUNK.bit_length() - 1, idx_step,
                                                     jnp.zeros((1, LANES), jnp.int32))

        def demote(c, carry):
            r0 = pl.multiple_of(c * COUNT_CHUNK, COUNT_CHUNK)
            x = sc_ref[pl.ds(r0, COUNT_CHUNK), :]
            drop = (eq_ref[pl.ds(r0, COUNT_CHUNK), :] > 0.0) & (r0 + crow_iota > lim) & tie_lane
            sc_ref[pl.ds(r0, COUNT_CHUNK), :] = jnp.where(drop, neg_inf, x)
            return carry

        lax.fori_loop(0, ncc, demote, 0)

    mask_off = float(jnp.finfo(MXU_DTYPE).min)

    def attn_scores(c, buf):
        sq_ref, mx_ref = buf
        r0 = chunk_start(c)
        mask = jnp.where(sc_ref[pl.ds(r0, KEY_CHUNK), :] >= thr, 0.0, mask_off).astype(MXU_DTYPE)
        kaug = jnp.concatenate([kbd_ref[0, pl.ds(r0, KEY_CHUNK), :], mask], axis=1)
        for p in range(N_HEADS // 2):
            s = _nt_dot(kaug, pair_q(qbs_ref, p))
            sq_ref[:, p * 2 * BLOCK:(p + 1) * 2 * BLOCK] = s
            for hh in range(2):
                cols = slice((2 * p + hh) * BLOCK, (2 * p + hh + 1) * BLOCK)
                mx_ref[:, cols] = _fold_rows(s[:, hh * BLOCK:(hh + 1) * BLOCK], jnp.max)

    def accumulate(c, buf):
        sq_ref, mx_ref = buf
        m_run = m_ref[...]
        m_new = jnp.maximum(m_run, jnp.max(mx_ref[...], axis=0, keepdims=True))
        m_ref[...] = m_new
        acc_ref[...] = acc_ref[...] * jnp.exp2(m_run - m_new)
        vt = vbt_ref[0, c]
        for p in range(N_HEADS // 2):
            pcols = slice(p * 2 * BLOCK, (p + 1) * 2 * BLOCK)
            e = jnp.exp2(sq_ref[:, pcols] - m_new[:, pcols])
            acc_ref[:, pcols] += jnp.dot(vt, e.astype(MXU_DTYPE), preferred_element_type=jnp.float32)

    acc_ref[...] = jnp.zeros(acc_ref.shape, jnp.float32)
    m_ref[...] = jnp.full(m_ref.shape, f32_lowest, jnp.float32)
    pipeline(attn_scores, accumulate)

    o_t = acc_ref[:HEAD_DIM, :] / acc_ref[HEAD_DIM:HEAD_DIM + 1, :]
    for p in range(N_HEADS // 2):
        pair = jnp.concatenate([o_t[:, (2 * p) * BLOCK:(2 * p + 1) * BLOCK],
                                o_t[:, (2 * p + 1) * BLOCK:(2 * p + 2) * BLOCK]], axis=0)
        out_ref[0, :, p * LANES:(p + 1) * LANES] = pair.T.astype(out_ref.dtype)


def _sparse(qi, qb, wt, kid, kbd, vbt, batch, seq):
    topk = min(TOPK_MAX, seq // 4)
    nq = seq // BLOCK
    qblk = pl.BlockSpec((1, BLOCK, D_HEADS), lambda b, i: (b, i, 0))
    full = pl.BlockSpec((1, seq, LANES), lambda b, i: (b, 0, 0))
    kern = functools.partial(_sparse_kernel, topk=topk)
    return pl.pallas_call(
        kern,
        grid=(batch, nq),
        in_specs=[qblk, qblk,
                  pl.BlockSpec((1, IDX_HEADS, BLOCK), lambda b, i: (b, 0, i)),
                  full, full,
                  pl.BlockSpec((1, seq // KEY_CHUNK, HEAD_DIM + L_ROWS, KEY_CHUNK), lambda b, i: (b, 0, 0, 0))],
        out_specs=qblk,
        out_shape=jax.ShapeDtypeStruct((batch, seq, D_HEADS), MXU_DTYPE),
        scratch_shapes=[pltpu.VMEM((seq, LANES), jnp.float32),
                        pltpu.VMEM((seq, LANES), jnp.float32),
                        pltpu.VMEM((COUNT_CHUNK, LANES), jnp.float32),
                        pltpu.VMEM((N_HEADS * BLOCK, LANES), MXU_DTYPE),
                        pltpu.VMEM((N_HEADS * BLOCK, 2 * LANES), MXU_DTYPE),
                        pltpu.VMEM((HEAD_DIM + L_ROWS, N_HEADS * BLOCK), jnp.float32),
                        pltpu.VMEM((1, N_HEADS * BLOCK), jnp.float32),
                        pltpu.VMEM((KEY_CHUNK, N_HEADS * BLOCK), jnp.float32),
                        pltpu.VMEM((KEY_CHUNK, N_HEADS * BLOCK), jnp.float32),
                        pltpu.VMEM((8, N_HEADS * BLOCK), jnp.float32),
                        pltpu.VMEM((8, N_HEADS * BLOCK), jnp.float32)],
        compiler_params=pltpu.CompilerParams(dimension_semantics=("arbitrary", "arbitrary"),
                                             vmem_limit_bytes=VMEM_LIMIT),
        name="indexer_sparse_attn",
    )(qi, qb, wt, kid, kbd, vbt)


def _merge_kernel(x_ref, g_ref, *refs):
    n_pat = len(DILATED_PATTERNS)
    o_refs, l_refs = refs[:n_pat], refs[n_pat:2 * n_pat]
    yb_ref, wg_ref, wua_ref, wub_ref, wo_ref, x1_ref = refs[2 * n_pat:2 * n_pat + 6]
    slabs = refs[2 * n_pat + 6:]
    x = x_ref[...]
    h = _rms(x, g_ref[...]).astype(MXU_DTYPE)
    tm = x.shape[0]
    n_grp = D_HEADS // LANES

    o_src, l_src, k = [], [], 0
    for (_, d), o_ref, l_ref in zip(DILATED_PATTERNS, o_refs, l_refs):
        if d == 1:
            o_src.append(lambda g, r=o_ref: r[:, g * LANES:(g + 1) * LANES].astype(jnp.float32))
            l_src.append(lambda g, r=l_ref: r[:, g * LANES:(g + 1) * LANES])
            continue
        o_slab, l_slab = slabs[2 * k], slabs[2 * k + 1]
        k += 1
        for r in range(d):
            for g in range(n_grp):
                cols = slice(g * LANES, (g + 1) * LANES)
                o_slab[g, pl.ds(r, tm // d, stride=d), :] = o_ref[0, r, :, cols].astype(jnp.float32)
                l_slab[g, pl.ds(r, tm // d, stride=d), :] = l_ref[0, r, :, cols]
        o_src.append(lambda g, s=o_slab: s[g])
        l_src.append(lambda g, s=l_slab: s[g])

    parts = []
    for g in range(n_grp):
        ls = [f(g) for f in l_src]
        m = functools.reduce(jnp.maximum, ls)
        es = [jnp.exp(l - m) for l in ls]
        num = functools.reduce(jnp.add, [e * f(g) for e, f in zip(es, o_src)])
        parts.append((num / functools.reduce(jnp.add, es)).astype(MXU_DTYPE))
    ya = jnp.concatenate(parts, axis=1)
    ua = jnp.dot(ya, wua_ref[...], preferred_element_type=jnp.float32)
    ub = jnp.dot(yb_ref[...], wub_ref[...], preferred_element_type=jnp.float32)
    ga = jnp.dot(h, wg_ref[:, :D_MODEL], preferred_element_type=jnp.float32)
    gb = jnp.dot(h, wg_ref[:, D_MODEL:], preferred_element_type=jnp.float32)
    merged = jax.nn.sigmoid(ga) * ua + jax.nn.sigmoid(gb) * ub
    x1_ref[...] = x + jnp.dot(merged.astype(MXU_DTYPE), wo_ref[...], preferred_element_type=jnp.float32)


def _merge(x2, g, os_, lses, yb, wg, wua, wub, wo, seq):
    n = x2.shape[0]
    tm = ROW_TILE
    tiles_per_seq = seq // tm
    row = lambda i: (i, 0)
    const = lambda i: (0, 0)
    half = pl.BlockSpec((tm, D_HEADS), row)
    pat_specs = [half if d == 1 else
                 pl.BlockSpec((1, d, tm // d, D_HEADS), lambda i: (i // tiles_per_seq, 0, i % tiles_per_seq, 0))
                 for _, d in DILATED_PATTERNS]
    return pl.pallas_call(
        _merge_kernel,
        grid=(n // tm,),
        in_specs=[pl.BlockSpec((tm, D_MODEL), row), pl.BlockSpec((1, D_MODEL), const)] + pat_specs * 2 + [half] + [
            pl.BlockSpec((D_MODEL, 2 * D_MODEL), const),
            pl.BlockSpec((D_HEADS, D_MODEL), const),
            pl.BlockSpec((D_HEADS, D_MODEL), const),
            pl.BlockSpec((D_MODEL, D_MODEL), const)],
        out_specs=pl.BlockSpec((tm, D_MODEL), row),
        out_shape=jax.ShapeDtypeStruct((n, D_MODEL), jnp.float32),
        scratch_shapes=[pltpu.VMEM((D_HEADS // LANES, tm, LANES), jnp.float32)] * (2 * len(RESIDUE_DILATIONS)),
        compiler_params=pltpu.CompilerParams(dimension_semantics=("arbitrary",), vmem_limit_bytes=VMEM_LIMIT),
        name="mix_gate_out",
    )(x2, g, *os_, *lses, yb, wg, wua, wub, wo)


def _ffn_kernel(x_ref, g_ref, wgate_ref, wup_ref, wdown_ref, gf_ref, out_ref):
    x = x_ref[...]
    h = _rms(x, g_ref[...]).astype(MXU_DTYPE)
    y = x
    for c0 in range(0, D_FF, FF_CHUNK):
        a = jnp.dot(h, wgate_ref[:, c0:c0 + FF_CHUNK], preferred_element_type=jnp.float32)
        u = jnp.dot(h, wup_ref[:, c0:c0 + FF_CHUNK], preferred_element_type=jnp.float32)
        act = (a * jax.nn.sigmoid(a) * u).astype(MXU_DTYPE)
        y = y + jnp.dot(act, wdown_ref[c0:c0 + FF_CHUNK, :], preferred_element_type=jnp.float32)
    out_ref[...] = _rms(y, gf_ref[...])


def _ffn(x1, g, wgate, wup, wdown, gf):
    n = x1.shape[0]
    tm = ROW_TILE
    row = lambda i: (i, 0)
    const = lambda i: (0, 0)
    return pl.pallas_call(
        _ffn_kernel,
        grid=(n // tm,),
        in_specs=[pl.BlockSpec((tm, D_MODEL), row), pl.BlockSpec((1, D_MODEL), const),
                  pl.BlockSpec((D_MODEL, D_FF), const), pl.BlockSpec((D_MODEL, D_FF), const),
                  pl.BlockSpec((D_FF, D_MODEL), const), pl.BlockSpec((1, D_MODEL), const)],
        out_specs=pl.BlockSpec((tm, D_MODEL), row),
        out_shape=jax.ShapeDtypeStruct((n, D_MODEL), jnp.float32),
        compiler_params=pltpu.CompilerParams(dimension_semantics=("arbitrary",), vmem_limit_bytes=VMEM_LIMIT),
        name="ffn_norm",
    )(x1, g, wgate, wup, wdown, gf)


def _rope_tables(seq):
    inv_freq = ROPE_THETA ** (-jnp.arange(HALF, dtype=jnp.float32) / HALF)
    ang = jnp.arange(seq, dtype=jnp.int32).astype(jnp.float32)[:, None] * inv_freq[None, :]
    cos, sin = jnp.cos(ang), jnp.sin(ang)
    return jnp.tile(cos, (1, 4)), jnp.concatenate([-sin, -sin, sin, sin], axis=1)


def kernel(x, norm_mix, w_in, w_up_a, w_up_b, w_out, norm_ffn, w_gate, w_up, w_down, norm_final):
    batch, seq, _ = x.shape
    assert seq % max(d * BLOCK for _, d in DILATED_PATTERNS) == 0 and seq % KEY_CHUNK == 0
    assert all(w // d == BLOCK for w, d in DILATED_PATTERNS)
    n = batch * seq
    bf = MXU_DTYPE
    xf = x.reshape(n, D_MODEL)
    cos_t, sin_t = _rope_tables(seq)
    for layer in range(w_in.shape[0]):
        w = w_in[layer]
        w_pack = jnp.pad(w[:, _packed_columns()], ((0, 0), (0, _P_TOTAL - _P_MISC - HEAD_DIM - IDX_HEADS))).astype(bf)
        w_gates = w[:, _GA:].astype(bf)
        qas, kas, vas, qb, qi, kbd, kid, misc = _project(xf, norm_mix[layer][None], w_pack, cos_t, sin_t, seq)

        dil = []
        for (_, d), q, k, v in zip(DILATED_PATTERNS, qas, kas, vas):
            if d == 1:
                o, lse = _dilated(*(z.reshape(batch, 1, seq, D_HEADS) for z in (q, k, v)), d)
                dil.append((o.reshape(n, D_HEADS), lse.reshape(n, D_HEADS)))
            else:
                dil.append(_dilated(q, k, v, d))

        ones_col = (jnp.arange(L_ROWS) == 0).astype(bf)[None, :]
        vb = jnp.concatenate([misc[:, :HEAD_DIM].astype(bf), jnp.broadcast_to(ones_col, (n, L_ROWS))], axis=1)
        vbt = jnp.swapaxes(vb.reshape(batch, seq // KEY_CHUNK, KEY_CHUNK, HEAD_DIM + L_ROWS), 2, 3)
        wt = jnp.swapaxes(misc[:, HEAD_DIM:HEAD_DIM + IDX_HEADS].reshape(batch, seq, IDX_HEADS), 1, 2)
        r3 = lambda z: z.reshape(batch, seq, z.shape[-1])
        yb = _sparse(r3(qi), r3(qb), wt, r3(kid), r3(kbd), vbt, batch, seq).reshape(n, D_HEADS)

        x1 = _merge(xf, norm_mix[layer][None], [o for o, _ in dil], [l for _, l in dil], yb,
                    w_gates, w_up_a[layer].astype(bf), w_up_b[layer].astype(bf), w_out[layer].astype(bf), seq)
        last = layer == w_in.shape[0] - 1
        assert last, "the final norm is fused into the FFN kernel of the last layer"
        xf = _ffn(x1, norm_ffn[layer][None], w_gate[layer].astype(bf), w_up[layer].astype(bf),
                  w_down[layer].astype(bf), norm_final[None])
    return xf.reshape(batch, seq, D_MODEL)
```

```python
import functools

import numpy as np
import jax
import jax.numpy as jnp
from jax import lax
from jax.experimental import pallas as pl
from jax.experimental.pallas import tpu as pltpu

D_MODEL = 1024
HEAD_DIM = 64
HALF = HEAD_DIM // 2
N_HEADS = 8
D_HEADS = N_HEADS * HEAD_DIM
IDX_HEADS = 8
DILATED_PATTERNS = ((128, 1), (512, 4), (2048, 16))
RESIDUE_DILATIONS = tuple(d for _, d in DILATED_PATTERNS if d > 1)
TOPK_MAX = 256
D_FF = 2816
ROPE_THETA = 10000.0
RMS_EPS = 1e-6
BLOCK = 128
ATTN_SCALE = HEAD_DIM ** -0.5
IDX_SCALE = (HEAD_DIM ** -0.5) * (IDX_HEADS ** -0.5)
LOG2_E = float(np.log2(np.e))

LANES = 128
MIN_NORMAL_KEY = 1 << 23
KEY_CHUNK = 1024
COUNT_CHUNK = 512
COUNT_FOLD = 8
VALUE_PASSES = 14
DESCENT_STEPS = 6
SEARCH_PASSES_MIN = 12
L_ROWS = 8
ROW_TILE = 512
DIL_BLOCKS = 8
FF_CHUNK = 1408
VMEM_LIMIT = 56 * 1024 * 1024
MXU_DTYPE = jnp.bfloat16

_SPLITS = (D_HEADS, D_HEADS, D_HEADS, D_HEADS, HEAD_DIM, HEAD_DIM, IDX_HEADS * HEAD_DIM, HEAD_DIM, IDX_HEADS,
           D_MODEL, D_MODEL)
_OFF = np.concatenate([[0], np.cumsum(_SPLITS)])
(_QA, _KA, _VA, _QB, _KB, _VB, _QI, _KI, _WI, _GA, _GB) = (int(o) for o in _OFF[:-1])

_P_QA, _P_KA, _P_VA, _P_QB, _P_QI = 0, 512, 1024, 1536, 2048
_P_KBD, _P_KID, _P_MISC = 2560, 2688, 2816
_P_TOTAL = 2944


def _pair_perm():
    idx = np.empty(D_HEADS, np.int64)
    for j in range(D_HEADS):
        g, l = divmod(j, LANES)
        quarter, e = divmod(l, HALF)
        head = 2 * g + (quarter % 2)
        idx[j] = head * HEAD_DIM + (quarter // 2) * HALF + e
    return idx


def _dup_perm():
    idx = np.empty(LANES, np.int64)
    for l in range(LANES):
        quarter, e = divmod(l, HALF)
        idx[l] = (quarter // 2) * HALF + e
    return idx


def _packed_columns():
    pp, dp = _pair_perm(), _dup_perm()
    return np.concatenate([
        _QA + pp, _KA + pp, _VA + np.arange(D_HEADS), _QB + pp, _QI + pp,
        _KB + dp, _KI + dp, _VB + np.arange(HEAD_DIM), _WI + np.arange(IDX_HEADS)])


def _rms(x, g):
    ms = jnp.mean(x * x, axis=-1, keepdims=True)
    return x * lax.rsqrt(ms + RMS_EPS) * g


def _nt_dot(a, b):
    return lax.dot_general(a, b, (((1,), (1,)), ((), ())), preferred_element_type=jnp.float32)


def _proj_kernel(x_ref, g_ref, w_ref, cos_ref, sin_ref, *refs):
    n_lay = 1 + len(RESIDUE_DILATIONS)
    qa_refs, ka_refs, va_refs = refs[:n_lay], refs[n_lay:2 * n_lay], refs[2 * n_lay:3 * n_lay]
    qb_ref, qi_ref, kbd_ref, kid_ref, misc_ref, h_ref, slab_ref = refs[3 * n_lay:]
    h_ref[...] = _rms(x_ref[...], g_ref[...]).astype(MXU_DTYPE)
    cos = cos_ref[...]
    sin = sin_ref[...]

    def mm(c0, width):
        return jnp.dot(h_ref[...], w_ref[:, c0:c0 + width], preferred_element_type=jnp.float32)

    def rope(z):
        parts = []
        for g in range(z.shape[1] // LANES):
            zg = z[:, g * LANES:(g + 1) * LANES]
            parts.append(zg * cos + pltpu.roll(zg, 2 * HALF, axis=1) * sin)
        return parts[0] if len(parts) == 1 else jnp.concatenate(parts, axis=1)

    def emit(y, out_refs):
        out_refs[0][...] = y.astype(MXU_DTYPE)
        for g in range(D_HEADS // LANES):
            slab_ref[g] = y[:, g * LANES:(g + 1) * LANES]
        for d, ref in zip(RESIDUE_DILATIONS, out_refs[1:]):
            rows = y.shape[0] // d
            for r in range(d):
                for g in range(D_HEADS // LANES):
                    ref[0, r, :, g * LANES:(g + 1) * LANES] = (
                        slab_ref[g, pl.ds(r, rows, stride=d), :].astype(MXU_DTYPE))

    emit(rope(mm(_P_QA, D_HEADS)) * ATTN_SCALE, qa_refs)
    emit(rope(mm(_P_KA, D_HEADS)), ka_refs)
    emit(mm(_P_VA, D_HEADS), va_refs)
    qb_ref[...] = (rope(mm(_P_QB, D_HEADS)) * (ATTN_SCALE * LOG2_E)).astype(MXU_DTYPE)
    qi_ref[...] = rope(mm(_P_QI, D_HEADS)).astype(MXU_DTYPE)
    kbd_ref[...] = rope(mm(_P_KBD, LANES)).astype(MXU_DTYPE)
    kid_ref[...] = rope(mm(_P_KID, LANES)).astype(MXU_DTYPE)
    misc_ref[...] = mm(_P_MISC, LANES)


def _project(x2, g, w_pack, cos_t, sin_t, seq):
    n = x2.shape[0]
    tm = ROW_TILE
    tiles_per_seq = seq // tm
    row = lambda i: (i, 0)
    const = lambda i: (0, 0)
    pos = lambda i: (i % tiles_per_seq, 0)
    batch = n // seq
    wide = jax.ShapeDtypeStruct((n, D_HEADS), MXU_DTYPE)
    narrow = jax.ShapeDtypeStruct((n, LANES), MXU_DTYPE)
    wide_spec = pl.BlockSpec((tm, D_HEADS), row)
    lay_shapes = [wide] + [jax.ShapeDtypeStruct((batch, d, seq // d, D_HEADS), MXU_DTYPE) for d in RESIDUE_DILATIONS]
    lay_specs = [wide_spec] + [
        pl.BlockSpec((1, d, tm // d, D_HEADS), lambda i: (i // tiles_per_seq, 0, i % tiles_per_seq, 0))
        for d in RESIDUE_DILATIONS]
    n_lay = len(lay_shapes)
    outs = pl.pallas_call(
        _proj_kernel,
        grid=(n // tm,),
        in_specs=[
            pl.BlockSpec((tm, D_MODEL), row),
            pl.BlockSpec((1, D_MODEL), const),
            pl.BlockSpec((D_MODEL, _P_TOTAL), const),
            pl.BlockSpec((tm, LANES), pos),
            pl.BlockSpec((tm, LANES), pos),
        ],
        out_specs=lay_specs * 3 + [wide_spec] * 2 + [pl.BlockSpec((tm, LANES), row)] * 3,
        out_shape=lay_shapes * 3 + [wide] * 2 + [narrow, narrow, jax.ShapeDtypeStruct((n, LANES), jnp.float32)],
        scratch_shapes=[pltpu.VMEM((tm, D_MODEL), MXU_DTYPE),
                        pltpu.VMEM((D_HEADS // LANES, tm, LANES), jnp.float32)],
        compiler_params=pltpu.CompilerParams(dimension_semantics=("arbitrary",), vmem_limit_bytes=VMEM_LIMIT),
        name="proj_rope",
    )(x2, g, w_pack, cos_t, sin_t)
    return (outs[:n_lay], outs[n_lay:2 * n_lay], outs[2 * n_lay:3 * n_lay]) + tuple(outs[3 * n_lay:])


def _dil_kernel(q_ref, k_ref, v_ref, kp_ref, vp_ref, o_ref, lse_ref, *, n_blocks):
    n = pl.program_id(2)
    kj = lax.broadcasted_iota(jnp.int32, (2 * BLOCK, LANES), 0)
    qi = lax.broadcasted_iota(jnp.int32, (2 * BLOCK, LANES), 1)
    band = (kj >= qi) & (kj <= qi + BLOCK)
    lane = lax.broadcasted_iota(jnp.int32, (1, LANES), 1)
    pairs = range(N_HEADS // 2)
    col = lambda p: slice(p * LANES, (p + 1) * LANES)

    for j in range(n_blocks):
        rows = slice(j * BLOCK, (j + 1) * BLOCK)
        if j == 0:
            k_prev = lambda p: kp_ref[0, 0, :, col(p)]
            v_prev = lambda p: vp_ref[0, 0, :, col(p)]
            bias = jnp.where(band & ((kj >= BLOCK) | (n > 0)), 0.0, -jnp.inf)
        else:
            prev_rows = slice((j - 1) * BLOCK, j * BLOCK)
            k_prev = lambda p, r=prev_rows: k_ref[0, 0, r, col(p)]
            v_prev = lambda p, r=prev_rows: v_ref[0, 0, r, col(p)]
            bias = jnp.where(band, 0.0, -jnp.inf)

        scores = []
        for p in pairs:
            qp = q_ref[0, 0, rows, col(p)]
            qs = jnp.concatenate([jnp.where(((lane // HALF) % 2) == hh, qp, jnp.zeros_like(qp))
                                  for hh in range(2)], axis=0)
            k2 = jnp.concatenate([k_prev(p), k_ref[0, 0, rows, col(p)]], axis=0)
            scores.append(_nt_dot(k2, qs))

        probs, dens, lses = [], [], []
        for p in pairs:
            for hh in range(2):
                sh = scores[p][:, hh * BLOCK:(hh + 1) * BLOCK] + bias
                m = jnp.max(sh, axis=0, keepdims=True)
                e = jnp.exp(sh - m)
                den = jnp.sum(e, axis=0, keepdims=True)
                probs.append(e.astype(MXU_DTYPE))
                dens.append(den)
                lses.append(m + jnp.log(den))

        for p in pairs:
            v2 = jnp.concatenate([v_prev(p), v_ref[0, 0, rows, col(p)]], axis=0)
            o_t = lax.dot_general(v2, jnp.concatenate(probs[2 * p:2 * p + 2], axis=1), (((0,), (0,)), ((), ())),
                                  preferred_element_type=jnp.float32)
            tile = jnp.concatenate([o_t[:HEAD_DIM, :BLOCK] / dens[2 * p],
                                    o_t[HEAD_DIM:, BLOCK:] / dens[2 * p + 1]], axis=0)
            o_ref[0, 0, rows, col(p)] = tile.T.astype(o_ref.dtype)
            lse_tile = jnp.concatenate([jnp.broadcast_to(l, (HEAD_DIM, BLOCK)) for l in lses[2 * p:2 * p + 2]],
                                       axis=0)
            lse_ref[0, 0, rows, col(p)] = lse_tile.T


def _dilated(q, k, v, dilation):
    batch, _, m_len, _ = q.shape
    n_blocks = min(DIL_BLOCKS, m_len // BLOCK)
    tile_rows = n_blocks * BLOCK
    cur = lambda b, r, n: (b, r, n, 0)
    prev = lambda b, r, n: (b, r, jnp.maximum(n * n_blocks - 1, 0), 0)
    tile = (1, 1, tile_rows, D_HEADS)
    blk = (1, 1, BLOCK, D_HEADS)
    return pl.pallas_call(
        functools.partial(_dil_kernel, n_blocks=n_blocks),
        grid=(batch, dilation, m_len // tile_rows),
        in_specs=[pl.BlockSpec(tile, cur), pl.BlockSpec(tile, cur), pl.BlockSpec(tile, cur),
                  pl.BlockSpec(blk, prev), pl.BlockSpec(blk, prev)],
        out_specs=[pl.BlockSpec(tile, cur), pl.BlockSpec(tile, cur)],
        out_shape=[jax.ShapeDtypeStruct(q.shape, MXU_DTYPE), jax.ShapeDtypeStruct(q.shape, jnp.float32)],
        compiler_params=pltpu.CompilerParams(dimension_semantics=("arbitrary",) * 3),
        name=f"dilated_d{dilation}",
    )(q, k, v, k, v)


def _key_to_f32(key):
    bits = key ^ ((key >> 31) & jnp.int32(0x7FFFFFFF))
    return lax.bitcast_convert_type(bits, jnp.float32)


def _f32_to_key(x):
    bits = lax.bitcast_convert_type(x, jnp.int32)
    return bits ^ ((bits >> 31) & jnp.int32(0x7FFFFFFF))


def _fold_rows(x, op):
    rows = x.shape[0]
    y = op(x.reshape(rows // 64, 64, LANES), axis=0)
    return op(y.reshape(8, 8, LANES), axis=0)


def _sparse_kernel(qi_ref, qb_ref, wt_ref, kid_ref, kbd_ref, vbt_ref, out_ref,
                   sc_ref, eq_ref, eqc_ref, qis_ref, qbs_ref, acc_ref, m_ref, sqa_ref, sqb_ref, mxa_ref,
                   mxb_ref, *, topk):
    i = pl.program_id(1)
    nch = i // (KEY_CHUNK // BLOCK) + 1
    ncc = i // (COUNT_CHUNK // BLOCK) + 1
    lane = lax.broadcasted_iota(jnp.int32, (1, LANES), 1)
    t_idx = i * BLOCK + lane
    row_iota = lax.broadcasted_iota(jnp.int32, (KEY_CHUNK, LANES), 0)
    crow_iota = lax.broadcasted_iota(jnp.int32, (COUNT_CHUNK, LANES), 0)
    neg_inf = jnp.float32(-jnp.inf)
    f32_lowest = jnp.float32(jnp.finfo(jnp.float32).min)

    eye = jnp.where(lax.broadcasted_iota(jnp.int32, (BLOCK, LANES), 0) == lane, 1.0, 0.0).astype(MXU_DTYPE)
    for h in range(N_HEADS):
        cols = slice((h // 2) * LANES, (h // 2 + 1) * LANES)
        head_lanes = ((lane // HALF) % 2) == (h % 2)
        rows = slice(h * BLOCK, (h + 1) * BLOCK)
        qis_ref[rows, :] = jnp.where(head_lanes, qi_ref[0, :, cols], jnp.zeros((), MXU_DTYPE))
        qbs_ref[rows, :LANES] = jnp.where(head_lanes, qb_ref[0, :, cols], jnp.zeros((), MXU_DTYPE))
        qbs_ref[rows, LANES:] = eye

    def chunk_start(c):
        return pl.multiple_of(c * KEY_CHUNK, KEY_CHUNK)

    def pair_q(ref, p):
        return ref[p * 2 * BLOCK:(p + 1) * 2 * BLOCK, :]

    buf_a, buf_b = (sqa_ref, mxa_ref), (sqb_ref, mxb_ref)

    def pipeline(produce, consume):
        produce(0, buf_a)

        def two_chunks(j):
            c = 2 * j
            produce(c + 1, buf_b)
            consume(c, buf_a)
            produce(c + 2, buf_a)
            consume(c + 1, buf_b)

        def four_chunks(t, carry):
            two_chunks(2 * t)
            two_chunks(2 * t + 1)
            return carry

        n_two = (nch - 1) // 2
        lax.fori_loop(0, n_two // 2, four_chunks, 0)

        @pl.when(n_two % 2 == 1)
        def _():
            two_chunks(n_two - 1)

        last = nch - 1

        @pl.when(last % 2 == 1)
        def _():
            produce(last, buf_b)
            consume(last - 1, buf_a)
            consume(last, buf_b)

        @pl.when(last % 2 == 0)
        def _():
            consume(last, buf_a)

    def score_chunk(c, carry):
        r0 = chunk_start(c)
        kc = kid_ref[0, pl.ds(r0, KEY_CHUNK), :]
        acc = jnp.zeros((KEY_CHUNK, LANES), jnp.float32)
        for p in range(N_HEADS // 2):
            d = _nt_dot(kc, pair_q(qis_ref, p))
            for hh in range(2):
                h = 2 * p + hh
                acc = acc + jnp.maximum(d[:, hh * BLOCK:(hh + 1) * BLOCK], 0.0) * wt_ref[0, h:h + 1, :]
        admissible = r0 + row_iota <= t_idx
        scores = acc * IDX_SCALE
        masked = jnp.where(admissible, scores, neg_inf)
        sc_ref[pl.ds(r0, KEY_CHUNK), :] = masked
        top8, bot8 = carry
        return (jnp.maximum(top8, _fold_rows(masked, jnp.max)),
                jnp.minimum(bot8, _fold_rows(jnp.where(admissible, scores, -neg_inf), jnp.min)))

    def for_chunks(n_chunks, body, init):
        carry = lax.fori_loop(0, n_chunks // 4,
                              lambda j, a: body(4 * j + 3, body(4 * j + 2, body(4 * j + 1, body(4 * j, a)))), init)
        done = (n_chunks // 4) * 4
        carry = lax.cond((n_chunks & 2) != 0, lambda a: body(done + 1, body(done, a)), lambda a: a, carry)
        done = done + (n_chunks & 2)
        return lax.cond((n_chunks & 1) != 0, lambda a: body(done, a), lambda a: a, carry)

    top8, bot8 = for_chunks(nch, score_chunk, (jnp.full((8, LANES), neg_inf, jnp.float32),
                                               jnp.full((8, LANES), -neg_inf, jnp.float32)))
    top = jnp.max(top8, axis=0, keepdims=True)
    bot = jnp.min(bot8, axis=0, keepdims=True)

    def fold_scores(value, op, pairwise, start, src_ref=sc_ref):
        def body(c, acc):
            r0 = pl.multiple_of(c * COUNT_CHUNK, COUNT_CHUNK)
            val = value(src_ref[pl.ds(r0, COUNT_CHUNK), :], r0)
            return pairwise(acc, op(val.reshape(COUNT_FOLD, COUNT_CHUNK // COUNT_FOLD, LANES), axis=0))

        acc = for_chunks(ncc, body, jnp.full((COUNT_CHUNK // COUNT_FOLD, LANES), start, jnp.float32))
        return op(acc, axis=0, keepdims=True)

    def count(pred):
        return fold_scores(lambda x, r0: jnp.where(pred(x, r0), 1.0, 0.0), jnp.sum, jnp.add, 0.0)

    key_top = _f32_to_key(top)
    lo = _f32_to_key(bot) - jnp.int32(1 << 23)
    hi = key_top + 1
    hi = jnp.where((hi >= -MIN_NORMAL_KEY) & (hi < MIN_NORMAL_KEY), jnp.int32(MIN_NORMAL_KEY), hi)
    unknown = jnp.float32(2 * sc_ref.shape[0])

    def is_settled(carry):
        lo, hi, cnt_lo = carry[:3]
        return (cnt_lo == topk) | (hi - lo == 1) | (t_idx < topk)

    def probe(mid, carry):
        lo, hi, cnt_lo, thr_lo, thr_hi = carry
        thr_c = _key_to_f32(mid)
        cnt = count(lambda x, r0: x >= thr_c)
        take = (cnt >= topk) & ~is_settled(carry)
        drop = (cnt < topk) & ~is_settled(carry)
        return (jnp.where(take, mid, lo), jnp.where(drop, mid, hi), jnp.where(take, cnt, cnt_lo),
                jnp.where(take, thr_c, thr_lo), jnp.where(drop, thr_c, thr_hi))

    def halve(by_value, carry):
        lo, hi = carry[:2]
        mid = lo + lax.shift_right_logical(hi - lo, 1)
        if by_value:
            mid_val = _f32_to_key(0.5 * _key_to_f32(lo) + 0.5 * _key_to_f32(hi))
            mid = jnp.where((mid_val > lo) & (mid_val < hi), mid_val, mid)
        return probe(mid, carry)

    def descend(carry):
        lo, hi, cnt_lo, thr_lo, thr_hi = carry
        v = fold_scores(lambda x, r0: jnp.where(x < thr_hi, x, neg_inf), jnp.max, jnp.maximum, neg_inf)
        cnt = count(lambda x, r0: x >= v)
        key_v = _f32_to_key(v)
        take = (cnt >= topk) & ~is_settled(carry)
        drop = (cnt < topk) & ~is_settled(carry)
        return (jnp.where(take, key_v, lo), jnp.where(take, key_v + 1, jnp.where(drop, key_v, hi)),
                jnp.where(take, cnt, cnt_lo), jnp.where(take, v, thr_lo), jnp.where(drop, v, thr_hi))

    def pending(carry):
        return jnp.max(jnp.where(is_settled(carry), 0.0, 1.0))

    state = lax.fori_loop(0, SEARCH_PASSES_MIN, lambda b, c: halve(True, c),
                          (lo, hi, jnp.full((1, LANES), unknown), _key_to_f32(lo), _key_to_f32(hi)))

    def two_passes(carry):
        b, state, _ = carry
        state = lax.cond(b < VALUE_PASSES, lambda s: halve(True, halve(True, s)),
                         lambda s: lax.cond(b < VALUE_PASSES + 2 * DESCENT_STEPS, descend,
                                            lambda t: halve(False, halve(False, t)), s), state)
        return b + 2, state, pending(state)

    _, (lo, hi, cnt_ge, thr_raw, _), _ = lax.while_loop(
        lambda c: (c[0] < VALUE_PASSES + 2 * DESCENT_STEPS + 32) & (c[2] > 0.0), two_passes,
        (jnp.int32(SEARCH_PASSES_MIN), state, pending(state)))
    enough = thr_raw > f32_lowest
    thr = jnp.where(enough, thr_raw, f32_lowest)

    tie_lane = enough & (cnt_ge > topk) & (t_idx >= topk)

    @pl.when(jnp.max(jnp.where(tie_lane, 1.0, 0.0)) > 0.0)
    def _():
        need = topk - count(lambda x, r0: x > thr)

        def mark_equal(c, carry):
            seen, chunk_of, seen_before = carry
            r0 = pl.multiple_of(c * COUNT_CHUNK, COUNT_CHUNK)
            eq = jnp.where(sc_ref[pl.ds(r0, COUNT_CHUNK), :] == thr, 1.0, 0.0)
            eq_ref[pl.ds(r0, COUNT_CHUNK), :] = eq
            here = jnp.sum(_fold_rows(eq, jnp.sum), axis=0, keepdims=True)
            first_reach = (seen < need) & (seen + here >= need)
            return (seen + here, jnp.where(first_reach, c, chunk_of), jnp.where(first_reach, seen, seen_before))

        zero = jnp.zeros((1, LANES), jnp.float32)
        _, chunk_of, seen_before = lax.fori_loop(0, ncc, mark_equal, (zero, jnp.zeros((1, LANES), jnp.int32), zero))
        need_here = need - seen_before

        eqc_ref[...] = jnp.zeros(eqc_ref.shape, jnp.float32)

        def gather_chunk(c, carry):
            r0 = pl.multiple_of(c * COUNT_CHUNK, COUNT_CHUNK)
            eqc_ref[...] += jnp.where(chunk_of == c, eq_ref[pl.ds(r0, COUNT_CHUNK), :], 0.0)
            return carry

        lax.fori_loop(0, ncc, gather_chunk, 0)

        def idx_step(b, lim):
            cand = lim | (jnp.int32(1) << (COUNT_CHUNK.bit_length() - 2 - b))
            below = jnp.sum(_fold_rows(jnp.where(crow_iota < cand, eqc_ref[...], 0.0), jnp.sum), axis=0, keepdims=True)
            return jnp.where(below < need_here, cand, lim)

        lim = chunk_of * COUNT_CHUNK + lax.fori_loop(0, COUNT_CHUNK.bit_length() - 1, idx_step,
                                                     jnp.zeros((1, LANES), jnp.int32))

        def demote(c, carry):
            r0 = pl.multiple_of(c * COUNT_CHUNK, COUNT_CHUNK)
            x = sc_ref[pl.ds(r0, COUNT_CHUNK), :]
            drop = (eq_ref[pl.ds(r0, COUNT_CHUNK), :] > 0.0) & (r0 + crow_iota > lim) & tie_lane
            sc_ref[pl.ds(r0, COUNT_CHUNK), :] = jnp.where(drop, neg_inf, x)
            return carry

        lax.fori_loop(0, ncc, demote, 0)

    mask_off = float(jnp.finfo(MXU_DTYPE).min)

    def attn_scores(c, buf):
        sq_ref, mx_ref = buf
        r0 = chunk_start(c)
        mask = jnp.where(sc_ref[pl.ds(r0, KEY_CHUNK), :] >= thr, 0.0, mask_off).astype(MXU_DTYPE)
        kaug = jnp.concatenate([kbd_ref[0, pl.ds(r0, KEY_CHUNK), :], mask], axis=1)
        for p in range(N_HEADS // 2):
            s = _nt_dot(kaug, pair_q(qbs_ref, p))
            sq_ref[:, p * 2 * BLOCK:(p + 1) * 2 * BLOCK] = s
            for hh in range(2):
                cols = slice((2 * p + hh) * BLOCK, (2 * p + hh + 1) * BLOCK)
                mx_ref[:, cols] = _fold_rows(s[:, hh * BLOCK:(hh + 1) * BLOCK], jnp.max)

    def accumulate(c, buf):
        sq_ref, mx_ref = buf
        m_run = m_ref[...]
        m_new = jnp.maximum(m_run, jnp.max(mx_ref[...], axis=0, keepdims=True))
        m_ref[...] = m_new
        acc_ref[...] = acc_ref[...] * jnp.exp2(m_run - m_new)
        vt = vbt_ref[0, c]
        for p in range(N_HEADS // 2):
            pcols = slice(p * 2 * BLOCK, (p + 1) * 2 * BLOCK)
            e = jnp.exp2(sq_ref[:, pcols] - m_new[:, pcols])
            acc_ref[:, pcols] += jnp.dot(vt, e.astype(MXU_DTYPE), preferred_element_type=jnp.float32)

    acc_ref[...] = jnp.zeros(acc_ref.shape, jnp.float32)
    m_ref[...] = jnp.full(m_ref.shape, f32_lowest, jnp.float32)
    pipeline(attn_scores, accumulate)

    o_t = acc_ref[:HEAD_DIM, :] / acc_ref[HEAD_DIM:HEAD_DIM + 1, :]
    for p in range(N_HEADS // 2):
        pair = jnp.concatenate([o_t[:, (2 * p) * BLOCK:(2 * p + 1) * BLOCK],
                                o_t[:, (2 * p + 1) * BLOCK:(2 * p + 2) * BLOCK]], axis=0)
        out_ref[0, :, p * LANES:(p + 1) * LANES] = pair.T.astype(out_ref.dtype)


def _sparse(qi, qb, wt, kid, kbd, vbt, batch, seq):
    topk = min(TOPK_MAX, seq // 4)
    nq = seq // BLOCK
    qblk = pl.BlockSpec((1, BLOCK, D_HEADS), lambda b, i: (b, i, 0))
    full = pl.BlockSpec((1, seq, LANES), lambda b, i: (b, 0, 0))
    kern = functools.partial(_sparse_kernel, topk=topk)
    return pl.pallas_call(
        kern,
        grid=(batch, nq),
        in_specs=[qblk, qblk,
                  pl.BlockSpec((1, IDX_HEADS, BLOCK), lambda b, i: (b, 0, i)),
                  full, full,
                  pl.BlockSpec((1, seq // KEY_CHUNK, HEAD_DIM + L_ROWS, KEY_CHUNK), lambda b, i: (b, 0, 0, 0))],
        out_specs=qblk,
        out_shape=jax.ShapeDtypeStruct((batch, seq, D_HEADS), MXU_DTYPE),
        scratch_shapes=[pltpu.VMEM((seq, LANES), jnp.float32),
                        pltpu.VMEM((seq, LANES), jnp.float32),
                        pltpu.VMEM((COUNT_CHUNK, LANES), jnp.float32),
                        pltpu.VMEM((N_HEADS * BLOCK, LANES), MXU_DTYPE),
                        pltpu.VMEM((N_HEADS * BLOCK, 2 * LANES), MXU_DTYPE),
                        pltpu.VMEM((HEAD_DIM + L_ROWS, N_HEADS * BLOCK), jnp.float32),
                        pltpu.VMEM((1, N_HEADS * BLOCK), jnp.float32),
                        pltpu.VMEM((KEY_CHUNK, N_HEADS * BLOCK), jnp.float32),
                        pltpu.VMEM((KEY_CHUNK, N_HEADS * BLOCK), jnp.float32),
                        pltpu.VMEM((8, N_HEADS * BLOCK), jnp.float32),
                        pltpu.VMEM((8, N_HEADS * BLOCK), jnp.float32)],
        compiler_params=pltpu.CompilerParams(dimension_semantics=("arbitrary", "arbitrary"),
                                             vmem_limit_bytes=VMEM_LIMIT),
        name="indexer_sparse_attn",
    )(qi, qb, wt, kid, kbd, vbt)


def _merge_kernel(x_ref, g_ref, *refs):
    n_pat = len(DILATED_PATTERNS)
    o_refs, l_refs = refs[:n_pat], refs[n_pat:2 * n_pat]
    yb_ref, wg_ref, wua_ref, wub_ref, wo_ref, x1_ref = refs[2 * n_pat:2 * n_pat + 6]
    slabs = refs[2 * n_pat + 6:]
    x = x_ref[...]
    h = _rms(x, g_ref[...]).astype(MXU_DTYPE)
    tm = x.shape[0]
    n_grp = D_HEADS // LANES

    o_src, l_src, k = [], [], 0
    for (_, d), o_ref, l_ref in zip(DILATED_PATTERNS, o_refs, l_refs):
        if d == 1:
            o_src.append(lambda g, r=o_ref: r[:, g * LANES:(g + 1) * LANES].astype(jnp.float32))
            l_src.append(lambda g, r=l_ref: r[:, g * LANES:(g + 1) * LANES])
            continue
        o_slab, l_slab = slabs[2 * k], slabs[2 * k + 1]
        k += 1
        for r in range(d):
            for g in range(n_grp):
                cols = slice(g * LANES, (g + 1) * LANES)
                o_slab[g, pl.ds(r, tm // d, stride=d), :] = o_ref[0, r, :, cols].astype(jnp.float32)
                l_slab[g, pl.ds(r, tm // d, stride=d), :] = l_ref[0, r, :, cols]
        o_src.append(lambda g, s=o_slab: s[g])
        l_src.append(lambda g, s=l_slab: s[g])

    parts = []
    for g in range(n_grp):
        ls = [f(g) for f in l_src]
        m = functools.reduce(jnp.maximum, ls)
        es = [jnp.exp(l - m) for l in ls]
        num = functools.reduce(jnp.add, [e * f(g) for e, f in zip(es, o_src)])
        parts.append((num / functools.reduce(jnp.add, es)).astype(MXU_DTYPE))
    ya = jnp.concatenate(parts, axis=1)
    ua = jnp.dot(ya, wua_ref[...], preferred_element_type=jnp.float32)
    ub = jnp.dot(yb_ref[...], wub_ref[...], preferred_element_type=jnp.float32)
    ga = jnp.dot(h, wg_ref[:, :D_MODEL], preferred_element_type=jnp.float32)
    gb = jnp.dot(h, wg_ref[:, D_MODEL:], preferred_element_type=jnp.float32)
    merged = jax.nn.sigmoid(ga) * ua + jax.nn.sigmoid(gb) * ub
    x1_ref[...] = x + jnp.dot(merged.astype(MXU_DTYPE), wo_ref[...], preferred_element_type=jnp.float32)


def _merge(x2, g, os_, lses, yb, wg, wua, wub, wo, seq):
    n = x2.shape[0]
    tm = ROW_TILE
    tiles_per_seq = seq // tm
    row = lambda i: (i, 0)
    const = lambda i: (0, 0)
    half = pl.BlockSpec((tm, D_HEADS), row)
    pat_specs = [half if d == 1 else
                 pl.BlockSpec((1, d, tm // d, D_HEADS), lambda i: (i // tiles_per_seq, 0, i % tiles_per_seq, 0))
                 for _, d in DILATED_PATTERNS]
    return pl.pallas_call(
        _merge_kernel,
        grid=(n // tm,),
        in_specs=[pl.BlockSpec((tm, D_MODEL), row), pl.BlockSpec((1, D_MODEL), const)] + pat_specs * 2 + [half] + [
            pl.BlockSpec((D_MODEL, 2 * D_MODEL), const),
            pl.BlockSpec((D_HEADS, D_MODEL), const),
            pl.BlockSpec((D_HEADS, D_MODEL), const),
            pl.BlockSpec((D_MODEL, D_MODEL), const)],
        out_specs=pl.BlockSpec((tm, D_MODEL), row),
        out_shape=jax.ShapeDtypeStruct((n, D_MODEL), jnp.float32),
        scratch_shapes=[pltpu.VMEM((D_HEADS // LANES, tm, LANES), jnp.float32)] * (2 * len(RESIDUE_DILATIONS)),
        compiler_params=pltpu.CompilerParams(dimension_semantics=("arbitrary",), vmem_limit_bytes=VMEM_LIMIT),
        name="mix_gate_out",
    )(x2, g, *os_, *lses, yb, wg, wua, wub, wo)


def _ffn_kernel(x_ref, g_ref, wgate_ref, wup_ref, wdown_ref, gf_ref, out_ref):
    x = x_ref[...]
    h = _rms(x, g_ref[...]).astype(MXU_DTYPE)
    y = x
    for c0 in range(0, D_FF, FF_CHUNK):
        a = jnp.dot(h, wgate_ref[:, c0:c0 + FF_CHUNK], preferred_element_type=jnp.float32)
        u = jnp.dot(h, wup_ref[:, c0:c0 + FF_CHUNK], preferred_element_type=jnp.float32)
        act = (a * jax.nn.sigmoid(a) * u).astype(MXU_DTYPE)
        y = y + jnp.dot(act, wdown_ref[c0:c0 + FF_CHUNK, :], preferred_element_type=jnp.float32)
    out_ref[...] = _rms(y, gf_ref[...])


def _ffn(x1, g, wgate, wup, wdown, gf):
    n = x1.shape[0]
    tm = ROW_TILE
    row = lambda i: (i, 0)
    const = lambda i: (0, 0)
    return pl.pallas_call(
        _ffn_kernel,
        grid=(n // tm,),
        in_specs=[pl.BlockSpec((tm, D_MODEL), row), pl.BlockSpec((1, D_MODEL), const),
                  pl.BlockSpec((D_MODEL, D_FF), const), pl.BlockSpec((D_MODEL, D_FF), const),
                  pl.BlockSpec((D_FF, D_MODEL), const), pl.BlockSpec((1, D_MODEL), const)],
        out_specs=pl.BlockSpec((tm, D_MODEL), row),
        out_shape=jax.ShapeDtypeStruct((n, D_MODEL), jnp.float32),
        compiler_params=pltpu.CompilerParams(dimension_semantics=("arbitrary",), vmem_limit_bytes=VMEM_LIMIT),
        name="ffn_norm",
    )(x1, g, wgate, wup, wdown, gf)


def _rope_tables(seq):
    inv_freq = ROPE_THETA ** (-jnp.arange(HALF, dtype=jnp.float32) / HALF)
    ang = jnp.arange(seq, dtype=jnp.int32).astype(jnp.float32)[:, None] * inv_freq[None, :]
    cos, sin = jnp.cos(ang), jnp.sin(ang)
    return jnp.tile(cos, (1, 4)), jnp.concatenate([-sin, -sin, sin, sin], axis=1)


def kernel(x, norm_mix, w_in, w_up_a, w_up_b, w_out, norm_ffn, w_gate, w_up, w_down, norm_final):
    batch, seq, _ = x.shape
    assert seq % max(d * BLOCK for _, d in DILATED_PATTERNS) == 0 and seq % KEY_CHUNK == 0
    assert all(w // d == BLOCK for w, d in DILATED_PATTERNS)
    n = batch * seq
    bf = MXU_DTYPE
    xf = x.reshape(n, D_MODEL)
    cos_t, sin_t = _rope_tables(seq)
    for layer in range(w_in.shape[0]):
        w = w_in[layer]
        w_pack = jnp.pad(w[:, _packed_columns()], ((0, 0), (0, _P_TOTAL - _P_MISC - HEAD_DIM - IDX_HEADS))).astype(bf)
        w_gates = w[:, _GA:].astype(bf)
        qas, kas, vas, qb, qi, kbd, kid, misc = _project(xf, norm_mix[layer][None], w_pack, cos_t, sin_t, seq)

        dil = []
        for (_, d), q, k, v in zip(DILATED_PATTERNS, qas, kas, vas):
            if d == 1:
                o, lse = _dilated(*(z.reshape(batch, 1, seq, D_HEADS) for z in (q, k, v)), d)
                dil.append((o.reshape(n, D_HEADS), lse.reshape(n, D_HEADS)))
            else:
                dil.append(_dilated(q, k, v, d))

        ones_col = (jnp.arange(L_ROWS) == 0).astype(bf)[None, :]
        vb = jnp.concatenate([misc[:, :HEAD_DIM].astype(bf), jnp.broadcast_to(ones_col, (n, L_ROWS))], axis=1)
        vbt = jnp.swapaxes(vb.reshape(batch, seq // KEY_CHUNK, KEY_CHUNK, HEAD_DIM + L_ROWS), 2, 3)
        wt = jnp.swapaxes(misc[:, HEAD_DIM:HEAD_DIM + IDX_HEADS].reshape(batch, seq, IDX_HEADS), 1, 2)
        r3 = lambda z: z.reshape(batch, seq, z.shape[-1])
        yb = _sparse(r3(qi), r3(qb), wt, r3(kid), r3(kbd), vbt, batch, seq).reshape(n, D_HEADS)

        x1 = _merge(xf, norm_mix[layer][None], [o for o, _ in dil], [l for _, l in dil], yb,
                    w_gates, w_up_a[layer].astype(bf), w_up_b[layer].astype(bf), w_out[layer].astype(bf), seq)
        last = layer == w_in.shape[0] - 1
        assert last, "the final norm is fused into the FFN kernel of the last layer"
        xf = _ffn(x1, norm_ffn[layer][None], w_gate[layer].astype(bf), w_up[layer].astype(bf),
                  w_down[layer].astype(bf), norm_final[None])
    return xf.reshape(batch, seq, D_MODEL)
```

```python
import functools

import numpy as np
import jax
import jax.numpy as jnp
from jax import lax
from jax.experimental import pallas as pl
from jax.experimental.pallas import tpu as pltpu

D_MODEL = 1024
HEAD_DIM = 64
HALF = HEAD_DIM // 2
N_HEADS = 8
D_HEADS = N_HEADS * HEAD_DIM
IDX_HEADS = 8
DILATED_PATTERNS = ((128, 1), (512, 4), (2048, 16))
RESIDUE_DILATIONS = tuple(d for _, d in DILATED_PATTERNS if d > 1)
TOPK_MAX = 256
D_FF = 2816
ROPE_THETA = 10000.0
RMS_EPS = 1e-6
BLOCK = 128
ATTN_SCALE = HEAD_DIM ** -0.5
IDX_SCALE = (HEAD_DIM ** -0.5) * (IDX_HEADS ** -0.5)
LOG2_E = float(np.log2(np.e))

LANES = 128
MIN_NORMAL_KEY = 1 << 23
KEY_CHUNK = 1024
COUNT_CHUNK = 512
COUNT_FOLD = 8
VALUE_PASSES = 14
DESCENT_STEPS = 6
SEARCH_PASSES_MIN = 12
L_ROWS = 8
ROW_TILE = 512
DIL_BLOCKS = 8
FF_CHUNK = 1408
VMEM_LIMIT = 56 * 1024 * 1024
MXU_DTYPE = jnp.bfloat16

_SPLITS = (D_HEADS, D_HEADS, D_HEADS, D_HEADS, HEAD_DIM, HEAD_DIM, IDX_HEADS * HEAD_DIM, HEAD_DIM, IDX_HEADS,
           D_MODEL, D_MODEL)
_OFF = np.concatenate([[0], np.cumsum(_SPLITS)])
(_QA, _KA, _VA, _QB, _KB, _VB, _QI, _KI, _WI, _GA, _GB) = (int(o) for o in _OFF[:-1])

_P_QA, _P_KA, _P_VA, _P_QB, _P_QI = 0, 512, 1024, 1536, 2048
_P_KBD, _P_KID, _P_MISC = 2560, 2688, 2816
_P_TOTAL = 2944


def _pair_perm():
    idx = np.empty(D_HEADS, np.int64)
    for j in range(D_HEADS):
        g, l = divmod(j, LANES)
        quarter, e = divmod(l, HALF)
        head = 2 * g + (quarter % 2)
        idx[j] = head * HEAD_DIM + (quarter // 2) * HALF + e
    return idx


def _dup_perm():
    idx = np.empty(LANES, np.int64)
    for l in range(LANES):
        quarter, e = divmod(l, HALF)
        idx[l] = (quarter // 2) * HALF + e
    return idx


def _packed_columns():
    pp, dp = _pair_perm(), _dup_perm()
    return np.concatenate([
        _QA + pp, _KA + pp, _VA + np.arange(D_HEADS), _QB + pp, _QI + pp,
        _KB + dp, _KI + dp, _VB + np.arange(HEAD_DIM), _WI + np.arange(IDX_HEADS)])


def _rms(x, g):
    ms = jnp.mean(x * x, axis=-1, keepdims=True)
    return x * lax.rsqrt(ms + RMS_EPS) * g


def _nt_dot(a, b):
    return lax.dot_general(a, b, (((1,), (1,)), ((), ())), preferred_element_type=jnp.float32)


def _proj_kernel(x_ref, g_ref, w_ref, cos_ref, sin_ref, *refs):
    n_lay = 1 + len(RESIDUE_DILATIONS)
    qa_refs, ka_refs, va_refs = refs[:n_lay], refs[n_lay:2 * n_lay], refs[2 * n_lay:3 * n_lay]
    qb_ref, qi_ref, kbd_ref, kid_ref, misc_ref, h_ref, slab_ref = refs[3 * n_lay:]
    h_ref[...] = _rms(x_ref[...], g_ref[...]).astype(MXU_DTYPE)
    cos = cos_ref[...]
    sin = sin_ref[...]

    def mm(c0, width):
        return jnp.dot(h_ref[...], w_ref[:, c0:c0 + width], preferred_element_type=jnp.float32)

    def rope(z):
        parts = []
        for g in range(z.shape[1] // LANES):
            zg = z[:, g * LANES:(g + 1) * LANES]
            parts.append(zg * cos + pltpu.roll(zg, 2 * HALF, axis=1) * sin)
        return parts[0] if len(parts) == 1 else jnp.concatenate(parts, axis=1)

    def emit(y, out_refs):
        out_refs[0][...] = y.astype(MXU_DTYPE)
        for g in range(D_HEADS // LANES):
            slab_ref[g] = y[:, g * LANES:(g + 1) * LANES]
        for d, ref in zip(RESIDUE_DILATIONS, out_refs[1:]):
            rows = y.shape[0] // d
            for r in range(d):
                for g in range(D_HEADS // LANES):
                    ref[0, r, :, g * LANES:(g + 1) * LANES] = (
                        slab_ref[g, pl.ds(r, rows, stride=d), :].astype(MXU_DTYPE))

    emit(rope(mm(_P_QA, D_HEADS)) * ATTN_SCALE, qa_refs)
    emit(rope(mm(_P_KA, D_HEADS)), ka_refs)
    emit(mm(_P_VA, D_HEADS), va_refs)
    qb_ref[...] = (rope(mm(_P_QB, D_HEADS)) * (ATTN_SCALE * LOG2_E)).astype(MXU_DTYPE)
    qi_ref[...] = rope(mm(_P_QI, D_HEADS)).astype(MXU_DTYPE)
    kbd_ref[...] = rope(mm(_P_KBD, LANES)).astype(MXU_DTYPE)
    kid_ref[...] = rope(mm(_P_KID, LANES)).astype(MXU_DTYPE)
    misc_ref[...] = mm(_P_MISC, LANES)


def _project(x2, g, w_pack, cos_t, sin_t, seq):
    n = x2.shape[0]
    tm = ROW_TILE
    tiles_per_seq = seq // tm
    row = lambda i: (i, 0)
    const = lambda i: (0, 0)
    pos = lambda i: (i % tiles_per_seq, 0)
    batch = n // seq
    wide = jax.ShapeDtypeStruct((n, D_HEADS), MXU_DTYPE)
    narrow = jax.ShapeDtypeStruct((n, LANES), MXU_DTYPE)
    wide_spec = pl.BlockSpec((tm, D_HEADS), row)
    lay_shapes = [wide] + [jax.ShapeDtypeStruct((batch, d, seq // d, D_HEADS), MXU_DTYPE) for d in RESIDUE_DILATIONS]
    lay_specs = [wide_spec] + [
        pl.BlockSpec((1, d, tm // d, D_HEADS), lambda i: (i // tiles_per_seq, 0, i % tiles_per_seq, 0))
        for d in RESIDUE_DILATIONS]
    n_lay = len(lay_shapes)
    outs = pl.pallas_call(
        _proj_kernel,
        grid=(n // tm,),
        in_specs=[
            pl.BlockSpec((tm, D_MODEL), row),
            pl.BlockSpec((1, D_MODEL), const),
            pl.BlockSpec((D_MODEL, _P_TOTAL), const),
            pl.BlockSpec((tm, LANES), pos),
            pl.BlockSpec((tm, LANES), pos),
        ],
        out_specs=lay_specs * 3 + [wide_spec] * 2 + [pl.BlockSpec((tm, LANES), row)] * 3,
        out_shape=lay_shapes * 3 + [wide] * 2 + [narrow, narrow, jax.ShapeDtypeStruct((n, LANES), jnp.float32)],
        scratch_shapes=[pltpu.VMEM((tm, D_MODEL), MXU_DTYPE),
                        pltpu.VMEM((D_HEADS // LANES, tm, LANES), jnp.float32)],
        compiler_params=pltpu.CompilerParams(dimension_semantics=("arbitrary",), vmem_limit_bytes=VMEM_LIMIT),
        name="proj_rope",
    )(x2, g, w_pack, cos_t, sin_t)
    return (outs[:n_lay], outs[n_lay:2 * n_lay], outs[2 * n_lay:3 * n_lay]) + tuple(outs[3 * n_lay:])


def _dil_kernel(q_ref, k_ref, v_ref, kp_ref, vp_ref, o_ref, lse_ref, *, n_blocks):
    n = pl.program_id(2)
    kj = lax.broadcasted_iota(jnp.int32, (2 * BLOCK, LANES), 0)
    qi = lax.broadcasted_iota(jnp.int32, (2 * BLOCK, LANES), 1)
    band = (kj >= qi) & (kj <= qi + BLOCK)
    lane = lax.broadcasted_iota(jnp.int32, (1, LANES), 1)
    pairs = range(N_HEADS // 2)
    col = lambda p: slice(p * LANES, (p + 1) * LANES)

    for j in range(n_blocks):
        rows = slice(j * BLOCK, (j + 1) * BLOCK)
        if j == 0:
            k_prev = lambda p: kp_ref[0, 0, :, col(p)]
            v_prev = lambda p: vp_ref[0, 0, :, col(p)]
            bias = jnp.where(band & ((kj >= BLOCK) | (n > 0)), 0.0, -jnp.inf)
        else:
            prev_rows = slice((j - 1) * BLOCK, j * BLOCK)
            k_prev = lambda p, r=prev_rows: k_ref[0, 0, r, col(p)]
            v_prev = lambda p, r=prev_rows: v_ref[0, 0, r, col(p)]
            bias = jnp.where(band, 0.0, -jnp.inf)

        scores = []
        for p in pairs:
            qp = q_ref[0, 0, rows, col(p)]
            qs = jnp.concatenate([jnp.where(((lane // HALF) % 2) == hh, qp, jnp.zeros_like(qp))
                                  for hh in range(2)], axis=0)
            k2 = jnp.concatenate([k_prev(p), k_ref[0, 0, rows, col(p)]], axis=0)
            scores.append(_nt_dot(k2, qs))

        probs, dens, lses = [], [], []
        for p in pairs:
            for hh in range(2):
                sh = scores[p][:, hh * BLOCK:(hh + 1) * BLOCK] + bias
                m = jnp.max(sh, axis=0, keepdims=True)
                e = jnp.exp(sh - m)
                den = jnp.sum(e, axis=0, keepdims=True)
                probs.append(e.astype(MXU_DTYPE))
                dens.append(den)
                lses.append(m + jnp.log(den))

        for p in pairs:
            v2 = jnp.concatenate([v_prev(p), v_ref[0, 0, rows, col(p)]], axis=0)
            o_t = lax.dot_general(v2, jnp.concatenate(probs[2 * p:2 * p + 2], axis=1), (((0,), (0,)), ((), ())),
                                  preferred_element_type=jnp.float32)
            tile = jnp.concatenate([o_t[:HEAD_DIM, :BLOCK] / dens[2 * p],
                                    o_t[HEAD_DIM:, BLOCK:] / dens[2 * p + 1]], axis=0)
            o_ref[0, 0, rows, col(p)] = tile.T.astype(o_ref.dtype)
            lse_tile = jnp.concatenate([jnp.broadcast_to(l, (HEAD_DIM, BLOCK)) for l in lses[2 * p:2 * p + 2]],
                                       axis=0)
            lse_ref[0, 0, rows, col(p)] = lse_tile.T


def _dilated(q, k, v, dilation):
    batch, _, m_len, _ = q.shape
    n_blocks = min(DIL_BLOCKS, m_len // BLOCK)
    tile_rows = n_blocks * BLOCK
    cur = lambda b, r, n: (b, r, n, 0)
    prev = lambda b, r, n: (b, r, jnp.maximum(n * n_blocks - 1, 0), 0)
    tile = (1, 1, tile_rows, D_HEADS)
    blk = (1, 1, BLOCK, D_HEADS)
    return pl.pallas_call(
        functools.partial(_dil_kernel, n_blocks=n_blocks),
        grid=(batch, dilation, m_len // tile_rows),
        in_specs=[pl.BlockSpec(tile, cur), pl.BlockSpec(tile, cur), pl.BlockSpec(tile, cur),
                  pl.BlockSpec(blk, prev), pl.BlockSpec(blk, prev)],
        out_specs=[pl.BlockSpec(tile, cur), pl.BlockSpec(tile, cur)],
        out_shape=[jax.ShapeDtypeStruct(q.shape, MXU_DTYPE), jax.ShapeDtypeStruct(q.shape, jnp.float32)],
        compiler_params=pltpu.CompilerParams(dimension_semantics=("arbitrary",) * 3),
        name=f"dilated_d{dilation}",
    )(q, k, v, k, v)


def _key_to_f32(key):
    bits = key ^ ((key >> 31) & jnp.int32(0x7FFFFFFF))
    return lax.bitcast_convert_type(bits, jnp.float32)


def _f32_to_key(x):
    bits = lax.bitcast_convert_type(x, jnp.int32)
    return bits ^ ((bits >> 31) & jnp.int32(0x7FFFFFFF))


def _fold_rows(x, op):
    rows = x.shape[0]
    y = op(x.reshape(rows // 64, 64, LANES), axis=0)
    return op(y.reshape(8, 8, LANES), axis=0)


def _sparse_kernel(qi_ref, qb_ref, wt_ref, kid_ref, kbd_ref, vbt_ref, out_ref,
                   sc_ref, eq_ref, eqc_ref, qis_ref, qbs_ref, acc_ref, m_ref, sqa_ref, sqb_ref, mxa_ref,
                   mxb_ref, *, topk):
    i = pl.program_id(1)
    nch = i // (KEY_CHUNK // BLOCK) + 1
    ncc = i // (COUNT_CHUNK // BLOCK) + 1
    lane = lax.broadcasted_iota(jnp.int32, (1, LANES), 1)
    t_idx = i * BLOCK + lane
    row_iota = lax.broadcasted_iota(jnp.int32, (KEY_CHUNK, LANES), 0)
    crow_iota = lax.broadcasted_iota(jnp.int32, (COUNT_CHUNK, LANES), 0)
    neg_inf = jnp.float32(-jnp.inf)
    f32_lowest = jnp.float32(jnp.finfo(jnp.float32).min)

    eye = jnp.where(lax.broadcasted_iota(jnp.int32, (BLOCK, LANES), 0) == lane, 1.0, 0.0).astype(MXU_DTYPE)
    for h in range(N_HEADS):
        cols = slice((h // 2) * LANES, (h // 2 + 1) * LANES)
        head_lanes = ((lane // HALF) % 2) == (h % 2)
        rows = slice(h * BLOCK, (h + 1) * BLOCK)
        qis_ref[rows, :] = jnp.where(head_lanes, qi_ref[0, :, cols], jnp.zeros((), MXU_DTYPE))
        qbs_ref[rows, :LANES] = jnp.where(head_lanes, qb_ref[0, :, cols], jnp.zeros((), MXU_DTYPE))
        qbs_ref[rows, LANES:] = eye

    def chunk_start(c):
        return pl.multiple_of(c * KEY_CHUNK, KEY_CHUNK)

    def pair_q(ref, p):
        return ref[p * 2 * BLOCK:(p + 1) * 2 * BLOCK, :]

    buf_a, buf_b = (sqa_ref, mxa_ref), (sqb_ref, mxb_ref)

    def pipeline(produce, consume):
        produce(0, buf_a)

        def two_chunks(j, carry):
            c = 2 * j
            produce(c + 1, buf_b)
            consume(c, buf_a)
            produce(c + 2, buf_a)
            consume(c + 1, buf_b)
            return carry

        lax.fori_loop(0, (nch - 1) // 2, two_chunks, 0)
        last = nch - 1

        @pl.when(last % 2 == 1)
        def _():
            produce(last, buf_b)
            consume(last - 1, buf_a)
            consume(last, buf_b)

        @pl.when(last % 2 == 0)
        def _():
            consume(last, buf_a)

    def score_chunk(c, carry):
        r0 = chunk_start(c)
        kc = kid_ref[0, pl.ds(r0, KEY_CHUNK), :]
        acc = jnp.zeros((KEY_CHUNK, LANES), jnp.float32)
        for p in range(N_HEADS // 2):
            d = _nt_dot(kc, pair_q(qis_ref, p))
            for hh in range(2):
                h = 2 * p + hh
                acc = acc + jnp.maximum(d[:, hh * BLOCK:(hh + 1) * BLOCK], 0.0) * wt_ref[0, h:h + 1, :]
        admissible = r0 + row_iota <= t_idx
        scores = acc * IDX_SCALE
        masked = jnp.where(admissible, scores, neg_inf)
        sc_ref[pl.ds(r0, KEY_CHUNK), :] = masked
        top8, bot8 = carry
        return (jnp.maximum(top8, _fold_rows(masked, jnp.max)),
                jnp.minimum(bot8, _fold_rows(jnp.where(admissible, scores, -neg_inf), jnp.min)))

    def for_chunks(n_chunks, body, init):
        carry = lax.fori_loop(0, n_chunks // 4,
                              lambda j, a: body(4 * j + 3, body(4 * j + 2, body(4 * j + 1, body(4 * j, a)))), init)
        done = (n_chunks // 4) * 4
        carry = lax.cond((n_chunks & 2) != 0, lambda a: body(done + 1, body(done, a)), lambda a: a, carry)
        done = done + (n_chunks & 2)
        return lax.cond((n_chunks & 1) != 0, lambda a: body(done, a), lambda a: a, carry)

    top8, bot8 = for_chunks(nch, score_chunk, (jnp.full((8, LANES), neg_inf, jnp.float32),
                                               jnp.full((8, LANES), -neg_inf, jnp.float32)))
    top = jnp.max(top8, axis=0, keepdims=True)
    bot = jnp.min(bot8, axis=0, keepdims=True)

    def fold_scores(value, op, pairwise, start, src_ref=sc_ref):
        def body(c, acc):
            r0 = pl.multiple_of(c * COUNT_CHUNK, COUNT_CHUNK)
            val = value(src_ref[pl.ds(r0, COUNT_CHUNK), :], r0)
            return pairwise(acc, op(val.reshape(COUNT_FOLD, COUNT_CHUNK // COUNT_FOLD, LANES), axis=0))

        acc = for_chunks(ncc, body, jnp.full((COUNT_CHUNK // COUNT_FOLD, LANES), start, jnp.float32))
        return op(acc, axis=0, keepdims=True)

    def count(pred):
        return fold_scores(lambda x, r0: jnp.where(pred(x, r0), 1.0, 0.0), jnp.sum, jnp.add, 0.0)

    key_top = _f32_to_key(top)
    lo = _f32_to_key(bot) - jnp.int32(1 << 23)
    hi = key_top + 1
    hi = jnp.where((hi >= -MIN_NORMAL_KEY) & (hi < MIN_NORMAL_KEY), jnp.int32(MIN_NORMAL_KEY), hi)
    unknown = jnp.float32(2 * sc_ref.shape[0])

    def is_settled(carry):
        lo, hi, cnt_lo = carry[:3]
        return (cnt_lo == topk) | (hi - lo == 1) | (t_idx < topk)

    def probe(mid, carry):
        lo, hi, cnt_lo, thr_lo, thr_hi = carry
        thr_c = _key_to_f32(mid)
        cnt = count(lambda x, r0: x >= thr_c)
        take = (cnt >= topk) & ~is_settled(carry)
        drop = (cnt < topk) & ~is_settled(carry)
        return (jnp.where(take, mid, lo), jnp.where(drop, mid, hi), jnp.where(take, cnt, cnt_lo),
                jnp.where(take, thr_c, thr_lo), jnp.where(drop, thr_c, thr_hi))

    def halve(by_value, carry):
        lo, hi = carry[:2]
        mid = lo + lax.shift_right_logical(hi - lo, 1)
        if by_value:
            mid_val = _f32_to_key(0.5 * _key_to_f32(lo) + 0.5 * _key_to_f32(hi))
            mid = jnp.where((mid_val > lo) & (mid_val < hi), mid_val, mid)
        return probe(mid, carry)

    def descend(carry):
        lo, hi, cnt_lo, thr_lo, thr_hi = carry
        v = fold_scores(lambda x, r0: jnp.where(x < thr_hi, x, neg_inf), jnp.max, jnp.maximum, neg_inf)
        cnt = count(lambda x, r0: x >= v)
        key_v = _f32_to_key(v)
        take = (cnt >= topk) & ~is_settled(carry)
        drop = (cnt < topk) & ~is_settled(carry)
        return (jnp.where(take, key_v, lo), jnp.where(take, key_v + 1, jnp.where(drop, key_v, hi)),
                jnp.where(take, cnt, cnt_lo), jnp.where(take, v, thr_lo), jnp.where(drop, v, thr_hi))

    def pending(carry):
        return jnp.max(jnp.where(is_settled(carry), 0.0, 1.0))

    state = lax.fori_loop(0, SEARCH_PASSES_MIN, lambda b, c: halve(True, c),
                          (lo, hi, jnp.full((1, LANES), unknown), _key_to_f32(lo), _key_to_f32(hi)))

    def two_passes(carry):
        b, state, _ = carry
        state = lax.cond(b < VALUE_PASSES, lambda s: halve(True, halve(True, s)),
                         lambda s: lax.cond(b < VALUE_PASSES + 2 * DESCENT_STEPS, descend,
                                            lambda t: halve(False, halve(False, t)), s), state)
        return b + 2, state, pending(state)

    _, (lo, hi, cnt_ge, thr_raw, _), _ = lax.while_loop(
        lambda c: (c[0] < VALUE_PASSES + 2 * DESCENT_STEPS + 32) & (c[2] > 0.0), two_passes,
        (jnp.int32(SEARCH_PASSES_MIN), state, pending(state)))
    enough = thr_raw > f32_lowest
    thr = jnp.where(enough, thr_raw, f32_lowest)

    tie_lane = enough & (cnt_ge > topk) & (t_idx >= topk)

    @pl.when(jnp.max(jnp.where(tie_lane, 1.0, 0.0)) > 0.0)
    def _():
        need = topk - count(lambda x, r0: x > thr)

        def mark_equal(c, carry):
            seen, chunk_of, seen_before = carry
            r0 = pl.multiple_of(c * COUNT_CHUNK, COUNT_CHUNK)
            eq = jnp.where(sc_ref[pl.ds(r0, COUNT_CHUNK), :] == thr, 1.0, 0.0)
            eq_ref[pl.ds(r0, COUNT_CHUNK), :] = eq
            here = jnp.sum(_fold_rows(eq, jnp.sum), axis=0, keepdims=True)
            first_reach = (seen < need) & (seen + here >= need)
            return (seen + here, jnp.where(first_reach, c, chunk_of), jnp.where(first_reach, seen, seen_before))

        zero = jnp.zeros((1, LANES), jnp.float32)
        _, chunk_of, seen_before = lax.fori_loop(0, ncc, mark_equal, (zero, jnp.zeros((1, LANES), jnp.int32), zero))
        need_here = need - seen_before

        eqc_ref[...] = jnp.zeros(eqc_ref.shape, jnp.float32)

        def gather_chunk(c, carry):
            r0 = pl.multiple_of(c * COUNT_CHUNK, COUNT_CHUNK)
            eqc_ref[...] += jnp.where(chunk_of == c, eq_ref[pl.ds(r0, COUNT_CHUNK), :], 0.0)
            return carry

        lax.fori_loop(0, ncc, gather_chunk, 0)

        def idx_step(b, lim):
            cand = lim | (jnp.int32(1) << (COUNT_CHUNK.bit_length() - 2 - b))
            below = jnp.sum(_fold_rows(jnp.where(crow_iota < cand, eqc_ref[...], 0.0), jnp.sum), axis=0, keepdims=True)
            return jnp.where(below < need_here, cand, lim)

        lim = chunk_of * COUNT_CHUNK + lax.fori_loop(0, COUNT_CHUNK.bit_length() - 1, idx_step,
                                                     jnp.zeros((1, LANES), jnp.int32))

        def demote(c, carry):
            r0 = pl.multiple_of(c * COUNT_CHUNK, COUNT_CHUNK)
            x = sc_ref[pl.ds(r0, COUNT_CHUNK), :]
            drop = (eq_ref[pl.ds(r0, COUNT_CHUNK), :] > 0.0) & (r0 + crow_iota > lim) & tie_lane
            sc_ref[pl.ds(r0, COUNT_CHUNK), :] = jnp.where(drop, neg_inf, x)
            return carry

        lax.fori_loop(0, ncc, demote, 0)

    mask_off = float(jnp.finfo(MXU_DTYPE).min)

    def attn_scores(c, buf):
        sq_ref, mx_ref = buf
        r0 = chunk_start(c)
        mask = jnp.where(sc_ref[pl.ds(r0, KEY_CHUNK), :] >= thr, 0.0, mask_off).astype(MXU_DTYPE)
        kaug = jnp.concatenate([kbd_ref[0, pl.ds(r0, KEY_CHUNK), :], mask], axis=1)
        for p in range(N_HEADS // 2):
            s = _nt_dot(kaug, pair_q(qbs_ref, p))
            sq_ref[:, p * 2 * BLOCK:(p + 1) * 2 * BLOCK] = s
            for hh in range(2):
                cols = slice((2 * p + hh) * BLOCK, (2 * p + hh + 1) * BLOCK)
                mx_ref[:, cols] = _fold_rows(s[:, hh * BLOCK:(hh + 1) * BLOCK], jnp.max)

    def accumulate(c, buf):
        sq_ref, mx_ref = buf
        m_run = m_ref[...]
        m_new = jnp.maximum(m_run, jnp.max(mx_ref[...], axis=0, keepdims=True))
        m_ref[...] = m_new
        acc_ref[...] = acc_ref[...] * jnp.exp2(m_run - m_new)
        vt = vbt_ref[0, c]
        for p in range(N_HEADS // 2):
            pcols = slice(p * 2 * BLOCK, (p + 1) * 2 * BLOCK)
            e = jnp.exp2(sq_ref[:, pcols] - m_new[:, pcols])
            acc_ref[:, pcols] += jnp.dot(vt, e.astype(MXU_DTYPE), preferred_element_type=jnp.float32)

    acc_ref[...] = jnp.zeros(acc_ref.shape, jnp.float32)
    m_ref[...] = jnp.full(m_ref.shape, f32_lowest, jnp.float32)
    pipeline(attn_scores, accumulate)

    o_t = acc_ref[:HEAD_DIM, :] / acc_ref[HEAD_DIM:HEAD_DIM + 1, :]
    for p in range(N_HEADS // 2):
        pair = jnp.concatenate([o_t[:, (2 * p) * BLOCK:(2 * p + 1) * BLOCK],
                                o_t[:, (2 * p + 1) * BLOCK:(2 * p + 2) * BLOCK]], axis=0)
        out_ref[0, :, p * LANES:(p + 1) * LANES] = pair.T.astype(out_ref.dtype)


def _sparse(qi, qb, wt, kid, kbd, vbt, batch, seq):
    topk = min(TOPK_MAX, seq // 4)
    nq = seq // BLOCK
    qblk = pl.BlockSpec((1, BLOCK, D_HEADS), lambda b, i: (b, i, 0))
    full = pl.BlockSpec((1, seq, LANES), lambda b, i: (b, 0, 0))
    kern = functools.partial(_sparse_kernel, topk=topk)
    return pl.pallas_call(
        kern,
        grid=(batch, nq),
        in_specs=[qblk, qblk,
                  pl.BlockSpec((1, IDX_HEADS, BLOCK), lambda b, i: (b, 0, i)),
                  full, full,
                  pl.BlockSpec((1, seq // KEY_CHUNK, HEAD_DIM + L_ROWS, KEY_CHUNK), lambda b, i: (b, 0, 0, 0))],
        out_specs=qblk,
        out_shape=jax.ShapeDtypeStruct((batch, seq, D_HEADS), MXU_DTYPE),
        scratch_shapes=[pltpu.VMEM((seq, LANES), jnp.float32),
                        pltpu.VMEM((seq, LANES), jnp.float32),
                        pltpu.VMEM((COUNT_CHUNK, LANES), jnp.float32),
                        pltpu.VMEM((N_HEADS * BLOCK, LANES), MXU_DTYPE),
                        pltpu.VMEM((N_HEADS * BLOCK, 2 * LANES), MXU_DTYPE),
                        pltpu.VMEM((HEAD_DIM + L_ROWS, N_HEADS * BLOCK), jnp.float32),
                        pltpu.VMEM((1, N_HEADS * BLOCK), jnp.float32),
                        pltpu.VMEM((KEY_CHUNK, N_HEADS * BLOCK), jnp.float32),
                        pltpu.VMEM((KEY_CHUNK, N_HEADS * BLOCK), jnp.float32),
                        pltpu.VMEM((8, N_HEADS * BLOCK), jnp.float32),
                        pltpu.VMEM((8, N_HEADS * BLOCK), jnp.float32)],
        compiler_params=pltpu.CompilerParams(dimension_semantics=("arbitrary", "arbitrary"),
                                             vmem_limit_bytes=VMEM_LIMIT),
        name="indexer_sparse_attn",
    )(qi, qb, wt, kid, kbd, vbt)


def _ffn_tile(x, g_ref, wgate_ref, wup_ref, wdown_ref, gf_ref, out_ref):
    h = _rms(x, g_ref[...]).astype(MXU_DTYPE)
    y = x
    for c0 in range(0, D_FF, FF_CHUNK):
        a = jnp.dot(h, wgate_ref[:, c0:c0 + FF_CHUNK], preferred_element_type=jnp.float32)
        u = jnp.dot(h, wup_ref[:, c0:c0 + FF_CHUNK], preferred_element_type=jnp.float32)
        act = (a * jax.nn.sigmoid(a) * u).astype(MXU_DTYPE)
        y = y + jnp.dot(act, wdown_ref[c0:c0 + FF_CHUNK, :], preferred_element_type=jnp.float32)
    out_ref[...] = _rms(y, gf_ref[...])


def _merge_kernel(x_ref, g_ref, *refs):
    n_pat = len(DILATED_PATTERNS)
    o_refs, l_refs = refs[:n_pat], refs[n_pat:2 * n_pat]
    (yb_ref, wg_ref, wua_ref, wub_ref, wo_ref,
     g2_ref, wgate_ref, wup_ref, wdown_ref, gf_ref, out_ref) = refs[2 * n_pat:2 * n_pat + 11]
    slabs = refs[2 * n_pat + 11:]
    x = x_ref[...]
    h = _rms(x, g_ref[...]).astype(MXU_DTYPE)
    tm = x.shape[0]
    n_grp = D_HEADS // LANES

    o_src, l_src, k = [], [], 0
    for (_, d), o_ref, l_ref in zip(DILATED_PATTERNS, o_refs, l_refs):
        if d == 1:
            o_src.append(lambda g, r=o_ref: r[:, g * LANES:(g + 1) * LANES].astype(jnp.float32))
            l_src.append(lambda g, r=l_ref: r[:, g * LANES:(g + 1) * LANES])
            continue
        o_slab, l_slab = slabs[2 * k], slabs[2 * k + 1]
        k += 1
        for r in range(d):
            for g in range(n_grp):
                cols = slice(g * LANES, (g + 1) * LANES)
                o_slab[g, pl.ds(r, tm // d, stride=d), :] = o_ref[0, r, :, cols].astype(jnp.float32)
                l_slab[g, pl.ds(r, tm // d, stride=d), :] = l_ref[0, r, :, cols]
        o_src.append(lambda g, s=o_slab: s[g])
        l_src.append(lambda g, s=l_slab: s[g])

    parts = []
    for g in range(n_grp):
        ls = [f(g) for f in l_src]
        m = functools.reduce(jnp.maximum, ls)
        es = [jnp.exp(l - m) for l in ls]
        num = functools.reduce(jnp.add, [e * f(g) for e, f in zip(es, o_src)])
        parts.append((num / functools.reduce(jnp.add, es)).astype(MXU_DTYPE))
    ya = jnp.concatenate(parts, axis=1)
    ua = jnp.dot(ya, wua_ref[...], preferred_element_type=jnp.float32)
    ub = jnp.dot(yb_ref[...], wub_ref[...], preferred_element_type=jnp.float32)
    ga = jnp.dot(h, wg_ref[:, :D_MODEL], preferred_element_type=jnp.float32)
    gb = jnp.dot(h, wg_ref[:, D_MODEL:], preferred_element_type=jnp.float32)
    merged = jax.nn.sigmoid(ga) * ua + jax.nn.sigmoid(gb) * ub
    x1 = x + jnp.dot(merged.astype(MXU_DTYPE), wo_ref[...], preferred_element_type=jnp.float32)
    _ffn_tile(x1, g2_ref, wgate_ref, wup_ref, wdown_ref, gf_ref, out_ref)


def _merge(x2, g, os_, lses, yb, wg, wua, wub, wo, g2, wgate, wup, wdown, gf, seq):
    n = x2.shape[0]
    tm = ROW_TILE
    tiles_per_seq = seq // tm
    row = lambda i: (i, 0)
    const = lambda i: (0, 0)
    half = pl.BlockSpec((tm, D_HEADS), row)
    resident = lambda shape: pl.BlockSpec(shape, const, pipeline_mode=pl.Buffered(1))
    pat_specs = [half if d == 1 else
                 pl.BlockSpec((1, d, tm // d, D_HEADS), lambda i: (i // tiles_per_seq, 0, i % tiles_per_seq, 0))
                 for _, d in DILATED_PATTERNS]
    return pl.pallas_call(
        _merge_kernel,
        grid=(n // tm,),
        in_specs=[pl.BlockSpec((tm, D_MODEL), row), pl.BlockSpec((1, D_MODEL), const)] + pat_specs * 2 + [half] + [
            resident((D_MODEL, 2 * D_MODEL)), resident((D_HEADS, D_MODEL)), resident((D_HEADS, D_MODEL)),
            resident((D_MODEL, D_MODEL)), resident((1, D_MODEL)), resident((D_MODEL, D_FF)),
            resident((D_MODEL, D_FF)), resident((D_FF, D_MODEL)), resident((1, D_MODEL))],
        out_specs=pl.BlockSpec((tm, D_MODEL), row),
        out_shape=jax.ShapeDtypeStruct((n, D_MODEL), jnp.float32),
        scratch_shapes=[pltpu.VMEM((D_HEADS // LANES, tm, LANES), jnp.float32)] * (2 * len(RESIDUE_DILATIONS)),
        compiler_params=pltpu.CompilerParams(dimension_semantics=("arbitrary",), vmem_limit_bytes=VMEM_LIMIT),
        name="mix_gate_ffn_norm",
    )(x2, g, *os_, *lses, yb, wg, wua, wub, wo, g2, wgate, wup, wdown, gf)


def _rope_tables(seq):
    inv_freq = ROPE_THETA ** (-jnp.arange(HALF, dtype=jnp.float32) / HALF)
    ang = jnp.arange(seq, dtype=jnp.int32).astype(jnp.float32)[:, None] * inv_freq[None, :]
    cos, sin = jnp.cos(ang), jnp.sin(ang)
    return jnp.tile(cos, (1, 4)), jnp.concatenate([-sin, -sin, sin, sin], axis=1)


def kernel(x, norm_mix, w_in, w_up_a, w_up_b, w_out, norm_ffn, w_gate, w_up, w_down, norm_final):
    batch, seq, _ = x.shape
    assert seq % max(d * BLOCK for _, d in DILATED_PATTERNS) == 0 and seq % KEY_CHUNK == 0
    assert all(w // d == BLOCK for w, d in DILATED_PATTERNS)
    n = batch * seq
    bf = MXU_DTYPE
    xf = x.reshape(n, D_MODEL)
    cos_t, sin_t = _rope_tables(seq)
    for layer in range(w_in.shape[0]):
        w = w_in[layer]
        w_pack = jnp.pad(w[:, _packed_columns()], ((0, 0), (0, _P_TOTAL - _P_MISC - HEAD_DIM - IDX_HEADS))).astype(bf)
        w_gates = w[:, _GA:].astype(bf)
        qas, kas, vas, qb, qi, kbd, kid, misc = _project(xf, norm_mix[layer][None], w_pack, cos_t, sin_t, seq)

        dil = []
        for (_, d), q, k, v in zip(DILATED_PATTERNS, qas, kas, vas):
            if d == 1:
                o, lse = _dilated(*(z.reshape(batch, 1, seq, D_HEADS) for z in (q, k, v)), d)
                dil.append((o.reshape(n, D_HEADS), lse.reshape(n, D_HEADS)))
            else:
                dil.append(_dilated(q, k, v, d))

        ones_col = (jnp.arange(L_ROWS) == 0).astype(bf)[None, :]
        vb = jnp.concatenate([misc[:, :HEAD_DIM].astype(bf), jnp.broadcast_to(ones_col, (n, L_ROWS))], axis=1)
        vbt = jnp.swapaxes(vb.reshape(batch, seq // KEY_CHUNK, KEY_CHUNK, HEAD_DIM + L_ROWS), 2, 3)
        wt = jnp.swapaxes(misc[:, HEAD_DIM:HEAD_DIM + IDX_HEADS].reshape(batch, seq, IDX_HEADS), 1, 2)
        r3 = lambda z: z.reshape(batch, seq, z.shape[-1])
        yb = _sparse(r3(qi), r3(qb), wt, r3(kid), r3(kbd), vbt, batch, seq).reshape(n, D_HEADS)

        assert layer == w_in.shape[0] - 1, "the final norm is fused into the merge/FFN kernel of the last layer"
        xf = _merge(xf, norm_mix[layer][None], [o for o, _ in dil], [l for _, l in dil], yb,
                    w_gates, w_up_a[layer].astype(bf), w_up_b[layer].astype(bf), w_out[layer].astype(bf),
                    norm_ffn[layer][None], w_gate[layer].astype(bf), w_up[layer].astype(bf),
                    w_down[layer].astype(bf), norm_final[None], seq)
    return xf.reshape(batch, seq, D_MODEL)
```
